```python
import math
import jax
import jax.numpy as jnp
from jax import lax
import numpy as np

D_MODEL = 1024
BATCH = 8
SEQ = 4096
DEPTH = 2

GRID_W = 64
CTX_LEN = 256
ROPE_BASE = 10000.0
NORM_EPS = 1e-6
Q_BLOCK = 128
MASK_VALUE = -1e30
GATE_FLOOR = 1e-30

N_BRANCH = 4
BRANCH_W = D_MODEL // N_BRANCH

A_HEADS = 4
A_KV_HEADS = 2
A_HEAD_DIM = 64
A_WINDOW = 128

B_HEADS = 4
B_KEY_DIM = 64
B_VAL_DIM = 64
B_CHUNK = 64

C_HEADS = 4
C_Q_LORA = 256
C_KV_LORA = 128
C_NOPE = 64
C_ROPE = 32
C_V = 64

D_HEADS = 4
D_HEAD_DIM = 32

N_GROUPS = 4
EXPERTS_PER_GROUP = 8
N_EXPERTS = N_GROUPS * EXPERTS_PER_GROUP
EXPERT_TOP_K = 2
EXPERT_HIDDEN = 256

IN_SPLITS = (
    A_HEADS * A_HEAD_DIM, A_KV_HEADS * A_HEAD_DIM, A_KV_HEADS * A_HEAD_DIM,
    B_HEADS * B_KEY_DIM, B_HEADS * B_KEY_DIM, B_HEADS * B_KEY_DIM,
    B_HEADS * B_VAL_DIM, B_HEADS * B_VAL_DIM,
    C_Q_LORA, C_KV_LORA, C_ROPE,
    2 * D_HEADS * D_HEAD_DIM, 2 * D_HEADS * D_HEAD_DIM, 2 * D_HEADS * D_HEAD_DIM,
)
D_IN = sum(IN_SPLITS)

kernel_name = 'hybrid_parallel_mixer_dit_block'


def rms_norm(x, gain):
    xf = x.astype(jnp.float32)
    y = xf * lax.rsqrt(jnp.mean(jnp.square(xf), axis=-1, keepdims=True) + NORM_EPS)
    return (y * gain.astype(jnp.float32)).astype(x.dtype)


def rope_2d(x, row, col):
    half = x.shape[-1] // 2
    quarter = half // 2
    inv_freq = 1.0 / (ROPE_BASE ** (jnp.arange(quarter, dtype=jnp.float32) / quarter))

    def rotate(xa, pos):
        ang = pos.astype(jnp.float32)[:, None] * inv_freq[None, :]
        cos = jnp.cos(ang)[None, :, None, :]
        sin = jnp.sin(ang)[None, :, None, :]
        xa = xa.astype(jnp.float32)
        x1, x2 = xa[..., :quarter], xa[..., quarter:]
        return jnp.concatenate([x1 * cos - x2 * sin, x2 * cos + x1 * sin], axis=-1)

    out = jnp.concatenate([rotate(x[..., :half], row), rotate(x[..., half:], col)], axis=-1)
    return out.astype(x.dtype)


def _map_query_blocks(fn, q):
    b, l = q.shape[0], q.shape[1]
    nb = l // Q_BLOCK
    qb = jnp.moveaxis(q.reshape((b, nb, Q_BLOCK) + q.shape[2:]), 1, 0)
    out = lax.map(fn, qb)
    return jnp.moveaxis(out, 0, 1).reshape((b, l) + out.shape[3:])


def _softmax_attend(q, k, v, scale):
    s = jnp.einsum('bqhd,bkhd->bhqk', q, k, preferred_element_type=jnp.float32) * scale
    p = jax.nn.softmax(s, axis=-1)
    return jnp.einsum('bhqk,bkhe->bqhe', p.astype(v.dtype), v)


def _sink_softmax(s, sink):
    sink = jnp.broadcast_to(sink, s.shape[:-1] + (1,))
    p = jax.nn.softmax(jnp.concatenate([s, sink], axis=-1), axis=-1)
    return p[..., :-1]


def window_gqa_mixer(q_l, k_l, v_l, q_c, k_c, v_c, sink, row, col, with_ctx):
    b, l, _ = q_l.shape
    n_ctx = q_c.shape[1]
    g = A_HEADS // A_KV_HEADS
    w = A_WINDOW
    nb = l // w
    scale = A_HEAD_DIM ** -0.5
    q_l = rope_2d(q_l.reshape(b, l, A_HEADS, A_HEAD_DIM), row, col)
    k_l = rope_2d(k_l.reshape(b, l, A_KV_HEADS, A_HEAD_DIM), row, col)
    v_l = v_l.reshape(b, l, A_KV_HEADS, A_HEAD_DIM)
    k_c = k_c.reshape(b, n_ctx, A_KV_HEADS, A_HEAD_DIM)
    v_c = v_c.reshape(b, n_ctx, A_KV_HEADS, A_HEAD_DIM)
    sink_g = sink.astype(jnp.float32).reshape(A_KV_HEADS, g)

    def band(t):
        tb = jnp.pad(t.reshape(b, nb, w, A_KV_HEADS, A_HEAD_DIM), ((0, 0), (1, 1), (0, 0), (0, 0), (0, 0)))
        return jnp.concatenate([tb[:, :-2], tb[:, 1:-1], tb[:, 2:]], axis=2)

    k_band, v_band = band(k_l), band(v_l)
    qb = q_l.reshape(b, nb, w, A_KV_HEADS, g, A_HEAD_DIM)
    rel = jnp.arange(3 * w)[None, :] - w - jnp.arange(w)[:, None]
    kpos = (jnp.arange(nb)[:, None] - 1) * w + jnp.arange(3 * w)[None, :]
    mask = (jnp.abs(rel) <= w)[None] & ((kpos >= 0) & (kpos < l))[:, None, :]
    s_band = jnp.einsum('bnqhgd,bnkhd->bnhgqk', qb, k_band, preferred_element_type=jnp.float32) * scale
    s_band = jnp.where(mask[None, :, None, None], s_band, MASK_VALUE)
    s_ctx = jnp.einsum('bnqhgd,bchd->bnhgqc', qb, k_c, preferred_element_type=jnp.float32) * scale
    p = _sink_softmax(jnp.concatenate([s_band, s_ctx], axis=-1), sink_g[None, None, :, :, None, None])
    p = p.astype(v_l.dtype)
    o = (jnp.einsum('bnhgqk,bnkhd->bnqhgd', p[..., :3 * w], v_band)
         + jnp.einsum('bnhgqc,bchd->bnqhgd', p[..., 3 * w:], v_c))
    o_l = o.reshape(b, l, A_HEADS * A_HEAD_DIM)
    o_c = None
    if with_ctx:
        qc = q_c.reshape(b, n_ctx, A_KV_HEADS, g, A_HEAD_DIM)
        s = jnp.einsum('bqhgd,bkhd->bhgqk', qc, k_c, preferred_element_type=jnp.float32) * scale
        pc = _sink_softmax(s, sink_g[None, :, :, None, None]).astype(v_c.dtype)
        o_c = jnp.einsum('bhgqk,bkhd->bqhgd', pc, v_c).reshape(b, n_ctx, A_HEADS * A_HEAD_DIM)
    return o_l, o_c


def _hgrn_gate(zf, lb):
    zf = zf.astype(jnp.float32)
    lb = lb.astype(jnp.float32)
    f = lb + (1.0 - lb) * jax.nn.sigmoid(zf)
    logf = jnp.log(jnp.maximum(f, GATE_FLOOR))
    k = (1.0 - lb) * jax.nn.sigmoid(-zf)
    return logf, k


def _gla_chunk_scan(q, k, v, logf, s0, with_output):
    b, l, h, _ = q.shape
    dv = v.shape[-1]
    n = l // B_CHUNK

    def chunks(t):
        t = t.astype(jnp.float32).reshape(b, n, B_CHUNK, h, t.shape[-1])
        return jnp.moveaxis(t, 1, 0).swapaxes(2, 3)

    xs = (chunks(q), chunks(k), chunks(v), chunks(logf))
    tri = jnp.tril(jnp.ones((B_CHUNK, B_CHUNK), dtype=bool))[:, :, None]

    def step(state, inp):
        qi, ki, vi, fi = inp
        a = jnp.cumsum(fi, axis=2)
        a_end = a[:, :, -1:, :]
        state_next = (jnp.exp(a_end[:, :, 0, :])[..., None] * state
                      + jnp.einsum('bhcd,bhce->bhde', ki * jnp.exp(a_end - a), vi))
        if not with_output:
            return state_next, None
        o_inter = jnp.einsum('bhtd,bhde->bhte', qi * jnp.exp(a), state)
        rel = a[:, :, :, None, :] - a[:, :, None, :, :]
        decay = jnp.where(tri, jnp.exp(jnp.where(tri, rel, 0.0)), 0.0)
        wts = jnp.einsum('bhtd,bhsd,bhtsd->bhts', qi, ki, decay)
        return state_next, o_inter + jnp.einsum('bhts,bhse->bhte', wts, vi)

    s_fin, o = lax.scan(step, s0.astype(jnp.float32), xs)
    if not with_output:
        return None, s_fin
    o = jnp.moveaxis(o.swapaxes(2, 3), 0, 1).reshape(b, l, h, dv).astype(v.dtype)
    return o, s_fin


def hgrn2_mixer(z_l, z_c, lb, g_norm, with_ctx):
    def prep(z):
        zq, zff, zfb, zi, zg = z
        bt, lt, _ = zq.shape
        hk = lambda t: t.reshape(bt, lt, B_HEADS, B_KEY_DIM)
        logf_f, k_f = _hgrn_gate(zff, lb[0])
        logf_b, k_b = _hgrn_gate(zfb, lb[1])
        return (hk(zq), hk(k_f), hk(logf_f), hk(k_b), hk(logf_b),
                zi.reshape(bt, lt, B_HEADS, B_VAL_DIM), zg.reshape(bt, lt, B_HEADS, B_VAL_DIM))

    def flip(t):
        return jnp.flip(t, axis=1)

    def readout(o, g):
        o = rms_norm(o, g_norm) * jax.nn.silu(g)
        return o.reshape(o.shape[0], o.shape[1], B_HEADS * B_VAL_DIM)

    q_c, kf_c, lf_c, kb_c, lbk_c, v_c, g_c = prep(z_c)
    s0 = jnp.zeros((q_c.shape[0], B_HEADS, B_KEY_DIM, B_VAL_DIM), jnp.float32)
    oc_f, s_f = _gla_chunk_scan(q_c, kf_c, v_c, lf_c, s0, with_ctx)
    oc_b, s_b = _gla_chunk_scan(flip(q_c), flip(kb_c), flip(v_c), flip(lbk_c), s0, with_ctx)
    q_l, kf_l, lf_l, kb_l, lbk_l, v_l, g_l = prep(z_l)
    ol_f, _ = _gla_chunk_scan(q_l, kf_l, v_l, lf_l, s_f, True)
    ol_b, _ = _gla_chunk_scan(flip(q_l), flip(kb_l), flip(v_l), flip(lbk_l), s_b, True)
    o_l = readout(ol_f + flip(ol_b), g_l)
    o_c = readout(oc_f + flip(oc_b), g_c) if with_ctx else None
    return o_l, o_c


def mla_mixer(cq_l, ckv_l, kr_l, cq_c, ckv_c, kr_c, q_norm, kv_norm, w_uq, w_ukv, row, col, with_ctx):
    scale = (C_NOPE + C_ROPE) ** -0.5

    def queries(cq, rotary):
        bt, lt, _ = cq.shape
        q = (rms_norm(cq, q_norm) @ w_uq).reshape(bt, lt, C_HEADS, C_NOPE + C_ROPE)
        q_rope = q[..., C_NOPE:]
        if rotary:
            q_rope = rope_2d(q_rope, row, col)
        return jnp.concatenate([q[..., :C_NOPE], q_rope], axis=-1)

    def keys_values(ckv, kr, rotary):
        bt, lt, _ = ckv.shape
        kv = (rms_norm(ckv, kv_norm) @ w_ukv).reshape(bt, lt, C_HEADS, C_NOPE + C_V)
        k_rope = kr.reshape(bt, lt, 1, C_ROPE)
        if rotary:
            k_rope = rope_2d(k_rope, row, col)
        k = jnp.concatenate([kv[..., :C_NOPE], jnp.broadcast_to(k_rope, (bt, lt, C_HEADS, C_ROPE))], axis=-1)
        return k, kv[..., C_NOPE:]

    k_c, v_c = keys_values(ckv_c, kr_c, False)
    k_l, v_l = keys_values(ckv_l, kr_l, True)
    k_all = jnp.concatenate([k_l, k_c], axis=1)
    v_all = jnp.concatenate([v_l, v_c], axis=1)
    o_l = _map_query_blocks(lambda qb: _softmax_attend(qb, k_all, v_all, scale), queries(cq_l, True))
    o_l = o_l.reshape(o_l.shape[0], o_l.shape[1], C_HEADS * C_V)
    o_c = None
    if with_ctx:
        o_c = _softmax_attend(queries(cq_c, False), k_c, v_c, scale)
        o_c = o_c.reshape(o_c.shape[0], o_c.shape[1], C_HEADS * C_V)
    return o_l, o_c


def diff_mixer(q_l, k_l, v_l, q_c, k_c, v_c, lam_p, subln, layer, row, col, with_ctx):
    lam_init = 0.8 - 0.6 * math.exp(-0.3 * layer)
    lam_p = lam_p.astype(jnp.float32)
    lam = jnp.exp(jnp.sum(lam_p[0] * lam_p[1])) - jnp.exp(jnp.sum(lam_p[2] * lam_p[3])) + lam_init
    scale = D_HEAD_DIM ** -0.5

    def qk(t, rotary):
        bt, lt, _ = t.shape
        t = t.reshape(bt, lt, 2 * D_HEADS, D_HEAD_DIM)
        if rotary:
            t = rope_2d(t, row, col)
        return t.reshape(bt, lt, D_HEADS, 2, D_HEAD_DIM)

    def vals(t):
        return t.reshape(t.shape[0], t.shape[1], D_HEADS, 2 * D_HEAD_DIM)

    def attend(q, k, v):
        s = jnp.einsum('bqhmd,bkhmd->bmhqk', q, k, preferred_element_type=jnp.float32) * scale
        p = jax.nn.softmax(s, axis=-1)
        a = p[:, 0] - lam * p[:, 1]
        return jnp.einsum('bhqk,bkhe->bqhe', a.astype(v.dtype), v)

    def finish(o):
        o = rms_norm(o, subln) * (1.0 - lam_init)
        return o.reshape(o.shape[0], o.shape[1], D_HEADS * 2 * D_HEAD_DIM)

    kc, vc = qk(k_c, False), vals(v_c)
    k_all = jnp.concatenate([qk(k_l, True), kc], axis=1)
    v_all = jnp.concatenate([vals(v_l), vc], axis=1)
    o_l = finish(_map_query_blocks(lambda qb: attend(qb, k_all, v_all), qk(q_l, True)))
    o_c = finish(attend(qk(q_c, False), kc, vc)) if with_ctx else None
    return o_l, o_c


def merge_branches(h, branches, w_gate, w_branch, w_out):
    gates = jax.nn.sigmoid(h @ w_gate)
    y = jnp.zeros_like(h)
    for i, br in enumerate(branches):
        y = y + gates[..., i * D_MODEL:(i + 1) * D_MODEL] * (br @ w_branch[i])
    return y @ w_out


def hier_moe(h, w_rg, b_rg, w_re, b_re, w_gate, w_up, w_down):
    n = h.shape[0]
    p_group = jax.nn.softmax((h @ w_rg).astype(jnp.float32) + b_rg.astype(jnp.float32), axis=-1)
    g_val, g_idx = lax.top_k(p_group, 1)
    e_logits = ((h @ w_re).astype(jnp.float32) + b_re.astype(jnp.float32)).reshape(n, N_GROUPS, EXPERTS_PER_GROUP)
    e_logits = jnp.take_along_axis(e_logits, g_idx[:, :, None], axis=1)[:, 0]
    e_val, e_idx = lax.top_k(jax.nn.softmax(e_logits, axis=-1), EXPERT_TOP_K)
    e_val = e_val / jnp.sum(e_val, axis=-1, keepdims=True)
    expert_id = g_idx * EXPERTS_PER_GROUP + e_idx
    weights = jnp.sum(jax.nn.one_hot(expert_id, N_EXPERTS, dtype=jnp.float32) * (g_val * e_val)[..., None], axis=1)
    weights = weights.astype(h.dtype)
    y = jnp.zeros_like(h)
    for e in range(N_EXPERTS):
        he = jax.nn.silu(h @ w_gate[e]) * (h @ w_up[e])
        y = y + weights[:, e:e + 1] * (he @ w_down[e])
    return y


def setup_inputs(seed: int = 0) -> dict:
    key = jax.random.key(seed)
    ks = iter(jax.random.split(key, 40))
    d = D_MODEL

    def nrm(shape, s):
        return jax.random.normal(next(ks), shape, jnp.float32) * s

    def gain(shape):
        return 1.0 + nrm(shape, 0.02)

    return {
        'x': nrm((BATCH, SEQ, d), 1.0),
        'c': nrm((BATCH, d), 1.0),
        'ctx': nrm((BATCH, CTX_LEN, d), 1.0),
        'c_ctx': nrm((d,), 1.0),
        'w_mod': nrm((DEPTH, d, 6 * d), d ** -0.5),
        'b_mod': nrm((DEPTH, 6 * d), 0.02),
        'norm1': gain((DEPTH, d)),
        'norm2': gain((DEPTH, d)),
        'w_in': nrm((DEPTH, d, D_IN), d ** -0.5),
        'w_gate': nrm((DEPTH, d, N_BRANCH * d), d ** -0.5),
        'w_branch': nrm((DEPTH, N_BRANCH, BRANCH_W, d), BRANCH_W ** -0.5),
        'w_out': nrm((DEPTH, d, d), d ** -0.5),
        'attn_sink': nrm((DEPTH, A_HEADS), 0.5),
        'hgrn_lb_logits': nrm((DEPTH, 2, B_HEADS * B_KEY_DIM), 0.5),
        'hgrn_norm': gain((DEPTH, B_VAL_DIM)),
        'mla_q_norm': gain((DEPTH, C_Q_LORA)),
        'mla_kv_norm': gain((DEPTH, C_KV_LORA)),
        'mla_w_uq': nrm((DEPTH, C_Q_LORA, C_HEADS * (C_NOPE + C_ROPE)), C_Q_LORA ** -0.5),
        'mla_w_ukv': nrm((DEPTH, C_KV_LORA, C_HEADS * (C_NOPE + C_V)), C_KV_LORA ** -0.5),
        'diff_lambda': nrm((DEPTH, 4, D_HEAD_DIM), 0.1),
        'diff_subln': gain((DEPTH, 2 * D_HEAD_DIM)),
        'w_router_group': nrm((DEPTH, d, N_GROUPS), d ** -0.5),
        'b_router_group': nrm((DEPTH, N_GROUPS), 0.01),
        'w_router_expert': nrm((DEPTH, d, N_EXPERTS), d ** -0.5),
        'b_router_expert': nrm((DEPTH, N_EXPERTS), 0.01),
        'w_expert_gate': nrm((DEPTH, N_EXPERTS, d, EXPERT_HIDDEN), d ** -0.5),
        'w_expert_up': nrm((DEPTH, N_EXPERTS, d, EXPERT_HIDDEN), d ** -0.5),
        'w_expert_down': nrm((DEPTH, N_EXPERTS, EXPERT_HIDDEN, d), EXPERT_HIDDEN ** -0.5),
        'final_norm': gain((d,)),
    }


def reference(x, c, ctx, c_ctx, w_mod, b_mod, norm1, norm2, w_in, w_gate, w_branch, w_out,
              attn_sink, hgrn_lb_logits, hgrn_norm, mla_q_norm, mla_kv_norm, mla_w_uq, mla_w_ukv,
              diff_lambda, diff_subln, w_router_group, b_router_group, w_router_expert,
              b_router_expert, w_expert_gate, w_expert_up, w_expert_down, final_norm):
    b, l, d = x.shape
    rows = l // GRID_W
    row = jnp.repeat(jnp.arange(rows, dtype=jnp.int32), GRID_W)
    col = jnp.arange(rows * GRID_W, dtype=jnp.int32) % GRID_W
    sm = jax.nn.softmax(hgrn_lb_logits.astype(jnp.float32), axis=0)
    lower_bounds = jnp.cumsum(sm, axis=0) - sm[0]
    split_at = [int(s) for s in np.cumsum(IN_SPLITS)[:-1]]
    silu_c = jax.nn.silu(c)
    silu_cc = jax.nn.silu(c_ctx)
    for li in range(DEPTH):
        with_ctx = li < DEPTH - 1
        m_l = (silu_c @ w_mod[li] + b_mod[li]).reshape(b, 6, 1, d)
        m_c = (silu_cc @ w_mod[li] + b_mod[li]).reshape(6, d)
        h_l = rms_norm(x, norm1[li]) * (1 + m_l[:, 1]) + m_l[:, 0]
        h_c = rms_norm(ctx, norm1[li]) * (1 + m_c[1]) + m_c[0]
        z_l = jnp.split(h_l @ w_in[li], split_at, axis=-1)
        z_c = jnp.split(h_c @ w_in[li], split_at, axis=-1)
        a_l, a_c = window_gqa_mixer(*z_l[0:3], *z_c[0:3], attn_sink[li], row, col, with_ctx)
        b_l, b_c = hgrn2_mixer(z_l[3:8], z_c[3:8], lower_bounds[li], hgrn_norm[li], with_ctx)
        c_l, c_c = mla_mixer(*z_l[8:11], *z_c[8:11], mla_q_norm[li], mla_kv_norm[li],
                             mla_w_uq[li], mla_w_ukv[li], row, col, with_ctx)
        d_l, d_c = diff_mixer(*z_l[11:14], *z_c[11:14], diff_lambda[li], diff_subln[li], li, row, col, with_ctx)
        x = x + m_l[:, 2] * merge_branches(h_l, (a_l, b_l, c_l, d_l), w_gate[li], w_branch[li], w_out[li])
        moe_w = (w_router_group[li], b_router_group[li], w_router_expert[li], b_router_expert[li],
                 w_expert_gate[li], w_expert_up[li], w_expert_down[li])
        h2_l = rms_norm(x, norm2[li]) * (1 + m_l[:, 4]) + m_l[:, 3]
        if with_ctx:
            ctx = ctx + m_c[2] * merge_branches(h_c, (a_c, b_c, c_c, d_c), w_gate[li], w_branch[li], w_out[li])
            h2_c = rms_norm(ctx, norm2[li]) * (1 + m_c[4]) + m_c[3]
            f = hier_moe(jnp.concatenate([h2_l.reshape(-1, d), h2_c.reshape(-1, d)], axis=0), *moe_w)
            x = x + m_l[:, 5] * f[:b * l].reshape(b, l, d)
            ctx = ctx + m_c[5] * f[b * l:].reshape(ctx.shape)
        else:
            x = x + m_l[:, 5] * hier_moe(h2_l.reshape(-1, d), *moe_w).reshape(b, l, d)
    return rms_norm(x, final_norm)
```

```python
import functools
import math

import numpy as np
import jax
import jax.numpy as jnp
from jax import lax
from jax.experimental import pallas as pl
from jax.experimental.pallas import tpu as pltpu

D_MODEL = 1024
GRID_W = 64
ROPE_BASE = 10000.0
NORM_EPS = 1e-6
MASK_VALUE = -1e30
GATE_FLOOR = 1e-30

A_HEADS, A_KV_HEADS, A_HEAD_DIM, A_WINDOW = 4, 2, 64, 128
B_HEADS, B_KEY_DIM, B_VAL_DIM = 4, 64, 64
C_HEADS, C_Q_LORA, C_KV_LORA, C_NOPE, C_ROPE, C_V = 4, 256, 128, 64, 32, 64
D_HEADS, D_HEAD_DIM = 4, 32
N_GROUPS, EXPERTS_PER_GROUP, EXPERT_HIDDEN = 4, 8, 256
N_EXPERTS = N_GROUPS * EXPERTS_PER_GROUP

TM = 256
TQ = 256
KV_CHUNK = 1024
HG_CHUNK = 64
HG_LEVELS = 6
LANES = 128
VMEM_LIMIT = 52 * 1024 * 1024

OFF_A, OFF_B, OFF_C, OFF_D, N_PERM = 0, 768, 2048, 2944, 3712

F32 = jnp.float32
BF16 = jnp.bfloat16


def _dot(a, b):
    return jnp.dot(a, b, preferred_element_type=F32)


def _dot_nt(a, b):
    return lax.dot_general(a, b, (((1,), (1,)), ((), ())), preferred_element_type=F32)


def _dot_tn(a, b):
    return lax.dot_general(a, b, (((0,), (0,)), ((), ())), preferred_element_type=F32)


def _dot_hi(a, b):
    return jnp.dot(a, b, preferred_element_type=F32, precision=lax.Precision.HIGHEST)


def _sigmoid(x):
    return 1.0 / (1.0 + jnp.exp(-x))


def _rms(x, gain):
    return x * lax.rsqrt(jnp.mean(x * x, axis=-1, keepdims=True) + NORM_EPS) * gain


def _head_rms(o, gain, width):
    r = lax.broadcasted_iota(jnp.int32, (width, width), 0) // 64
    c = lax.broadcasted_iota(jnp.int32, (width, width), 1) // 64
    ones = jnp.where(r == c, 1.0 / 64.0, 0.0).astype(F32)
    ms = _dot_hi(o * o, ones)
    return o * lax.rsqrt(ms + NORM_EPS) * gain


def _rope(x, cos, sin, quarter):
    w = x.shape[-1]
    lane = lax.broadcasted_iota(jnp.int32, x.shape, 1)
    first = (lane % (2 * quarter)) < quarter
    sw = jnp.where(first, pltpu.roll(x, w - quarter, 1), pltpu.roll(x, quarter, 1))
    return x * cos + sw * sin


def _cparams(sem):
    return pltpu.CompilerParams(dimension_semantics=sem, vmem_limit_bytes=VMEM_LIMIT)


def _const_spec(shape):
    n = len(shape)
    return pl.BlockSpec(shape, lambda *_: (0,) * n)


def _mod_kernel(c_ref, w_ref, b_ref, o_ref):
    c = c_ref[...]
    o_ref[0] = _dot_hi(c * _sigmoid(c), w_ref[0]) + b_ref[0]


def _modulation(c_all, w_mod, b_mod):
    depth, d, n = w_mod.shape
    nb = 1536
    return pl.pallas_call(
        _mod_kernel,
        out_shape=jax.ShapeDtypeStruct((depth, 16, n), F32),
        grid=(depth, n // nb),
        in_specs=[pl.BlockSpec((16, d), lambda l, j: (0, 0)),
                  pl.BlockSpec((1, d, nb), lambda l, j: (l, 0, j)),
                  pl.BlockSpec((1, 1, nb), lambda l, j: (l, 0, j))],
        out_specs=pl.BlockSpec((1, 16, nb), lambda l, j: (l, 0, j)),
        compiler_params=_cparams(("arbitrary", "arbitrary")),
        name="modulation",
    )(c_all, w_mod, b_mod.reshape(depth, 1, n))


def _inproj_kernel(*refs, has_prev):
    if has_prev:
        x_ref, f_ref, modp_ref = refs[:3]
        refs = refs[3:]
    else:
        x_ref = refs[0]
        refs = refs[1:]
    (mod_ref, n1_ref, w_ref, qn_ref, kvn_ref, wuq_ref, wukv_ref,
     cosa_ref, sina_ref, cosc_ref, sinc_ref, cosd_ref, sind_ref) = refs[:13]
    outs = refs[13:]
    if has_prev:
        xo_ref = outs[0]
        outs = outs[1:]
    (qa_ref, ka_ref, va_ref, bq_ref, bff_ref, bfb_ref, bv_ref, bg_ref,
     cq_ref, ck_ref, cv_ref, dq_ref, dk_ref, dv_ref) = outs

    x = x_ref[0]
    if has_prev:
        x = x + modp_ref[0, 0, 5:6, :] * f_ref[0]
        xo_ref[0] = x
    m = mod_ref[0, 0]
    h = (_rms(x, n1_ref[...]) * (1.0 + m[1:2]) + m[0:1]).astype(BF16)

    z = _dot(h, w_ref[:, OFF_A:OFF_A + 768])
    cosa, sina = cosa_ref[...], sina_ref[...]
    qa_ref[0] = (_rope(z[:, 0:256], cosa, sina, 16) * (A_HEAD_DIM ** -0.5)).astype(BF16)
    ka_ref[0] = _rope(z[:, 256:512], cosa, sina, 16).astype(BF16)
    va_ref[0] = z[:, 512:768].astype(BF16)

    z = _dot(h, w_ref[:, OFF_B:OFF_B + 1280])
    bq_ref[0] = z[:, 0:256].astype(BF16)
    bff_ref[0] = z[:, 256:512]
    bfb_ref[0] = z[:, 512:768]
    bv_ref[0] = z[:, 768:1024].astype(BF16)
    bg_ref[0] = z[:, 1024:1280].astype(BF16)

    z = _dot(h, w_ref[:, OFF_C:OFF_C + 896])
    cosc, sinc = cosc_ref[...], sinc_ref[...]
    cq = _rms(z[:, 0:256], qn_ref[...]).astype(BF16)
    q = _rope(_dot(cq, wuq_ref[...]), cosc, sinc, 8)
    cq_ref[0] = (q * ((C_NOPE + C_ROPE) ** -0.5)).astype(BF16)
    ckv = _rms(z[:, 256:384], kvn_ref[...]).astype(BF16)
    kv = _dot(ckv, wukv_ref[...])
    ck_ref[0] = (kv[:, 0:512] + _rope(z[:, 384:896], cosc, sinc, 8)).astype(BF16)
    cv_ref[0] = kv[:, 512:768].astype(BF16)

    z = _dot(h, w_ref[:, OFF_D:OFF_D + 768])
    cosd, sind = cosd_ref[...], sind_ref[...]
    dq_ref[0] = (_rope(z[:, 0:256], cosd, sind, 8) * (D_HEAD_DIM ** -0.5)).astype(BF16)
    dk_ref[0] = _rope(z[:, 256:512], cosd, sind, 8).astype(BF16)
    dv_ref[0] = z[:, 512:768].astype(BF16)


def _inproj(x, f_prev, mod_prev, mod, n1, w_perm, qn, kvn, wuq, wukv, tabs, n_lat_tiles):
    b, t, d = x.shape
    nt = t // TM
    has_prev = f_prev is not None
    tile = lambda w: pl.BlockSpec((1, TM, w), lambda bi, i: (bi, i, 0))
    modspec = pl.BlockSpec((1, 1, 8, d), lambda bi, i: (bi, i // n_lat_tiles, 0, 0))
    tab = lambda w: pl.BlockSpec((TM, w), lambda bi, i: (i, 0))
    in_specs, args = [tile(d)], [x]
    if has_prev:
        in_specs += [tile(d), modspec]
        args += [f_prev, mod_prev]
    in_specs += [modspec, _const_spec((1, d)), _const_spec((d, N_PERM)), _const_spec((1, 256)),
                 _const_spec((1, 128)), _const_spec((256, 512)), _const_spec((128, 768)),
                 tab(256), tab(256), tab(512), tab(512), tab(256), tab(256)]
    args += [mod, n1, w_perm, qn, kvn, wuq, wukv, *tabs]
    widths = [(256, BF16), (256, BF16), (256, BF16),
              (256, BF16), (256, F32), (256, F32), (256, BF16), (256, BF16),
              (512, BF16), (512, BF16), (256, BF16),
              (256, BF16), (256, BF16), (256, BF16)]
    out_shape = [jax.ShapeDtypeStruct((b, t, w), dt) for w, dt in widths]
    out_specs = [tile(w) for w, _ in widths]
    if has_prev:
        out_shape = [jax.ShapeDtypeStruct((b, t, d), F32)] + out_shape
        out_specs = [tile(d)] + out_specs
    res = pl.pallas_call(
        functools.partial(_inproj_kernel, has_prev=has_prev),
        out_shape=out_shape, grid=(b, nt), in_specs=in_specs, out_specs=out_specs,
        compiler_params=_cparams(("parallel", "parallel")),
        name="inproj",
    )(*args)
    if has_prev:
        return res[0], res[1:]
    return x, res


def _win_kernel(sink_ref, q_ref, k_ref, v_ref, o_ref, *, n_lat, t_all):
    w = A_WINDOW
    n = pl.program_id(1)
    nb_lat = n_lat // w
    start = pl.multiple_of(jnp.clip((n - 1) * w, 0, n_lat - 3 * w), w)
    q = q_ref[0]
    lane = lax.broadcasted_iota(jnp.int32, (w, LANES), 1)
    row = lax.broadcasted_iota(jnp.int32, (2 * w, 3 * w), 0)
    col = lax.broadcasted_iota(jnp.int32, (2 * w, 3 * w), 1)
    rel = (start + col) - (n * w + row % w)
    valid = (jnp.abs(rel) <= w) & (n < nb_lat)
    row1 = lax.broadcasted_iota(jnp.int32, (2 * w, 1), 0)
    for j in range(A_KV_HEADS):
        sl = slice(LANES * j, LANES * (j + 1))
        qp = q[:, sl]
        zero = jnp.zeros_like(qp)
        lhs = jnp.concatenate([jnp.where(lane < 64, qp, zero), jnp.where(lane >= 64, qp, zero)], axis=0)
        sb = _dot_nt(lhs, k_ref[0, pl.ds(start, 3 * w), sl])
        sc = _dot_nt(lhs, k_ref[0, n_lat:t_all, sl])
        sb = jnp.where(valid, sb, MASK_VALUE)
        sink = jnp.where(row1 < w, sink_ref[2 * j], sink_ref[2 * j + 1])
        m = jnp.maximum(jnp.maximum(jnp.max(sb, axis=1, keepdims=True), jnp.max(sc, axis=1, keepdims=True)), sink)
        pb = jnp.exp(sb - m)
        pc = jnp.exp(sc - m)
        den = jnp.sum(pb, axis=1, keepdims=True) + jnp.sum(pc, axis=1, keepdims=True) + jnp.exp(sink - m)
        o = (_dot(pb.astype(BF16), v_ref[0, pl.ds(start, 3 * w), sl])
             + _dot(pc.astype(BF16), v_ref[0, n_lat:t_all, sl])) / den
        o_ref[0, :, sl] = jnp.where(lane < 64, o[:w], o[w:]).astype(o_ref.dtype)


def _win_attention(sink, q, k, v, n_lat):
    b, t, _ = q.shape
    w = A_WINDOW
    return pl.pallas_call(
        functools.partial(_win_kernel, n_lat=n_lat, t_all=t),
        out_shape=jax.ShapeDtypeStruct((b, t, 256), BF16),
        grid=(b, t // w),
        in_specs=[pl.BlockSpec(memory_space=pltpu.SMEM),
                  pl.BlockSpec((1, w, 256), lambda bi, i: (bi, i, 0)),
                  pl.BlockSpec((1, t, 256), lambda bi, i: (bi, 0, 0)),
                  pl.BlockSpec((1, t, 256), lambda bi, i: (bi, 0, 0))],
        out_specs=pl.BlockSpec((1, w, 256), lambda bi, i: (bi, i, 0)),
        compiler_params=_cparams(("parallel", "arbitrary")),
        name="win_attention",
    )(sink, q, k, v)


def _hgrn_level_matrix(rev):
    c = HG_CHUNK
    m = np.zeros((8, c, c), np.float32)
    for lvl in range(HG_LEVELS):
        size = c >> lvl
        for t in range(c):
            mid = (t // size) * size + size // 2
            upper = t >= mid
            if not rev:
                rng = range(mid, t + 1) if upper else range(t + 1, mid)
            else:
                rng = range(mid, t) if upper else range(t, mid)
            m[lvl, t, list(rng)] = 1.0
    for t in range(c):
        if not rev:
            m[6, t, :t + 1] = 1.0
            m[7, t, t + 1:] = 1.0
        else:
            m[6, t, t:] = 1.0
            m[7, t, :t] = 1.0
    return m.reshape(8 * c, c)


def _hgrn_chunk(q, k, logf, v, st, mlev, rev):
    c = HG_CHUNK
    hi = logf.astype(BF16)
    lo = (logf - hi.astype(F32)).astype(BF16)
    e2 = _dot(mlev, jnp.concatenate([hi, lo], axis=1))
    wgt = jnp.exp(e2[:, :256] + e2[:, 256:])
    t = lax.broadcasted_iota(jnp.int32, (c, 256), 0)
    s_col = lax.broadcasted_iota(jnp.int32, (c, 256), 1) % c
    r_bd = lax.broadcasted_iota(jnp.int32, (256, 256), 0) // 64
    c_bd = lax.broadcasted_iota(jnp.int32, (256, 256), 1) // 64
    bd = r_bd == c_bd

    def block_diag(a):
        return jnp.where(bd, jnp.concatenate([a] * 4, axis=0), 0.0).astype(BF16)

    scores = jnp.where(t == s_col, _dot_nt(q.astype(BF16), block_diag(k)), 0.0)
    for lvl in range(HG_LEVELS):
        wl = wgt[c * lvl:c * (lvl + 1)]
        q_side = ((t >> (HG_LEVELS - 1 - lvl)) & 1) == (0 if rev else 1)
        ql = jnp.where(q_side, q * wl, 0.0).astype(BF16)
        kl = jnp.where(q_side, 0.0, k * wl)
        s = _dot_nt(ql, block_diag(kl))
        if lvl > 0:
            sh = HG_LEVELS - lvl
            s = jnp.where((t >> sh) == (s_col >> sh), s, 0.0)
        scores = scores + s
    q_in = (q * wgt[6 * c:7 * c]).astype(BF16)
    k_end = (k * wgt[7 * c:8 * c]).astype(BF16)
    o = _dot(scores.astype(BF16), block_diag(v)) + _dot_nt(q_in, st.astype(BF16))
    edge = 6 * c if rev else 7 * c - 1
    total = wgt[edge:edge + 1]
    st_new = st * total + jnp.where(bd, _dot_tn(v.astype(BF16), k_end), 0.0)
    return o, st_new


def _hgrn_kernel(mf_ref, mb_ref, lb_ref, qf_ref, zf_ref, vf_ref, qb_ref, zb_ref, vb_ref,
                 of_ref, ob_ref, stf_ref, stb_ref, lff_ref, kff_ref, lfb_ref, kfb_ref):
    c = HG_CHUNK
    nc = TM // c

    @pl.when(pl.program_id(1) == 0)
    def _():
        stf_ref[...] = jnp.zeros_like(stf_ref)
        stb_ref[...] = jnp.zeros_like(stb_ref)

    def gates(z, lb, lf_ref, k_ref):
        f = lb + (1.0 - lb) * _sigmoid(z)
        lf_ref[...] = jnp.log(jnp.maximum(f, GATE_FLOOR))
        k_ref[...] = (1.0 - lb) * _sigmoid(-z)

    gates(zf_ref[0], lb_ref[0:1], lff_ref, kff_ref)
    gates(zb_ref[0], lb_ref[1:2], lfb_ref, kfb_ref)

    def body(ci, carry):
        r0 = pl.multiple_of(ci * c, c)
        r1 = pl.multiple_of((nc - 1 - ci) * c, c)
        o, st = _hgrn_chunk(qf_ref[0, pl.ds(r0, c), :].astype(F32), kff_ref[pl.ds(r0, c), :],
                            lff_ref[pl.ds(r0, c), :], vf_ref[0, pl.ds(r0, c), :].astype(F32),
                            stf_ref[...], mf_ref[...], False)
        of_ref[0, pl.ds(r0, c), :] = o
        stf_ref[...] = st
        o, st = _hgrn_chunk(qb_ref[0, pl.ds(r1, c), :].astype(F32), kfb_ref[pl.ds(r1, c), :],
                            lfb_ref[pl.ds(r1, c), :], vb_ref[0, pl.ds(r1, c), :].astype(F32),
                            stb_ref[...], mb_ref[...], True)
        ob_ref[0, pl.ds(r1, c), :] = o
        stb_ref[...] = st
        return carry

    lax.fori_loop(0, nc, body, 0)


def _hgrn(lb, q, zff, zfb, v, n_lat):
    b, t, _ = q.shape
    nlb = n_lat // TM
    ncb = (t - n_lat) // TM
    fwd = lambda bi, i: (bi, jnp.where(i < ncb, nlb + i, i - ncb), 0)
    bwd = lambda bi, i: (bi, jnp.where(i < ncb, nlb + ncb - 1 - i, nlb + ncb - 1 - i), 0)
    blk = lambda im: pl.BlockSpec((1, TM, 256), im)
    mf = jnp.asarray(_hgrn_level_matrix(False), BF16)
    mb = jnp.asarray(_hgrn_level_matrix(True), BF16)
    return pl.pallas_call(
        _hgrn_kernel,
        out_shape=[jax.ShapeDtypeStruct((b, t, 256), F32)] * 2,
        grid=(b, t // TM),
        in_specs=[_const_spec((8 * HG_CHUNK, HG_CHUNK)), _const_spec((8 * HG_CHUNK, HG_CHUNK)),
                  _const_spec((2, 256)),
                  blk(fwd), blk(fwd), blk(fwd), blk(bwd), blk(bwd), blk(bwd)],
        out_specs=[blk(fwd), blk(bwd)],
        scratch_shapes=[pltpu.VMEM((256, 256), F32), pltpu.VMEM((256, 256), F32),
                        pltpu.VMEM((TM, 256), F32), pltpu.VMEM((TM, 256), F32),
                        pltpu.VMEM((TM, 256), F32), pltpu.VMEM((TM, 256), F32)],
        compiler_params=_cparams(("parallel", "arbitrary")),
        name="hgrn",
    )(mf, mb, lb, q, zff, v, q, zfb, v)


def _key_chunks(n_lat, t_all):
    chunks = [(lo, min(lo + KV_CHUNK, n_lat)) for lo in range(0, n_lat, KV_CHUNK)]
    return chunks + [(n_lat, t_all)], [(n_lat, t_all)]


def _online_softmax(lhs, k_ref, v_ref, k_lanes, chunks):
    rows = lhs.shape[0]
    m = jnp.full((rows, 1), -jnp.inf, F32)
    l = jnp.zeros((rows, 1), F32)
    acc = jnp.zeros((rows, LANES), F32)
    for lo, hi in chunks:
        s = _dot_nt(lhs, k_ref[0, lo:hi, k_lanes])
        m_new = jnp.maximum(m, jnp.max(s, axis=1, keepdims=True))
        alpha = jnp.exp(m - m_new)
        p = jnp.exp(s - m_new)
        l = alpha * l + jnp.sum(p, axis=1, keepdims=True)
        acc = alpha * acc + _dot(p.astype(BF16), v_ref[0, lo:hi, :])
        m = m_new
    return acc / l


def _mla_kernel(q_ref, k_ref, v_ref, o_ref, *, n_lat, t_all):
    i = pl.program_id(2)
    lane = lax.broadcasted_iota(jnp.int32, (TQ, LANES), 1)
    lat_chunks, ctx_chunks = _key_chunks(n_lat, t_all)

    def run(chunks):
        o = [_online_softmax(q_ref[0, :, LANES * h:LANES * (h + 1)], k_ref, v_ref,
                             slice(LANES * h, LANES * (h + 1)), chunks) for h in range(2)]
        o_ref[0] = jnp.where(lane < 64, o[0], o[1]).astype(o_ref.dtype)

    pl.when(i < n_lat // TQ)(lambda: run(lat_chunks))
    pl.when(i >= n_lat // TQ)(lambda: run(ctx_chunks))


def _mla_attention(q, k, v, n_lat):
    b, t, _ = q.shape
    return pl.pallas_call(
        functools.partial(_mla_kernel, n_lat=n_lat, t_all=t),
        out_shape=jax.ShapeDtypeStruct((b, t, 256), BF16),
        grid=(b, 2, t // TQ),
        in_specs=[pl.BlockSpec((1, TQ, 256), lambda bi, p, i: (bi, i, p)),
                  pl.BlockSpec((1, t, 256), lambda bi, p, i: (bi, 0, p)),
                  pl.BlockSpec((1, t, LANES), lambda bi, p, i: (bi, 0, p))],
        out_specs=pl.BlockSpec((1, TQ, LANES), lambda bi, p, i: (bi, i, p)),
        compiler_params=_cparams(("parallel", "parallel", "arbitrary")),
        name="mla_attention",
    )(q, k, v)


def _diff_kernel(lam_ref, q_ref, k_ref, v_ref, g_ref, o_ref, *, n_lat, t_all, lam_init):
    i = pl.program_id(2)
    lane = lax.broadcasted_iota(jnp.int32, (TQ, LANES), 1)
    lat_chunks, ctx_chunks = _key_chunks(n_lat, t_all)

    def run(chunks):
        q = q_ref[0]
        zero = jnp.zeros_like(q)
        lhs = jnp.concatenate([jnp.where(lane // D_HEAD_DIM == r, q, zero) for r in range(4)], axis=0)
        o = _online_softmax(lhs, k_ref, v_ref, slice(0, LANES), chunks)
        lam = lam_ref[0]
        o0 = o[0:TQ] - lam * o[TQ:2 * TQ]
        o1 = o[2 * TQ:3 * TQ] - lam * o[3 * TQ:4 * TQ]
        o = jnp.where(lane < 64, o0, o1)
        o_ref[0] = (_head_rms(o, g_ref[...], LANES) * (1.0 - lam_init)).astype(o_ref.dtype)

    pl.when(i < n_lat // TQ)(lambda: run(lat_chunks))
    pl.when(i >= n_lat // TQ)(lambda: run(ctx_chunks))


def _diff_attention(lam, q, k, v, gain, n_lat, lam_init):
    b, t, _ = q.shape
    return pl.pallas_call(
        functools.partial(_diff_kernel, n_lat=n_lat, t_all=t, lam_init=lam_init),
        out_shape=jax.ShapeDtypeStruct((b, t, 256), BF16),
        grid=(b, 2, t // TQ),
        in_specs=[pl.BlockSpec(memory_space=pltpu.SMEM),
                  pl.BlockSpec((1, TQ, LANES), lambda bi, p, i: (bi, i, p)),
                  pl.BlockSpec((1, t, LANES), lambda bi, p, i: (bi, 0, p)),
                  pl.BlockSpec((1, t, LANES), lambda bi, p, i: (bi, 0, p)),
                  _const_spec((1, LANES))],
        out_specs=pl.BlockSpec((1, TQ, LANES), lambda bi, p, i: (bi, i, p)),
        compiler_params=_cparams(("parallel", "parallel", "arbitrary")),
        name="diff_attention",
    )(lam, q, k, v, gain)


def _merge_kernel(x_ref, mod_ref, n1_ref, n2_ref, a_ref, of_ref, ob_ref, bg_ref, hg_ref, c_ref, d_ref,
                  wg_ref, wb_ref, wo_ref, wr_ref, br_ref, xo_ref, h2_ref, rw_ref):
    d = D_MODEL
    x = x_ref[0]
    m = mod_ref[0, 0]
    h = (_rms(x, n1_ref[...]) * (1.0 + m[1:2]) + m[0:1]).astype(BF16)
    g = bg_ref[0].astype(F32)
    b_out = _head_rms(of_ref[0] + ob_ref[0], hg_ref[...], 256) * (g * _sigmoid(g))
    branches = (a_ref[0], b_out.astype(BF16), c_ref[0], d_ref[0])
    y = jnp.zeros((TM, d), F32)
    for i, br in enumerate(branches):
        gate = _sigmoid(_dot(h, wg_ref[:, d * i:d * (i + 1)]))
        y = y + gate * _dot(br, wb_ref[i])
    x = x + m[2:3] * _dot(y.astype(BF16), wo_ref[...])
    xo_ref[0] = x
    h2 = _rms(x, n2_ref[...]) * (1.0 + m[4:5]) + m[3:4]
    h2b = h2.astype(BF16)
    h2_ref[0] = h2b

    h2lo = (h2 - h2b.astype(F32)).astype(BF16)
    r = _dot(jnp.concatenate([h2b, h2lo], axis=0), wr_ref[...])
    logits = r[:TM, :LANES] + r[:TM, LANES:] + r[TM:, :LANES] + br_ref[...]
    lane = lax.broadcasted_iota(jnp.int32, (TM, LANES), 1)
    neg = -jnp.inf
    is_g = (lane >= N_EXPERTS) & (lane < N_EXPERTS + N_GROUPS)
    lg = jnp.where(is_g, logits, neg)
    mg = jnp.max(lg, axis=1, keepdims=True)
    g_val = 1.0 / jnp.sum(jnp.exp(lg - mg), axis=1, keepdims=True)
    g_idx = jnp.min(jnp.where(lg == mg, lane, 4 * LANES), axis=1, keepdims=True) - N_EXPERTS
    in_group = (lane >= g_idx * EXPERTS_PER_GROUP) & (lane < (g_idx + 1) * EXPERTS_PER_GROUP)
    le = jnp.where(in_group, logits, neg)
    m1 = jnp.max(le, axis=1, keepdims=True)
    e1 = jnp.min(jnp.where(le == m1, lane, 4 * LANES), axis=1, keepdims=True)
    le2 = jnp.where(lane == e1, neg, le)
    m2 = jnp.max(le2, axis=1, keepdims=True)
    e2 = jnp.min(jnp.where(le2 == m2, lane, 4 * LANES), axis=1, keepdims=True)
    r2 = jnp.exp(m2 - m1)
    v1 = g_val / (1.0 + r2)
    rw_ref[0] = jnp.where(lane == e1, v1, jnp.where(lane == e2, v1 * r2, 0.0))


def _merge(x, mod, n1, n2, a_o, o_f, o_b, bg, hg, c_o, d_o, wg, wb, wo, wr, br, n_lat_tiles):
    b, t, d = x.shape
    tile = lambda w: pl.BlockSpec((1, TM, w), lambda bi, i: (bi, i, 0))
    return pl.pallas_call(
        _merge_kernel,
        out_shape=[jax.ShapeDtypeStruct((b, t, d), F32), jax.ShapeDtypeStruct((b, t, d), BF16),
                   jax.ShapeDtypeStruct((b, t, LANES), F32)],
        grid=(b, t // TM),
        in_specs=[tile(d), pl.BlockSpec((1, 1, 8, d), lambda bi, i: (bi, i // n_lat_tiles, 0, 0)),
                  _const_spec((1, d)), _const_spec((1, d)),
                  tile(256), tile(256), tile(256), tile(256), _const_spec((1, 256)), tile(256), tile(256),
                  _const_spec((d, 4 * d)), _const_spec((4, 256, d)), _const_spec((d, d)),
                  _const_spec((d, 256)), _const_spec((1, LANES))],
        out_specs=[tile(d), tile(d), tile(LANES)],
        compiler_params=_cparams(("parallel", "parallel")),
        name="merge",
    )(x, mod, n1, n2, a_o, o_f, o_b, bg, hg, c_o, d_o, wg, wb, wo, wr, br)


def _moe_kernel(h_ref, rw_ref, wgu_ref, wd_ref, o_ref):
    e = pl.program_id(1)

    @pl.when(e == 0)
    def _():
        o_ref[...] = jnp.zeros_like(o_ref)

    rw = rw_ref[...]
    lane = lax.broadcasted_iota(jnp.int32, rw.shape, 1)
    w_col = jnp.sum(jnp.where(lane == e, rw, 0.0), axis=1, keepdims=True)
    gu = _dot(h_ref[...], wgu_ref[0])
    gt, up = gu[:, :EXPERT_HIDDEN], gu[:, EXPERT_HIDDEN:]
    he = (gt * _sigmoid(gt) * up * w_col).astype(BF16)
    o_ref[...] += _dot(he, wd_ref[0])


def _moe_tile(n):
    best = 16
    for cand in range(16, 1100, 16):
        if n % cand == 0:
            best = cand
    return best


def _moe(h2, rw, wgu, wd):
    n, d = h2.shape
    tmo = _moe_tile(n)
    return pl.pallas_call(
        _moe_kernel,
        out_shape=jax.ShapeDtypeStruct((n, d), F32),
        grid=(n // tmo, N_EXPERTS),
        in_specs=[pl.BlockSpec((tmo, d), lambda ti, e: (ti, 0)),
                  pl.BlockSpec((tmo, LANES), lambda ti, e: (ti, 0)),
                  pl.BlockSpec((1, d, 2 * EXPERT_HIDDEN), lambda ti, e: (e, 0, 0)),
                  pl.BlockSpec((1, EXPERT_HIDDEN, d), lambda ti, e: (e, 0, 0))],
        out_specs=pl.BlockSpec((tmo, d), lambda ti, e: (ti, 0)),
        compiler_params=_cparams(("parallel", "arbitrary")),
        name="moe",
    )(h2, rw, wgu, wd)


def _final_kernel(x_ref, f_ref, mod_ref, g_ref, o_ref):
    x = x_ref[0] + mod_ref[0, 0, 5:6, :] * f_ref[0]
    o_ref[0] = _rms(x, g_ref[...])


def _final(x, f, mod, gain, n_lat):
    b, t, d = x.shape
    tile = pl.BlockSpec((1, TM, d), lambda bi, i: (bi, i, 0))
    return pl.pallas_call(
        _final_kernel,
        out_shape=jax.ShapeDtypeStruct((b, n_lat, d), F32),
        grid=(b, n_lat // TM),
        in_specs=[tile, tile, pl.BlockSpec((1, 1, 8, d), lambda bi, i: (bi, 0, 0, 0)), _const_spec((1, d))],
        out_specs=tile,
        compiler_params=_cparams(("parallel", "parallel")),
        name="final_norm",
    )(x, f, mod, gain)


def _rope_tables(n_lat, t_all, lane_rope, lane_off, dr):
    half, quarter = dr // 2, dr // 4
    inv_freq = 1.0 / (ROPE_BASE ** (jnp.arange(quarter, dtype=F32) / quarter))
    off = np.asarray(lane_off)
    use_col = off >= half
    j = (off % half) % quarter
    first = (off % half) < quarter
    tok = jnp.arange(n_lat, dtype=jnp.int32)
    row = (tok // GRID_W).astype(F32)
    col = (tok % GRID_W).astype(F32)
    pos = jnp.where(jnp.asarray(use_col)[None, :], col[:, None], row[:, None])
    ang = pos * inv_freq[jnp.asarray(j)][None, :]
    rope = jnp.asarray(lane_rope)[None, :]
    cos = jnp.where(rope, jnp.cos(ang), 1.0)
    sin = jnp.where(rope, jnp.sin(ang) * jnp.where(jnp.asarray(first), -1.0, 1.0)[None, :], 0.0)
    pad = t_all - n_lat
    w = off.shape[0]
    cos = jnp.concatenate([cos, jnp.ones((pad, w), F32)], axis=0)
    sin = jnp.concatenate([sin, jnp.zeros((pad, w), F32)], axis=0)
    return cos, sin


def _all_rope_tables(n_lat, t_all):
    la = np.arange(256)
    ta = _rope_tables(n_lat, t_all, np.ones(256, bool), la % 64, 64)
    lc = np.arange(512) % 128
    tc = _rope_tables(n_lat, t_all, (lc >= 64) & (lc < 96), np.clip(lc - 64, 0, 31), 32)
    td = _rope_tables(n_lat, t_all, np.ones(256, bool), la % 32, 32)
    return (*ta, *tc, *td)


def _permute_w_in(w):
    s = np.cumsum([0, 256, 128, 128, 256, 256, 256, 256, 256, 256, 128, 32, 256, 256, 256])
    seg = lambda i: w[:, s[i]:s[i + 1]]
    dup = lambda a: jnp.concatenate([a[:, 0:64], a[:, 0:64], a[:, 64:128], a[:, 64:128]], axis=1)
    z64 = jnp.zeros((w.shape[0], 64), w.dtype)
    z32 = jnp.zeros((w.shape[0], 32), w.dtype)
    kr = jnp.concatenate([z64, seg(10), z32] * C_HEADS, axis=1)
    cols = [seg(0), dup(seg(1)), dup(seg(2)),
            seg(3), seg(4), seg(5), seg(6), seg(7),
            seg(8), seg(9), kr,
            seg(11), seg(12), seg(13)]
    return jnp.concatenate(cols, axis=1).astype(BF16)


def _permute_mla(w_uq, w_ukv):
    z32 = jnp.zeros((w_uq.shape[0], 32), w_uq.dtype)
    qd = C_NOPE + C_ROPE
    uq = jnp.concatenate([a for h in range(C_HEADS) for a in (w_uq[:, qd * h:qd * (h + 1)], z32)], axis=1)
    z64 = jnp.zeros((w_ukv.shape[0], 64), w_ukv.dtype)
    kd = C_NOPE + C_V
    uk = jnp.concatenate([a for h in range(C_HEADS) for a in (w_ukv[:, kd * h:kd * h + C_NOPE], z64)], axis=1)
    uv = jnp.concatenate([w_ukv[:, kd * h + C_NOPE:kd * (h + 1)] for h in range(C_HEADS)], axis=1)
    return uq.astype(BF16), jnp.concatenate([uk, uv], axis=1).astype(BF16)


def kernel(x, c, ctx, c_ctx, w_mod, b_mod, norm1, norm2, w_in, w_gate, w_branch, w_out, attn_sink, hgrn_lb_logits, hgrn_norm, mla_q_norm, mla_kv_norm, mla_w_uq, mla_w_ukv, diff_lambda, diff_subln, w_router_group, b_router_group, w_router_expert, b_router_expert, w_expert_gate, w_expert_up, w_expert_down, final_norm):
    b, n_lat, d = x.shape
    n_ctx = ctx.shape[1]
    depth = w_mod.shape[0]
    t_all = n_lat + n_ctx
    assert d == D_MODEL and n_lat % TM == 0 and n_ctx % TM == 0 and n_lat % GRID_W == 0
    assert n_lat >= 3 * A_WINDOW and b <= 15
    n_lat_tiles = n_lat // TM

    xs = jnp.concatenate([x, ctx], axis=1)
    c_all = jnp.concatenate([c, c_ctx[None], jnp.zeros((15 - b, d), F32)], axis=0)
    mod = _modulation(c_all, w_mod, b_mod).reshape(depth, 16, 6, d)
    mod = jnp.pad(mod, ((0, 0), (0, 0), (0, 2), (0, 0)))
    mod = jnp.stack([mod[:, :b], jnp.broadcast_to(mod[:, b:b + 1], (depth, b, 8, d))], axis=2)

    sm = jax.nn.softmax(hgrn_lb_logits.astype(F32), axis=0)
    lower_bounds = jnp.cumsum(sm, axis=0) - sm[0]
    tabs = _all_rope_tables(n_lat, t_all)

    f_prev, mod_prev = None, None
    for li in range(depth):
        lam_init = 0.8 - 0.6 * math.exp(-0.3 * li)
        lp = diff_lambda[li].astype(F32)
        lam = (jnp.exp(jnp.sum(lp[0] * lp[1])) - jnp.exp(jnp.sum(lp[2] * lp[3])) + lam_init).reshape(1)
        w_perm = _permute_w_in(w_in[li])
        wuq, wukv = _permute_mla(mla_w_uq[li], mla_w_ukv[li])
        xs, z = _inproj(xs, f_prev, mod_prev, mod[li], norm1[li][None], w_perm,
                        mla_q_norm[li][None], mla_kv_norm[li][None], wuq, wukv, tabs, n_lat_tiles)
        qa, ka, va, bq, bff, bfb, bv, bg, cq, ck, cv, dq, dk, dv = z
        a_o = _win_attention(attn_sink[li].astype(F32), qa, ka, va, n_lat)
        o_f, o_b = _hgrn(lower_bounds[li], bq, bff, bfb, bv, n_lat)
        c_o = _mla_attention(cq, ck, cv, n_lat)
        d_o = _diff_attention(lam, dq, dk, dv, jnp.tile(diff_subln[li], 2)[None], n_lat, lam_init)
        wr = jnp.concatenate([w_router_expert[li], w_router_group[li],
                              jnp.zeros((d, LANES - N_EXPERTS - N_GROUPS), F32)], axis=1)
        wr_hi = wr.astype(BF16)
        wr_lo = (wr - wr_hi.astype(F32)).astype(BF16)
        br = jnp.concatenate([b_router_expert[li], b_router_group[li],
                              jnp.zeros((LANES - N_EXPERTS - N_GROUPS,), F32)])[None]
        xs, h2, rw = _merge(xs, mod[li], norm1[li][None], norm2[li][None], a_o, o_f, o_b, bg,
                            jnp.tile(hgrn_norm[li], B_HEADS)[None], c_o, d_o,
                            w_gate[li].astype(BF16), w_branch[li].astype(BF16), w_out[li].astype(BF16),
                            jnp.concatenate([wr_hi, wr_lo], axis=1), br, n_lat_tiles)
        wgu = jnp.concatenate([w_expert_gate[li], w_expert_up[li]], axis=2).astype(BF16)
        f = _moe(h2.reshape(b * t_all, d), rw.reshape(b * t_all, LANES), wgu, w_expert_down[li].astype(BF16))
        f_prev, mod_prev = f.reshape(b, t_all, d), mod[li]
    return _final(xs, f_prev, mod_prev, final_norm[None], n_lat)
```

```python
import functools
import math

import numpy as np
import jax
import jax.numpy as jnp
from jax import lax
from jax.experimental import pallas as pl
from jax.experimental.pallas import tpu as pltpu

D_MODEL = 1024
GRID_W = 64
ROPE_BASE = 10000.0
NORM_EPS = 1e-6
MASK_VALUE = -1e30
GATE_FLOOR = 1e-30

A_HEADS, A_KV_HEADS, A_HEAD_DIM, A_WINDOW = 4, 2, 64, 128
B_HEADS, B_KEY_DIM, B_VAL_DIM = 4, 64, 64
C_HEADS, C_Q_LORA, C_KV_LORA, C_NOPE, C_ROPE, C_V = 4, 256, 128, 64, 32, 64
D_HEADS, D_HEAD_DIM = 4, 32
N_GROUPS, EXPERTS_PER_GROUP, EXPERT_HIDDEN = 4, 8, 256
N_EXPERTS = N_GROUPS * EXPERTS_PER_GROUP

TM = 256
TQ = 256
KV_CHUNK = 1024
MOE_TILE = 256
HG_CHUNK = 64
HG_LEVELS = 6
LANES = 128
VMEM_LIMIT = 52 * 1024 * 1024

OFF_A, OFF_B, OFF_C, OFF_D, N_PERM = 0, 768, 2048, 2944, 3712

F32 = jnp.float32
BF16 = jnp.bfloat16


def _dot(a, b):
    return jnp.dot(a, b, preferred_element_type=F32)


def _dot_nt(a, b):
    return lax.dot_general(a, b, (((1,), (1,)), ((), ())), preferred_element_type=F32)


def _dot_tn(a, b):
    return lax.dot_general(a, b, (((0,), (0,)), ((), ())), preferred_element_type=F32)


def _dot_hi(a, b):
    return jnp.dot(a, b, preferred_element_type=F32, precision=lax.Precision.HIGHEST)


def _sigmoid(x):
    return 1.0 / (1.0 + jnp.exp(-x))


def _rms(x, gain):
    return x * lax.rsqrt(jnp.mean(x * x, axis=-1, keepdims=True) + NORM_EPS) * gain


def _head_rms(o, gain, width):
    r = lax.broadcasted_iota(jnp.int32, (width, width), 0) // 64
    c = lax.broadcasted_iota(jnp.int32, (width, width), 1) // 64
    ones = jnp.where(r == c, 1.0 / 64.0, 0.0).astype(F32)
    ms = _dot_hi(o * o, ones)
    return o * lax.rsqrt(ms + NORM_EPS) * gain


def _rope(x, cos, sin, quarter):
    w = x.shape[-1]
    lane = lax.broadcasted_iota(jnp.int32, x.shape, 1)
    first = (lane % (2 * quarter)) < quarter
    sw = jnp.where(first, pltpu.roll(x, w - quarter, 1), pltpu.roll(x, quarter, 1))
    return x * cos + sw * sin


def _cparams(sem):
    return pltpu.CompilerParams(dimension_semantics=sem, vmem_limit_bytes=VMEM_LIMIT)


def _const_spec(shape):
    n = len(shape)
    return pl.BlockSpec(shape, lambda *_: (0,) * n)


def _mod_kernel(c_ref, w_ref, b_ref, o_ref):
    c = c_ref[...]
    o_ref[0] = _dot_hi(c * _sigmoid(c), w_ref[0]) + b_ref[0]


def _modulation(c_all, w_mod, b_mod):
    depth, d, n = w_mod.shape
    nb = 1536
    return pl.pallas_call(
        _mod_kernel,
        out_shape=jax.ShapeDtypeStruct((depth, 16, n), F32),
        grid=(depth, n // nb),
        in_specs=[pl.BlockSpec((16, d), lambda l, j: (0, 0)),
                  pl.BlockSpec((1, d, nb), lambda l, j: (l, 0, j)),
                  pl.BlockSpec((1, 1, nb), lambda l, j: (l, 0, j))],
        out_specs=pl.BlockSpec((1, 16, nb), lambda l, j: (l, 0, j)),
        compiler_params=_cparams(("arbitrary", "arbitrary")),
        name="modulation",
    )(c_all, w_mod, b_mod.reshape(depth, 1, n))


def _inproj_kernel(*refs, has_prev):
    if has_prev:
        x_ref, f_ref, modp_ref = refs[:3]
        refs = refs[3:]
    else:
        x_ref = refs[0]
        refs = refs[1:]
    (mod_ref, n1_ref, w_ref, qn_ref, kvn_ref, wuq_ref, wukv_ref,
     cosa_ref, sina_ref, cosc_ref, sinc_ref, cosd_ref, sind_ref) = refs[:13]
    outs = refs[13:]
    if has_prev:
        xo_ref = outs[0]
        outs = outs[1:]
    (qa_ref, ka_ref, va_ref, bq_ref, bff_ref, bfb_ref, bv_ref, bg_ref,
     cq_ref, ck_ref, cv_ref, dq_ref, dk_ref, dv_ref) = outs

    x = x_ref[0]
    if has_prev:
        x = x + modp_ref[0, 0, 5:6, :] * f_ref[0]
        xo_ref[0] = x
    m = mod_ref[0, 0]
    h = (_rms(x, n1_ref[...]) * (1.0 + m[1:2]) + m[0:1]).astype(BF16)

    z = _dot(h, w_ref[:, OFF_A:OFF_A + 768])
    cosa, sina = cosa_ref[...], sina_ref[...]
    qa_ref[0] = (_rope(z[:, 0:256], cosa, sina, 16) * (A_HEAD_DIM ** -0.5)).astype(BF16)
    ka_ref[0] = _rope(z[:, 256:512], cosa, sina, 16).astype(BF16)
    va_ref[0] = z[:, 512:768].astype(BF16)

    z = _dot(h, w_ref[:, OFF_B:OFF_B + 1280])
    bq_ref[0] = z[:, 0:256].astype(BF16)
    bff_ref[0] = z[:, 256:512]
    bfb_ref[0] = z[:, 512:768]
    bv_ref[0] = z[:, 768:1024].astype(BF16)
    bg_ref[0] = z[:, 1024:1280].astype(BF16)

    z = _dot(h, w_ref[:, OFF_C:OFF_C + 896])
    cosc, sinc = cosc_ref[...], sinc_ref[...]
    cq = _rms(z[:, 0:256], qn_ref[...]).astype(BF16)
    q = _rope(_dot(cq, wuq_ref[...]), cosc, sinc, 8)
    cq_ref[0] = (q * ((C_NOPE + C_ROPE) ** -0.5)).astype(BF16)
    ckv = _rms(z[:, 256:384], kvn_ref[...]).astype(BF16)
    kv = _dot(ckv, wukv_ref[...])
    ck_ref[0] = (kv[:, 0:512] + _rope(z[:, 384:896], cosc, sinc, 8)).astype(BF16)
    cv_ref[0] = kv[:, 512:768].astype(BF16)

    z = _dot(h, w_ref[:, OFF_D:OFF_D + 768])
    cosd, sind = cosd_ref[...], sind_ref[...]
    dq_ref[0] = (_rope(z[:, 0:256], cosd, sind, 8) * (D_HEAD_DIM ** -0.5)).astype(BF16)
    dk_ref[0] = _rope(z[:, 256:512], cosd, sind, 8).astype(BF16)
    dv_ref[0] = z[:, 512:768].astype(BF16)


def _inproj(x, f_prev, mod_prev, mod, n1, w_perm, qn, kvn, wuq, wukv, tabs, n_lat_tiles):
    b, t, d = x.shape
    nt = t // TM
    has_prev = f_prev is not None
    tile = lambda w: pl.BlockSpec((1, TM, w), lambda bi, i: (bi, i, 0))
    modspec = pl.BlockSpec((1, 1, 8, d), lambda bi, i: (bi, i // n_lat_tiles, 0, 0))
    tab = lambda w: pl.BlockSpec((TM, w), lambda bi, i: (i, 0))
    in_specs, args = [tile(d)], [x]
    if has_prev:
        in_specs += [tile(d), modspec]
        args += [f_prev, mod_prev]
    in_specs += [modspec, _const_spec((1, d)), _const_spec((d, N_PERM)), _const_spec((1, 256)),
                 _const_spec((1, 128)), _const_spec((256, 512)), _const_spec((128, 768)),
                 tab(256), tab(256), tab(512), tab(512), tab(256), tab(256)]
    args += [mod, n1, w_perm, qn, kvn, wuq, wukv, *tabs]
    widths = [(256, BF16), (256, BF16), (256, BF16),
              (256, BF16), (256, F32), (256, F32), (256, BF16), (256, BF16),
              (512, BF16), (512, BF16), (256, BF16),
              (256, BF16), (256, BF16), (256, BF16)]
    out_shape = [jax.ShapeDtypeStruct((b, t, w), dt) for w, dt in widths]
    out_specs = [tile(w) for w, _ in widths]
    if has_prev:
        out_shape = [jax.ShapeDtypeStruct((b, t, d), F32)] + out_shape
        out_specs = [tile(d)] + out_specs
    res = pl.pallas_call(
        functools.partial(_inproj_kernel, has_prev=has_prev),
        out_shape=out_shape, grid=(b, nt), in_specs=in_specs, out_specs=out_specs,
        compiler_params=_cparams(("parallel", "parallel")),
        name="inproj",
    )(*args)
    if has_prev:
        return res[0], res[1:]
    return x, res


def _win_kernel(sink_ref, q_ref, k_ref, v_ref, o_ref, *, n_lat, t_all):
    w = A_WINDOW
    n = pl.program_id(1)
    nb_lat = n_lat // w
    start = pl.multiple_of(jnp.clip((n - 1) * w, 0, n_lat - 3 * w), w)
    q = q_ref[0]
    lane = lax.broadcasted_iota(jnp.int32, (w, LANES), 1)
    row = lax.broadcasted_iota(jnp.int32, (2 * w, 3 * w), 0)
    col = lax.broadcasted_iota(jnp.int32, (2 * w, 3 * w), 1)
    rel = (start + col) - (n * w + row % w)
    valid = (jnp.abs(rel) <= w) & (n < nb_lat)
    row1 = lax.broadcasted_iota(jnp.int32, (2 * w, 1), 0)
    for j in range(A_KV_HEADS):
        sl = slice(LANES * j, LANES * (j + 1))
        qp = q[:, sl]
        zero = jnp.zeros_like(qp)
        lhs = jnp.concatenate([jnp.where(lane < 64, qp, zero), jnp.where(lane >= 64, qp, zero)], axis=0)
        sb = _dot_nt(lhs, k_ref[0, pl.ds(start, 3 * w), sl])
        sc = _dot_nt(lhs, k_ref[0, n_lat:t_all, sl])
        sb = jnp.where(valid, sb, MASK_VALUE)
        sink = jnp.where(row1 < w, sink_ref[2 * j], sink_ref[2 * j + 1])
        m = jnp.maximum(jnp.maximum(jnp.max(sb, axis=1, keepdims=True), jnp.max(sc, axis=1, keepdims=True)), sink)
        pb = jnp.exp(sb - m)
        pc = jnp.exp(sc - m)
        den = jnp.sum(pb, axis=1, keepdims=True) + jnp.sum(pc, axis=1, keepdims=True) + jnp.exp(sink - m)
        o = (_dot(pb.astype(BF16), v_ref[0, pl.ds(start, 3 * w), sl])
             + _dot(pc.astype(BF16), v_ref[0, n_lat:t_all, sl])) / den
        o_ref[0, :, sl] = jnp.where(lane < 64, o[:w], o[w:]).astype(o_ref.dtype)


def _win_attention(sink, q, k, v, n_lat):
    b, t, _ = q.shape
    w = A_WINDOW
    return pl.pallas_call(
        functools.partial(_win_kernel, n_lat=n_lat, t_all=t),
        out_shape=jax.ShapeDtypeStruct((b, t, 256), BF16),
        grid=(b, t // w),
        in_specs=[pl.BlockSpec(memory_space=pltpu.SMEM),
                  pl.BlockSpec((1, w, 256), lambda bi, i: (bi, i, 0)),
                  pl.BlockSpec((1, t, 256), lambda bi, i: (bi, 0, 0)),
                  pl.BlockSpec((1, t, 256), lambda bi, i: (bi, 0, 0))],
        out_specs=pl.BlockSpec((1, w, 256), lambda bi, i: (bi, i, 0)),
        compiler_params=_cparams(("parallel", "arbitrary")),
        name="win_attention",
    )(sink, q, k, v)


def _hgrn_level_matrix(rev):
    c = HG_CHUNK
    m = np.zeros((8, c, c), np.float32)
    for lvl in range(HG_LEVELS):
        size = c >> lvl
        for t in range(c):
            mid = (t // size) * size + size // 2
            upper = t >= mid
            if not rev:
                rng = range(mid, t + 1) if upper else range(t + 1, mid)
            else:
                rng = range(mid, t) if upper else range(t, mid)
            m[lvl, t, list(rng)] = 1.0
    for t in range(c):
        if not rev:
            m[6, t, :t + 1] = 1.0
            m[7, t, t + 1:] = 1.0
        else:
            m[6, t, t:] = 1.0
            m[7, t, :t] = 1.0
    return m.reshape(8 * c, c)


def _hgrn_chunk(q, k, logf, v, st, mlev, rev):
    c = HG_CHUNK
    hi = logf.astype(BF16)
    lo = (logf - hi.astype(F32)).astype(BF16)
    e2 = _dot(mlev, jnp.concatenate([hi, lo], axis=1))
    wgt = jnp.exp(e2[:, :256] + e2[:, 256:])
    t = lax.broadcasted_iota(jnp.int32, (c, 256), 0)
    s_col = lax.broadcasted_iota(jnp.int32, (c, 256), 1) % c
    r_bd = lax.broadcasted_iota(jnp.int32, (256, 256), 0) // 64
    c_bd = lax.broadcasted_iota(jnp.int32, (256, 256), 1) // 64
    bd = r_bd == c_bd

    def block_diag(a):
        return jnp.where(bd, jnp.concatenate([a] * 4, axis=0), 0.0).astype(BF16)

    scores = jnp.where(t == s_col, _dot_nt(q.astype(BF16), block_diag(k)), 0.0)
    for lvl in range(HG_LEVELS):
        wl = wgt[c * lvl:c * (lvl + 1)]
        q_side = ((t >> (HG_LEVELS - 1 - lvl)) & 1) == (0 if rev else 1)
        ql = jnp.where(q_side, q * wl, 0.0).astype(BF16)
        kl = jnp.where(q_side, 0.0, k * wl)
        s = _dot_nt(ql, block_diag(kl))
        if lvl > 0:
            sh = HG_LEVELS - lvl
            s = jnp.where((t >> sh) == (s_col >> sh), s, 0.0)
        scores = scores + s
    q_in = (q * wgt[6 * c:7 * c]).astype(BF16)
    k_end = (k * wgt[7 * c:8 * c]).astype(BF16)
    o = _dot(scores.astype(BF16), block_diag(v)) + _dot_nt(q_in, st.astype(BF16))
    edge = 6 * c if rev else 7 * c - 1
    total = wgt[edge:edge + 1]
    st_new = st * total + jnp.where(bd, _dot_tn(v.astype(BF16), k_end), 0.0)
    return o, st_new


def _hgrn_kernel(mf_ref, mb_ref, lb_ref, qf_ref, zf_ref, vf_ref, qb_ref, zb_ref, vb_ref,
                 of_ref, ob_ref, stf_ref, stb_ref, lff_ref, kff_ref, lfb_ref, kfb_ref):
    c = HG_CHUNK
    nc = TM // c

    @pl.when(pl.program_id(1) == 0)
    def _():
        stf_ref[...] = jnp.zeros_like(stf_ref)
        stb_ref[...] = jnp.zeros_like(stb_ref)

    def gates(z, lb, lf_ref, k_ref):
        f = lb + (1.0 - lb) * _sigmoid(z)
        lf_ref[...] = jnp.log(jnp.maximum(f, GATE_FLOOR))
        k_ref[...] = (1.0 - lb) * _sigmoid(-z)

    gates(zf_ref[0], lb_ref[0:1], lff_ref, kff_ref)
    gates(zb_ref[0], lb_ref[1:2], lfb_ref, kfb_ref)

    def body(ci, carry):
        r0 = pl.multiple_of(ci * c, c)
        r1 = pl.multiple_of((nc - 1 - ci) * c, c)
        o, st = _hgrn_chunk(qf_ref[0, pl.ds(r0, c), :].astype(F32), kff_ref[pl.ds(r0, c), :],
                            lff_ref[pl.ds(r0, c), :], vf_ref[0, pl.ds(r0, c), :].astype(F32),
                            stf_ref[...], mf_ref[...], False)
        of_ref[0, pl.ds(r0, c), :] = o
        stf_ref[...] = st
        o, st = _hgrn_chunk(qb_ref[0, pl.ds(r1, c), :].astype(F32), kfb_ref[pl.ds(r1, c), :],
                            lfb_ref[pl.ds(r1, c), :], vb_ref[0, pl.ds(r1, c), :].astype(F32),
                            stb_ref[...], mb_ref[...], True)
        ob_ref[0, pl.ds(r1, c), :] = o
        stb_ref[...] = st
        return carry

    lax.fori_loop(0, nc, body, 0)


def _hgrn(lb, q, zff, zfb, v, n_lat):
    b, t, _ = q.shape
    nlb = n_lat // TM
    ncb = (t - n_lat) // TM
    fwd = lambda bi, i: (bi, jnp.where(i < ncb, nlb + i, i - ncb), 0)
    bwd = lambda bi, i: (bi, jnp.where(i < ncb, nlb + ncb - 1 - i, nlb + ncb - 1 - i), 0)
    blk = lambda im: pl.BlockSpec((1, TM, 256), im)
    mf = jnp.asarray(_hgrn_level_matrix(False), BF16)
    mb = jnp.asarray(_hgrn_level_matrix(True), BF16)
    return pl.pallas_call(
        _hgrn_kernel,
        out_shape=[jax.ShapeDtypeStruct((b, t, 256), F32)] * 2,
        grid=(b, t // TM),
        in_specs=[_const_spec((8 * HG_CHUNK, HG_CHUNK)), _const_spec((8 * HG_CHUNK, HG_CHUNK)),
                  _const_spec((2, 256)),
                  blk(fwd), blk(fwd), blk(fwd), blk(bwd), blk(bwd), blk(bwd)],
        out_specs=[blk(fwd), blk(bwd)],
        scratch_shapes=[pltpu.VMEM((256, 256), F32), pltpu.VMEM((256, 256), F32),
                        pltpu.VMEM((TM, 256), F32), pltpu.VMEM((TM, 256), F32),
                        pltpu.VMEM((TM, 256), F32), pltpu.VMEM((TM, 256), F32)],
        compiler_params=_cparams(("parallel", "arbitrary")),
        name="hgrn",
    )(mf, mb, lb, q, zff, v, q, zfb, v)


def _key_chunks(n_lat, t_all):
    chunks = [(lo, min(lo + KV_CHUNK, n_lat)) for lo in range(0, n_lat, KV_CHUNK)]
    return chunks + [(n_lat, t_all)], [(n_lat, t_all)]


def _online_softmax(lhs, k_ref, v_ref, k_lanes, chunks):
    rows = lhs.shape[0]
    m = jnp.full((rows, 1), -jnp.inf, F32)
    l = jnp.zeros((rows, 1), F32)
    acc = jnp.zeros((rows, LANES), F32)
    for lo, hi in chunks:
        s = _dot_nt(lhs, k_ref[0, lo:hi, k_lanes])
        m_new = jnp.maximum(m, jnp.max(s, axis=1, keepdims=True))
        alpha = jnp.exp(m - m_new)
        p = jnp.exp(s - m_new)
        l = alpha * l + jnp.sum(p, axis=1, keepdims=True)
        acc = alpha * acc + _dot(p.astype(BF16), v_ref[0, lo:hi, :])
        m = m_new
    return acc / l


def _mla_kernel(q_ref, k_ref, v_ref, o_ref, *, n_lat, t_all):
    i = pl.program_id(2)
    lane = lax.broadcasted_iota(jnp.int32, (TQ, LANES), 1)
    lat_chunks, ctx_chunks = _key_chunks(n_lat, t_all)

    def run(chunks):
        o = [_online_softmax(q_ref[0, :, LANES * h:LANES * (h + 1)], k_ref, v_ref,
                             slice(LANES * h, LANES * (h + 1)), chunks) for h in range(2)]
        o_ref[0] = jnp.where(lane < 64, o[0], o[1]).astype(o_ref.dtype)

    pl.when(i < n_lat // TQ)(lambda: run(lat_chunks))
    pl.when(i >= n_lat // TQ)(lambda: run(ctx_chunks))


def _mla_attention(q, k, v, n_lat):
    b, t, _ = q.shape
    return pl.pallas_call(
        functools.partial(_mla_kernel, n_lat=n_lat, t_all=t),
        out_shape=jax.ShapeDtypeStruct((b, t, 256), BF16),
        grid=(b, 2, t // TQ),
        in_specs=[pl.BlockSpec((1, TQ, 256), lambda bi, p, i: (bi, i, p)),
                  pl.BlockSpec((1, t, 256), lambda bi, p, i: (bi, 0, p)),
                  pl.BlockSpec((1, t, LANES), lambda bi, p, i: (bi, 0, p))],
        out_specs=pl.BlockSpec((1, TQ, LANES), lambda bi, p, i: (bi, i, p)),
        compiler_params=_cparams(("parallel", "parallel", "arbitrary")),
        name="mla_attention",
    )(q, k, v)


def _diff_kernel(lam_ref, q_ref, k_ref, v_ref, g_ref, o_ref, *, n_lat, t_all, lam_init):
    i = pl.program_id(2)
    lane = lax.broadcasted_iota(jnp.int32, (TQ, LANES), 1)
    lat_chunks, ctx_chunks = _key_chunks(n_lat, t_all)

    def run(chunks):
        q = q_ref[0]
        zero = jnp.zeros_like(q)
        lhs = jnp.concatenate([jnp.where(lane // D_HEAD_DIM == r, q, zero) for r in range(4)], axis=0)
        o = _online_softmax(lhs, k_ref, v_ref, slice(0, LANES), chunks)
        lam = lam_ref[0]
        o0 = o[0:TQ] - lam * o[TQ:2 * TQ]
        o1 = o[2 * TQ:3 * TQ] - lam * o[3 * TQ:4 * TQ]
        o = jnp.where(lane < 64, o0, o1)
        o_ref[0] = (_head_rms(o, g_ref[...], LANES) * (1.0 - lam_init)).astype(o_ref.dtype)

    pl.when(i < n_lat // TQ)(lambda: run(lat_chunks))
    pl.when(i >= n_lat // TQ)(lambda: run(ctx_chunks))


def _diff_attention(lam, q, k, v, gain, n_lat, lam_init):
    b, t, _ = q.shape
    return pl.pallas_call(
        functools.partial(_diff_kernel, n_lat=n_lat, t_all=t, lam_init=lam_init),
        out_shape=jax.ShapeDtypeStruct((b, t, 256), BF16),
        grid=(b, 2, t // TQ),
        in_specs=[pl.BlockSpec(memory_space=pltpu.SMEM),
                  pl.BlockSpec((1, TQ, LANES), lambda bi, p, i: (bi, i, p)),
                  pl.BlockSpec((1, t, LANES), lambda bi, p, i: (bi, 0, p)),
                  pl.BlockSpec((1, t, LANES), lambda bi, p, i: (bi, 0, p)),
                  _const_spec((1, LANES))],
        out_specs=pl.BlockSpec((1, TQ, LANES), lambda bi, p, i: (bi, i, p)),
        compiler_params=_cparams(("parallel", "parallel", "arbitrary")),
        name="diff_attention",
    )(lam, q, k, v, gain)


def _merge_kernel(x_ref, mod_ref, n1_ref, n2_ref, a_ref, of_ref, ob_ref, bg_ref, hg_ref, c_ref, d_ref,
                  wg_ref, wb_ref, wo_ref, wr_ref, br_ref, xo_ref, h2_ref, ri_ref, rw_ref):
    d = D_MODEL
    x = x_ref[0]
    m = mod_ref[0, 0]
    h = (_rms(x, n1_ref[...]) * (1.0 + m[1:2]) + m[0:1]).astype(BF16)
    g = bg_ref[0].astype(F32)
    b_out = _head_rms(of_ref[0] + ob_ref[0], hg_ref[...], 256) * (g * _sigmoid(g))
    branches = (a_ref[0], b_out.astype(BF16), c_ref[0], d_ref[0])
    y = jnp.zeros((TM, d), F32)
    for i, br in enumerate(branches):
        gate = _sigmoid(_dot(h, wg_ref[:, d * i:d * (i + 1)]))
        y = y + gate * _dot(br, wb_ref[i])
    x = x + m[2:3] * _dot(y.astype(BF16), wo_ref[...])
    xo_ref[0] = x
    h2 = _rms(x, n2_ref[...]) * (1.0 + m[4:5]) + m[3:4]
    h2b = h2.astype(BF16)
    h2_ref[0] = h2

    h2lo = (h2 - h2b.astype(F32)).astype(BF16)
    r = _dot(jnp.concatenate([h2b, h2lo], axis=0), wr_ref[...])
    logits = r[:TM, :LANES] + r[:TM, LANES:] + r[TM:, :LANES] + br_ref[...]
    lane = lax.broadcasted_iota(jnp.int32, (TM, LANES), 1)
    neg = -jnp.inf
    is_g = (lane >= N_EXPERTS) & (lane < N_EXPERTS + N_GROUPS)
    lg = jnp.where(is_g, logits, neg)
    mg = jnp.max(lg, axis=1, keepdims=True)
    g_val = 1.0 / jnp.sum(jnp.exp(lg - mg), axis=1, keepdims=True)
    g_idx = jnp.min(jnp.where(lg == mg, lane, 4 * LANES), axis=1, keepdims=True) - N_EXPERTS
    in_group = (lane >= g_idx * EXPERTS_PER_GROUP) & (lane < (g_idx + 1) * EXPERTS_PER_GROUP)
    le = jnp.where(in_group, logits, neg)
    m1 = jnp.max(le, axis=1, keepdims=True)
    e1 = jnp.min(jnp.where(le == m1, lane, 4 * LANES), axis=1, keepdims=True)
    le2 = jnp.where(lane == e1, neg, le)
    m2 = jnp.max(le2, axis=1, keepdims=True)
    e2 = jnp.min(jnp.where(le2 == m2, lane, 4 * LANES), axis=1, keepdims=True)
    r2 = jnp.exp(m2 - m1)
    v1 = g_val / (1.0 + r2)
    ri_ref[0] = jnp.where(lane == 0, e1, jnp.where(lane == 1, e2, 0))
    rw_ref[0] = jnp.where(lane == 0, v1, jnp.where(lane == 1, v1 * r2, 0.0))


def _merge(x, mod, n1, n2, a_o, o_f, o_b, bg, hg, c_o, d_o, wg, wb, wo, wr, br, n_lat_tiles):
    b, t, d = x.shape
    tile = lambda w: pl.BlockSpec((1, TM, w), lambda bi, i: (bi, i, 0))
    return pl.pallas_call(
        _merge_kernel,
        out_shape=[jax.ShapeDtypeStruct((b, t, d), F32), jax.ShapeDtypeStruct((b, t, d), F32),
                   jax.ShapeDtypeStruct((b, t, LANES), jnp.int32), jax.ShapeDtypeStruct((b, t, LANES), F32)],
        grid=(b, t // TM),
        in_specs=[tile(d), pl.BlockSpec((1, 1, 8, d), lambda bi, i: (bi, i // n_lat_tiles, 0, 0)),
                  _const_spec((1, d)), _const_spec((1, d)),
                  tile(256), tile(256), tile(256), tile(256), _const_spec((1, 256)), tile(256), tile(256),
                  _const_spec((d, 4 * d)), _const_spec((4, 256, d)), _const_spec((d, d)),
                  _const_spec((d, 256)), _const_spec((1, LANES))],
        out_specs=[tile(d), tile(d), tile(LANES), tile(LANES)],
        compiler_params=_cparams(("parallel", "parallel")),
        name="merge",
    )(x, mod, n1, n2, a_o, o_f, o_b, bg, hg, c_o, d_o, wg, wb, wo, wr, br)


def _route_positions(ri, n_tok):
    ef = ri[:, :2].reshape(-1)
    oh = (ef[:, None] == jnp.arange(N_EXPERTS, dtype=jnp.int32)[None, :]).astype(jnp.int32)
    csum = jnp.cumsum(oh, axis=0)
    rank = jnp.sum(csum * oh, axis=1) - 1
    cnt = csum[-1]
    pcnt = ((cnt + MOE_TILE - 1) // MOE_TILE) * MOE_TILE
    end = jnp.cumsum(pcnt)
    pos = (end - pcnt)[ef] + rank
    n_tiles = 2 * n_tok // MOE_TILE + N_EXPERTS
    n_act = (end[-1] // MOE_TILE).astype(jnp.int32)
    tile_row = jnp.minimum(jnp.arange(n_tiles, dtype=jnp.int32), n_act - 1) * MOE_TILE
    tile_expert = jnp.minimum(jnp.searchsorted(end, tile_row, side="right"), N_EXPERTS - 1).astype(jnp.int32)
    return pos.reshape(n_tok // TM, 1, 2 * TM).astype(jnp.int32), tile_expert, n_act.reshape(1)


def _dispatch_kernel(pos_ref, h_ref, xs_in_ref, xs_ref, sem):
    del xs_in_ref

    def body(r, carry):
        for s in range(2):
            p = pos_ref[0, 0, 2 * r + s]
            pltpu.make_async_copy(h_ref.at[pl.ds(r, 1)], xs_ref.at[pl.ds(p, 1)], sem.at[s]).start()
        return carry

    lax.fori_loop(0, TM, body, 0)
    for s in range(2):
        pltpu.make_async_copy(h_ref, h_ref, sem.at[s]).wait()


def _dispatch(pos, h2, xs_init):
    n, d = h2.shape
    return pl.pallas_call(
        _dispatch_kernel,
        out_shape=jax.ShapeDtypeStruct(xs_init.shape, F32),
        grid=(n // TM,),
        in_specs=[pl.BlockSpec((1, 1, 2 * TM), lambda t: (t, 0, 0), memory_space=pltpu.SMEM),
                  pl.BlockSpec((TM, d), lambda t: (t, 0)),
                  pl.BlockSpec(memory_space=pl.ANY)],
        out_specs=pl.BlockSpec(memory_space=pl.ANY),
        scratch_shapes=[pltpu.SemaphoreType.DMA((2,))],
        input_output_aliases={2: 0},
        compiler_params=_cparams(("arbitrary",)),
        name="moe_dispatch",
    )(pos, h2, xs_init)


def _experts_kernel(te_ref, na_ref, x_ref, wgu_ref, wd_ref, o_ref):
    del te_ref
    t = pl.program_id(0)

    @pl.when(t < na_ref[0])
    def _():
        gu = _dot(x_ref[...].astype(BF16), wgu_ref[0])
        gt, up = gu[:, :EXPERT_HIDDEN], gu[:, EXPERT_HIDDEN:]
        o_ref[...] = _dot((gt * _sigmoid(gt) * up).astype(BF16), wd_ref[0])

    @pl.when(t >= na_ref[0])
    def _():
        o_ref[...] = jnp.zeros_like(o_ref)


def _experts(tile_expert, n_act, xs, wgu, wd):
    p, d = xs.shape
    return pl.pallas_call(
        _experts_kernel,
        out_shape=jax.ShapeDtypeStruct((p, d), F32),
        grid_spec=pltpu.PrefetchScalarGridSpec(
            num_scalar_prefetch=2, grid=(p // MOE_TILE,),
            in_specs=[pl.BlockSpec((MOE_TILE, d), lambda t, te, na: (t, 0)),
                      pl.BlockSpec((1, d, 2 * EXPERT_HIDDEN), lambda t, te, na: (te[t], 0, 0)),
                      pl.BlockSpec((1, EXPERT_HIDDEN, d), lambda t, te, na: (te[t], 0, 0))],
            out_specs=pl.BlockSpec((MOE_TILE, d), lambda t, te, na: (t, 0))),
        compiler_params=_cparams(("arbitrary",)),
        name="moe_experts",
    )(tile_expert, n_act, xs, wgu, wd)


def _combine_kernel(pos_ref, ys_ref, w_ref, o_ref, buf, sem):
    def body(r, carry):
        for s in range(2):
            p = pos_ref[0, 0, 2 * r + s]
            pltpu.make_async_copy(ys_ref.at[pl.ds(p, 1)], buf.at[s, pl.ds(r, 1)], sem.at[s]).start()
        return carry

    lax.fori_loop(0, TM, body, 0)
    for s in range(2):
        pltpu.make_async_copy(buf.at[s], buf.at[s], sem.at[s]).wait()
    w = w_ref[...]
    o_ref[...] = w[:, 0:1] * buf[0] + w[:, 1:2] * buf[1]


def _combine(pos, ys, rwt):
    n = rwt.shape[0]
    d = ys.shape[1]
    return pl.pallas_call(
        _combine_kernel,
        out_shape=jax.ShapeDtypeStruct((n, d), F32),
        grid=(n // TM,),
        in_specs=[pl.BlockSpec((1, 1, 2 * TM), lambda t: (t, 0, 0), memory_space=pltpu.SMEM),
                  pl.BlockSpec(memory_space=pl.ANY),
                  pl.BlockSpec((TM, LANES), lambda t: (t, 0))],
        out_specs=pl.BlockSpec((TM, d), lambda t: (t, 0)),
        scratch_shapes=[pltpu.VMEM((2, TM, d), F32), pltpu.SemaphoreType.DMA((2,))],
        compiler_params=_cparams(("arbitrary",)),
        name="moe_combine",
    )(pos, ys, rwt)


def _moe(h2, ri, rwt, wgu, wd, xs_init):
    n = h2.shape[0]
    pos, tile_expert, n_act = _route_positions(ri, n)
    xs = _dispatch(pos, h2, xs_init)
    ys = _experts(tile_expert, n_act, xs, wgu, wd)
    return _combine(pos, ys, rwt), xs


def _final_kernel(x_ref, f_ref, mod_ref, g_ref, o_ref):
    x = x_ref[0] + mod_ref[0, 0, 5:6, :] * f_ref[0]
    o_ref[0] = _rms(x, g_ref[...])


def _final(x, f, mod, gain, n_lat):
    b, t, d = x.shape
    tile = pl.BlockSpec((1, TM, d), lambda bi, i: (bi, i, 0))
    return pl.pallas_call(
        _final_kernel,
        out_shape=jax.ShapeDtypeStruct((b, n_lat, d), F32),
        grid=(b, n_lat // TM),
        in_specs=[tile, tile, pl.BlockSpec((1, 1, 8, d), lambda bi, i: (bi, 0, 0, 0)), _const_spec((1, d))],
        out_specs=tile,
        compiler_params=_cparams(("parallel", "parallel")),
        name="final_norm",
    )(x, f, mod, gain)


def _rope_tables(n_lat, t_all, lane_rope, lane_off, dr):
    half, quarter = dr // 2, dr // 4
    inv_freq = 1.0 / (ROPE_BASE ** (jnp.arange(quarter, dtype=F32) / quarter))
    off = np.asarray(lane_off)
    use_col = off >= half
    j = (off % half) % quarter
    first = (off % half) < quarter
    tok = jnp.arange(n_lat, dtype=jnp.int32)
    row = (tok // GRID_W).astype(F32)
    col = (tok % GRID_W).astype(F32)
    pos = jnp.where(jnp.asarray(use_col)[None, :], col[:, None], row[:, None])
    ang = pos * inv_freq[jnp.asarray(j)][None, :]
    rope = jnp.asarray(lane_rope)[None, :]
    cos = jnp.where(rope, jnp.cos(ang), 1.0)
    sin = jnp.where(rope, jnp.sin(ang) * jnp.where(jnp.asarray(first), -1.0, 1.0)[None, :], 0.0)
    pad = t_all - n_lat
    w = off.shape[0]
    cos = jnp.concatenate([cos, jnp.ones((pad, w), F32)], axis=0)
    sin = jnp.concatenate([sin, jnp.zeros((pad, w), F32)], axis=0)
    return cos, sin


def _all_rope_tables(n_lat, t_all):
    la = np.arange(256)
    ta = _rope_tables(n_lat, t_all, np.ones(256, bool), la % 64, 64)
    lc = np.arange(512) % 128
    tc = _rope_tables(n_lat, t_all, (lc >= 64) & (lc < 96), np.clip(lc - 64, 0, 31), 32)
    td = _rope_tables(n_lat, t_all, np.ones(256, bool), la % 32, 32)
    return (*ta, *tc, *td)


def _permute_w_in(w):
    s = np.cumsum([0, 256, 128, 128, 256, 256, 256, 256, 256, 256, 128, 32, 256, 256, 256])
    seg = lambda i: w[:, s[i]:s[i + 1]]
    dup = lambda a: jnp.concatenate([a[:, 0:64], a[:, 0:64], a[:, 64:128], a[:, 64:128]], axis=1)
    z64 = jnp.zeros((w.shape[0], 64), w.dtype)
    z32 = jnp.zeros((w.shape[0], 32), w.dtype)
    kr = jnp.concatenate([z64, seg(10), z32] * C_HEADS, axis=1)
    cols = [seg(0), dup(seg(1)), dup(seg(2)),
            seg(3), seg(4), seg(5), seg(6), seg(7),
            seg(8), seg(9), kr,
            seg(11), seg(12), seg(13)]
    return jnp.concatenate(cols, axis=1).astype(BF16)


def _permute_mla(w_uq, w_ukv):
    z32 = jnp.zeros((w_uq.shape[0], 32), w_uq.dtype)
    qd = C_NOPE + C_ROPE
    uq = jnp.concatenate([a for h in range(C_HEADS) for a in (w_uq[:, qd * h:qd * (h + 1)], z32)], axis=1)
    z64 = jnp.zeros((w_ukv.shape[0], 64), w_ukv.dtype)
    kd = C_NOPE + C_V
    uk = jnp.concatenate([a for h in range(C_HEADS) for a in (w_ukv[:, kd * h:kd * h + C_NOPE], z64)], axis=1)
    uv = jnp.concatenate([w_ukv[:, kd * h + C_NOPE:kd * (h + 1)] for h in range(C_HEADS)], axis=1)
    return uq.astype(BF16), jnp.concatenate([uk, uv], axis=1).astype(BF16)


def kernel(x, c, ctx, c_ctx, w_mod, b_mod, norm1, norm2, w_in, w_gate, w_branch, w_out, attn_sink, hgrn_lb_logits, hgrn_norm, mla_q_norm, mla_kv_norm, mla_w_uq, mla_w_ukv, diff_lambda, diff_subln, w_router_group, b_router_group, w_router_expert, b_router_expert, w_expert_gate, w_expert_up, w_expert_down, final_norm):
    b, n_lat, d = x.shape
    n_ctx = ctx.shape[1]
    depth = w_mod.shape[0]
    t_all = n_lat + n_ctx
    assert d == D_MODEL and n_lat % TM == 0 and n_ctx % TM == 0 and n_lat % GRID_W == 0
    assert n_lat >= 3 * A_WINDOW and b <= 15
    n_lat_tiles = n_lat // TM

    xs = jnp.concatenate([x, ctx], axis=1)
    c_all = jnp.concatenate([c, c_ctx[None], jnp.zeros((15 - b, d), F32)], axis=0)
    mod = _modulation(c_all, w_mod, b_mod).reshape(depth, 16, 6, d)
    mod = jnp.pad(mod, ((0, 0), (0, 0), (0, 2), (0, 0)))
    mod = jnp.stack([mod[:, :b], jnp.broadcast_to(mod[:, b:b + 1], (depth, b, 8, d))], axis=2)

    sm = jax.nn.softmax(hgrn_lb_logits.astype(F32), axis=0)
    lower_bounds = jnp.cumsum(sm, axis=0) - sm[0]
    tabs = _all_rope_tables(n_lat, t_all)

    n_tok = b * t_all
    moe_buf = jnp.zeros((2 * n_tok + N_EXPERTS * MOE_TILE, d), F32)
    f_prev, mod_prev = None, None
    for li in range(depth):
        lam_init = 0.8 - 0.6 * math.exp(-0.3 * li)
        lp = diff_lambda[li].astype(F32)
        lam = (jnp.exp(jnp.sum(lp[0] * lp[1])) - jnp.exp(jnp.sum(lp[2] * lp[3])) + lam_init).reshape(1)
        w_perm = _permute_w_in(w_in[li])
        wuq, wukv = _permute_mla(mla_w_uq[li], mla_w_ukv[li])
        xs, z = _inproj(xs, f_prev, mod_prev, mod[li], norm1[li][None], w_perm,
                        mla_q_norm[li][None], mla_kv_norm[li][None], wuq, wukv, tabs, n_lat_tiles)
        qa, ka, va, bq, bff, bfb, bv, bg, cq, ck, cv, dq, dk, dv = z
        a_o = _win_attention(attn_sink[li].astype(F32), qa, ka, va, n_lat)
        o_f, o_b = _hgrn(lower_bounds[li], bq, bff, bfb, bv, n_lat)
        c_o = _mla_attention(cq, ck, cv, n_lat)
        d_o = _diff_attention(lam, dq, dk, dv, jnp.tile(diff_subln[li], 2)[None], n_lat, lam_init)
        wr = jnp.concatenate([w_router_expert[li], w_router_group[li],
                              jnp.zeros((d, LANES - N_EXPERTS - N_GROUPS), F32)], axis=1)
        wr_hi = wr.astype(BF16)
        wr_lo = (wr - wr_hi.astype(F32)).astype(BF16)
        br = jnp.concatenate([b_router_expert[li], b_router_group[li],
                              jnp.zeros((LANES - N_EXPERTS - N_GROUPS,), F32)])[None]
        xs, h2, ri, rwt = _merge(xs, mod[li], norm1[li][None], norm2[li][None], a_o, o_f, o_b, bg,
                                 jnp.tile(hgrn_norm[li], B_HEADS)[None], c_o, d_o,
                                 w_gate[li].astype(BF16), w_branch[li].astype(BF16), w_out[li].astype(BF16),
                                 jnp.concatenate([wr_hi, wr_lo], axis=1), br, n_lat_tiles)
        wgu = jnp.concatenate([w_expert_gate[li], w_expert_up[li]], axis=2).astype(BF16)
        f, moe_buf = _moe(h2.reshape(n_tok, d), ri.reshape(n_tok, LANES), rwt.reshape(n_tok, LANES),
                          wgu, w_expert_down[li].astype(BF16), moe_buf)
        f_prev, mod_prev = f.reshape(b, t_all, d), mod[li]
    return _final(xs, f_prev, mod_prev, final_norm[None], n_lat)
```

```python
import functools
import math

import numpy as np
import jax
import jax.numpy as jnp
from jax import lax
from jax.experimental import pallas as pl
from jax.experimental.pallas import tpu as pltpu

D_MODEL = 1024
GRID_W = 64
ROPE_BASE = 10000.0
NORM_EPS = 1e-6
MASK_VALUE = -1e30
GATE_FLOOR = 1e-30

A_HEADS, A_KV_HEADS, A_HEAD_DIM, A_WINDOW = 4, 2, 64, 128
B_HEADS, B_KEY_DIM, B_VAL_DIM = 4, 64, 64
C_HEADS, C_Q_LORA, C_KV_LORA, C_NOPE, C_ROPE, C_V = 4, 256, 128, 64, 32, 64
D_HEADS, D_HEAD_DIM = 4, 32
N_GROUPS, EXPERTS_PER_GROUP, EXPERT_HIDDEN = 4, 8, 256
N_EXPERTS = N_GROUPS * EXPERTS_PER_GROUP

TM = 256
TQ = 256
TQ_MLA = 512
LOG2E = 1.4426950408889634
KV_CHUNK = 1024
MOE_TILE = 256
HG_CHUNK = 64
HG_LEVELS = 6
LANES = 128
VMEM_LIMIT = 52 * 1024 * 1024

OFF_A, OFF_B, OFF_C, OFF_D, N_PERM = 0, 768, 2048, 2944, 3712

F32 = jnp.float32
BF16 = jnp.bfloat16


def _dot(a, b):
    return jnp.dot(a, b, preferred_element_type=F32)


def _dot_nt(a, b):
    return lax.dot_general(a, b, (((1,), (1,)), ((), ())), preferred_element_type=F32)


def _dot_tn(a, b):
    return lax.dot_general(a, b, (((0,), (0,)), ((), ())), preferred_element_type=F32)


def _dot_hi(a, b):
    return jnp.dot(a, b, preferred_element_type=F32, precision=lax.Precision.HIGHEST)


def _sigmoid(x):
    return 1.0 / (1.0 + jnp.exp(-x))


def _rms(x, gain):
    return x * lax.rsqrt(jnp.mean(x * x, axis=-1, keepdims=True) + NORM_EPS) * gain


def _head_rms(o, gain, width):
    r = lax.broadcasted_iota(jnp.int32, (width, width), 0) // 64
    c = lax.broadcasted_iota(jnp.int32, (width, width), 1) // 64
    ones = jnp.where(r == c, 1.0 / 64.0, 0.0).astype(F32)
    ms = _dot_hi(o * o, ones)
    return o * lax.rsqrt(ms + NORM_EPS) * gain


def _rope(x, cos, sin, quarter):
    w = x.shape[-1]
    lane = lax.broadcasted_iota(jnp.int32, x.shape, 1)
    first = (lane % (2 * quarter)) < quarter
    sw = jnp.where(first, pltpu.roll(x, w - quarter, 1), pltpu.roll(x, quarter, 1))
    return x * cos + sw * sin


def _cparams(sem):
    return pltpu.CompilerParams(dimension_semantics=sem, vmem_limit_bytes=VMEM_LIMIT)


def _const_spec(shape):
    n = len(shape)
    return pl.BlockSpec(shape, lambda *_: (0,) * n)


def _mod_kernel(c_ref, w_ref, b_ref, o_ref):
    c = c_ref[...]
    o_ref[0] = _dot_hi(c * _sigmoid(c), w_ref[0]) + b_ref[0]


def _modulation(c_all, w_mod, b_mod):
    depth, d, n = w_mod.shape
    nb = 1536
    return pl.pallas_call(
        _mod_kernel,
        out_shape=jax.ShapeDtypeStruct((depth, 16, n), F32),
        grid=(depth, n // nb),
        in_specs=[pl.BlockSpec((16, d), lambda l, j: (0, 0)),
                  pl.BlockSpec((1, d, nb), lambda l, j: (l, 0, j)),
                  pl.BlockSpec((1, 1, nb), lambda l, j: (l, 0, j))],
        out_specs=pl.BlockSpec((1, 16, nb), lambda l, j: (l, 0, j)),
        compiler_params=_cparams(("arbitrary", "arbitrary")),
        name="modulation",
    )(c_all, w_mod, b_mod.reshape(depth, 1, n))


def _inproj_kernel(*refs, has_prev):
    if has_prev:
        x_ref, f_ref, modp_ref = refs[:3]
        refs = refs[3:]
    else:
        x_ref = refs[0]
        refs = refs[1:]
    (mod_ref, n1_ref, w_ref, qn_ref, kvn_ref, wuq_ref, wukv_ref,
     cosa_ref, sina_ref, cosc_ref, sinc_ref, cosd_ref, sind_ref) = refs[:13]
    outs = refs[13:]
    if has_prev:
        xo_ref = outs[0]
        outs = outs[1:]
    (qa_ref, ka_ref, va_ref, bq_ref, bff_ref, bfb_ref, bv_ref, bg_ref,
     cq_ref, ck_ref, cv_ref, dq_ref, dk_ref, dv_ref) = outs

    x = x_ref[0]
    if has_prev:
        x = x + modp_ref[0, 0, 5:6, :] * f_ref[0]
        xo_ref[0] = x
    m = mod_ref[0, 0]
    h = (_rms(x, n1_ref[...]) * (1.0 + m[1:2]) + m[0:1]).astype(BF16)

    z = _dot(h, w_ref[:, OFF_A:OFF_A + 768])
    cosa, sina = cosa_ref[...], sina_ref[...]
    qa_ref[0] = (_rope(z[:, 0:256], cosa, sina, 16) * (A_HEAD_DIM ** -0.5)).astype(BF16)
    ka_ref[0] = _rope(z[:, 256:512], cosa, sina, 16).astype(BF16)
    va_ref[0] = z[:, 512:768].astype(BF16)

    z = _dot(h, w_ref[:, OFF_B:OFF_B + 1280])
    bq_ref[0] = z[:, 0:256].astype(BF16)
    bff_ref[0] = z[:, 256:512]
    bfb_ref[0] = z[:, 512:768]
    bv_ref[0] = z[:, 768:1024].astype(BF16)
    bg_ref[0] = z[:, 1024:1280].astype(BF16)

    z = _dot(h, w_ref[:, OFF_C:OFF_C + 896])
    cosc, sinc = cosc_ref[...], sinc_ref[...]
    cq = _rms(z[:, 0:256], qn_ref[...]).astype(BF16)
    q = _rope(_dot(cq, wuq_ref[...]), cosc, sinc, 8)
    cq_ref[0] = (q * ((C_NOPE + C_ROPE) ** -0.5 * LOG2E)).astype(BF16)
    ckv = _rms(z[:, 256:384], kvn_ref[...]).astype(BF16)
    kv = _dot(ckv, wukv_ref[...])
    ck_ref[0] = (kv[:, 0:512] + _rope(z[:, 384:896], cosc, sinc, 8)).astype(BF16)
    cv_ref[0] = kv[:, 512:768].astype(BF16)

    z = _dot(h, w_ref[:, OFF_D:OFF_D + 768])
    cosd, sind = cosd_ref[...], sind_ref[...]
    dq_ref[0] = (_rope(z[:, 0:256], cosd, sind, 8) * (D_HEAD_DIM ** -0.5 * LOG2E)).astype(BF16)
    dk_ref[0] = _rope(z[:, 256:512], cosd, sind, 8).astype(BF16)
    dv_ref[0] = z[:, 512:768].astype(BF16)


def _inproj(x, f_prev, mod_prev, mod, n1, w_perm, qn, kvn, wuq, wukv, tabs, n_lat_tiles):
    b, t, d = x.shape
    nt = t // TM
    has_prev = f_prev is not None
    tile = lambda w: pl.BlockSpec((1, TM, w), lambda bi, i: (bi, i, 0))
    modspec = pl.BlockSpec((1, 1, 8, d), lambda bi, i: (bi, i // n_lat_tiles, 0, 0))
    tab = lambda w: pl.BlockSpec((TM, w), lambda bi, i: (i, 0))
    in_specs, args = [tile(d)], [x]
    if has_prev:
        in_specs += [tile(d), modspec]
        args += [f_prev, mod_prev]
    in_specs += [modspec, _const_spec((1, d)), _const_spec((d, N_PERM)), _const_spec((1, 256)),
                 _const_spec((1, 128)), _const_spec((256, 512)), _const_spec((128, 768)),
                 tab(256), tab(256), tab(512), tab(512), tab(256), tab(256)]
    args += [mod, n1, w_perm, qn, kvn, wuq, wukv, *tabs]
    widths = [(256, BF16), (256, BF16), (256, BF16),
              (256, BF16), (256, F32), (256, F32), (256, BF16), (256, BF16),
              (512, BF16), (512, BF16), (256, BF16),
              (256, BF16), (256, BF16), (256, BF16)]
    out_shape = [jax.ShapeDtypeStruct((b, t, w), dt) for w, dt in widths]
    out_specs = [tile(w) for w, _ in widths]
    if has_prev:
        out_shape = [jax.ShapeDtypeStruct((b, t, d), F32)] + out_shape
        out_specs = [tile(d)] + out_specs
    res = pl.pallas_call(
        functools.partial(_inproj_kernel, has_prev=has_prev),
        out_shape=out_shape, grid=(b, nt), in_specs=in_specs, out_specs=out_specs,
        compiler_params=_cparams(("parallel", "parallel")),
        name="inproj",
    )(*args)
    if has_prev:
        return res[0], res[1:]
    return x, res


def _win_kernel(sink_ref, q_ref, k_ref, v_ref, o_ref, *, n_lat, t_all):
    w = A_WINDOW
    n = pl.program_id(1)
    nb_lat = n_lat // w
    start = pl.multiple_of(jnp.clip((n - 1) * w, 0, n_lat - 3 * w), w)
    q = q_ref[0]
    lane = lax.broadcasted_iota(jnp.int32, (w, LANES), 1)
    row = lax.broadcasted_iota(jnp.int32, (2 * w, 3 * w), 0)
    col = lax.broadcasted_iota(jnp.int32, (2 * w, 3 * w), 1)
    rel = (start + col) - (n * w + row % w)
    valid = (jnp.abs(rel) <= w) & (n < nb_lat)
    row1 = lax.broadcasted_iota(jnp.int32, (2 * w, 1), 0)
    for j in range(A_KV_HEADS):
        sl = slice(LANES * j, LANES * (j + 1))
        qp = q[:, sl]
        zero = jnp.zeros_like(qp)
        lhs = jnp.concatenate([jnp.where(lane < 64, qp, zero), jnp.where(lane >= 64, qp, zero)], axis=0)
        sb = _dot_nt(lhs, k_ref[0, pl.ds(start, 3 * w), sl])
        sc = _dot_nt(lhs, k_ref[0, n_lat:t_all, sl])
        sb = jnp.where(valid, sb, MASK_VALUE)
        sink = jnp.where(row1 < w, sink_ref[2 * j], sink_ref[2 * j + 1])
        m = jnp.maximum(jnp.maximum(jnp.max(sb, axis=1, keepdims=True), jnp.max(sc, axis=1, keepdims=True)), sink)
        pb = jnp.exp(sb - m)
        pc = jnp.exp(sc - m)
        den = jnp.sum(pb, axis=1, keepdims=True) + jnp.sum(pc, axis=1, keepdims=True) + jnp.exp(sink - m)
        o = (_dot(pb.astype(BF16), v_ref[0, pl.ds(start, 3 * w), sl])
             + _dot(pc.astype(BF16), v_ref[0, n_lat:t_all, sl])) / den
        o_ref[0, :, sl] = jnp.where(lane < 64, o[:w], o[w:]).astype(o_ref.dtype)


def _win_attention(sink, q, k, v, n_lat):
    b, t, _ = q.shape
    w = A_WINDOW
    return pl.pallas_call(
        functools.partial(_win_kernel, n_lat=n_lat, t_all=t),
        out_shape=jax.ShapeDtypeStruct((b, t, 256), BF16),
        grid=(b, t // w),
        in_specs=[pl.BlockSpec(memory_space=pltpu.SMEM),
                  pl.BlockSpec((1, w, 256), lambda bi, i: (bi, i, 0)),
                  pl.BlockSpec((1, t, 256), lambda bi, i: (bi, 0, 0)),
                  pl.BlockSpec((1, t, 256), lambda bi, i: (bi, 0, 0))],
        out_specs=pl.BlockSpec((1, w, 256), lambda bi, i: (bi, i, 0)),
        compiler_params=_cparams(("parallel", "arbitrary")),
        name="win_attention",
    )(sink, q, k, v)


def _hgrn_level_matrix(rev):
    c = HG_CHUNK
    m = np.zeros((8, c, c), np.float32)
    for lvl in range(HG_LEVELS):
        size = c >> lvl
        for t in range(c):
            mid = (t // size) * size + size // 2
            upper = t >= mid
            if not rev:
                rng = range(mid, t + 1) if upper else range(t + 1, mid)
            else:
                rng = range(mid, t) if upper else range(t, mid)
            m[lvl, t, list(rng)] = 1.0
    for t in range(c):
        if not rev:
            m[6, t, :t + 1] = 1.0
            m[7, t, t + 1:] = 1.0
        else:
            m[6, t, t:] = 1.0
            m[7, t, :t] = 1.0
    return m.reshape(8 * c, c)


def _hgrn_chunk(q, k, logf, v, st, mlev, rev):
    c = HG_CHUNK
    hi = logf.astype(BF16)
    lo = (logf - hi.astype(F32)).astype(BF16)
    e2 = _dot(mlev, jnp.concatenate([hi, lo], axis=1))
    wgt = jnp.exp(e2[:, :256] + e2[:, 256:])
    t = lax.broadcasted_iota(jnp.int32, (c, 256), 0)
    s_col = lax.broadcasted_iota(jnp.int32, (c, 256), 1) % c
    r_bd = lax.broadcasted_iota(jnp.int32, (256, 256), 0) // 64
    c_bd = lax.broadcasted_iota(jnp.int32, (256, 256), 1) // 64
    bd = r_bd == c_bd

    def block_diag(a):
        return jnp.where(bd, jnp.concatenate([a] * 4, axis=0), 0.0).astype(BF16)

    scores = jnp.where(t == s_col, _dot_nt(q.astype(BF16), block_diag(k)), 0.0)
    for lvl in range(HG_LEVELS):
        wl = wgt[c * lvl:c * (lvl + 1)]
        q_side = ((t >> (HG_LEVELS - 1 - lvl)) & 1) == (0 if rev else 1)
        ql = jnp.where(q_side, q * wl, 0.0).astype(BF16)
        kl = jnp.where(q_side, 0.0, k * wl)
        s = _dot_nt(ql, block_diag(kl))
        if lvl > 0:
            sh = HG_LEVELS - lvl
            s = jnp.where((t >> sh) == (s_col >> sh), s, 0.0)
        scores = scores + s
    q_in = (q * wgt[6 * c:7 * c]).astype(BF16)
    k_end = (k * wgt[7 * c:8 * c]).astype(BF16)
    o = _dot(scores.astype(BF16), block_diag(v)) + _dot_nt(q_in, st.astype(BF16))
    edge = 6 * c if rev else 7 * c - 1
    total = wgt[edge:edge + 1]
    st_new = st * total + jnp.where(bd, _dot_tn(v.astype(BF16), k_end), 0.0)
    return o, st_new


def _hgrn_kernel(mf_ref, mb_ref, lb_ref, qf_ref, zf_ref, vf_ref, qb_ref, zb_ref, vb_ref,
                 of_ref, ob_ref, stf_ref, stb_ref, lff_ref, kff_ref, lfb_ref, kfb_ref):
    c = HG_CHUNK
    nc = TM // c

    @pl.when(pl.program_id(1) == 0)
    def _():
        stf_ref[...] = jnp.zeros_like(stf_ref)
        stb_ref[...] = jnp.zeros_like(stb_ref)

    def gates(z, lb, lf_ref, k_ref):
        f = lb + (1.0 - lb) * _sigmoid(z)
        lf_ref[...] = jnp.log(jnp.maximum(f, GATE_FLOOR))
        k_ref[...] = (1.0 - lb) * _sigmoid(-z)

    gates(zf_ref[0], lb_ref[0:1], lff_ref, kff_ref)
    gates(zb_ref[0], lb_ref[1:2], lfb_ref, kfb_ref)

    def body(ci, carry):
        r0 = pl.multiple_of(ci * c, c)
        r1 = pl.multiple_of((nc - 1 - ci) * c, c)
        o, st = _hgrn_chunk(qf_ref[0, pl.ds(r0, c), :].astype(F32), kff_ref[pl.ds(r0, c), :],
                            lff_ref[pl.ds(r0, c), :], vf_ref[0, pl.ds(r0, c), :].astype(F32),
                            stf_ref[...], mf_ref[...], False)
        of_ref[0, pl.ds(r0, c), :] = o
        stf_ref[...] = st
        o, st = _hgrn_chunk(qb_ref[0, pl.ds(r1, c), :].astype(F32), kfb_ref[pl.ds(r1, c), :],
                            lfb_ref[pl.ds(r1, c), :], vb_ref[0, pl.ds(r1, c), :].astype(F32),
                            stb_ref[...], mb_ref[...], True)
        ob_ref[0, pl.ds(r1, c), :] = o
        stb_ref[...] = st
        return carry

    lax.fori_loop(0, nc, body, 0)


def _hgrn(lb, q, zff, zfb, v, n_lat):
    b, t, _ = q.shape
    nlb = n_lat // TM
    ncb = (t - n_lat) // TM
    fwd = lambda bi, i: (bi, jnp.where(i < ncb, nlb + i, i - ncb), 0)
    bwd = lambda bi, i: (bi, jnp.where(i < ncb, nlb + ncb - 1 - i, nlb + ncb - 1 - i), 0)
    blk = lambda im: pl.BlockSpec((1, TM, 256), im)
    mf = jnp.asarray(_hgrn_level_matrix(False), BF16)
    mb = jnp.asarray(_hgrn_level_matrix(True), BF16)
    return pl.pallas_call(
        _hgrn_kernel,
        out_shape=[jax.ShapeDtypeStruct((b, t, 256), F32)] * 2,
        grid=(b, t // TM),
        in_specs=[_const_spec((8 * HG_CHUNK, HG_CHUNK)), _const_spec((8 * HG_CHUNK, HG_CHUNK)),
                  _const_spec((2, 256)),
                  blk(fwd), blk(fwd), blk(fwd), blk(bwd), blk(bwd), blk(bwd)],
        out_specs=[blk(fwd), blk(bwd)],
        scratch_shapes=[pltpu.VMEM((256, 256), F32), pltpu.VMEM((256, 256), F32),
                        pltpu.VMEM((TM, 256), F32), pltpu.VMEM((TM, 256), F32),
                        pltpu.VMEM((TM, 256), F32), pltpu.VMEM((TM, 256), F32)],
        compiler_params=_cparams(("parallel", "arbitrary")),
        name="hgrn",
    )(mf, mb, lb, q, zff, v, q, zfb, v)


def _key_chunks(n_lat, t_all):
    chunks = [(lo, min(lo + KV_CHUNK, n_lat)) for lo in range(0, n_lat, KV_CHUNK)]
    return chunks + [(n_lat, t_all)], [(n_lat, t_all)]


def _online_softmax(lhs, k_ref, v_ref, chunks):
    rows = lhs.shape[0]
    m = jnp.full((rows, 1), -jnp.inf, F32)
    l = jnp.zeros((rows, 1), F32)
    acc = jnp.zeros((rows, LANES), F32)
    for lo, hi in chunks:
        s = _dot_nt(lhs, k_ref[0, lo:hi, :])
        m_new = jnp.maximum(m, jnp.max(s, axis=1, keepdims=True))
        alpha = jnp.exp2(m - m_new)
        p = jnp.exp2(s - m_new)
        l = alpha * l + jnp.sum(p, axis=1, keepdims=True)
        acc = alpha * acc + _dot(p.astype(BF16), v_ref[0, lo:hi, :])
        m = m_new
    return acc / l


def _mla_kernel(q_ref, k_ref, v_ref, o_ref, *, chunks):
    tq = q_ref.shape[1]
    q = q_ref[0]
    zero = jnp.zeros_like(q)
    lane2 = lax.broadcasted_iota(jnp.int32, q.shape, 1)
    lhs = jnp.concatenate([jnp.where(lane2 < LANES, q, zero), jnp.where(lane2 >= LANES, q, zero)], axis=0)
    o = _online_softmax(lhs, k_ref, v_ref, chunks)
    lane = lax.broadcasted_iota(jnp.int32, (tq, LANES), 1)
    o_ref[0] = jnp.where(lane < 64, o[:tq], o[tq:]).astype(o_ref.dtype)


def _diff_kernel(lam_ref, g_ref, q_ref, k_ref, v_ref, o_ref, *, chunks, lam_init):
    tq = q_ref.shape[1]
    q = q_ref[0]
    zero = jnp.zeros_like(q)
    lane = lax.broadcasted_iota(jnp.int32, (tq, LANES), 1)
    lhs = jnp.concatenate([jnp.where(lane // D_HEAD_DIM == r, q, zero) for r in range(4)], axis=0)
    o = _online_softmax(lhs, k_ref, v_ref, chunks)
    lam = lam_ref[0]
    o0 = o[0:tq] - lam * o[tq:2 * tq]
    o1 = o[2 * tq:3 * tq] - lam * o[3 * tq:4 * tq]
    o = jnp.where(lane < 64, o0, o1)
    o_ref[0] = (_head_rms(o, g_ref[...], LANES) * (1.0 - lam_init)).astype(o_ref.dtype)


def _attention_call(body, name, extras, extra_specs, q, k, v, tq, row0, n_rows, out_init):
    b, t, qw = q.shape
    blk0 = row0 // tq
    n_extra = len(extras)
    in_specs = list(extra_specs) + [
        pl.BlockSpec((1, tq, qw // 2), lambda bi, p, i: (bi, blk0 + i, p)),
        pl.BlockSpec((1, t, k.shape[2] // 2), lambda bi, p, i: (bi, 0, p)),
        pl.BlockSpec((1, t, LANES), lambda bi, p, i: (bi, 0, p))]
    args = list(extras) + [q, k, v]
    aliases = {}
    if out_init is not None:
        in_specs.append(pl.BlockSpec(memory_space=pl.ANY))
        aliases = {len(args): 0}
        args.append(out_init)

    def kern(*refs):
        body(*refs[:n_extra + 3], refs[-1])

    return pl.pallas_call(
        kern,
        out_shape=jax.ShapeDtypeStruct((b, t, 256), BF16),
        grid=(b, 2, n_rows // tq),
        in_specs=in_specs,
        out_specs=pl.BlockSpec((1, tq, LANES), lambda bi, p, i: (bi, blk0 + i, p)),
        input_output_aliases=aliases,
        compiler_params=_cparams(("parallel", "parallel", "arbitrary")),
        name=name,
    )(*args)


def _two_pass_attention(body_of, name, extras, extra_specs, q, k, v, n_lat, tq_lat):
    t = q.shape[1]
    lat_chunks, ctx_chunks = _key_chunks(n_lat, t)
    out = jnp.zeros((q.shape[0], t, 256), BF16)
    out = _attention_call(body_of(lat_chunks), name, extras, extra_specs, q, k, v, tq_lat, 0, n_lat, out)
    return _attention_call(body_of(ctx_chunks), name + "_ctx", extras, extra_specs, q, k, v,
                           TQ, n_lat, t - n_lat, out)


def _mla_attention(q, k, v, n_lat):
    body_of = lambda chunks: functools.partial(_mla_kernel, chunks=chunks)
    return _two_pass_attention(body_of, "mla_attention", [], [], q, k, v, n_lat, TQ_MLA)


def _diff_attention(lam, q, k, v, gain, n_lat, lam_init):
    body_of = lambda chunks: functools.partial(_diff_kernel, chunks=chunks, lam_init=lam_init)
    specs = [pl.BlockSpec(memory_space=pltpu.SMEM), _const_spec((1, LANES))]
    return _two_pass_attention(body_of, "diff_attention", [lam, gain], specs, q, k, v, n_lat, TQ)


def _merge_kernel(x_ref, mod_ref, n1_ref, n2_ref, a_ref, of_ref, ob_ref, bg_ref, hg_ref, c_ref, d_ref,
                  wg_ref, wb_ref, wo_ref, wr_ref, br_ref, xo_ref, h2_ref, ri_ref, rw_ref):
    d = D_MODEL
    x = x_ref[0]
    m = mod_ref[0, 0]
    h = (_rms(x, n1_ref[...]) * (1.0 + m[1:2]) + m[0:1]).astype(BF16)
    g = bg_ref[0].astype(F32)
    b_out = _head_rms(of_ref[0] + ob_ref[0], hg_ref[...], 256) * (g * _sigmoid(g))
    branches = (a_ref[0], b_out.astype(BF16), c_ref[0], d_ref[0])
    y = jnp.zeros((TM, d), F32)
    for i, br in enumerate(branches):
        gate = _sigmoid(_dot(h, wg_ref[:, d * i:d * (i + 1)]))
        y = y + gate * _dot(br, wb_ref[i])
    x = x + m[2:3] * _dot(y.astype(BF16), wo_ref[...])
    xo_ref[0] = x
    h2 = _rms(x, n2_ref[...]) * (1.0 + m[4:5]) + m[3:4]
    h2b = h2.astype(BF16)
    h2_ref[0] = h2

    h2lo = (h2 - h2b.astype(F32)).astype(BF16)
    r = _dot(jnp.concatenate([h2b, h2lo], axis=0), wr_ref[...])
    logits = r[:TM, :LANES] + r[:TM, LANES:] + r[TM:, :LANES] + br_ref[...]
    lane = lax.broadcasted_iota(jnp.int32, (TM, LANES), 1)
    neg = -jnp.inf
    is_g = (lane >= N_EXPERTS) & (lane < N_EXPERTS + N_GROUPS)
    lg = jnp.where(is_g, logits, neg)
    mg = jnp.max(lg, axis=1, keepdims=True)
    g_val = 1.0 / jnp.sum(jnp.exp(lg - mg), axis=1, keepdims=True)
    g_idx = jnp.min(jnp.where(lg == mg, lane, 4 * LANES), axis=1, keepdims=True) - N_EXPERTS
    in_group = (lane >= g_idx * EXPERTS_PER_GROUP) & (lane < (g_idx + 1) * EXPERTS_PER_GROUP)
    le = jnp.where(in_group, logits, neg)
    m1 = jnp.max(le, axis=1, keepdims=True)
    e1 = jnp.min(jnp.where(le == m1, lane, 4 * LANES), axis=1, keepdims=True)
    le2 = jnp.where(lane == e1, neg, le)
    m2 = jnp.max(le2, axis=1, keepdims=True)
    e2 = jnp.min(jnp.where(le2 == m2, lane, 4 * LANES), axis=1, keepdims=True)
    r2 = jnp.exp(m2 - m1)
    v1 = g_val / (1.0 + r2)
    ri_ref[0] = jnp.where(lane == 0, e1, jnp.where(lane == 1, e2, 0))
    rw_ref[0] = jnp.where(lane == 0, v1, jnp.where(lane == 1, v1 * r2, 0.0))


def _merge(x, mod, n1, n2, a_o, o_f, o_b, bg, hg, c_o, d_o, wg, wb, wo, wr, br, n_lat_tiles):
    b, t, d = x.shape
    tile = lambda w: pl.BlockSpec((1, TM, w), lambda bi, i: (bi, i, 0))
    return pl.pallas_call(
        _merge_kernel,
        out_shape=[jax.ShapeDtypeStruct((b, t, d), F32), jax.ShapeDtypeStruct((b, t, d), F32),
                   jax.ShapeDtypeStruct((b, t, LANES), jnp.int32), jax.ShapeDtypeStruct((b, t, LANES), F32)],
        grid=(b, t // TM),
        in_specs=[tile(d), pl.BlockSpec((1, 1, 8, d), lambda bi, i: (bi, i // n_lat_tiles, 0, 0)),
                  _const_spec((1, d)), _const_spec((1, d)),
                  tile(256), tile(256), tile(256), tile(256), _const_spec((1, 256)), tile(256), tile(256),
                  _const_spec((d, 4 * d)), _const_spec((4, 256, d)), _const_spec((d, d)),
                  _const_spec((d, 256)), _const_spec((1, LANES))],
        out_specs=[tile(d), tile(d), tile(LANES), tile(LANES)],
        compiler_params=_cparams(("parallel", "parallel")),
        name="merge",
    )(x, mod, n1, n2, a_o, o_f, o_b, bg, hg, c_o, d_o, wg, wb, wo, wr, br)


def _route_positions(ri, n_tok):
    ef = ri[:, :2].reshape(-1)
    rb = 2 * TM
    oh = (ef[:, None] == jnp.arange(N_EXPERTS, dtype=jnp.int32)[None, :]).astype(F32).reshape(-1, rb, N_EXPERTS)
    tri = (jnp.arange(rb)[:, None] >= jnp.arange(rb)[None, :]).astype(F32)
    within = jnp.einsum("ij,gje->gie", tri, oh)
    tot = within[:, -1, :]
    before = jnp.cumsum(tot, axis=0) - tot
    cnt = jnp.sum(tot, axis=0).astype(jnp.int32)
    pcnt = ((cnt + MOE_TILE - 1) // MOE_TILE) * MOE_TILE
    end = jnp.cumsum(pcnt)
    start = (end - pcnt).astype(F32)
    pos = jnp.sum((within + (before + start[None, :])[:, None, :]) * oh, axis=2) - 1.0
    pos = pos.reshape(-1)
    n_tiles = 2 * n_tok // MOE_TILE + N_EXPERTS
    n_act = (end[-1] // MOE_TILE).astype(jnp.int32)
    tile_row = jnp.minimum(jnp.arange(n_tiles, dtype=jnp.int32), n_act - 1) * MOE_TILE
    tile_expert = jnp.sum((end[None, :] <= tile_row[:, None]).astype(jnp.int32), axis=1)
    tile_expert = jnp.minimum(tile_expert, N_EXPERTS - 1)
    return pos.reshape(n_tok // TM, 1, 2 * TM).astype(jnp.int32), tile_expert, n_act.reshape(1)


def _dispatch_kernel(pos_ref, h_ref, xs_in_ref, xs_ref, sem):
    del xs_in_ref

    def body(r, carry):
        for s in range(2):
            p = pos_ref[0, 0, 2 * r + s]
            pltpu.make_async_copy(h_ref.at[pl.ds(r, 1)], xs_ref.at[pl.ds(p, 1)], sem.at[s]).start()
        return carry

    lax.fori_loop(0, TM, body, 0)
    for s in range(2):
        pltpu.make_async_copy(h_ref, h_ref, sem.at[s]).wait()


def _dispatch(pos, h2, xs_init):
    n, d = h2.shape
    return pl.pallas_call(
        _dispatch_kernel,
        out_shape=jax.ShapeDtypeStruct(xs_init.shape, F32),
        grid=(n // TM,),
        in_specs=[pl.BlockSpec((1, 1, 2 * TM), lambda t: (t, 0, 0), memory_space=pltpu.SMEM),
                  pl.BlockSpec((TM, d), lambda t: (t, 0)),
                  pl.BlockSpec(memory_space=pl.ANY)],
        out_specs=pl.BlockSpec(memory_space=pl.ANY),
        scratch_shapes=[pltpu.SemaphoreType.DMA((2,))],
        input_output_aliases={2: 0},
        compiler_params=_cparams(("arbitrary",)),
        name="moe_dispatch",
    )(pos, h2, xs_init)


def _experts_kernel(te_ref, na_ref, x_ref, wgu_ref, wd_ref, o_ref):
    del te_ref
    t = pl.program_id(0)

    @pl.when(t < na_ref[0])
    def _():
        gu = _dot(x_ref[...].astype(BF16), wgu_ref[0])
        gt, up = gu[:, :EXPERT_HIDDEN], gu[:, EXPERT_HIDDEN:]
        o_ref[...] = _dot((gt * _sigmoid(gt) * up).astype(BF16), wd_ref[0])

    @pl.when(t >= na_ref[0])
    def _():
        o_ref[...] = jnp.zeros_like(o_ref)


def _experts(tile_expert, n_act, xs, wgu, wd):
    p, d = xs.shape
    return pl.pallas_call(
        _experts_kernel,
        out_shape=jax.ShapeDtypeStruct((p, d), F32),
        grid_spec=pltpu.PrefetchScalarGridSpec(
            num_scalar_prefetch=2, grid=(p // MOE_TILE,),
            in_specs=[pl.BlockSpec((MOE_TILE, d), lambda t, te, na: (t, 0)),
                      pl.BlockSpec((1, d, 2 * EXPERT_HIDDEN), lambda t, te, na: (te[t], 0, 0)),
                      pl.BlockSpec((1, EXPERT_HIDDEN, d), lambda t, te, na: (te[t], 0, 0))],
            out_specs=pl.BlockSpec((MOE_TILE, d), lambda t, te, na: (t, 0))),
        compiler_params=_cparams(("arbitrary",)),
        name="moe_experts",
    )(tile_expert, n_act, xs, wgu, wd)


def _combine_kernel(pos_ref, ys_ref, w_ref, o_ref, buf, sem):
    def body(r, carry):
        for s in range(2):
            p = pos_ref[0, 0, 2 * r + s]
            pltpu.make_async_copy(ys_ref.at[pl.ds(p, 1)], buf.at[s, pl.ds(r, 1)], sem.at[s]).start()
        return carry

    lax.fori_loop(0, TM, body, 0)
    for s in range(2):
        pltpu.make_async_copy(buf.at[s], buf.at[s], sem.at[s]).wait()
    w = w_ref[...]
    o_ref[...] = w[:, 0:1] * buf[0] + w[:, 1:2] * buf[1]


def _combine(pos, ys, rwt):
    n = rwt.shape[0]
    d = ys.shape[1]
    return pl.pallas_call(
        _combine_kernel,
        out_shape=jax.ShapeDtypeStruct((n, d), F32),
        grid=(n // TM,),
        in_specs=[pl.BlockSpec((1, 1, 2 * TM), lambda t: (t, 0, 0), memory_space=pltpu.SMEM),
                  pl.BlockSpec(memory_space=pl.ANY),
                  pl.BlockSpec((TM, LANES), lambda t: (t, 0))],
        out_specs=pl.BlockSpec((TM, d), lambda t: (t, 0)),
        scratch_shapes=[pltpu.VMEM((2, TM, d), F32), pltpu.SemaphoreType.DMA((2,))],
        compiler_params=_cparams(("arbitrary",)),
        name="moe_combine",
    )(pos, ys, rwt)


def _moe(h2, ri, rwt, wgu, wd, xs_init):
    n = h2.shape[0]
    pos, tile_expert, n_act = _route_positions(ri, n)
    xs = _dispatch(pos, h2, xs_init)
    ys = _experts(tile_expert, n_act, xs, wgu, wd)
    return _combine(pos, ys, rwt), xs


def _final_kernel(x_ref, f_ref, mod_ref, g_ref, o_ref):
    x = x_ref[0] + mod_ref[0, 0, 5:6, :] * f_ref[0]
    o_ref[0] = _rms(x, g_ref[...])


def _final(x, f, mod, gain, n_lat):
    b, t, d = x.shape
    tile = pl.BlockSpec((1, TM, d), lambda bi, i: (bi, i, 0))
    return pl.pallas_call(
        _final_kernel,
        out_shape=jax.ShapeDtypeStruct((b, n_lat, d), F32),
        grid=(b, n_lat // TM),
        in_specs=[tile, tile, pl.BlockSpec((1, 1, 8, d), lambda bi, i: (bi, 0, 0, 0)), _const_spec((1, d))],
        out_specs=tile,
        compiler_params=_cparams(("parallel", "parallel")),
        name="final_norm",
    )(x, f, mod, gain)


def _rope_tables(n_lat, t_all, lane_rope, lane_off, dr):
    half, quarter = dr // 2, dr // 4
    inv_freq = 1.0 / (ROPE_BASE ** (jnp.arange(quarter, dtype=F32) / quarter))
    off = np.asarray(lane_off)
    use_col = off >= half
    j = (off % half) % quarter
    first = (off % half) < quarter
    tok = jnp.arange(n_lat, dtype=jnp.int32)
    row = (tok // GRID_W).astype(F32)
    col = (tok % GRID_W).astype(F32)
    pos = jnp.where(jnp.asarray(use_col)[None, :], col[:, None], row[:, None])
    ang = pos * inv_freq[jnp.asarray(j)][None, :]
    rope = jnp.asarray(lane_rope)[None, :]
    cos = jnp.where(rope, jnp.cos(ang), 1.0)
    sin = jnp.where(rope, jnp.sin(ang) * jnp.where(jnp.asarray(first), -1.0, 1.0)[None, :], 0.0)
    pad = t_all - n_lat
    w = off.shape[0]
    cos = jnp.concatenate([cos, jnp.ones((pad, w), F32)], axis=0)
    sin = jnp.concatenate([sin, jnp.zeros((pad, w), F32)], axis=0)
    return cos, sin


def _all_rope_tables(n_lat, t_all):
    la = np.arange(256)
    ta = _rope_tables(n_lat, t_all, np.ones(256, bool), la % 64, 64)
    lc = np.arange(512) % 128
    tc = _rope_tables(n_lat, t_all, (lc >= 64) & (lc < 96), np.clip(lc - 64, 0, 31), 32)
    td = _rope_tables(n_lat, t_all, np.ones(256, bool), la % 32, 32)
    return (*ta, *tc, *td)


def _permute_w_in(w):
    s = np.cumsum([0, 256, 128, 128, 256, 256, 256, 256, 256, 256, 128, 32, 256, 256, 256])
    seg = lambda i: w[:, s[i]:s[i + 1]]
    dup = lambda a: jnp.concatenate([a[:, 0:64], a[:, 0:64], a[:, 64:128], a[:, 64:128]], axis=1)
    z64 = jnp.zeros((w.shape[0], 64), w.dtype)
    z32 = jnp.zeros((w.shape[0], 32), w.dtype)
    kr = jnp.concatenate([z64, seg(10), z32] * C_HEADS, axis=1)
    cols = [seg(0), dup(seg(1)), dup(seg(2)),
            seg(3), seg(4), seg(5), seg(6), seg(7),
            seg(8), seg(9), kr,
            seg(11), seg(12), seg(13)]
    return jnp.concatenate(cols, axis=1).astype(BF16)


def _permute_mla(w_uq, w_ukv):
    z32 = jnp.zeros((w_uq.shape[0], 32), w_uq.dtype)
    qd = C_NOPE + C_ROPE
    uq = jnp.concatenate([a for h in range(C_HEADS) for a in (w_uq[:, qd * h:qd * (h + 1)], z32)], axis=1)
    z64 = jnp.zeros((w_ukv.shape[0], 64), w_ukv.dtype)
    kd = C_NOPE + C_V
    uk = jnp.concatenate([a for h in range(C_HEADS) for a in (w_ukv[:, kd * h:kd * h + C_NOPE], z64)], axis=1)
    uv = jnp.concatenate([w_ukv[:, kd * h + C_NOPE:kd * (h + 1)] for h in range(C_HEADS)], axis=1)
    return uq.astype(BF16), jnp.concatenate([uk, uv], axis=1).astype(BF16)


def kernel(x, c, ctx, c_ctx, w_mod, b_mod, norm1, norm2, w_in, w_gate, w_branch, w_out, attn_sink, hgrn_lb_logits, hgrn_norm, mla_q_norm, mla_kv_norm, mla_w_uq, mla_w_ukv, diff_lambda, diff_subln, w_router_group, b_router_group, w_router_expert, b_router_expert, w_expert_gate, w_expert_up, w_expert_down, final_norm):
    b, n_lat, d = x.shape
    n_ctx = ctx.shape[1]
    depth = w_mod.shape[0]
    t_all = n_lat + n_ctx
    assert d == D_MODEL and n_lat % TM == 0 and n_ctx % TM == 0 and n_lat % GRID_W == 0
    assert n_lat >= 3 * A_WINDOW and b <= 15
    n_lat_tiles = n_lat // TM

    xs = jnp.concatenate([x, ctx], axis=1)
    c_all = jnp.concatenate([c, c_ctx[None], jnp.zeros((15 - b, d), F32)], axis=0)
    mod = _modulation(c_all, w_mod, b_mod).reshape(depth, 16, 6, d)
    mod = jnp.pad(mod, ((0, 0), (0, 0), (0, 2), (0, 0)))
    mod = jnp.stack([mod[:, :b], jnp.broadcast_to(mod[:, b:b + 1], (depth, b, 8, d))], axis=2)

    sm = jax.nn.softmax(hgrn_lb_logits.astype(F32), axis=0)
    lower_bounds = jnp.cumsum(sm, axis=0) - sm[0]
    tabs = _all_rope_tables(n_lat, t_all)

    n_tok = b * t_all
    moe_buf = jnp.zeros((2 * n_tok + N_EXPERTS * MOE_TILE, d), F32)
    f_prev, mod_prev = None, None
    for li in range(depth):
        lam_init = 0.8 - 0.6 * math.exp(-0.3 * li)
        lp = diff_lambda[li].astype(F32)
        lam = (jnp.exp(jnp.sum(lp[0] * lp[1])) - jnp.exp(jnp.sum(lp[2] * lp[3])) + lam_init).reshape(1)
        w_perm = _permute_w_in(w_in[li])
        wuq, wukv = _permute_mla(mla_w_uq[li], mla_w_ukv[li])
        xs, z = _inproj(xs, f_prev, mod_prev, mod[li], norm1[li][None], w_perm,
                        mla_q_norm[li][None], mla_kv_norm[li][None], wuq, wukv, tabs, n_lat_tiles)
        qa, ka, va, bq, bff, bfb, bv, bg, cq, ck, cv, dq, dk, dv = z
        a_o = _win_attention(attn_sink[li].astype(F32), qa, ka, va, n_lat)
        o_f, o_b = _hgrn(lower_bounds[li], bq, bff, bfb, bv, n_lat)
        c_o = _mla_attention(cq, ck, cv, n_lat)
        d_o = _diff_attention(lam, dq, dk, dv, jnp.tile(diff_subln[li], 2)[None], n_lat, lam_init)
        wr = jnp.concatenate([w_router_expert[li], w_router_group[li],
                              jnp.zeros((d, LANES - N_EXPERTS - N_GROUPS), F32)], axis=1)
        wr_hi = wr.astype(BF16)
        wr_lo = (wr - wr_hi.astype(F32)).astype(BF16)
        br = jnp.concatenate([b_router_expert[li], b_router_group[li],
                              jnp.zeros((LANES - N_EXPERTS - N_GROUPS,), F32)])[None]
        xs, h2, ri, rwt = _merge(xs, mod[li], norm1[li][None], norm2[li][None], a_o, o_f, o_b, bg,
                                 jnp.tile(hgrn_norm[li], B_HEADS)[None], c_o, d_o,
                                 w_gate[li].astype(BF16), w_branch[li].astype(BF16), w_out[li].astype(BF16),
                                 jnp.concatenate([wr_hi, wr_lo], axis=1), br, n_lat_tiles)
        wgu = jnp.concatenate([w_expert_gate[li], w_expert_up[li]], axis=2).astype(BF16)
        f, moe_buf = _moe(h2.reshape(n_tok, d), ri.reshape(n_tok, LANES), rwt.reshape(n_tok, LANES),
                          wgu, w_expert_down[li].astype(BF16), moe_buf)
        f_prev, mod_prev = f.reshape(b, t_all, d), mod[li]
    return _final(xs, f_prev, mod_prev, final_norm[None], n_lat)
```

```python
import functools
import math

import numpy as np
import jax
import jax.numpy as jnp
from jax import lax
from jax.experimental import pallas as pl
from jax.experimental.pallas import tpu as pltpu

D_MODEL = 1024
GRID_W = 64
ROPE_BASE = 10000.0
NORM_EPS = 1e-6
MASK_VALUE = -1e30
GATE_FLOOR = 1e-30

A_HEADS, A_KV_HEADS, A_HEAD_DIM, A_WINDOW = 4, 2, 64, 128
B_HEADS, B_KEY_DIM, B_VAL_DIM = 4, 64, 64
C_HEADS, C_Q_LORA, C_KV_LORA, C_NOPE, C_ROPE, C_V = 4, 256, 128, 64, 32, 64
D_HEADS, D_HEAD_DIM = 4, 32
N_GROUPS, EXPERTS_PER_GROUP, EXPERT_HIDDEN = 4, 8, 256
N_EXPERTS = N_GROUPS * EXPERTS_PER_GROUP

TM = 256
TQ = 256
TQ_MLA = 512
LOG2E = 1.4426950408889634
KV_CHUNK = 4096
MOE_TILE = 256
HG_CHUNK = 64
HG_LEVELS = 6
LANES = 128
VMEM_LIMIT = 52 * 1024 * 1024

OFF_A, OFF_B, OFF_C, OFF_D, N_PERM = 0, 768, 2048, 2944, 3712

F32 = jnp.float32
BF16 = jnp.bfloat16


def _dot(a, b):
    return jnp.dot(a, b, preferred_element_type=F32)


def _dot_nt(a, b):
    return lax.dot_general(a, b, (((1,), (1,)), ((), ())), preferred_element_type=F32)


def _dot_tn(a, b):
    return lax.dot_general(a, b, (((0,), (0,)), ((), ())), preferred_element_type=F32)


def _dot_hi(a, b):
    return jnp.dot(a, b, preferred_element_type=F32, precision=lax.Precision.HIGHEST)


def _sigmoid(x):
    return 1.0 / (1.0 + jnp.exp(-x))


def _rms(x, gain):
    return x * lax.rsqrt(jnp.mean(x * x, axis=-1, keepdims=True) + NORM_EPS) * gain


def _head_rms(o, gain, width):
    r = lax.broadcasted_iota(jnp.int32, (width, width), 0) // 64
    c = lax.broadcasted_iota(jnp.int32, (width, width), 1) // 64
    ones = jnp.where(r == c, 1.0 / 64.0, 0.0).astype(F32)
    ms = _dot_hi(o * o, ones)
    return o * lax.rsqrt(ms + NORM_EPS) * gain


def _rope(x, cos, sin, quarter):
    w = x.shape[-1]
    lane = lax.broadcasted_iota(jnp.int32, x.shape, 1)
    first = (lane % (2 * quarter)) < quarter
    sw = jnp.where(first, pltpu.roll(x, w - quarter, 1), pltpu.roll(x, quarter, 1))
    return x * cos + sw * sin


def _cparams(sem):
    return pltpu.CompilerParams(dimension_semantics=sem, vmem_limit_bytes=VMEM_LIMIT)


def _const_spec(shape):
    n = len(shape)
    return pl.BlockSpec(shape, lambda *_: (0,) * n)


def _mod_kernel(c_ref, w_ref, b_ref, o_ref):
    c = c_ref[...]
    o_ref[0] = _dot_hi(c * _sigmoid(c), w_ref[0]) + b_ref[0]


def _modulation(c_all, w_mod, b_mod):
    depth, d, n = w_mod.shape
    nb = 1536
    return pl.pallas_call(
        _mod_kernel,
        out_shape=jax.ShapeDtypeStruct((depth, 16, n), F32),
        grid=(depth, n // nb),
        in_specs=[pl.BlockSpec((16, d), lambda l, j: (0, 0)),
                  pl.BlockSpec((1, d, nb), lambda l, j: (l, 0, j)),
                  pl.BlockSpec((1, 1, nb), lambda l, j: (l, 0, j))],
        out_specs=pl.BlockSpec((1, 16, nb), lambda l, j: (l, 0, j)),
        compiler_params=_cparams(("arbitrary", "arbitrary")),
        name="modulation",
    )(c_all, w_mod, b_mod.reshape(depth, 1, n))


def _inproj_kernel(*refs, has_prev):
    if has_prev:
        x_ref, f_ref, modp_ref = refs[:3]
        refs = refs[3:]
    else:
        x_ref = refs[0]
        refs = refs[1:]
    (mod_ref, n1_ref, w_ref, qn_ref, kvn_ref, wuq_ref, wukv_ref,
     cosa_ref, sina_ref, cosc_ref, sinc_ref, cosd_ref, sind_ref) = refs[:13]
    outs = refs[13:]
    if has_prev:
        xo_ref = outs[0]
        outs = outs[1:]
    (qa_ref, ka_ref, va_ref, bq_ref, bff_ref, bfb_ref, bv_ref, bg_ref,
     cq_ref, ck_ref, cv_ref, dq_ref, dk_ref, dv_ref) = outs

    x = x_ref[0]
    if has_prev:
        x = x + modp_ref[0, 0, 5:6, :] * f_ref[0]
        xo_ref[0] = x
    m = mod_ref[0, 0]
    h = (_rms(x, n1_ref[...]) * (1.0 + m[1:2]) + m[0:1]).astype(BF16)

    z = _dot(h, w_ref[:, OFF_A:OFF_A + 768])
    cosa, sina = cosa_ref[...], sina_ref[...]
    qa_ref[0] = (_rope(z[:, 0:256], cosa, sina, 16) * (A_HEAD_DIM ** -0.5)).astype(BF16)
    ka_ref[0] = _rope(z[:, 256:512], cosa, sina, 16).astype(BF16)
    va_ref[0] = z[:, 512:768].astype(BF16)

    z = _dot(h, w_ref[:, OFF_B:OFF_B + 1280])
    bq_ref[0] = z[:, 0:256].astype(BF16)
    bff_ref[0] = z[:, 256:512]
    bfb_ref[0] = z[:, 512:768]
    bv_ref[0] = z[:, 768:1024].astype(BF16)
    bg_ref[0] = z[:, 1024:1280].astype(BF16)

    z = _dot(h, w_ref[:, OFF_C:OFF_C + 896])
    cosc, sinc = cosc_ref[...], sinc_ref[...]
    cq = _rms(z[:, 0:256], qn_ref[...]).astype(BF16)
    q = _rope(_dot(cq, wuq_ref[...]), cosc, sinc, 8)
    cq_ref[0] = (q * ((C_NOPE + C_ROPE) ** -0.5 * LOG2E)).astype(BF16)
    ckv = _rms(z[:, 256:384], kvn_ref[...]).astype(BF16)
    kv = _dot(ckv, wukv_ref[...])
    ck_ref[0] = (kv[:, 0:512] + _rope(z[:, 384:896], cosc, sinc, 8)).astype(BF16)
    cv_ref[0] = kv[:, 512:768].astype(BF16)

    z = _dot(h, w_ref[:, OFF_D:OFF_D + 768])
    cosd, sind = cosd_ref[...], sind_ref[...]
    dq_ref[0] = (_rope(z[:, 0:256], cosd, sind, 8) * (D_HEAD_DIM ** -0.5 * LOG2E)).astype(BF16)
    dk_ref[0] = _rope(z[:, 256:512], cosd, sind, 8).astype(BF16)
    dv_ref[0] = z[:, 512:768].astype(BF16)


def _inproj(x, f_prev, mod_prev, mod, n1, w_perm, qn, kvn, wuq, wukv, tabs, n_lat_tiles):
    b, t, d = x.shape
    nt = t // TM
    has_prev = f_prev is not None
    tile = lambda w: pl.BlockSpec((1, TM, w), lambda bi, i: (bi, i, 0))
    modspec = pl.BlockSpec((1, 1, 8, d), lambda bi, i: (bi, i // n_lat_tiles, 0, 0))
    tab = lambda w: pl.BlockSpec((TM, w), lambda bi, i: (i, 0))
    in_specs, args = [tile(d)], [x]
    if has_prev:
        in_specs += [tile(d), modspec]
        args += [f_prev, mod_prev]
    in_specs += [modspec, _const_spec((1, d)), _const_spec((d, N_PERM)), _const_spec((1, 256)),
                 _const_spec((1, 128)), _const_spec((256, 512)), _const_spec((128, 768)),
                 tab(256), tab(256), tab(512), tab(512), tab(256), tab(256)]
    args += [mod, n1, w_perm, qn, kvn, wuq, wukv, *tabs]
    widths = [(256, BF16), (256, BF16), (256, BF16),
              (256, BF16), (256, F32), (256, F32), (256, BF16), (256, BF16),
              (512, BF16), (512, BF16), (256, BF16),
              (256, BF16), (256, BF16), (256, BF16)]
    out_shape = [jax.ShapeDtypeStruct((b, t, w), dt) for w, dt in widths]
    out_specs = [tile(w) for w, _ in widths]
    if has_prev:
        out_shape = [jax.ShapeDtypeStruct((b, t, d), F32)] + out_shape
        out_specs = [tile(d)] + out_specs
    res = pl.pallas_call(
        functools.partial(_inproj_kernel, has_prev=has_prev),
        out_shape=out_shape, grid=(b, nt), in_specs=in_specs, out_specs=out_specs,
        compiler_params=_cparams(("parallel", "parallel")),
        name="inproj",
    )(*args)
    if has_prev:
        return res[0], res[1:]
    return x, res


def _win_kernel(sink_ref, q_ref, k_ref, v_ref, o_ref, *, n_lat, t_all):
    w = A_WINDOW
    n = pl.program_id(1)
    nb_lat = n_lat // w
    start = pl.multiple_of(jnp.clip((n - 1) * w, 0, n_lat - 3 * w), w)
    q = q_ref[0]
    lane = lax.broadcasted_iota(jnp.int32, (w, LANES), 1)
    row = lax.broadcasted_iota(jnp.int32, (2 * w, 3 * w), 0)
    col = lax.broadcasted_iota(jnp.int32, (2 * w, 3 * w), 1)
    rel = (start + col) - (n * w + row % w)
    valid = (jnp.abs(rel) <= w) & (n < nb_lat)
    row1 = lax.broadcasted_iota(jnp.int32, (2 * w, 1), 0)
    for j in range(A_KV_HEADS):
        sl = slice(LANES * j, LANES * (j + 1))
        qp = q[:, sl]
        zero = jnp.zeros_like(qp)
        lhs = jnp.concatenate([jnp.where(lane < 64, qp, zero), jnp.where(lane >= 64, qp, zero)], axis=0)
        sb = _dot_nt(lhs, k_ref[0, pl.ds(start, 3 * w), sl])
        sc = _dot_nt(lhs, k_ref[0, n_lat:t_all, sl])
        sb = jnp.where(valid, sb, MASK_VALUE)
        sink = jnp.where(row1 < w, sink_ref[2 * j], sink_ref[2 * j + 1])
        m = jnp.maximum(jnp.maximum(jnp.max(sb, axis=1, keepdims=True), jnp.max(sc, axis=1, keepdims=True)), sink)
        pb = jnp.exp(sb - m)
        pc = jnp.exp(sc - m)
        den = jnp.sum(pb, axis=1, keepdims=True) + jnp.sum(pc, axis=1, keepdims=True) + jnp.exp(sink - m)
        o = (_dot(pb.astype(BF16), v_ref[0, pl.ds(start, 3 * w), sl])
             + _dot(pc.astype(BF16), v_ref[0, n_lat:t_all, sl])) / den
        o_ref[0, :, sl] = jnp.where(lane < 64, o[:w], o[w:]).astype(o_ref.dtype)


def _win_attention(sink, q, k, v, n_lat):
    b, t, _ = q.shape
    w = A_WINDOW
    return pl.pallas_call(
        functools.partial(_win_kernel, n_lat=n_lat, t_all=t),
        out_shape=jax.ShapeDtypeStruct((b, t, 256), BF16),
        grid=(b, t // w),
        in_specs=[pl.BlockSpec(memory_space=pltpu.SMEM),
                  pl.BlockSpec((1, w, 256), lambda bi, i: (bi, i, 0)),
                  pl.BlockSpec((1, t, 256), lambda bi, i: (bi, 0, 0)),
                  pl.BlockSpec((1, t, 256), lambda bi, i: (bi, 0, 0))],
        out_specs=pl.BlockSpec((1, w, 256), lambda bi, i: (bi, i, 0)),
        compiler_params=_cparams(("parallel", "arbitrary")),
        name="win_attention",
    )(sink, q, k, v)


def _hgrn_level_matrix(rev):
    c = HG_CHUNK
    m = np.zeros((8, c, c), np.float32)
    for lvl in range(HG_LEVELS):
        size = c >> lvl
        for t in range(c):
            mid = (t // size) * size + size // 2
            upper = t >= mid
            if not rev:
                rng = range(mid, t + 1) if upper else range(t + 1, mid)
            else:
                rng = range(mid, t) if upper else range(t, mid)
            m[lvl, t, list(rng)] = 1.0
    for t in range(c):
        if not rev:
            m[6, t, :t + 1] = 1.0
            m[7, t, t + 1:] = 1.0
        else:
            m[6, t, t:] = 1.0
            m[7, t, :t] = 1.0
    m = m.reshape(8 * c, c)
    return np.concatenate([m, m], axis=1)


def _hgrn_chunk(q, k, logf, v, mlev, rev):
    c = HG_CHUNK
    hi = logf.astype(BF16)
    lo = (logf - hi.astype(F32)).astype(BF16)
    wgt = jnp.exp(_dot(mlev, jnp.concatenate([hi, lo], axis=0)))
    t = lax.broadcasted_iota(jnp.int32, (c, 256), 0)
    s_col = lax.broadcasted_iota(jnp.int32, (c, 256), 1) % c
    r_bd = lax.broadcasted_iota(jnp.int32, (256, 256), 0) // 64
    c_bd = lax.broadcasted_iota(jnp.int32, (256, 256), 1) // 64
    bd = r_bd == c_bd

    def block_diag(a):
        return jnp.where(bd, jnp.concatenate([a] * 4, axis=0), 0.0).astype(BF16)

    scores = jnp.where(t == s_col, _dot_nt(q.astype(BF16), block_diag(k)), 0.0)
    for lvl in range(HG_LEVELS):
        wl = wgt[c * lvl:c * (lvl + 1)]
        q_side = ((t >> (HG_LEVELS - 1 - lvl)) & 1) == (0 if rev else 1)
        ql = jnp.where(q_side, q * wl, 0.0).astype(BF16)
        kl = jnp.where(q_side, 0.0, k * wl)
        s = _dot_nt(ql, block_diag(kl))
        if lvl > 0:
            sh = HG_LEVELS - lvl
            s = jnp.where((t >> sh) == (s_col >> sh), s, 0.0)
        scores = scores + s
    q_in = (q * wgt[6 * c:7 * c]).astype(BF16)
    k_end = (k * wgt[7 * c:8 * c]).astype(BF16)
    o_intra = _dot(scores.astype(BF16), block_diag(v))
    edge = 6 * c if rev else 7 * c - 1
    total = wgt[edge:edge + 1]
    update = jnp.where(bd, _dot_tn(v.astype(BF16), k_end), 0.0)
    return o_intra, q_in, update, total


def _hgrn_kernel(mf_ref, mb_ref, lb_ref, qf_ref, zf_ref, vf_ref, qb_ref, zb_ref, vb_ref,
                 of_ref, ob_ref, stf_ref, stb_ref):
    c = HG_CHUNK
    nc = TM // c

    @pl.when(pl.program_id(1) == 0)
    def _():
        stf_ref[...] = jnp.zeros_like(stf_ref)
        stb_ref[...] = jnp.zeros_like(stb_ref)

    def scan(q_ref, z_ref, v_ref, lb, mlev, o_ref, st_ref, rev):
        sig = _sigmoid(z_ref[0])
        logf = jnp.log(jnp.maximum(lb + (1.0 - lb) * sig, GATE_FLOOR))
        k = (1.0 - lb) * (1.0 - sig)
        order = range(nc - 1, -1, -1) if rev else range(nc)
        rows = [slice(ci * c, (ci + 1) * c) for ci in order]
        parts = [_hgrn_chunk(q_ref[0, r, :].astype(F32), k[r], logf[r], v_ref[0, r, :].astype(F32), mlev, rev)
                 for r in rows]
        st = st_ref[...]
        for r, (o_intra, q_in, update, total) in zip(rows, parts):
            o_ref[0, r, :] = o_intra + _dot_nt(q_in, st.astype(BF16))
            st = st * total + update
        st_ref[...] = st

    scan(qf_ref, zf_ref, vf_ref, lb_ref[0:1], mf_ref[...], of_ref, stf_ref, False)
    scan(qb_ref, zb_ref, vb_ref, lb_ref[1:2], mb_ref[...], ob_ref, stb_ref, True)


def _hgrn(lb, q, zff, zfb, v, n_lat):
    b, t, _ = q.shape
    nlb = n_lat // TM
    ncb = (t - n_lat) // TM
    fwd = lambda bi, i: (bi, jnp.where(i < ncb, nlb + i, i - ncb), 0)
    bwd = lambda bi, i: (bi, nlb + ncb - 1 - i, 0)
    blk = lambda im: pl.BlockSpec((1, TM, 256), im)
    mf = jnp.asarray(_hgrn_level_matrix(False), BF16)
    mb = jnp.asarray(_hgrn_level_matrix(True), BF16)
    return pl.pallas_call(
        _hgrn_kernel,
        out_shape=[jax.ShapeDtypeStruct((b, t, 256), F32)] * 2,
        grid=(b, t // TM),
        in_specs=[_const_spec((8 * HG_CHUNK, 2 * HG_CHUNK)), _const_spec((8 * HG_CHUNK, 2 * HG_CHUNK)),
                  _const_spec((2, 256)),
                  blk(fwd), blk(fwd), blk(fwd), blk(bwd), blk(bwd), blk(bwd)],
        out_specs=[blk(fwd), blk(bwd)],
        scratch_shapes=[pltpu.VMEM((256, 256), F32), pltpu.VMEM((256, 256), F32)],
        compiler_params=_cparams(("parallel", "arbitrary")),
        name="hgrn",
    )(mf, mb, lb, q, zff, v, q, zfb, v)


def _key_chunks(n_lat, t_all):
    chunks = [(lo, min(lo + KV_CHUNK, t_all)) for lo in range(0, t_all, KV_CHUNK)]
    return chunks, [(n_lat, t_all)]


def _online_softmax(lhs, k_ref, v_ref, chunks):
    rows = lhs.shape[0]
    m = jnp.full((rows, 1), -jnp.inf, F32)
    l = jnp.zeros((rows, 1), F32)
    acc = jnp.zeros((rows, LANES), F32)
    for lo, hi in chunks:
        s = _dot_nt(lhs, k_ref[0, lo:hi, :])
        m_new = jnp.maximum(m, jnp.max(s, axis=1, keepdims=True))
        alpha = jnp.exp2(m - m_new)
        p = jnp.exp2(s - m_new)
        l = alpha * l + jnp.sum(p, axis=1, keepdims=True)
        acc = alpha * acc + _dot(p.astype(BF16), v_ref[0, lo:hi, :])
        m = m_new
    return acc / l


def _mla_kernel(q_ref, k_ref, v_ref, o_ref, *, chunks):
    tq = q_ref.shape[1]
    q = q_ref[0]
    zero = jnp.zeros_like(q)
    lane2 = lax.broadcasted_iota(jnp.int32, q.shape, 1)
    lhs = jnp.concatenate([jnp.where(lane2 < LANES, q, zero), jnp.where(lane2 >= LANES, q, zero)], axis=0)
    o = _online_softmax(lhs, k_ref, v_ref, chunks)
    lane = lax.broadcasted_iota(jnp.int32, (tq, LANES), 1)
    o_ref[0] = jnp.where(lane < 64, o[:tq], o[tq:]).astype(o_ref.dtype)


def _diff_kernel(lam_ref, g_ref, q_ref, k_ref, v_ref, o_ref, *, chunks, lam_init):
    tq = q_ref.shape[1]
    q = q_ref[0]
    zero = jnp.zeros_like(q)
    lane = lax.broadcasted_iota(jnp.int32, (tq, LANES), 1)
    lhs = jnp.concatenate([jnp.where(lane // D_HEAD_DIM == r, q, zero) for r in range(4)], axis=0)
    o = _online_softmax(lhs, k_ref, v_ref, chunks)
    lam = lam_ref[0]
    o0 = o[0:tq] - lam * o[tq:2 * tq]
    o1 = o[2 * tq:3 * tq] - lam * o[3 * tq:4 * tq]
    o = jnp.where(lane < 64, o0, o1)
    o_ref[0] = (_head_rms(o, g_ref[...], LANES) * (1.0 - lam_init)).astype(o_ref.dtype)


def _attention_call(body, name, extras, extra_specs, q, k, v, tq, row0, n_rows, out_init):
    b, t, qw = q.shape
    blk0 = row0 // tq
    n_extra = len(extras)
    in_specs = list(extra_specs) + [
        pl.BlockSpec((1, tq, qw // 2), lambda bi, p, i: (bi, blk0 + i, p)),
        pl.BlockSpec((1, t, k.shape[2] // 2), lambda bi, p, i: (bi, 0, p)),
        pl.BlockSpec((1, t, LANES), lambda bi, p, i: (bi, 0, p))]
    args = list(extras) + [q, k, v]
    aliases = {}
    if out_init is not None:
        in_specs.append(pl.BlockSpec(memory_space=pl.ANY))
        aliases = {len(args): 0}
        args.append(out_init)

    def kern(*refs):
        body(*refs[:n_extra + 3], refs[-1])

    return pl.pallas_call(
        kern,
        out_shape=jax.ShapeDtypeStruct((b, t, 256), BF16),
        grid=(b, 2, n_rows // tq),
        in_specs=in_specs,
        out_specs=pl.BlockSpec((1, tq, LANES), lambda bi, p, i: (bi, blk0 + i, p)),
        input_output_aliases=aliases,
        compiler_params=_cparams(("parallel", "parallel", "arbitrary")),
        name=name,
    )(*args)


def _two_pass_attention(body_of, name, extras, extra_specs, q, k, v, n_lat, tq_lat):
    t = q.shape[1]
    lat_chunks, ctx_chunks = _key_chunks(n_lat, t)
    out = jnp.zeros((q.shape[0], t, 256), BF16)
    out = _attention_call(body_of(lat_chunks), name, extras, extra_specs, q, k, v, tq_lat, 0, n_lat, out)
    return _attention_call(body_of(ctx_chunks), name + "_ctx", extras, extra_specs, q, k, v,
                           TQ, n_lat, t - n_lat, out)


def _mla_attention(q, k, v, n_lat):
    body_of = lambda chunks: functools.partial(_mla_kernel, chunks=chunks)
    return _two_pass_attention(body_of, "mla_attention", [], [], q, k, v, n_lat, TQ_MLA)


def _diff_attention(lam, q, k, v, gain, n_lat, lam_init):
    body_of = lambda chunks: functools.partial(_diff_kernel, chunks=chunks, lam_init=lam_init)
    specs = [pl.BlockSpec(memory_space=pltpu.SMEM), _const_spec((1, LANES))]
    return _two_pass_attention(body_of, "diff_attention", [lam, gain], specs, q, k, v, n_lat, TQ)


def _merge_kernel(x_ref, mod_ref, n1_ref, n2_ref, a_ref, of_ref, ob_ref, bg_ref, hg_ref, c_ref, d_ref,
                  wg_ref, wb_ref, wo_ref, wr_ref, br_ref, xo_ref, h2_ref, ri_ref, rw_ref):
    d = D_MODEL
    x = x_ref[0]
    m = mod_ref[0, 0]
    h = (_rms(x, n1_ref[...]) * (1.0 + m[1:2]) + m[0:1]).astype(BF16)
    g = bg_ref[0].astype(F32)
    b_out = _head_rms(of_ref[0] + ob_ref[0], hg_ref[...], 256) * (g * _sigmoid(g))
    branches = (a_ref[0], b_out.astype(BF16), c_ref[0], d_ref[0])
    y = jnp.zeros((TM, d), F32)
    for i, br in enumerate(branches):
        gate = _sigmoid(_dot(h, wg_ref[:, d * i:d * (i + 1)]))
        y = y + gate * _dot(br, wb_ref[i])
    x = x + m[2:3] * _dot(y.astype(BF16), wo_ref[...])
    xo_ref[0] = x
    h2 = _rms(x, n2_ref[...]) * (1.0 + m[4:5]) + m[3:4]
    h2b = h2.astype(BF16)
    h2_ref[0] = h2

    h2lo = (h2 - h2b.astype(F32)).astype(BF16)
    r = _dot(jnp.concatenate([h2b, h2lo], axis=0), wr_ref[...])
    logits = r[:TM, :LANES] + r[:TM, LANES:] + r[TM:, :LANES] + br_ref[...]
    lane = lax.broadcasted_iota(jnp.int32, (TM, LANES), 1)
    neg = -jnp.inf
    is_g = (lane >= N_EXPERTS) & (lane < N_EXPERTS + N_GROUPS)
    lg = jnp.where(is_g, logits, neg)
    mg = jnp.max(lg, axis=1, keepdims=True)
    g_val = 1.0 / jnp.sum(jnp.exp(lg - mg), axis=1, keepdims=True)
    g_idx = jnp.min(jnp.where(lg == mg, lane, 4 * LANES), axis=1, keepdims=True) - N_EXPERTS
    in_group = (lane >= g_idx * EXPERTS_PER_GROUP) & (lane < (g_idx + 1) * EXPERTS_PER_GROUP)
    le = jnp.where(in_group, logits, neg)
    m1 = jnp.max(le, axis=1, keepdims=True)
    e1 = jnp.min(jnp.where(le == m1, lane, 4 * LANES), axis=1, keepdims=True)
    le2 = jnp.where(lane == e1, neg, le)
    m2 = jnp.max(le2, axis=1, keepdims=True)
    e2 = jnp.min(jnp.where(le2 == m2, lane, 4 * LANES), axis=1, keepdims=True)
    r2 = jnp.exp(m2 - m1)
    v1 = g_val / (1.0 + r2)
    ri_ref[0] = jnp.where(lane == 0, e1, jnp.where(lane == 1, e2, 0))
    rw_ref[0] = jnp.where(lane == 0, v1, jnp.where(lane == 1, v1 * r2, 0.0))


def _merge(x, mod, n1, n2, a_o, o_f, o_b, bg, hg, c_o, d_o, wg, wb, wo, wr, br, n_lat_tiles):
    b, t, d = x.shape
    tile = lambda w: pl.BlockSpec((1, TM, w), lambda bi, i: (bi, i, 0))
    return pl.pallas_call(
        _merge_kernel,
        out_shape=[jax.ShapeDtypeStruct((b, t, d), F32), jax.ShapeDtypeStruct((b, t, d), F32),
                   jax.ShapeDtypeStruct((b, t, LANES), jnp.int32), jax.ShapeDtypeStruct((b, t, LANES), F32)],
        grid=(b, t // TM),
        in_specs=[tile(d), pl.BlockSpec((1, 1, 8, d), lambda bi, i: (bi, i // n_lat_tiles, 0, 0)),
                  _const_spec((1, d)), _const_spec((1, d)),
                  tile(256), tile(256), tile(256), tile(256), _const_spec((1, 256)), tile(256), tile(256),
                  _const_spec((d, 4 * d)), _const_spec((4, 256, d)), _const_spec((d, d)),
                  _const_spec((d, 256)), _const_spec((1, LANES))],
        out_specs=[tile(d), tile(d), tile(LANES), tile(LANES)],
        compiler_params=_cparams(("parallel", "parallel")),
        name="merge",
    )(x, mod, n1, n2, a_o, o_f, o_b, bg, hg, c_o, d_o, wg, wb, wo, wr, br)


def _route_positions(ri, n_tok):
    ef = ri[:, :2].reshape(-1)
    rb = 2 * TM
    oh = (ef[:, None] == jnp.arange(N_EXPERTS, dtype=jnp.int32)[None, :]).astype(F32).reshape(-1, rb, N_EXPERTS)
    tri = (jnp.arange(rb)[:, None] >= jnp.arange(rb)[None, :]).astype(F32)
    within = jnp.einsum("ij,gje->gie", tri, oh)
    tot = within[:, -1, :]
    before = jnp.cumsum(tot, axis=0) - tot
    cnt = jnp.sum(tot, axis=0).astype(jnp.int32)
    pcnt = ((cnt + MOE_TILE - 1) // MOE_TILE) * MOE_TILE
    end = jnp.cumsum(pcnt)
    start = (end - pcnt).astype(F32)
    pos = jnp.sum((within + (before + start[None, :])[:, None, :]) * oh, axis=2) - 1.0
    pos = pos.reshape(-1)
    n_tiles = 2 * n_tok // MOE_TILE + N_EXPERTS
    n_act = (end[-1] // MOE_TILE).astype(jnp.int32)
    tile_row = jnp.minimum(jnp.arange(n_tiles, dtype=jnp.int32), n_act - 1) * MOE_TILE
    tile_expert = jnp.sum((end[None, :] <= tile_row[:, None]).astype(jnp.int32), axis=1)
    tile_expert = jnp.minimum(tile_expert, N_EXPERTS - 1)
    return pos.reshape(n_tok // TM, 1, 2 * TM).astype(jnp.int32), tile_expert, n_act.reshape(1)


def _dispatch_kernel(pos_ref, h_ref, xs_in_ref, xs_ref, sem):
    del xs_in_ref

    def body(r, carry):
        for s in range(2):
            p = pos_ref[0, 0, 2 * r + s]
            pltpu.make_async_copy(h_ref.at[pl.ds(r, 1)], xs_ref.at[pl.ds(p, 1)], sem.at[s]).start()
        return carry

    lax.fori_loop(0, TM, body, 0)
    for s in range(2):
        pltpu.make_async_copy(h_ref, h_ref, sem.at[s]).wait()


def _dispatch(pos, h2, xs_init):
    n, d = h2.shape
    return pl.pallas_call(
        _dispatch_kernel,
        out_shape=jax.ShapeDtypeStruct(xs_init.shape, F32),
        grid=(n // TM,),
        in_specs=[pl.BlockSpec((1, 1, 2 * TM), lambda t: (t, 0, 0), memory_space=pltpu.SMEM),
                  pl.BlockSpec((TM, d), lambda t: (t, 0)),
                  pl.BlockSpec(memory_space=pl.ANY)],
        out_specs=pl.BlockSpec(memory_space=pl.ANY),
        scratch_shapes=[pltpu.SemaphoreType.DMA((2,))],
        input_output_aliases={2: 0},
        compiler_params=_cparams(("arbitrary",)),
        name="moe_dispatch",
    )(pos, h2, xs_init)


def _experts_kernel(te_ref, na_ref, x_ref, wgu_ref, wd_ref, o_ref):
    del te_ref
    t = pl.program_id(0)

    @pl.when(t < na_ref[0])
    def _():
        gu = _dot(x_ref[...].astype(BF16), wgu_ref[0])
        gt, up = gu[:, :EXPERT_HIDDEN], gu[:, EXPERT_HIDDEN:]
        o_ref[...] = _dot((gt * _sigmoid(gt) * up).astype(BF16), wd_ref[0])

    @pl.when(t >= na_ref[0])
    def _():
        o_ref[...] = jnp.zeros_like(o_ref)


def _experts(tile_expert, n_act, xs, wgu, wd):
    p, d = xs.shape
    return pl.pallas_call(
        _experts_kernel,
        out_shape=jax.ShapeDtypeStruct((p, d), F32),
        grid_spec=pltpu.PrefetchScalarGridSpec(
            num_scalar_prefetch=2, grid=(p // MOE_TILE,),
            in_specs=[pl.BlockSpec((MOE_TILE, d), lambda t, te, na: (t, 0)),
                      pl.BlockSpec((1, d, 2 * EXPERT_HIDDEN), lambda t, te, na: (te[t], 0, 0)),
                      pl.BlockSpec((1, EXPERT_HIDDEN, d), lambda t, te, na: (te[t], 0, 0))],
            out_specs=pl.BlockSpec((MOE_TILE, d), lambda t, te, na: (t, 0))),
        compiler_params=_cparams(("arbitrary",)),
        name="moe_experts",
    )(tile_expert, n_act, xs, wgu, wd)


def _combine_kernel(pos_ref, ys_ref, w_ref, o_ref, buf, sem):
    def body(r, carry):
        for s in range(2):
            p = pos_ref[0, 0, 2 * r + s]
            pltpu.make_async_copy(ys_ref.at[pl.ds(p, 1)], buf.at[s, pl.ds(r, 1)], sem.at[s]).start()
        return carry

    lax.fori_loop(0, TM, body, 0)
    for s in range(2):
        pltpu.make_async_copy(buf.at[s], buf.at[s], sem.at[s]).wait()
    w = w_ref[...]
    o_ref[...] = w[:, 0:1] * buf[0] + w[:, 1:2] * buf[1]


def _combine(pos, ys, rwt):
    n = rwt.shape[0]
    d = ys.shape[1]
    return pl.pallas_call(
        _combine_kernel,
        out_shape=jax.ShapeDtypeStruct((n, d), F32),
        grid=(n // TM,),
        in_specs=[pl.BlockSpec((1, 1, 2 * TM), lambda t: (t, 0, 0), memory_space=pltpu.SMEM),
                  pl.BlockSpec(memory_space=pl.ANY),
                  pl.BlockSpec((TM, LANES), lambda t: (t, 0))],
        out_specs=pl.BlockSpec((TM, d), lambda t: (t, 0)),
        scratch_shapes=[pltpu.VMEM((2, TM, d), F32), pltpu.SemaphoreType.DMA((2,))],
        compiler_params=_cparams(("arbitrary",)),
        name="moe_combine",
    )(pos, ys, rwt)


def _moe(h2, ri, rwt, wgu, wd, xs_init):
    n = h2.shape[0]
    pos, tile_expert, n_act = _route_positions(ri, n)
    xs = _dispatch(pos, h2, xs_init)
    ys = _experts(tile_expert, n_act, xs, wgu, wd)
    return _combine(pos, ys, rwt), xs


def _final_kernel(x_ref, f_ref, mod_ref, g_ref, o_ref):
    x = x_ref[0] + mod_ref[0, 0, 5:6, :] * f_ref[0]
    o_ref[0] = _rms(x, g_ref[...])


def _final(x, f, mod, gain, n_lat):
    b, t, d = x.shape
    tile = pl.BlockSpec((1, TM, d), lambda bi, i: (bi, i, 0))
    return pl.pallas_call(
        _final_kernel,
        out_shape=jax.ShapeDtypeStruct((b, n_lat, d), F32),
        grid=(b, n_lat // TM),
        in_specs=[tile, tile, pl.BlockSpec((1, 1, 8, d), lambda bi, i: (bi, 0, 0, 0)), _const_spec((1, d))],
        out_specs=tile,
        compiler_params=_cparams(("parallel", "parallel")),
        name="final_norm",
    )(x, f, mod, gain)


def _rope_tables(n_lat, t_all, lane_rope, lane_off, dr):
    half, quarter = dr // 2, dr // 4
    inv_freq = 1.0 / (ROPE_BASE ** (jnp.arange(quarter, dtype=F32) / quarter))
    off = np.asarray(lane_off)
    use_col = off >= half
    j = (off % half) % quarter
    first = (off % half) < quarter
    tok = jnp.arange(n_lat, dtype=jnp.int32)
    row = (tok // GRID_W).astype(F32)
    col = (tok % GRID_W).astype(F32)
    pos = jnp.where(jnp.asarray(use_col)[None, :], col[:, None], row[:, None])
    ang = pos * inv_freq[jnp.asarray(j)][None, :]
    rope = jnp.asarray(lane_rope)[None, :]
    cos = jnp.where(rope, jnp.cos(ang), 1.0)
    sin = jnp.where(rope, jnp.sin(ang) * jnp.where(jnp.asarray(first), -1.0, 1.0)[None, :], 0.0)
    pad = t_all - n_lat
    w = off.shape[0]
    cos = jnp.concatenate([cos, jnp.ones((pad, w), F32)], axis=0)
    sin = jnp.concatenate([sin, jnp.zeros((pad, w), F32)], axis=0)
    return cos, sin


def _all_rope_tables(n_lat, t_all):
    la = np.arange(256)
    ta = _rope_tables(n_lat, t_all, np.ones(256, bool), la % 64, 64)
    lc = np.arange(512) % 128
    tc = _rope_tables(n_lat, t_all, (lc >= 64) & (lc < 96), np.clip(lc - 64, 0, 31), 32)
    td = _rope_tables(n_lat, t_all, np.ones(256, bool), la % 32, 32)
    return (*ta, *tc, *td)


def _permute_w_in(w):
    s = np.cumsum([0, 256, 128, 128, 256, 256, 256, 256, 256, 256, 128, 32, 256, 256, 256])
    seg = lambda i: w[:, s[i]:s[i + 1]]
    dup = lambda a: jnp.concatenate([a[:, 0:64], a[:, 0:64], a[:, 64:128], a[:, 64:128]], axis=1)
    z64 = jnp.zeros((w.shape[0], 64), w.dtype)
    z32 = jnp.zeros((w.shape[0], 32), w.dtype)
    kr = jnp.concatenate([z64, seg(10), z32] * C_HEADS, axis=1)
    cols = [seg(0), dup(seg(1)), dup(seg(2)),
            seg(3), seg(4), seg(5), seg(6), seg(7),
            seg(8), seg(9), kr,
            seg(11), seg(12), seg(13)]
    return jnp.concatenate(cols, axis=1).astype(BF16)


def _permute_mla(w_uq, w_ukv):
    z32 = jnp.zeros((w_uq.shape[0], 32), w_uq.dtype)
    qd = C_NOPE + C_ROPE
    uq = jnp.concatenate([a for h in range(C_HEADS) for a in (w_uq[:, qd * h:qd * (h + 1)], z32)], axis=1)
    z64 = jnp.zeros((w_ukv.shape[0], 64), w_ukv.dtype)
    kd = C_NOPE + C_V
    uk = jnp.concatenate([a for h in range(C_HEADS) for a in (w_ukv[:, kd * h:kd * h + C_NOPE], z64)], axis=1)
    uv = jnp.concatenate([w_ukv[:, kd * h + C_NOPE:kd * (h + 1)] for h in range(C_HEADS)], axis=1)
    return uq.astype(BF16), jnp.concatenate([uk, uv], axis=1).astype(BF16)


def kernel(x, c, ctx, c_ctx, w_mod, b_mod, norm1, norm2, w_in, w_gate, w_branch, w_out, attn_sink, hgrn_lb_logits, hgrn_norm, mla_q_norm, mla_kv_norm, mla_w_uq, mla_w_ukv, diff_lambda, diff_subln, w_router_group, b_router_group, w_router_expert, b_router_expert, w_expert_gate, w_expert_up, w_expert_down, final_norm):
    b, n_lat, d = x.shape
    n_ctx = ctx.shape[1]
    depth = w_mod.shape[0]
    t_all = n_lat + n_ctx
    assert d == D_MODEL and n_lat % TM == 0 and n_ctx % TM == 0 and n_lat % GRID_W == 0
    assert n_lat >= 3 * A_WINDOW and b <= 15
    n_lat_tiles = n_lat // TM

    xs = jnp.concatenate([x, ctx], axis=1)
    c_all = jnp.concatenate([c, c_ctx[None], jnp.zeros((15 - b, d), F32)], axis=0)
    mod = _modulation(c_all, w_mod, b_mod).reshape(depth, 16, 6, d)
    mod = jnp.pad(mod, ((0, 0), (0, 0), (0, 2), (0, 0)))
    mod = jnp.stack([mod[:, :b], jnp.broadcast_to(mod[:, b:b + 1], (depth, b, 8, d))], axis=2)

    sm = jax.nn.softmax(hgrn_lb_logits.astype(F32), axis=0)
    lower_bounds = jnp.cumsum(sm, axis=0) - sm[0]
    tabs = _all_rope_tables(n_lat, t_all)

    n_tok = b * t_all
    moe_buf = jnp.zeros((2 * n_tok + N_EXPERTS * MOE_TILE, d), F32)
    f_prev, mod_prev = None, None
    for li in range(depth):
        lam_init = 0.8 - 0.6 * math.exp(-0.3 * li)
        lp = diff_lambda[li].astype(F32)
        lam = (jnp.exp(jnp.sum(lp[0] * lp[1])) - jnp.exp(jnp.sum(lp[2] * lp[3])) + lam_init).reshape(1)
        w_perm = _permute_w_in(w_in[li])
        wuq, wukv = _permute_mla(mla_w_uq[li], mla_w_ukv[li])
        xs, z = _inproj(xs, f_prev, mod_prev, mod[li], norm1[li][None], w_perm,
                        mla_q_norm[li][None], mla_kv_norm[li][None], wuq, wukv, tabs, n_lat_tiles)
        qa, ka, va, bq, bff, bfb, bv, bg, cq, ck, cv, dq, dk, dv = z
        a_o = _win_attention(attn_sink[li].astype(F32), qa, ka, va, n_lat)
        o_f, o_b = _hgrn(lower_bounds[li], bq, bff, bfb, bv, n_lat)
        c_o = _mla_attention(cq, ck, cv, n_lat)
        d_o = _diff_attention(lam, dq, dk, dv, jnp.tile(diff_subln[li], 2)[None], n_lat, lam_init)
        wr = jnp.concatenate([w_router_expert[li], w_router_group[li],
                              jnp.zeros((d, LANES - N_EXPERTS - N_GROUPS), F32)], axis=1)
        wr_hi = wr.astype(BF16)
        wr_lo = (wr - wr_hi.astype(F32)).astype(BF16)
        br = jnp.concatenate([b_router_expert[li], b_router_group[li],
                              jnp.zeros((LANES - N_EXPERTS - N_GROUPS,), F32)])[None]
        xs, h2, ri, rwt = _merge(xs, mod[li], norm1[li][None], norm2[li][None], a_o, o_f, o_b, bg,
                                 jnp.tile(hgrn_norm[li], B_HEADS)[None], c_o, d_o,
                                 w_gate[li].astype(BF16), w_branch[li].astype(BF16), w_out[li].astype(BF16),
                                 jnp.concatenate([wr_hi, wr_lo], axis=1), br, n_lat_tiles)
        wgu = jnp.concatenate([w_expert_gate[li], w_expert_up[li]], axis=2).astype(BF16)
        f, moe_buf = _moe(h2.reshape(n_tok, d), ri.reshape(n_tok, LANES), rwt.reshape(n_tok, LANES),
                          wgu, w_expert_down[li].astype(BF16), moe_buf)
        f_prev, mod_prev = f.reshape(b, t_all, d), mod[li]
    return _final(xs, f_prev, mod_prev, final_norm[None], n_lat)
```

```python
import functools
import math

import numpy as np
import jax
import jax.numpy as jnp
from jax import lax
from jax.experimental import pallas as pl
from jax.experimental.pallas import tpu as pltpu

D_MODEL = 1024
GRID_W = 64
ROPE_BASE = 10000.0
NORM_EPS = 1e-6
MASK_VALUE = -1e30
GATE_FLOOR = 1e-30

A_HEADS, A_KV_HEADS, A_HEAD_DIM, A_WINDOW = 4, 2, 64, 128
B_HEADS, B_KEY_DIM, B_VAL_DIM = 4, 64, 64
C_HEADS, C_Q_LORA, C_KV_LORA, C_NOPE, C_ROPE, C_V = 4, 256, 128, 64, 32, 64
D_HEADS, D_HEAD_DIM = 4, 32
N_GROUPS, EXPERTS_PER_GROUP, EXPERT_HIDDEN = 4, 8, 256
N_EXPERTS = N_GROUPS * EXPERTS_PER_GROUP

TM = 256
TQ = 256
TQ_MLA = 512
LOG2E = 1.4426950408889634
KV_CHUNK = 4096
MOE_TILE = 256
HG_CHUNK = 64
HG_LEVELS = 6
LANES = 128
ROW_SLABS = D_MODEL // LANES
VMEM_LIMIT = 52 * 1024 * 1024

OFF_A, OFF_B, OFF_C, OFF_D, N_PERM = 0, 768, 2048, 2944, 3712

F32 = jnp.float32
BF16 = jnp.bfloat16


def _dot(a, b):
    return jnp.dot(a, b, preferred_element_type=F32)


def _dot_nt(a, b):
    return lax.dot_general(a, b, (((1,), (1,)), ((), ())), preferred_element_type=F32)


def _dot_tn(a, b):
    return lax.dot_general(a, b, (((0,), (0,)), ((), ())), preferred_element_type=F32)


def _dot_hi(a, b):
    return jnp.dot(a, b, preferred_element_type=F32, precision=lax.Precision.HIGHEST)


def _sigmoid(x):
    return 1.0 / (1.0 + jnp.exp(-x))


def _rms(x, gain):
    return x * lax.rsqrt(jnp.mean(x * x, axis=-1, keepdims=True) + NORM_EPS) * gain


def _head_rms(o, gain, width):
    r = lax.broadcasted_iota(jnp.int32, (width, width), 0) // 64
    c = lax.broadcasted_iota(jnp.int32, (width, width), 1) // 64
    ones = jnp.where(r == c, 1.0 / 64.0, 0.0).astype(F32)
    ms = _dot_hi(o * o, ones)
    return o * lax.rsqrt(ms + NORM_EPS) * gain


def _rope(x, cos, sin, quarter):
    w = x.shape[-1]
    lane = lax.broadcasted_iota(jnp.int32, x.shape, 1)
    first = (lane % (2 * quarter)) < quarter
    sw = jnp.where(first, pltpu.roll(x, w - quarter, 1), pltpu.roll(x, quarter, 1))
    return x * cos + sw * sin


def _slabs_first(x):
    return pltpu.einshape("tsl->stl", x)


def _rows_first(y):
    slabs = jnp.stack([y[:, LANES * s:LANES * (s + 1)] for s in range(ROW_SLABS)], axis=0)
    return pltpu.einshape("stl->tsl", slabs)


def _cparams(sem):
    return pltpu.CompilerParams(dimension_semantics=sem, vmem_limit_bytes=VMEM_LIMIT)


def _const_spec(shape):
    n = len(shape)
    return pl.BlockSpec(shape, lambda *_: (0,) * n)


def _mod_kernel(c_ref, w_ref, b_ref, o_ref):
    c = c_ref[...]
    o_ref[0] = _dot_hi(c * _sigmoid(c), w_ref[0]) + b_ref[0]


def _modulation(c_all, w_mod, b_mod):
    depth, d, n = w_mod.shape
    nb = 1536
    return pl.pallas_call(
        _mod_kernel,
        out_shape=jax.ShapeDtypeStruct((depth, 16, n), F32),
        grid=(depth, n // nb),
        in_specs=[pl.BlockSpec((16, d), lambda l, j: (0, 0)),
                  pl.BlockSpec((1, d, nb), lambda l, j: (l, 0, j)),
                  pl.BlockSpec((1, 1, nb), lambda l, j: (l, 0, j))],
        out_specs=pl.BlockSpec((1, 16, nb), lambda l, j: (l, 0, j)),
        compiler_params=_cparams(("arbitrary", "arbitrary")),
        name="modulation",
    )(c_all, w_mod, b_mod.reshape(depth, 1, n))


def _inproj_kernel(*refs, has_prev):
    if has_prev:
        x_ref, f_ref, modp_ref = refs[:3]
        refs = refs[3:]
    else:
        x_ref = refs[0]
        refs = refs[1:]
    (mod_ref, n1_ref, w_ref, qn_ref, kvn_ref, wuq_ref, wukv_ref,
     cosa_ref, sina_ref, cosc_ref, sinc_ref, cosd_ref, sind_ref) = refs[:13]
    outs = refs[13:]
    if has_prev:
        xo_ref = outs[0]
        outs = outs[1:]
    (qa_ref, ka_ref, va_ref, bq_ref, bff_ref, bfb_ref, bv_ref, bg_ref,
     cq_ref, ck_ref, cv_ref, dq_ref, dk_ref, dv_ref) = outs

    x = x_ref[0]
    if has_prev:
        x = x + modp_ref[0, 0, 5:6, :] * f_ref[0]
        xo_ref[0] = x
    m = mod_ref[0, 0]
    h = (_rms(x, n1_ref[...]) * (1.0 + m[1:2]) + m[0:1]).astype(BF16)

    z = _dot(h, w_ref[:, OFF_A:OFF_A + 768])
    cosa, sina = cosa_ref[...], sina_ref[...]
    qa_ref[0] = (_rope(z[:, 0:256], cosa, sina, 16) * (A_HEAD_DIM ** -0.5)).astype(BF16)
    ka_ref[0] = _rope(z[:, 256:512], cosa, sina, 16).astype(BF16)
    va_ref[0] = z[:, 512:768].astype(BF16)

    z = _dot(h, w_ref[:, OFF_B:OFF_B + 1280])
    bq_ref[0] = z[:, 0:256].astype(BF16)
    bff_ref[0] = z[:, 256:512]
    bfb_ref[0] = z[:, 512:768]
    bv_ref[0] = z[:, 768:1024].astype(BF16)
    bg_ref[0] = z[:, 1024:1280].astype(BF16)

    z = _dot(h, w_ref[:, OFF_C:OFF_C + 896])
    cosc, sinc = cosc_ref[...], sinc_ref[...]
    cq = _rms(z[:, 0:256], qn_ref[...]).astype(BF16)
    q = _rope(_dot(cq, wuq_ref[...]), cosc, sinc, 8)
    cq_ref[0] = (q * ((C_NOPE + C_ROPE) ** -0.5 * LOG2E)).astype(BF16)
    ckv = _rms(z[:, 256:384], kvn_ref[...]).astype(BF16)
    kv = _dot(ckv, wukv_ref[...])
    ck_ref[0] = (kv[:, 0:512] + _rope(z[:, 384:896], cosc, sinc, 8)).astype(BF16)
    cv_ref[0] = kv[:, 512:768].astype(BF16)

    z = _dot(h, w_ref[:, OFF_D:OFF_D + 768])
    cosd, sind = cosd_ref[...], sind_ref[...]
    dq_ref[0] = (_rope(z[:, 0:256], cosd, sind, 8) * (D_HEAD_DIM ** -0.5 * LOG2E)).astype(BF16)
    dk_ref[0] = _rope(z[:, 256:512], cosd, sind, 8).astype(BF16)
    dv_ref[0] = z[:, 512:768].astype(BF16)


def _inproj(x, f_prev, mod_prev, mod, n1, w_perm, qn, kvn, wuq, wukv, tabs, n_lat_tiles):
    b, t, d = x.shape
    nt = t // TM
    has_prev = f_prev is not None
    tile = lambda w: pl.BlockSpec((1, TM, w), lambda bi, i: (bi, i, 0))
    modspec = pl.BlockSpec((1, 1, 8, d), lambda bi, i: (bi, i // n_lat_tiles, 0, 0))
    tab = lambda w: pl.BlockSpec((TM, w), lambda bi, i: (i, 0))
    in_specs, args = [tile(d)], [x]
    if has_prev:
        in_specs += [tile(d), modspec]
        args += [f_prev, mod_prev]
    in_specs += [modspec, _const_spec((1, d)), _const_spec((d, N_PERM)), _const_spec((1, 256)),
                 _const_spec((1, 128)), _const_spec((256, 512)), _const_spec((128, 768)),
                 tab(256), tab(256), tab(512), tab(512), tab(256), tab(256)]
    args += [mod, n1, w_perm, qn, kvn, wuq, wukv, *tabs]
    widths = [(256, BF16), (256, BF16), (256, BF16),
              (256, BF16), (256, F32), (256, F32), (256, BF16), (256, BF16),
              (512, BF16), (512, BF16), (256, BF16),
              (256, BF16), (256, BF16), (256, BF16)]
    out_shape = [jax.ShapeDtypeStruct((b, t, w), dt) for w, dt in widths]
    out_specs = [tile(w) for w, _ in widths]
    if has_prev:
        out_shape = [jax.ShapeDtypeStruct((b, t, d), F32)] + out_shape
        out_specs = [tile(d)] + out_specs
    res = pl.pallas_call(
        functools.partial(_inproj_kernel, has_prev=has_prev),
        out_shape=out_shape, grid=(b, nt), in_specs=in_specs, out_specs=out_specs,
        compiler_params=_cparams(("parallel", "parallel")),
        name="inproj",
    )(*args)
    if has_prev:
        return res[0], res[1:]
    return x, res


def _win_kernel(sink_ref, q_ref, k_ref, v_ref, o_ref, *, n_lat, t_all):
    w = A_WINDOW
    n = pl.program_id(1)
    nb_lat = n_lat // w
    start = pl.multiple_of(jnp.clip((n - 1) * w, 0, n_lat - 3 * w), w)
    q = q_ref[0]
    lane = lax.broadcasted_iota(jnp.int32, (w, LANES), 1)
    row = lax.broadcasted_iota(jnp.int32, (2 * w, 3 * w), 0)
    col = lax.broadcasted_iota(jnp.int32, (2 * w, 3 * w), 1)
    rel = (start + col) - (n * w + row % w)
    valid = (jnp.abs(rel) <= w) & (n < nb_lat)
    row1 = lax.broadcasted_iota(jnp.int32, (2 * w, 1), 0)
    for j in range(A_KV_HEADS):
        sl = slice(LANES * j, LANES * (j + 1))
        qp = q[:, sl]
        zero = jnp.zeros_like(qp)
        lhs = jnp.concatenate([jnp.where(lane < 64, qp, zero), jnp.where(lane >= 64, qp, zero)], axis=0)
        sb = _dot_nt(lhs, k_ref[0, pl.ds(start, 3 * w), sl])
        sc = _dot_nt(lhs, k_ref[0, n_lat:t_all, sl])
        sb = jnp.where(valid, sb, MASK_VALUE)
        sink = jnp.where(row1 < w, sink_ref[2 * j], sink_ref[2 * j + 1])
        m = jnp.maximum(jnp.maximum(jnp.max(sb, axis=1, keepdims=True), jnp.max(sc, axis=1, keepdims=True)), sink)
        pb = jnp.exp(sb - m)
        pc = jnp.exp(sc - m)
        den = jnp.sum(pb, axis=1, keepdims=True) + jnp.sum(pc, axis=1, keepdims=True) + jnp.exp(sink - m)
        o = (_dot(pb.astype(BF16), v_ref[0, pl.ds(start, 3 * w), sl])
             + _dot(pc.astype(BF16), v_ref[0, n_lat:t_all, sl])) / den
        o_ref[0, :, sl] = jnp.where(lane < 64, o[:w], o[w:]).astype(o_ref.dtype)


def _win_attention(sink, q, k, v, n_lat):
    b, t, _ = q.shape
    w = A_WINDOW
    return pl.pallas_call(
        functools.partial(_win_kernel, n_lat=n_lat, t_all=t),
        out_shape=jax.ShapeDtypeStruct((b, t, 256), BF16),
        grid=(b, t // w),
        in_specs=[pl.BlockSpec(memory_space=pltpu.SMEM),
                  pl.BlockSpec((1, w, 256), lambda bi, i: (bi, i, 0)),
                  pl.BlockSpec((1, t, 256), lambda bi, i: (bi, 0, 0)),
                  pl.BlockSpec((1, t, 256), lambda bi, i: (bi, 0, 0))],
        out_specs=pl.BlockSpec((1, w, 256), lambda bi, i: (bi, i, 0)),
        compiler_params=_cparams(("parallel", "arbitrary")),
        name="win_attention",
    )(sink, q, k, v)


def _hgrn_level_matrix(rev):
    c = HG_CHUNK
    m = np.zeros((8, c, c), np.float32)
    for lvl in range(HG_LEVELS):
        size = c >> lvl
        for t in range(c):
            mid = (t // size) * size + size // 2
            upper = t >= mid
            if not rev:
                rng = range(mid, t + 1) if upper else range(t + 1, mid)
            else:
                rng = range(mid, t) if upper else range(t, mid)
            m[lvl, t, list(rng)] = 1.0
    for t in range(c):
        if not rev:
            m[6, t, :t + 1] = 1.0
            m[7, t, t + 1:] = 1.0
        else:
            m[6, t, t:] = 1.0
            m[7, t, :t] = 1.0
    m = m.reshape(8 * c, c)
    return np.concatenate([m, m], axis=1)


def _hgrn_chunk(q, k, logf, v, mlev, rev):
    c = HG_CHUNK
    hi = logf.astype(BF16)
    lo = (logf - hi.astype(F32)).astype(BF16)
    wgt = jnp.exp(_dot(mlev, jnp.concatenate([hi, lo], axis=0)))
    t = lax.broadcasted_iota(jnp.int32, (c, 256), 0)
    s_col = lax.broadcasted_iota(jnp.int32, (c, 256), 1) % c
    r_bd = lax.broadcasted_iota(jnp.int32, (256, 256), 0) // 64
    c_bd = lax.broadcasted_iota(jnp.int32, (256, 256), 1) // 64
    bd = r_bd == c_bd

    def block_diag(a):
        return jnp.where(bd, jnp.concatenate([a] * 4, axis=0), 0.0).astype(BF16)

    scores = jnp.where(t == s_col, _dot_nt(q.astype(BF16), block_diag(k)), 0.0)
    for lvl in range(HG_LEVELS):
        wl = wgt[c * lvl:c * (lvl + 1)]
        q_side = ((t >> (HG_LEVELS - 1 - lvl)) & 1) == (0 if rev else 1)
        ql = jnp.where(q_side, q * wl, 0.0).astype(BF16)
        kl = jnp.where(q_side, 0.0, k * wl)
        s = _dot_nt(ql, block_diag(kl))
        if lvl > 0:
            sh = HG_LEVELS - lvl
            s = jnp.where((t >> sh) == (s_col >> sh), s, 0.0)
        scores = scores + s
    q_in = (q * wgt[6 * c:7 * c]).astype(BF16)
    k_end = (k * wgt[7 * c:8 * c]).astype(BF16)
    o_intra = _dot(scores.astype(BF16), block_diag(v))
    edge = 6 * c if rev else 7 * c - 1
    total = wgt[edge:edge + 1]
    update = jnp.where(bd, _dot_tn(v.astype(BF16), k_end), 0.0)
    return o_intra, q_in, update, total


def _hgrn_kernel(mf_ref, mb_ref, lb_ref, qf_ref, zf_ref, vf_ref, qb_ref, zb_ref, vb_ref,
                 of_ref, ob_ref, stf_ref, stb_ref):
    c = HG_CHUNK
    nc = TM // c

    @pl.when(pl.program_id(1) == 0)
    def _():
        stf_ref[...] = jnp.zeros_like(stf_ref)
        stb_ref[...] = jnp.zeros_like(stb_ref)

    def scan(q_ref, z_ref, v_ref, lb, mlev, o_ref, st_ref, rev):
        sig = _sigmoid(z_ref[0])
        logf = jnp.log(jnp.maximum(lb + (1.0 - lb) * sig, GATE_FLOOR))
        k = (1.0 - lb) * (1.0 - sig)
        order = range(nc - 1, -1, -1) if rev else range(nc)
        rows = [slice(ci * c, (ci + 1) * c) for ci in order]
        parts = [_hgrn_chunk(q_ref[0, r, :].astype(F32), k[r], logf[r], v_ref[0, r, :].astype(F32), mlev, rev)
                 for r in rows]
        st = st_ref[...]
        for r, (o_intra, q_in, update, total) in zip(rows, parts):
            o_ref[0, r, :] = o_intra + _dot_nt(q_in, st.astype(BF16))
            st = st * total + update
        st_ref[...] = st

    scan(qf_ref, zf_ref, vf_ref, lb_ref[0:1], mf_ref[...], of_ref, stf_ref, False)
    scan(qb_ref, zb_ref, vb_ref, lb_ref[1:2], mb_ref[...], ob_ref, stb_ref, True)


def _hgrn(lb, q, zff, zfb, v, n_lat):
    b, t, _ = q.shape
    nlb = n_lat // TM
    ncb = (t - n_lat) // TM
    fwd = lambda bi, i: (bi, jnp.where(i < ncb, nlb + i, i - ncb), 0)
    bwd = lambda bi, i: (bi, nlb + ncb - 1 - i, 0)
    blk = lambda im: pl.BlockSpec((1, TM, 256), im)
    mf = jnp.asarray(_hgrn_level_matrix(False), BF16)
    mb = jnp.asarray(_hgrn_level_matrix(True), BF16)
    return pl.pallas_call(
        _hgrn_kernel,
        out_shape=[jax.ShapeDtypeStruct((b, t, 256), F32)] * 2,
        grid=(b, t // TM),
        in_specs=[_const_spec((8 * HG_CHUNK, 2 * HG_CHUNK)), _const_spec((8 * HG_CHUNK, 2 * HG_CHUNK)),
                  _const_spec((2, 256)),
                  blk(fwd), blk(fwd), blk(fwd), blk(bwd), blk(bwd), blk(bwd)],
        out_specs=[blk(fwd), blk(bwd)],
        scratch_shapes=[pltpu.VMEM((256, 256), F32), pltpu.VMEM((256, 256), F32)],
        compiler_params=_cparams(("parallel", "arbitrary")),
        name="hgrn",
    )(mf, mb, lb, q, zff, v, q, zfb, v)


def _key_chunks(n_lat, t_all):
    chunks = [(lo, min(lo + KV_CHUNK, t_all)) for lo in range(0, t_all, KV_CHUNK)]
    return chunks, [(n_lat, t_all)]


def _online_softmax(lhs, k_ref, v_ref, chunks):
    rows = lhs.shape[0]
    m = jnp.full((rows, 1), -jnp.inf, F32)
    l = jnp.zeros((rows, 1), F32)
    acc = jnp.zeros((rows, LANES), F32)
    for lo, hi in chunks:
        s = _dot_nt(lhs, k_ref[0, lo:hi, :])
        m_new = jnp.maximum(m, jnp.max(s, axis=1, keepdims=True))
        alpha = jnp.exp2(m - m_new)
        p = jnp.exp2(s - m_new)
        l = alpha * l + jnp.sum(p, axis=1, keepdims=True)
        acc = alpha * acc + _dot(p.astype(BF16), v_ref[0, lo:hi, :])
        m = m_new
    return acc / l


def _mla_kernel(q_ref, k_ref, v_ref, o_ref, *, chunks):
    tq = q_ref.shape[1]
    q = q_ref[0]
    zero = jnp.zeros_like(q)
    lane2 = lax.broadcasted_iota(jnp.int32, q.shape, 1)
    lhs = jnp.concatenate([jnp.where(lane2 < LANES, q, zero), jnp.where(lane2 >= LANES, q, zero)], axis=0)
    o = _online_softmax(lhs, k_ref, v_ref, chunks)
    lane = lax.broadcasted_iota(jnp.int32, (tq, LANES), 1)
    o_ref[0] = jnp.where(lane < 64, o[:tq], o[tq:]).astype(o_ref.dtype)


def _diff_kernel(lam_ref, g_ref, q_ref, k_ref, v_ref, o_ref, *, chunks, lam_init):
    tq = q_ref.shape[1]
    q = q_ref[0]
    zero = jnp.zeros_like(q)
    lane = lax.broadcasted_iota(jnp.int32, (tq, LANES), 1)
    lhs = jnp.concatenate([jnp.where(lane // D_HEAD_DIM == r, q, zero) for r in range(4)], axis=0)
    o = _online_softmax(lhs, k_ref, v_ref, chunks)
    lam = lam_ref[0]
    o0 = o[0:tq] - lam * o[tq:2 * tq]
    o1 = o[2 * tq:3 * tq] - lam * o[3 * tq:4 * tq]
    o = jnp.where(lane < 64, o0, o1)
    o_ref[0] = (_head_rms(o, g_ref[...], LANES) * (1.0 - lam_init)).astype(o_ref.dtype)


def _attention_call(body, name, extras, extra_specs, q, k, v, tq, row0, n_rows, out_init):
    b, t, qw = q.shape
    blk0 = row0 // tq
    n_extra = len(extras)
    in_specs = list(extra_specs) + [
        pl.BlockSpec((1, tq, qw // 2), lambda bi, p, i: (bi, blk0 + i, p)),
        pl.BlockSpec((1, t, k.shape[2] // 2), lambda bi, p, i: (bi, 0, p)),
        pl.BlockSpec((1, t, LANES), lambda bi, p, i: (bi, 0, p))]
    args = list(extras) + [q, k, v]
    aliases = {}
    if out_init is not None:
        in_specs.append(pl.BlockSpec(memory_space=pl.ANY))
        aliases = {len(args): 0}
        args.append(out_init)

    def kern(*refs):
        body(*refs[:n_extra + 3], refs[-1])

    return pl.pallas_call(
        kern,
        out_shape=jax.ShapeDtypeStruct((b, t, 256), BF16),
        grid=(b, 2, n_rows // tq),
        in_specs=in_specs,
        out_specs=pl.BlockSpec((1, tq, LANES), lambda bi, p, i: (bi, blk0 + i, p)),
        input_output_aliases=aliases,
        compiler_params=_cparams(("parallel", "parallel", "arbitrary")),
        name=name,
    )(*args)


def _two_pass_attention(body_of, name, extras, extra_specs, q, k, v, n_lat, tq_lat):
    t = q.shape[1]
    lat_chunks, ctx_chunks = _key_chunks(n_lat, t)
    out = jnp.zeros((q.shape[0], t, 256), BF16)
    out = _attention_call(body_of(lat_chunks), name, extras, extra_specs, q, k, v, tq_lat, 0, n_lat, out)
    return _attention_call(body_of(ctx_chunks), name + "_ctx", extras, extra_specs, q, k, v,
                           TQ, n_lat, t - n_lat, out)


def _mla_attention(q, k, v, n_lat):
    body_of = lambda chunks: functools.partial(_mla_kernel, chunks=chunks)
    return _two_pass_attention(body_of, "mla_attention", [], [], q, k, v, n_lat, TQ_MLA)


def _diff_attention(lam, q, k, v, gain, n_lat, lam_init):
    body_of = lambda chunks: functools.partial(_diff_kernel, chunks=chunks, lam_init=lam_init)
    specs = [pl.BlockSpec(memory_space=pltpu.SMEM), _const_spec((1, LANES))]
    return _two_pass_attention(body_of, "diff_attention", [lam, gain], specs, q, k, v, n_lat, TQ)


def _merge_kernel(x_ref, mod_ref, n1_ref, n2_ref, a_ref, of_ref, ob_ref, bg_ref, hg_ref, c_ref, d_ref,
                  wg_ref, wb_ref, wo_ref, wr_ref, br_ref, xo_ref, h2_ref, ri_ref, rw_ref):
    d = D_MODEL
    x = x_ref[0]
    m = mod_ref[0, 0]
    h = (_rms(x, n1_ref[...]) * (1.0 + m[1:2]) + m[0:1]).astype(BF16)
    g = bg_ref[0].astype(F32)
    b_out = _head_rms(of_ref[0] + ob_ref[0], hg_ref[...], 256) * (g * _sigmoid(g))
    branches = (a_ref[0], b_out.astype(BF16), c_ref[0], d_ref[0])
    y = jnp.zeros((TM, d), F32)
    for i, br in enumerate(branches):
        gate = _sigmoid(_dot(h, wg_ref[:, d * i:d * (i + 1)]))
        y = y + gate * _dot(br, wb_ref[i])
    x = x + m[2:3] * _dot(y.astype(BF16), wo_ref[...])
    xo_ref[0] = x
    h2 = _rms(x, n2_ref[...]) * (1.0 + m[4:5]) + m[3:4]
    h2b = h2.astype(BF16)
    h2_ref[0] = _rows_first(h2)

    h2lo = (h2 - h2b.astype(F32)).astype(BF16)
    r = _dot(jnp.concatenate([h2b, h2lo], axis=0), wr_ref[...])
    logits = r[:TM, :LANES] + r[:TM, LANES:] + r[TM:, :LANES] + br_ref[...]
    lane = lax.broadcasted_iota(jnp.int32, (TM, LANES), 1)
    neg = -jnp.inf
    is_g = (lane >= N_EXPERTS) & (lane < N_EXPERTS + N_GROUPS)
    lg = jnp.where(is_g, logits, neg)
    mg = jnp.max(lg, axis=1, keepdims=True)
    g_val = 1.0 / jnp.sum(jnp.exp(lg - mg), axis=1, keepdims=True)
    g_idx = jnp.min(jnp.where(lg == mg, lane, 4 * LANES), axis=1, keepdims=True) - N_EXPERTS
    in_group = (lane >= g_idx * EXPERTS_PER_GROUP) & (lane < (g_idx + 1) * EXPERTS_PER_GROUP)
    le = jnp.where(in_group, logits, neg)
    m1 = jnp.max(le, axis=1, keepdims=True)
    e1 = jnp.min(jnp.where(le == m1, lane, 4 * LANES), axis=1, keepdims=True)
    le2 = jnp.where(lane == e1, neg, le)
    m2 = jnp.max(le2, axis=1, keepdims=True)
    e2 = jnp.min(jnp.where(le2 == m2, lane, 4 * LANES), axis=1, keepdims=True)
    r2 = jnp.exp(m2 - m1)
    v1 = g_val / (1.0 + r2)
    ri_ref[0] = jnp.where(lane == 0, e1, jnp.where(lane == 1, e2, 0))
    rw_ref[0] = jnp.where(lane == 0, v1, jnp.where(lane == 1, v1 * r2, 0.0))


def _merge(x, mod, n1, n2, a_o, o_f, o_b, bg, hg, c_o, d_o, wg, wb, wo, wr, br, n_lat_tiles):
    b, t, d = x.shape
    tile = lambda w: pl.BlockSpec((1, TM, w), lambda bi, i: (bi, i, 0))
    return pl.pallas_call(
        _merge_kernel,
        out_shape=[jax.ShapeDtypeStruct((b, t, d), F32), jax.ShapeDtypeStruct((b, t, ROW_SLABS, LANES), F32),
                   jax.ShapeDtypeStruct((b, t, LANES), jnp.int32), jax.ShapeDtypeStruct((b, t, LANES), F32)],
        grid=(b, t // TM),
        in_specs=[tile(d), pl.BlockSpec((1, 1, 8, d), lambda bi, i: (bi, i // n_lat_tiles, 0, 0)),
                  _const_spec((1, d)), _const_spec((1, d)),
                  tile(256), tile(256), tile(256), tile(256), _const_spec((1, 256)), tile(256), tile(256),
                  _const_spec((d, 4 * d)), _const_spec((4, 256, d)), _const_spec((d, d)),
                  _const_spec((d, 256)), _const_spec((1, LANES))],
        out_specs=[tile(d), pl.BlockSpec((1, TM, ROW_SLABS, LANES), lambda bi, i: (bi, i, 0, 0)),
                   tile(LANES), tile(LANES)],
        compiler_params=_cparams(("parallel", "parallel")),
        name="merge",
    )(x, mod, n1, n2, a_o, o_f, o_b, bg, hg, c_o, d_o, wg, wb, wo, wr, br)


def _route_positions(ri, n_tok):
    ef = ri[:, :2].reshape(-1)
    rb = 2 * TM
    oh = (ef[:, None] == jnp.arange(N_EXPERTS, dtype=jnp.int32)[None, :]).astype(F32).reshape(-1, rb, N_EXPERTS)
    tri = (jnp.arange(rb)[:, None] >= jnp.arange(rb)[None, :]).astype(F32)
    within = jnp.einsum("ij,gje->gie", tri, oh)
    tot = within[:, -1, :]
    before = jnp.cumsum(tot, axis=0) - tot
    cnt = jnp.sum(tot, axis=0).astype(jnp.int32)
    pcnt = ((cnt + MOE_TILE - 1) // MOE_TILE) * MOE_TILE
    end = jnp.cumsum(pcnt)
    start = (end - pcnt).astype(F32)
    pos = jnp.sum((within + (before + start[None, :])[:, None, :]) * oh, axis=2) - 1.0
    pos = pos.reshape(-1)
    n_tiles = 2 * n_tok // MOE_TILE + N_EXPERTS
    n_act = (end[-1] // MOE_TILE).astype(jnp.int32)
    tile_row = jnp.minimum(jnp.arange(n_tiles, dtype=jnp.int32), n_act - 1) * MOE_TILE
    tile_expert = jnp.sum((end[None, :] <= tile_row[:, None]).astype(jnp.int32), axis=1)
    tile_expert = jnp.minimum(tile_expert, N_EXPERTS - 1)
    return pos.reshape(n_tok // TM, 1, 2 * TM).astype(jnp.int32), tile_expert, n_act.reshape(1)


def _dispatch_kernel(pos_ref, h_ref, xs_in_ref, xs_ref, sem):
    del xs_in_ref

    def body(r, carry):
        for s in range(2):
            p = pos_ref[0, 0, 2 * r + s]
            pltpu.make_async_copy(h_ref.at[pl.ds(r, 1)], xs_ref.at[pl.ds(p, 1)], sem.at[s]).start()
        return carry

    lax.fori_loop(0, TM, body, 0)
    for s in range(2):
        pltpu.make_async_copy(h_ref, h_ref, sem.at[s]).wait()


def _dispatch(pos, h2, xs_init):
    n = h2.shape[0]
    return pl.pallas_call(
        _dispatch_kernel,
        out_shape=jax.ShapeDtypeStruct(xs_init.shape, F32),
        grid=(n // TM,),
        in_specs=[pl.BlockSpec((1, 1, 2 * TM), lambda t: (t, 0, 0), memory_space=pltpu.SMEM),
                  pl.BlockSpec((TM, ROW_SLABS, LANES), lambda t: (t, 0, 0)),
                  pl.BlockSpec(memory_space=pl.ANY)],
        out_specs=pl.BlockSpec(memory_space=pl.ANY),
        scratch_shapes=[pltpu.SemaphoreType.DMA((2,))],
        input_output_aliases={2: 0},
        compiler_params=_cparams(("arbitrary",)),
        name="moe_dispatch",
    )(pos, h2, xs_init)


def _experts_kernel(te_ref, na_ref, x_ref, wgu_ref, wd_ref, o_ref):
    del te_ref
    t = pl.program_id(0)

    @pl.when(t < na_ref[0])
    def _():
        x = _slabs_first(x_ref[...])
        gu = _dot(x[0].astype(BF16), wgu_ref[0, 0:LANES, :])
        for s in range(1, ROW_SLABS):
            gu = gu + _dot(x[s].astype(BF16), wgu_ref[0, LANES * s:LANES * (s + 1), :])
        gt, up = gu[:, :EXPERT_HIDDEN], gu[:, EXPERT_HIDDEN:]
        o_ref[...] = _rows_first(_dot((gt * _sigmoid(gt) * up).astype(BF16), wd_ref[0]))

    @pl.when(t >= na_ref[0])
    def _():
        o_ref[...] = jnp.zeros_like(o_ref)


def _experts(tile_expert, n_act, xs, wgu, wd):
    p = xs.shape[0]
    d = D_MODEL
    rows = pl.BlockSpec((MOE_TILE, ROW_SLABS, LANES), lambda t, te, na: (t, 0, 0))
    return pl.pallas_call(
        _experts_kernel,
        out_shape=jax.ShapeDtypeStruct(xs.shape, F32),
        grid_spec=pltpu.PrefetchScalarGridSpec(
            num_scalar_prefetch=2, grid=(p // MOE_TILE,),
            in_specs=[rows,
                      pl.BlockSpec((1, d, 2 * EXPERT_HIDDEN), lambda t, te, na: (te[t], 0, 0)),
                      pl.BlockSpec((1, EXPERT_HIDDEN, d), lambda t, te, na: (te[t], 0, 0))],
            out_specs=rows),
        compiler_params=_cparams(("arbitrary",)),
        name="moe_experts",
    )(tile_expert, n_act, xs, wgu, wd)


def _combine_kernel(pos_ref, ys_ref, w_ref, o_ref, buf, sem):
    def body(r, carry):
        for s in range(2):
            p = pos_ref[0, 0, 2 * r + s]
            pltpu.make_async_copy(ys_ref.at[pl.ds(p, 1)], buf.at[s, pl.ds(r, 1)], sem.at[s]).start()
        return carry

    lax.fori_loop(0, TM, body, 0)
    for s in range(2):
        pltpu.make_async_copy(buf.at[s], buf.at[s], sem.at[s]).wait()
    w = w_ref[...]
    y0, y1 = _slabs_first(buf[0]), _slabs_first(buf[1])
    for k in range(ROW_SLABS):
        o_ref[:, LANES * k:LANES * (k + 1)] = w[:, 0:1] * y0[k] + w[:, 1:2] * y1[k]


def _combine(pos, ys, rwt):
    n = rwt.shape[0]
    d = D_MODEL
    return pl.pallas_call(
        _combine_kernel,
        out_shape=jax.ShapeDtypeStruct((n, d), F32),
        grid=(n // TM,),
        in_specs=[pl.BlockSpec((1, 1, 2 * TM), lambda t: (t, 0, 0), memory_space=pltpu.SMEM),
                  pl.BlockSpec(memory_space=pl.ANY),
                  pl.BlockSpec((TM, LANES), lambda t: (t, 0))],
        out_specs=pl.BlockSpec((TM, d), lambda t: (t, 0)),
        scratch_shapes=[pltpu.VMEM((2, TM, ROW_SLABS, LANES), F32), pltpu.SemaphoreType.DMA((2,))],
        compiler_params=_cparams(("arbitrary",)),
        name="moe_combine",
    )(pos, ys, rwt)


def _moe(h2, ri, rwt, wgu, wd, xs_init):
    n = h2.shape[0]
    pos, tile_expert, n_act = _route_positions(ri, n)
    xs = _dispatch(pos, h2, xs_init)
    ys = _experts(tile_expert, n_act, xs, wgu, wd)
    return _combine(pos, ys, rwt), xs


def _final_kernel(x_ref, f_ref, mod_ref, g_ref, o_ref):
    x = x_ref[0] + mod_ref[0, 0, 5:6, :] * f_ref[0]
    o_ref[0] = _rms(x, g_ref[...])


def _final(x, f, mod, gain, n_lat):
    b, t, d = x.shape
    tile = pl.BlockSpec((1, TM, d), lambda bi, i: (bi, i, 0))
    return pl.pallas_call(
        _final_kernel,
        out_shape=jax.ShapeDtypeStruct((b, n_lat, d), F32),
        grid=(b, n_lat // TM),
        in_specs=[tile, tile, pl.BlockSpec((1, 1, 8, d), lambda bi, i: (bi, 0, 0, 0)), _const_spec((1, d))],
        out_specs=tile,
        compiler_params=_cparams(("parallel", "parallel")),
        name="final_norm",
    )(x, f, mod, gain)


def _rope_tables(n_lat, t_all, lane_rope, lane_off, dr):
    half, quarter = dr // 2, dr // 4
    inv_freq = 1.0 / (ROPE_BASE ** (jnp.arange(quarter, dtype=F32) / quarter))
    off = np.asarray(lane_off)
    use_col = off >= half
    j = (off % half) % quarter
    first = (off % half) < quarter
    tok = jnp.arange(n_lat, dtype=jnp.int32)
    row = (tok // GRID_W).astype(F32)
    col = (tok % GRID_W).astype(F32)
    pos = jnp.where(jnp.asarray(use_col)[None, :], col[:, None], row[:, None])
    ang = pos * inv_freq[jnp.asarray(j)][None, :]
    rope = jnp.asarray(lane_rope)[None, :]
    cos = jnp.where(rope, jnp.cos(ang), 1.0)
    sin = jnp.where(rope, jnp.sin(ang) * jnp.where(jnp.asarray(first), -1.0, 1.0)[None, :], 0.0)
    pad = t_all - n_lat
    w = off.shape[0]
    cos = jnp.concatenate([cos, jnp.ones((pad, w), F32)], axis=0)
    sin = jnp.concatenate([sin, jnp.zeros((pad, w), F32)], axis=0)
    return cos, sin


def _all_rope_tables(n_lat, t_all):
    la = np.arange(256)
    ta = _rope_tables(n_lat, t_all, np.ones(256, bool), la % 64, 64)
    lc = np.arange(512) % 128
    tc = _rope_tables(n_lat, t_all, (lc >= 64) & (lc < 96), np.clip(lc - 64, 0, 31), 32)
    td = _rope_tables(n_lat, t_all, np.ones(256, bool), la % 32, 32)
    return (*ta, *tc, *td)


def _permute_w_in(w):
    s = np.cumsum([0, 256, 128, 128, 256, 256, 256, 256, 256, 256, 128, 32, 256, 256, 256])
    seg = lambda i: w[:, s[i]:s[i + 1]]
    dup = lambda a: jnp.concatenate([a[:, 0:64], a[:, 0:64], a[:, 64:128], a[:, 64:128]], axis=1)
    z64 = jnp.zeros((w.shape[0], 64), w.dtype)
    z32 = jnp.zeros((w.shape[0], 32), w.dtype)
    kr = jnp.concatenate([z64, seg(10), z32] * C_HEADS, axis=1)
    cols = [seg(0), dup(seg(1)), dup(seg(2)),
            seg(3), seg(4), seg(5), seg(6), seg(7),
            seg(8), seg(9), kr,
            seg(11), seg(12), seg(13)]
    return jnp.concatenate(cols, axis=1).astype(BF16)


def _permute_mla(w_uq, w_ukv):
    z32 = jnp.zeros((w_uq.shape[0], 32), w_uq.dtype)
    qd = C_NOPE + C_ROPE
    uq = jnp.concatenate([a for h in range(C_HEADS) for a in (w_uq[:, qd * h:qd * (h + 1)], z32)], axis=1)
    z64 = jnp.zeros((w_ukv.shape[0], 64), w_ukv.dtype)
    kd = C_NOPE + C_V
    uk = jnp.concatenate([a for h in range(C_HEADS) for a in (w_ukv[:, kd * h:kd * h + C_NOPE], z64)], axis=1)
    uv = jnp.concatenate([w_ukv[:, kd * h + C_NOPE:kd * (h + 1)] for h in range(C_HEADS)], axis=1)
    return uq.astype(BF16), jnp.concatenate([uk, uv], axis=1).astype(BF16)


def kernel(x, c, ctx, c_ctx, w_mod, b_mod, norm1, norm2, w_in, w_gate, w_branch, w_out, attn_sink, hgrn_lb_logits, hgrn_norm, mla_q_norm, mla_kv_norm, mla_w_uq, mla_w_ukv, diff_lambda, diff_subln, w_router_group, b_router_group, w_router_expert, b_router_expert, w_expert_gate, w_expert_up, w_expert_down, final_norm):
    b, n_lat, d = x.shape
    n_ctx = ctx.shape[1]
    depth = w_mod.shape[0]
    t_all = n_lat + n_ctx
    assert d == D_MODEL and n_lat % TM == 0 and n_ctx % TM == 0 and n_lat % GRID_W == 0
    assert n_lat >= 3 * A_WINDOW and b <= 15
    n_lat_tiles = n_lat // TM

    xs = jnp.concatenate([x, ctx], axis=1)
    c_all = jnp.concatenate([c, c_ctx[None], jnp.zeros((15 - b, d), F32)], axis=0)
    mod = _modulation(c_all, w_mod, b_mod).reshape(depth, 16, 6, d)
    mod = jnp.pad(mod, ((0, 0), (0, 0), (0, 2), (0, 0)))
    mod = jnp.stack([mod[:, :b], jnp.broadcast_to(mod[:, b:b + 1], (depth, b, 8, d))], axis=2)

    sm = jax.nn.softmax(hgrn_lb_logits.astype(F32), axis=0)
    lower_bounds = jnp.cumsum(sm, axis=0) - sm[0]
    tabs = _all_rope_tables(n_lat, t_all)

    n_tok = b * t_all
    moe_buf = jnp.zeros((2 * n_tok + N_EXPERTS * MOE_TILE, ROW_SLABS, LANES), F32)
    f_prev, mod_prev = None, None
    for li in range(depth):
        lam_init = 0.8 - 0.6 * math.exp(-0.3 * li)
        lp = diff_lambda[li].astype(F32)
        lam = (jnp.exp(jnp.sum(lp[0] * lp[1])) - jnp.exp(jnp.sum(lp[2] * lp[3])) + lam_init).reshape(1)
        w_perm = _permute_w_in(w_in[li])
        wuq, wukv = _permute_mla(mla_w_uq[li], mla_w_ukv[li])
        xs, z = _inproj(xs, f_prev, mod_prev, mod[li], norm1[li][None], w_perm,
                        mla_q_norm[li][None], mla_kv_norm[li][None], wuq, wukv, tabs, n_lat_tiles)
        qa, ka, va, bq, bff, bfb, bv, bg, cq, ck, cv, dq, dk, dv = z
        a_o = _win_attention(attn_sink[li].astype(F32), qa, ka, va, n_lat)
        o_f, o_b = _hgrn(lower_bounds[li], bq, bff, bfb, bv, n_lat)
        c_o = _mla_attention(cq, ck, cv, n_lat)
        d_o = _diff_attention(lam, dq, dk, dv, jnp.tile(diff_subln[li], 2)[None], n_lat, lam_init)
        wr = jnp.concatenate([w_router_expert[li], w_router_group[li],
                              jnp.zeros((d, LANES - N_EXPERTS - N_GROUPS), F32)], axis=1)
        wr_hi = wr.astype(BF16)
        wr_lo = (wr - wr_hi.astype(F32)).astype(BF16)
        br = jnp.concatenate([b_router_expert[li], b_router_group[li],
                              jnp.zeros((LANES - N_EXPERTS - N_GROUPS,), F32)])[None]
        xs, h2, ri, rwt = _merge(xs, mod[li], norm1[li][None], norm2[li][None], a_o, o_f, o_b, bg,
                                 jnp.tile(hgrn_norm[li], B_HEADS)[None], c_o, d_o,
                                 w_gate[li].astype(BF16), w_branch[li].astype(BF16), w_out[li].astype(BF16),
                                 jnp.concatenate([wr_hi, wr_lo], axis=1), br, n_lat_tiles)
        wgu = jnp.concatenate([w_expert_gate[li], w_expert_up[li]], axis=2).astype(BF16)
        f, moe_buf = _moe(h2.reshape(n_tok, ROW_SLABS, LANES), ri.reshape(n_tok, LANES), rwt.reshape(n_tok, LANES),
                          wgu, w_expert_down[li].astype(BF16), moe_buf)
        f_prev, mod_prev = f.reshape(b, t_all, d), mod[li]
    return _final(xs, f_prev, mod_prev, final_norm[None], n_lat)
```

```python
import functools
import math

import numpy as np
import jax
import jax.numpy as jnp
from jax import lax
from jax.experimental import pallas as pl
from jax.experimental.pallas import tpu as pltpu

D_MODEL = 1024
GRID_W = 64
ROPE_BASE = 10000.0
NORM_EPS = 1e-6
MASK_VALUE = -1e30
GATE_FLOOR = 1e-30

A_HEADS, A_KV_HEADS, A_HEAD_DIM, A_WINDOW = 4, 2, 64, 128
B_HEADS, B_KEY_DIM, B_VAL_DIM = 4, 64, 64
C_HEADS, C_Q_LORA, C_KV_LORA, C_NOPE, C_ROPE, C_V = 4, 256, 128, 64, 32, 64
D_HEADS, D_HEAD_DIM = 4, 32
N_GROUPS, EXPERTS_PER_GROUP, EXPERT_HIDDEN = 4, 8, 256
N_EXPERTS = N_GROUPS * EXPERTS_PER_GROUP

TM = 256
WIN_Q = 256
TQ = 256
TQ_MLA = 512
LOG2E = 1.4426950408889634
KV_CHUNK = 4096
MOE_TILE = 256
HG_CHUNK = 64
HG_LEVELS = 6
LANES = 128
ROW_SLABS = D_MODEL // LANES
VMEM_LIMIT = 52 * 1024 * 1024

OFF_A, OFF_B, OFF_C, OFF_D, N_PERM = 0, 768, 2048, 2944, 3712

F32 = jnp.float32
BF16 = jnp.bfloat16


def _dot(a, b):
    return jnp.dot(a, b, preferred_element_type=F32)


def _dot_nt(a, b):
    return lax.dot_general(a, b, (((1,), (1,)), ((), ())), preferred_element_type=F32)


def _dot_tn(a, b):
    return lax.dot_general(a, b, (((0,), (0,)), ((), ())), preferred_element_type=F32)


def _dot_hi(a, b):
    return jnp.dot(a, b, preferred_element_type=F32, precision=lax.Precision.HIGHEST)


def _sigmoid(x):
    return 1.0 / (1.0 + jnp.exp(-x))


def _rms(x, gain):
    return x * lax.rsqrt(jnp.mean(x * x, axis=-1, keepdims=True) + NORM_EPS) * gain


def _head_rms(o, gain, width):
    r = lax.broadcasted_iota(jnp.int32, (width, width), 0) // 64
    c = lax.broadcasted_iota(jnp.int32, (width, width), 1) // 64
    ones = jnp.where(r == c, 1.0 / 64.0, 0.0).astype(F32)
    ms = _dot_hi(o * o, ones)
    return o * lax.rsqrt(ms + NORM_EPS) * gain


def _rope(x, cos, sin, quarter):
    w = x.shape[-1]
    lane = lax.broadcasted_iota(jnp.int32, x.shape, 1)
    first = (lane % (2 * quarter)) < quarter
    sw = jnp.where(first, pltpu.roll(x, w - quarter, 1), pltpu.roll(x, quarter, 1))
    return x * cos + sw * sin


def _slabs_first(x):
    return pltpu.einshape("tsl->stl", x)


def _rows_first(y):
    slabs = jnp.stack([y[:, LANES * s:LANES * (s + 1)] for s in range(ROW_SLABS)], axis=0)
    return pltpu.einshape("stl->tsl", slabs)


def _cparams(sem):
    return pltpu.CompilerParams(dimension_semantics=sem, vmem_limit_bytes=VMEM_LIMIT)


def _const_spec(shape):
    n = len(shape)
    return pl.BlockSpec(shape, lambda *_: (0,) * n)


def _mod_kernel(c_ref, w_ref, b_ref, o_ref):
    c = c_ref[...]
    o_ref[0] = _dot_hi(c * _sigmoid(c), w_ref[0]) + b_ref[0]


def _modulation(c_all, w_mod, b_mod):
    depth, d, n = w_mod.shape
    nb = 1536
    return pl.pallas_call(
        _mod_kernel,
        out_shape=jax.ShapeDtypeStruct((depth, 16, n), F32),
        grid=(depth, n // nb),
        in_specs=[pl.BlockSpec((16, d), lambda l, j: (0, 0)),
                  pl.BlockSpec((1, d, nb), lambda l, j: (l, 0, j)),
                  pl.BlockSpec((1, 1, nb), lambda l, j: (l, 0, j))],
        out_specs=pl.BlockSpec((1, 16, nb), lambda l, j: (l, 0, j)),
        compiler_params=_cparams(("arbitrary", "arbitrary")),
        name="modulation",
    )(c_all, w_mod, b_mod.reshape(depth, 1, n))


def _inproj_kernel(*refs, has_prev, n_lat_tiles):
    if has_prev:
        x_ref, f_ref, modp_ref = refs[:3]
        refs = refs[3:]
        x = x_ref[0] + modp_ref[0, 0, 5:6, :] * f_ref[0]
    else:
        x_ref, ctx_ref = refs[:2]
        refs = refs[2:]
        x = jnp.where(pl.program_id(1) < n_lat_tiles, x_ref[0], ctx_ref[0])
    (mod_ref, n1_ref, w_ref, qn_ref, kvn_ref, wuq_ref, wukv_ref,
     cosa_ref, sina_ref, cosc_ref, sinc_ref, cosd_ref, sind_ref) = refs[:13]
    (xo_ref, qa_ref, ka_ref, va_ref, bq_ref, bff_ref, bfb_ref, bv_ref, bg_ref,
     cq_ref, ck_ref, cv_ref, dq_ref, dk_ref, dv_ref) = refs[13:]
    xo_ref[0] = x
    m = mod_ref[0, 0]
    h = (_rms(x, n1_ref[...]) * (1.0 + m[1:2]) + m[0:1]).astype(BF16)

    z = _dot(h, w_ref[:, OFF_A:OFF_A + 768])
    cosa, sina = cosa_ref[...], sina_ref[...]
    qa_ref[0] = (_rope(z[:, 0:256], cosa, sina, 16) * (A_HEAD_DIM ** -0.5)).astype(BF16)
    ka_ref[0] = _rope(z[:, 256:512], cosa, sina, 16).astype(BF16)
    va_ref[0] = z[:, 512:768].astype(BF16)

    z = _dot(h, w_ref[:, OFF_B:OFF_B + 1280])
    bq_ref[0] = z[:, 0:256].astype(BF16)
    bff_ref[0] = z[:, 256:512]
    bfb_ref[0] = z[:, 512:768]
    bv_ref[0] = z[:, 768:1024].astype(BF16)
    bg_ref[0] = z[:, 1024:1280].astype(BF16)

    z = _dot(h, w_ref[:, OFF_C:OFF_C + 896])
    cosc, sinc = cosc_ref[...], sinc_ref[...]
    cq = _rms(z[:, 0:256], qn_ref[...]).astype(BF16)
    q = _rope(_dot(cq, wuq_ref[...]), cosc, sinc, 8)
    cq_ref[0] = (q * ((C_NOPE + C_ROPE) ** -0.5 * LOG2E)).astype(BF16)
    ckv = _rms(z[:, 256:384], kvn_ref[...]).astype(BF16)
    kv = _dot(ckv, wukv_ref[...])
    ck_ref[0] = (kv[:, 0:512] + _rope(z[:, 384:896], cosc, sinc, 8)).astype(BF16)
    cv_ref[0] = kv[:, 512:768].astype(BF16)

    z = _dot(h, w_ref[:, OFF_D:OFF_D + 768])
    cosd, sind = cosd_ref[...], sind_ref[...]
    dq_ref[0] = (_rope(z[:, 0:256], cosd, sind, 8) * (D_HEAD_DIM ** -0.5 * LOG2E)).astype(BF16)
    dk_ref[0] = _rope(z[:, 256:512], cosd, sind, 8).astype(BF16)
    dv_ref[0] = z[:, 512:768].astype(BF16)


def _inproj(x, other, mod_prev, mod, n1, w_perm, qn, kvn, wuq, wukv, tabs, n_lat_tiles):
    has_prev = mod_prev is not None
    b, _, d = x.shape
    t = x.shape[1] if has_prev else x.shape[1] + other.shape[1]
    nt = t // TM
    tile = lambda w: pl.BlockSpec((1, TM, w), lambda bi, i: (bi, i, 0))
    modspec = pl.BlockSpec((1, 1, 8, d), lambda bi, i: (bi, i // n_lat_tiles, 0, 0))
    tab = lambda w: pl.BlockSpec((TM, w), lambda bi, i: (i, 0))
    if has_prev:
        in_specs, args = [tile(d), tile(d), modspec], [x, other, mod_prev]
    else:
        lat = pl.BlockSpec((1, TM, d), lambda bi, i: (bi, jnp.minimum(i, n_lat_tiles - 1), 0))
        ctx = pl.BlockSpec((1, TM, d), lambda bi, i: (bi, jnp.maximum(i - n_lat_tiles, 0), 0))
        in_specs, args = [lat, ctx], [x, other]
    in_specs += [modspec, _const_spec((1, d)), _const_spec((d, N_PERM)), _const_spec((1, 256)),
                 _const_spec((1, 128)), _const_spec((256, 512)), _const_spec((128, 768)),
                 tab(256), tab(256), tab(512), tab(512), tab(256), tab(256)]
    args += [mod, n1, w_perm, qn, kvn, wuq, wukv, *tabs]
    widths = [(256, BF16), (256, BF16), (256, BF16),
              (256, BF16), (256, F32), (256, F32), (256, BF16), (256, BF16),
              (512, BF16), (512, BF16), (256, BF16),
              (256, BF16), (256, BF16), (256, BF16)]
    widths = [(d, F32)] + widths
    out_shape = [jax.ShapeDtypeStruct((b, t, w), dt) for w, dt in widths]
    out_specs = [tile(w) for w, _ in widths]
    res = pl.pallas_call(
        functools.partial(_inproj_kernel, has_prev=has_prev, n_lat_tiles=n_lat_tiles),
        out_shape=out_shape, grid=(b, nt), in_specs=in_specs, out_specs=out_specs,
        compiler_params=_cparams(("parallel", "parallel")),
        name="inproj",
    )(*args)
    return res[0], res[1:]


def _win_kernel(sink_ref, q_ref, k_ref, v_ref, o_ref, *, n_lat, t_all):
    w = A_WINDOW
    qb = WIN_Q
    kb = qb + 2 * w
    n = pl.program_id(1)
    start = pl.multiple_of(jnp.clip(n * qb - w, 0, n_lat - kb), w)
    q = q_ref[0]
    lane = lax.broadcasted_iota(jnp.int32, (qb, LANES), 1)
    row = lax.broadcasted_iota(jnp.int32, (2 * qb, kb), 0)
    col = lax.broadcasted_iota(jnp.int32, (2 * qb, kb), 1)
    rel = (start + col) - (n * qb + row % qb)
    valid = (jnp.abs(rel) <= w) & (n < n_lat // qb)
    row1 = lax.broadcasted_iota(jnp.int32, (2 * qb, 1), 0)
    for j in range(A_KV_HEADS):
        sl = slice(LANES * j, LANES * (j + 1))
        qp = q[:, sl]
        zero = jnp.zeros_like(qp)
        lhs = jnp.concatenate([jnp.where(lane < 64, qp, zero), jnp.where(lane >= 64, qp, zero)], axis=0)
        sb = _dot_nt(lhs, k_ref[0, pl.ds(start, kb), sl])
        sc = _dot_nt(lhs, k_ref[0, n_lat:t_all, sl])
        sb = jnp.where(valid, sb, MASK_VALUE)
        sink = jnp.where(row1 < qb, sink_ref[2 * j], sink_ref[2 * j + 1])
        m = jnp.maximum(jnp.maximum(jnp.max(sb, axis=1, keepdims=True), jnp.max(sc, axis=1, keepdims=True)), sink)
        pb = jnp.exp(sb - m)
        pc = jnp.exp(sc - m)
        den = jnp.sum(pb, axis=1, keepdims=True) + jnp.sum(pc, axis=1, keepdims=True) + jnp.exp(sink - m)
        o = (_dot(pb.astype(BF16), v_ref[0, pl.ds(start, kb), sl])
             + _dot(pc.astype(BF16), v_ref[0, n_lat:t_all, sl])) / den
        o_ref[0, :, sl] = jnp.where(lane < 64, o[:qb], o[qb:]).astype(o_ref.dtype)


def _win_attention(sink, q, k, v, n_lat):
    b, t, _ = q.shape
    qb = WIN_Q
    return pl.pallas_call(
        functools.partial(_win_kernel, n_lat=n_lat, t_all=t),
        out_shape=jax.ShapeDtypeStruct((b, t, 256), BF16),
        grid=(b, t // qb),
        in_specs=[pl.BlockSpec(memory_space=pltpu.SMEM),
                  pl.BlockSpec((1, qb, 256), lambda bi, i: (bi, i, 0)),
                  pl.BlockSpec((1, t, 256), lambda bi, i: (bi, 0, 0)),
                  pl.BlockSpec((1, t, 256), lambda bi, i: (bi, 0, 0))],
        out_specs=pl.BlockSpec((1, qb, 256), lambda bi, i: (bi, i, 0)),
        compiler_params=_cparams(("parallel", "arbitrary")),
        name="win_attention",
    )(sink, q, k, v)


def _hgrn_level_matrix(rev):
    c = HG_CHUNK
    m = np.zeros((8, c, c), np.float32)
    for lvl in range(HG_LEVELS):
        size = c >> lvl
        for t in range(c):
            mid = (t // size) * size + size // 2
            upper = t >= mid
            if not rev:
                rng = range(mid, t + 1) if upper else range(t + 1, mid)
            else:
                rng = range(mid, t) if upper else range(t, mid)
            m[lvl, t, list(rng)] = 1.0
    for t in range(c):
        if not rev:
            m[6, t, :t + 1] = 1.0
            m[7, t, t + 1:] = 1.0
        else:
            m[6, t, t:] = 1.0
            m[7, t, :t] = 1.0
    m = m.reshape(8 * c, c)
    return np.concatenate([m, m], axis=1)


def _hgrn_chunk(q, k, logf, v, mlev, rev):
    c = HG_CHUNK
    hi = logf.astype(BF16)
    lo = (logf - hi.astype(F32)).astype(BF16)
    wgt = jnp.exp(_dot(mlev, jnp.concatenate([hi, lo], axis=0)))
    t = lax.broadcasted_iota(jnp.int32, (c, 256), 0)
    s_col = lax.broadcasted_iota(jnp.int32, (c, 256), 1) % c
    r_bd = lax.broadcasted_iota(jnp.int32, (256, 256), 0) // 64
    c_bd = lax.broadcasted_iota(jnp.int32, (256, 256), 1) // 64
    bd = r_bd == c_bd

    def block_diag(a):
        return jnp.where(bd, jnp.concatenate([a] * 4, axis=0), 0.0).astype(BF16)

    scores = jnp.where(t == s_col, _dot_nt(q.astype(BF16), block_diag(k)), 0.0)
    for lvl in range(HG_LEVELS):
        wl = wgt[c * lvl:c * (lvl + 1)]
        q_side = ((t >> (HG_LEVELS - 1 - lvl)) & 1) == (0 if rev else 1)
        ql = jnp.where(q_side, q * wl, 0.0).astype(BF16)
        kl = jnp.where(q_side, 0.0, k * wl)
        s = _dot_nt(ql, block_diag(kl))
        if lvl > 0:
            sh = HG_LEVELS - lvl
            s = jnp.where((t >> sh) == (s_col >> sh), s, 0.0)
        scores = scores + s
    q_in = (q * wgt[6 * c:7 * c]).astype(BF16)
    k_end = (k * wgt[7 * c:8 * c]).astype(BF16)
    o_intra = _dot(scores.astype(BF16), block_diag(v))
    edge = 6 * c if rev else 7 * c - 1
    total = wgt[edge:edge + 1]
    update = jnp.where(bd, _dot_tn(v.astype(BF16), k_end), 0.0)
    return o_intra, q_in, update, total


def _hgrn_kernel(mf_ref, mb_ref, lb_ref, qf_ref, zf_ref, vf_ref, qb_ref, zb_ref, vb_ref,
                 of_ref, ob_ref, stf_ref, stb_ref):
    c = HG_CHUNK
    nc = TM // c

    @pl.when(pl.program_id(1) == 0)
    def _():
        stf_ref[...] = jnp.zeros_like(stf_ref)
        stb_ref[...] = jnp.zeros_like(stb_ref)

    def scan(q_ref, z_ref, v_ref, lb, mlev, o_ref, st_ref, rev):
        sig = _sigmoid(z_ref[0])
        logf = jnp.log(jnp.maximum(lb + (1.0 - lb) * sig, GATE_FLOOR))
        k = (1.0 - lb) * (1.0 - sig)
        order = range(nc - 1, -1, -1) if rev else range(nc)
        rows = [slice(ci * c, (ci + 1) * c) for ci in order]
        parts = [_hgrn_chunk(q_ref[0, r, :].astype(F32), k[r], logf[r], v_ref[0, r, :].astype(F32), mlev, rev)
                 for r in rows]
        st = st_ref[...]
        for r, (o_intra, q_in, update, total) in zip(rows, parts):
            o_ref[0, r, :] = o_intra + _dot_nt(q_in, st.astype(BF16))
            st = st * total + update
        st_ref[...] = st

    scan(qf_ref, zf_ref, vf_ref, lb_ref[0:1], mf_ref[...], of_ref, stf_ref, False)
    scan(qb_ref, zb_ref, vb_ref, lb_ref[1:2], mb_ref[...], ob_ref, stb_ref, True)


def _hgrn(lb, q, zff, zfb, v, n_lat):
    b, t, _ = q.shape
    nlb = n_lat // TM
    ncb = (t - n_lat) // TM
    fwd = lambda bi, i: (bi, jnp.where(i < ncb, nlb + i, i - ncb), 0)
    bwd = lambda bi, i: (bi, nlb + ncb - 1 - i, 0)
    blk = lambda im: pl.BlockSpec((1, TM, 256), im)
    mf = jnp.asarray(_hgrn_level_matrix(False), BF16)
    mb = jnp.asarray(_hgrn_level_matrix(True), BF16)
    return pl.pallas_call(
        _hgrn_kernel,
        out_shape=[jax.ShapeDtypeStruct((b, t, 256), F32)] * 2,
        grid=(b, t // TM),
        in_specs=[_const_spec((8 * HG_CHUNK, 2 * HG_CHUNK)), _const_spec((8 * HG_CHUNK, 2 * HG_CHUNK)),
                  _const_spec((2, 256)),
                  blk(fwd), blk(fwd), blk(fwd), blk(bwd), blk(bwd), blk(bwd)],
        out_specs=[blk(fwd), blk(bwd)],
        scratch_shapes=[pltpu.VMEM((256, 256), F32), pltpu.VMEM((256, 256), F32)],
        compiler_params=_cparams(("parallel", "arbitrary")),
        name="hgrn",
    )(mf, mb, lb, q, zff, v, q, zfb, v)


def _key_chunks(n_lat, t_all):
    chunks = [(lo, min(lo + KV_CHUNK, t_all)) for lo in range(0, t_all, KV_CHUNK)]
    return chunks, [(n_lat, t_all)]


def _online_softmax(lhs, k_ref, v_ref, chunks):
    rows = lhs.shape[0]
    m = jnp.full((rows, 1), -jnp.inf, F32)
    l = jnp.zeros((rows, 1), F32)
    acc = jnp.zeros((rows, LANES), F32)
    for lo, hi in chunks:
        s = _dot_nt(lhs, k_ref[0, lo:hi, :])
        m_new = jnp.maximum(m, jnp.max(s, axis=1, keepdims=True))
        alpha = jnp.exp2(m - m_new)
        p = jnp.exp2(s - m_new)
        l = alpha * l + jnp.sum(p, axis=1, keepdims=True)
        acc = alpha * acc + _dot(p.astype(BF16), v_ref[0, lo:hi, :])
        m = m_new
    return acc / l


def _mla_kernel(q_ref, k_ref, v_ref, o_ref, *, chunks):
    tq = q_ref.shape[1]
    q = q_ref[0]
    zero = jnp.zeros_like(q)
    lane2 = lax.broadcasted_iota(jnp.int32, q.shape, 1)
    lhs = jnp.concatenate([jnp.where(lane2 < LANES, q, zero), jnp.where(lane2 >= LANES, q, zero)], axis=0)
    o = _online_softmax(lhs, k_ref, v_ref, chunks)
    lane = lax.broadcasted_iota(jnp.int32, (tq, LANES), 1)
    o_ref[0] = jnp.where(lane < 64, o[:tq], o[tq:]).astype(o_ref.dtype)


def _diff_kernel(lam_ref, g_ref, q_ref, k_ref, v_ref, o_ref, *, chunks, lam_init):
    tq = q_ref.shape[1]
    q = q_ref[0]
    zero = jnp.zeros_like(q)
    lane = lax.broadcasted_iota(jnp.int32, (tq, LANES), 1)
    lhs = jnp.concatenate([jnp.where(lane // D_HEAD_DIM == r, q, zero) for r in range(4)], axis=0)
    o = _online_softmax(lhs, k_ref, v_ref, chunks)
    lam = lam_ref[0]
    o0 = o[0:tq] - lam * o[tq:2 * tq]
    o1 = o[2 * tq:3 * tq] - lam * o[3 * tq:4 * tq]
    o = jnp.where(lane < 64, o0, o1)
    o_ref[0] = (_head_rms(o, g_ref[...], LANES) * (1.0 - lam_init)).astype(o_ref.dtype)


def _attention_call(body, name, extras, extra_specs, q, k, v, tq, row0, n_rows, out_init):
    b, t, qw = q.shape
    blk0 = row0 // tq
    n_extra = len(extras)
    in_specs = list(extra_specs) + [
        pl.BlockSpec((1, tq, qw // 2), lambda bi, p, i: (bi, blk0 + i, p)),
        pl.BlockSpec((1, t, k.shape[2] // 2), lambda bi, p, i: (bi, 0, p)),
        pl.BlockSpec((1, t, LANES), lambda bi, p, i: (bi, 0, p))]
    args = list(extras) + [q, k, v]
    aliases = {}
    if out_init is not None:
        in_specs.append(pl.BlockSpec(memory_space=pl.ANY))
        aliases = {len(args): 0}
        args.append(out_init)

    def kern(*refs):
        body(*refs[:n_extra + 3], refs[-1])

    return pl.pallas_call(
        kern,
        out_shape=jax.ShapeDtypeStruct((b, t, 256), BF16),
        grid=(b, 2, n_rows // tq),
        in_specs=in_specs,
        out_specs=pl.BlockSpec((1, tq, LANES), lambda bi, p, i: (bi, blk0 + i, p)),
        input_output_aliases=aliases,
        compiler_params=_cparams(("parallel", "parallel", "arbitrary")),
        name=name,
    )(*args)


def _two_pass_attention(body_of, name, extras, extra_specs, q, k, v, n_lat, tq_lat):
    t = q.shape[1]
    lat_chunks, ctx_chunks = _key_chunks(n_lat, t)
    out = jnp.zeros((q.shape[0], t, 256), BF16)
    out = _attention_call(body_of(lat_chunks), name, extras, extra_specs, q, k, v, tq_lat, 0, n_lat, out)
    return _attention_call(body_of(ctx_chunks), name + "_ctx", extras, extra_specs, q, k, v,
                           TQ, n_lat, t - n_lat, out)


def _mla_attention(q, k, v, n_lat):
    body_of = lambda chunks: functools.partial(_mla_kernel, chunks=chunks)
    return _two_pass_attention(body_of, "mla_attention", [], [], q, k, v, n_lat, TQ_MLA)


def _diff_attention(lam, q, k, v, gain, n_lat, lam_init):
    body_of = lambda chunks: functools.partial(_diff_kernel, chunks=chunks, lam_init=lam_init)
    specs = [pl.BlockSpec(memory_space=pltpu.SMEM), _const_spec((1, LANES))]
    return _two_pass_attention(body_of, "diff_attention", [lam, gain], specs, q, k, v, n_lat, TQ)


def _merge_kernel(x_ref, mod_ref, n1_ref, n2_ref, a_ref, of_ref, ob_ref, bg_ref, hg_ref, c_ref, d_ref,
                  wg_ref, wb_ref, wo_ref, wr_ref, br_ref, xo_ref, h2_ref, ri_ref, rw_ref):
    d = D_MODEL
    x = x_ref[0]
    m = mod_ref[0, 0]
    h = (_rms(x, n1_ref[...]) * (1.0 + m[1:2]) + m[0:1]).astype(BF16)
    g = bg_ref[0].astype(F32)
    b_out = _head_rms(of_ref[0] + ob_ref[0], hg_ref[...], 256) * (g * _sigmoid(g))
    branches = (a_ref[0], b_out.astype(BF16), c_ref[0], d_ref[0])
    y = jnp.zeros((TM, d), F32)
    for i, br in enumerate(branches):
        gate = _sigmoid(_dot(h, wg_ref[:, d * i:d * (i + 1)]))
        y = y + gate * _dot(br, wb_ref[i])
    x = x + m[2:3] * _dot(y.astype(BF16), wo_ref[...])
    xo_ref[0] = x
    h2 = _rms(x, n2_ref[...]) * (1.0 + m[4:5]) + m[3:4]
    h2b = h2.astype(BF16)
    h2_ref[0] = _rows_first(h2)

    h2lo = (h2 - h2b.astype(F32)).astype(BF16)
    r = _dot(jnp.concatenate([h2b, h2lo], axis=0), wr_ref[...])
    logits = r[:TM, :LANES] + r[:TM, LANES:] + r[TM:, :LANES] + br_ref[...]
    lane = lax.broadcasted_iota(jnp.int32, (TM, LANES), 1)
    neg = -jnp.inf
    is_g = (lane >= N_EXPERTS) & (lane < N_EXPERTS + N_GROUPS)
    lg = jnp.where(is_g, logits, neg)
    mg = jnp.max(lg, axis=1, keepdims=True)
    g_val = 1.0 / jnp.sum(jnp.exp(lg - mg), axis=1, keepdims=True)
    g_idx = jnp.min(jnp.where(lg == mg, lane, 4 * LANES), axis=1, keepdims=True) - N_EXPERTS
    in_group = (lane >= g_idx * EXPERTS_PER_GROUP) & (lane < (g_idx + 1) * EXPERTS_PER_GROUP)
    le = jnp.where(in_group, logits, neg)
    m1 = jnp.max(le, axis=1, keepdims=True)
    e1 = jnp.min(jnp.where(le == m1, lane, 4 * LANES), axis=1, keepdims=True)
    le2 = jnp.where(lane == e1, neg, le)
    m2 = jnp.max(le2, axis=1, keepdims=True)
    e2 = jnp.min(jnp.where(le2 == m2, lane, 4 * LANES), axis=1, keepdims=True)
    r2 = jnp.exp(m2 - m1)
    v1 = g_val / (1.0 + r2)
    ri_ref[0] = jnp.where(lane == 0, e1, jnp.where(lane == 1, e2, 0))
    rw_ref[0] = jnp.where(lane == 0, v1, jnp.where(lane == 1, v1 * r2, 0.0))


def _merge(x, mod, n1, n2, a_o, o_f, o_b, bg, hg, c_o, d_o, wg, wb, wo, wr, br, n_lat_tiles):
    b, t, d = x.shape
    tile = lambda w: pl.BlockSpec((1, TM, w), lambda bi, i: (bi, i, 0))
    return pl.pallas_call(
        _merge_kernel,
        out_shape=[jax.ShapeDtypeStruct((b, t, d), F32), jax.ShapeDtypeStruct((b, t, ROW_SLABS, LANES), F32),
                   jax.ShapeDtypeStruct((b, t, LANES), jnp.int32), jax.ShapeDtypeStruct((b, t, LANES), F32)],
        grid=(b, t // TM),
        in_specs=[tile(d), pl.BlockSpec((1, 1, 8, d), lambda bi, i: (bi, i // n_lat_tiles, 0, 0)),
                  _const_spec((1, d)), _const_spec((1, d)),
                  tile(256), tile(256), tile(256), tile(256), _const_spec((1, 256)), tile(256), tile(256),
                  _const_spec((d, 4 * d)), _const_spec((4, 256, d)), _const_spec((d, d)),
                  _const_spec((d, 256)), _const_spec((1, LANES))],
        out_specs=[tile(d), pl.BlockSpec((1, TM, ROW_SLABS, LANES), lambda bi, i: (bi, i, 0, 0)),
                   tile(LANES), tile(LANES)],
        compiler_params=_cparams(("parallel", "parallel")),
        name="merge",
    )(x, mod, n1, n2, a_o, o_f, o_b, bg, hg, c_o, d_o, wg, wb, wo, wr, br)


def _route_positions(ri, n_tok):
    ef = ri[:, :2].reshape(-1)
    rb = 2 * TM
    oh = (ef[:, None] == jnp.arange(N_EXPERTS, dtype=jnp.int32)[None, :]).astype(F32).reshape(-1, rb, N_EXPERTS)
    tri = (jnp.arange(rb)[:, None] >= jnp.arange(rb)[None, :]).astype(F32)
    within = jnp.einsum("ij,gje->gie", tri, oh)
    tot = within[:, -1, :]
    before = jnp.cumsum(tot, axis=0) - tot
    cnt = jnp.sum(tot, axis=0).astype(jnp.int32)
    pcnt = ((cnt + MOE_TILE - 1) // MOE_TILE) * MOE_TILE
    end = jnp.cumsum(pcnt)
    start = (end - pcnt).astype(F32)
    pos = jnp.sum((within + (before + start[None, :])[:, None, :]) * oh, axis=2) - 1.0
    pos = pos.reshape(-1)
    n_tiles = 2 * n_tok // MOE_TILE + N_EXPERTS
    n_act = (end[-1] // MOE_TILE).astype(jnp.int32)
    tile_row = jnp.minimum(jnp.arange(n_tiles, dtype=jnp.int32), n_act - 1) * MOE_TILE
    tile_expert = jnp.sum((end[None, :] <= tile_row[:, None]).astype(jnp.int32), axis=1)
    tile_expert = jnp.minimum(tile_expert, N_EXPERTS - 1)
    return pos.reshape(n_tok // TM, 1, 2 * TM).astype(jnp.int32), tile_expert, n_act.reshape(1)


def _dispatch_kernel(pos_ref, h_ref, xs_in_ref, xs_ref, sem):
    del xs_in_ref

    def body(r, carry):
        for s in range(2):
            p = pos_ref[0, 0, 2 * r + s]
            pltpu.make_async_copy(h_ref.at[pl.ds(r, 1)], xs_ref.at[pl.ds(p, 1)], sem.at[s]).start(priority=s)
        return carry

    lax.fori_loop(0, TM, body, 0)
    for s in range(2):
        pltpu.make_async_copy(h_ref, h_ref, sem.at[s]).wait()


def _dispatch(pos, h2, xs_init):
    n = h2.shape[0]
    return pl.pallas_call(
        _dispatch_kernel,
        out_shape=jax.ShapeDtypeStruct(xs_init.shape, F32),
        grid=(n // TM,),
        in_specs=[pl.BlockSpec((1, 1, 2 * TM), lambda t: (t, 0, 0), memory_space=pltpu.SMEM),
                  pl.BlockSpec((TM, ROW_SLABS, LANES), lambda t: (t, 0, 0)),
                  pl.BlockSpec(memory_space=pl.ANY)],
        out_specs=pl.BlockSpec(memory_space=pl.ANY),
        scratch_shapes=[pltpu.SemaphoreType.DMA((2,))],
        input_output_aliases={2: 0},
        compiler_params=_cparams(("arbitrary",)),
        name="moe_dispatch",
    )(pos, h2, xs_init)


def _experts_kernel(te_ref, na_ref, x_ref, wgu_ref, wd_ref, o_ref):
    del te_ref
    t = pl.program_id(0)

    @pl.when(t < na_ref[0])
    def _():
        x = _slabs_first(x_ref[...])
        gu = _dot(x[0].astype(BF16), wgu_ref[0, 0:LANES, :])
        for s in range(1, ROW_SLABS):
            gu = gu + _dot(x[s].astype(BF16), wgu_ref[0, LANES * s:LANES * (s + 1), :])
        gt, up = gu[:, :EXPERT_HIDDEN], gu[:, EXPERT_HIDDEN:]
        o_ref[...] = _rows_first(_dot((gt * _sigmoid(gt) * up).astype(BF16), wd_ref[0]))

    @pl.when(t >= na_ref[0])
    def _():
        o_ref[...] = jnp.zeros_like(o_ref)


def _experts(tile_expert, n_act, xs, wgu, wd):
    p = xs.shape[0]
    d = D_MODEL
    rows = pl.BlockSpec((MOE_TILE, ROW_SLABS, LANES), lambda t, te, na: (t, 0, 0))
    return pl.pallas_call(
        _experts_kernel,
        out_shape=jax.ShapeDtypeStruct(xs.shape, F32),
        grid_spec=pltpu.PrefetchScalarGridSpec(
            num_scalar_prefetch=2, grid=(p // MOE_TILE,),
            in_specs=[rows,
                      pl.BlockSpec((1, d, 2 * EXPERT_HIDDEN), lambda t, te, na: (te[t], 0, 0)),
                      pl.BlockSpec((1, EXPERT_HIDDEN, d), lambda t, te, na: (te[t], 0, 0))],
            out_specs=rows),
        compiler_params=_cparams(("arbitrary",)),
        name="moe_experts",
    )(tile_expert, n_act, xs, wgu, wd)


def _combine_kernel(pos_ref, ys_ref, w_ref, o_ref, buf, sem):
    def body(r, carry):
        for s in range(2):
            p = pos_ref[0, 0, 2 * r + s]
            pltpu.make_async_copy(ys_ref.at[pl.ds(p, 1)], buf.at[s, pl.ds(r, 1)], sem.at[s]).start(priority=s)
        return carry

    lax.fori_loop(0, TM, body, 0)
    for s in range(2):
        pltpu.make_async_copy(buf.at[s], buf.at[s], sem.at[s]).wait()
    w = w_ref[...]
    y0, y1 = _slabs_first(buf[0]), _slabs_first(buf[1])
    for k in range(ROW_SLABS):
        o_ref[:, LANES * k:LANES * (k + 1)] = w[:, 0:1] * y0[k] + w[:, 1:2] * y1[k]


def _combine(pos, ys, rwt):
    n = rwt.shape[0]
    d = D_MODEL
    return pl.pallas_call(
        _combine_kernel,
        out_shape=jax.ShapeDtypeStruct((n, d), F32),
        grid=(n // TM,),
        in_specs=[pl.BlockSpec((1, 1, 2 * TM), lambda t: (t, 0, 0), memory_space=pltpu.SMEM),
                  pl.BlockSpec(memory_space=pl.ANY),
                  pl.BlockSpec((TM, LANES), lambda t: (t, 0))],
        out_specs=pl.BlockSpec((TM, d), lambda t: (t, 0)),
        scratch_shapes=[pltpu.VMEM((2, TM, ROW_SLABS, LANES), F32), pltpu.SemaphoreType.DMA((2,))],
        compiler_params=_cparams(("arbitrary",)),
        name="moe_combine",
    )(pos, ys, rwt)


def _moe(h2, ri, rwt, wgu, wd, xs_init):
    n = h2.shape[0]
    pos, tile_expert, n_act = _route_positions(ri, n)
    xs = _dispatch(pos, h2, xs_init)
    ys = _experts(tile_expert, n_act, xs, wgu, wd)
    return _combine(pos, ys, rwt), xs


def _final_kernel(x_ref, f_ref, mod_ref, g_ref, o_ref):
    x = x_ref[0] + mod_ref[0, 0, 5:6, :] * f_ref[0]
    o_ref[0] = _rms(x, g_ref[...])


def _final(x, f, mod, gain, n_lat):
    b, t, d = x.shape
    tile = pl.BlockSpec((1, TM, d), lambda bi, i: (bi, i, 0))
    return pl.pallas_call(
        _final_kernel,
        out_shape=jax.ShapeDtypeStruct((b, n_lat, d), F32),
        grid=(b, n_lat // TM),
        in_specs=[tile, tile, pl.BlockSpec((1, 1, 8, d), lambda bi, i: (bi, 0, 0, 0)), _const_spec((1, d))],
        out_specs=tile,
        compiler_params=_cparams(("parallel", "parallel")),
        name="final_norm",
    )(x, f, mod, gain)


def _rope_tables(n_lat, t_all, lane_rope, lane_off, dr):
    half, quarter = dr // 2, dr // 4
    inv_freq = 1.0 / (ROPE_BASE ** (jnp.arange(quarter, dtype=F32) / quarter))
    off = np.asarray(lane_off)
    use_col = off >= half
    j = (off % half) % quarter
    first = (off % half) < quarter
    tok = jnp.arange(n_lat, dtype=jnp.int32)
    row = (tok // GRID_W).astype(F32)
    col = (tok % GRID_W).astype(F32)
    pos = jnp.where(jnp.asarray(use_col)[None, :], col[:, None], row[:, None])
    ang = pos * inv_freq[jnp.asarray(j)][None, :]
    rope = jnp.asarray(lane_rope)[None, :]
    cos = jnp.where(rope, jnp.cos(ang), 1.0)
    sin = jnp.where(rope, jnp.sin(ang) * jnp.where(jnp.asarray(first), -1.0, 1.0)[None, :], 0.0)
    pad = t_all - n_lat
    w = off.shape[0]
    cos = jnp.concatenate([cos, jnp.ones((pad, w), F32)], axis=0)
    sin = jnp.concatenate([sin, jnp.zeros((pad, w), F32)], axis=0)
    return cos, sin


def _all_rope_tables(n_lat, t_all):
    la = np.arange(256)
    ta = _rope_tables(n_lat, t_all, np.ones(256, bool), la % 64, 64)
    lc = np.arange(512) % 128
    tc = _rope_tables(n_lat, t_all, (lc >= 64) & (lc < 96), np.clip(lc - 64, 0, 31), 32)
    td = _rope_tables(n_lat, t_all, np.ones(256, bool), la % 32, 32)
    return (*ta, *tc, *td)


def _permute_w_in(w):
    s = np.cumsum([0, 256, 128, 128, 256, 256, 256, 256, 256, 256, 128, 32, 256, 256, 256])
    seg = lambda i: w[:, s[i]:s[i + 1]]
    dup = lambda a: jnp.concatenate([a[:, 0:64], a[:, 0:64], a[:, 64:128], a[:, 64:128]], axis=1)
    z64 = jnp.zeros((w.shape[0], 64), w.dtype)
    z32 = jnp.zeros((w.shape[0], 32), w.dtype)
    kr = jnp.concatenate([z64, seg(10), z32] * C_HEADS, axis=1)
    cols = [seg(0), dup(seg(1)), dup(seg(2)),
            seg(3), seg(4), seg(5), seg(6), seg(7),
            seg(8), seg(9), kr,
            seg(11), seg(12), seg(13)]
    return jnp.concatenate(cols, axis=1).astype(BF16)


def _permute_mla(w_uq, w_ukv):
    z32 = jnp.zeros((w_uq.shape[0], 32), w_uq.dtype)
    qd = C_NOPE + C_ROPE
    uq = jnp.concatenate([a for h in range(C_HEADS) for a in (w_uq[:, qd * h:qd * (h + 1)], z32)], axis=1)
    z64 = jnp.zeros((w_ukv.shape[0], 64), w_ukv.dtype)
    kd = C_NOPE + C_V
    uk = jnp.concatenate([a for h in range(C_HEADS) for a in (w_ukv[:, kd * h:kd * h + C_NOPE], z64)], axis=1)
    uv = jnp.concatenate([w_ukv[:, kd * h + C_NOPE:kd * (h + 1)] for h in range(C_HEADS)], axis=1)
    return uq.astype(BF16), jnp.concatenate([uk, uv], axis=1).astype(BF16)


def kernel(x, c, ctx, c_ctx, w_mod, b_mod, norm1, norm2, w_in, w_gate, w_branch, w_out, attn_sink, hgrn_lb_logits, hgrn_norm, mla_q_norm, mla_kv_norm, mla_w_uq, mla_w_ukv, diff_lambda, diff_subln, w_router_group, b_router_group, w_router_expert, b_router_expert, w_expert_gate, w_expert_up, w_expert_down, final_norm):
    b, n_lat, d = x.shape
    n_ctx = ctx.shape[1]
    depth = w_mod.shape[0]
    t_all = n_lat + n_ctx
    assert d == D_MODEL and n_lat % TM == 0 and n_ctx % TM == 0 and n_lat % GRID_W == 0
    assert n_lat >= WIN_Q + 2 * A_WINDOW and n_lat % TQ_MLA == 0 and b <= 15
    n_lat_tiles = n_lat // TM

    c_all = jnp.concatenate([c, c_ctx[None], jnp.zeros((15 - b, d), F32)], axis=0)
    mod = _modulation(c_all, w_mod, b_mod).reshape(depth, 16, 6, d)
    mod = jnp.pad(mod, ((0, 0), (0, 0), (0, 2), (0, 0)))
    mod = jnp.stack([mod[:, :b], jnp.broadcast_to(mod[:, b:b + 1], (depth, b, 8, d))], axis=2)

    sm = jax.nn.softmax(hgrn_lb_logits.astype(F32), axis=0)
    lower_bounds = jnp.cumsum(sm, axis=0) - sm[0]
    tabs = _all_rope_tables(n_lat, t_all)

    n_tok = b * t_all
    moe_buf = jnp.zeros((2 * n_tok + N_EXPERTS * MOE_TILE, ROW_SLABS, LANES), F32)
    xs, f_prev, mod_prev = x, ctx, None
    for li in range(depth):
        lam_init = 0.8 - 0.6 * math.exp(-0.3 * li)
        lp = diff_lambda[li].astype(F32)
        lam = (jnp.exp(jnp.sum(lp[0] * lp[1])) - jnp.exp(jnp.sum(lp[2] * lp[3])) + lam_init).reshape(1)
        w_perm = _permute_w_in(w_in[li])
        wuq, wukv = _permute_mla(mla_w_uq[li], mla_w_ukv[li])
        xs, z = _inproj(xs, f_prev, mod_prev, mod[li], norm1[li][None], w_perm,
                        mla_q_norm[li][None], mla_kv_norm[li][None], wuq, wukv, tabs, n_lat_tiles)
        qa, ka, va, bq, bff, bfb, bv, bg, cq, ck, cv, dq, dk, dv = z
        a_o = _win_attention(attn_sink[li].astype(F32), qa, ka, va, n_lat)
        o_f, o_b = _hgrn(lower_bounds[li], bq, bff, bfb, bv, n_lat)
        c_o = _mla_attention(cq, ck, cv, n_lat)
        d_o = _diff_attention(lam, dq, dk, dv, jnp.tile(diff_subln[li], 2)[None], n_lat, lam_init)
        wr = jnp.concatenate([w_router_expert[li], w_router_group[li],
                              jnp.zeros((d, LANES - N_EXPERTS - N_GROUPS), F32)], axis=1)
        wr_hi = wr.astype(BF16)
        wr_lo = (wr - wr_hi.astype(F32)).astype(BF16)
        br = jnp.concatenate([b_router_expert[li], b_router_group[li],
                              jnp.zeros((LANES - N_EXPERTS - N_GROUPS,), F32)])[None]
        xs, h2, ri, rwt = _merge(xs, mod[li], norm1[li][None], norm2[li][None], a_o, o_f, o_b, bg,
                                 jnp.tile(hgrn_norm[li], B_HEADS)[None], c_o, d_o,
                                 w_gate[li].astype(BF16), w_branch[li].astype(BF16), w_out[li].astype(BF16),
                                 jnp.concatenate([wr_hi, wr_lo], axis=1), br, n_lat_tiles)
        wgu = jnp.concatenate([w_expert_gate[li], w_expert_up[li]], axis=2).astype(BF16)
        f, moe_buf = _moe(h2.reshape(n_tok, ROW_SLABS, LANES), ri.reshape(n_tok, LANES), rwt.reshape(n_tok, LANES),
                          wgu, w_expert_down[li].astype(BF16), moe_buf)
        f_prev, mod_prev = f.reshape(b, t_all, d), mod[li]
    return _final(xs, f_prev, mod_prev, final_norm[None], n_lat)
```

```python
import functools
import math

import numpy as np
import jax
import jax.numpy as jnp
from jax import lax
from jax.experimental import pallas as pl
from jax.experimental.pallas import tpu as pltpu

D_MODEL = 1024
GRID_W = 64
ROPE_BASE = 10000.0
NORM_EPS = 1e-6
MASK_VALUE = -1e30
GATE_FLOOR = 1e-30

A_HEADS, A_KV_HEADS, A_HEAD_DIM, A_WINDOW = 4, 2, 64, 128
B_HEADS, B_KEY_DIM, B_VAL_DIM = 4, 64, 64
C_HEADS, C_Q_LORA, C_KV_LORA, C_NOPE, C_ROPE, C_V = 4, 256, 128, 64, 32, 64
D_HEADS, D_HEAD_DIM = 4, 32
N_GROUPS, EXPERTS_PER_GROUP, EXPERT_HIDDEN = 4, 8, 256
N_EXPERTS = N_GROUPS * EXPERTS_PER_GROUP

TM = 256
TM_PROJ = 256
WIN_Q = 256
TQ = 256
TQ_MLA = 512
LOG2E = 1.4426950408889634
KV_CHUNK = 4096
MOE_TILE = 512
HG_CHUNK = 64
HG_LEVELS = 6
LANES = 128
ROW_SLABS = D_MODEL // LANES
VMEM_LIMIT = 52 * 1024 * 1024

OFF_A, OFF_B, OFF_C, OFF_D, N_PERM = 0, 768, 2048, 2944, 3712

F32 = jnp.float32
BF16 = jnp.bfloat16


def _dot(a, b):
    return jnp.dot(a, b, preferred_element_type=F32)


def _dot_nt(a, b):
    return lax.dot_general(a, b, (((1,), (1,)), ((), ())), preferred_element_type=F32)


def _dot_tn(a, b):
    return lax.dot_general(a, b, (((0,), (0,)), ((), ())), preferred_element_type=F32)


def _dot_hi(a, b):
    return jnp.dot(a, b, preferred_element_type=F32, precision=lax.Precision.HIGHEST)


def _sigmoid(x):
    return 1.0 / (1.0 + jnp.exp(-x))


def _rms(x, gain):
    return x * lax.rsqrt(jnp.mean(x * x, axis=-1, keepdims=True) + NORM_EPS) * gain


def _head_rms(o, gain, width):
    r = lax.broadcasted_iota(jnp.int32, (width, width), 0) // 64
    c = lax.broadcasted_iota(jnp.int32, (width, width), 1) // 64
    ones = jnp.where(r == c, 1.0 / 64.0, 0.0).astype(F32)
    ms = _dot_hi(o * o, ones)
    return o * lax.rsqrt(ms + NORM_EPS) * gain


def _rope(x, cos, sin, quarter):
    w = x.shape[-1]
    lane = lax.broadcasted_iota(jnp.int32, x.shape, 1)
    first = (lane % (2 * quarter)) < quarter
    sw = jnp.where(first, pltpu.roll(x, w - quarter, 1), pltpu.roll(x, quarter, 1))
    return x * cos + sw * sin


def _slabs_first(x):
    return pltpu.einshape("tsl->stl", x)


def _rows_first(y):
    slabs = jnp.stack([y[:, LANES * s:LANES * (s + 1)] for s in range(ROW_SLABS)], axis=0)
    return pltpu.einshape("stl->tsl", slabs)


def _cparams(sem):
    return pltpu.CompilerParams(dimension_semantics=sem, vmem_limit_bytes=VMEM_LIMIT)


def _const_spec(shape):
    n = len(shape)
    return pl.BlockSpec(shape, lambda *_: (0,) * n)


def _mod_kernel(c_ref, w_ref, b_ref, o_ref):
    c = c_ref[...]
    o_ref[0] = _dot_hi(c * _sigmoid(c), w_ref[0]) + b_ref[0]


def _modulation(c_all, w_mod, b_mod):
    depth, d, n = w_mod.shape
    nb = 1536
    return pl.pallas_call(
        _mod_kernel,
        out_shape=jax.ShapeDtypeStruct((depth, 16, n), F32),
        grid=(depth, n // nb),
        in_specs=[pl.BlockSpec((16, d), lambda l, j: (0, 0)),
                  pl.BlockSpec((1, d, nb), lambda l, j: (l, 0, j)),
                  pl.BlockSpec((1, 1, nb), lambda l, j: (l, 0, j))],
        out_specs=pl.BlockSpec((1, 16, nb), lambda l, j: (l, 0, j)),
        compiler_params=_cparams(("arbitrary", "arbitrary")),
        name="modulation",
    )(c_all, w_mod, b_mod.reshape(depth, 1, n))


def _inproj_kernel(*refs, has_prev, n_lat_tiles):
    if has_prev:
        x_ref, f_ref, modp_ref = refs[:3]
        refs = refs[3:]
        x = x_ref[0] + modp_ref[0, 0, 5:6, :] * f_ref[0]
    else:
        x_ref, ctx_ref = refs[:2]
        refs = refs[2:]
        ctx = ctx_ref[0]
        if ctx.shape[0] < x_ref.shape[1]:
            ctx = jnp.concatenate([ctx] * (x_ref.shape[1] // ctx.shape[0]), axis=0)
        x = jnp.where(pl.program_id(1) < n_lat_tiles, x_ref[0], ctx)
    (mod_ref, n1_ref, w_ref, qn_ref, kvn_ref, wuq_ref, wukv_ref,
     cosa_ref, sina_ref, cosc_ref, sinc_ref, cosd_ref, sind_ref) = refs[:13]
    (xo_ref, qa_ref, ka_ref, va_ref, bq_ref, bff_ref, bfb_ref, bv_ref, bg_ref,
     cq_ref, ck_ref, cv_ref, dq_ref, dk_ref, dv_ref) = refs[13:]
    xo_ref[0] = x
    m = mod_ref[0, 0]
    h = (_rms(x, n1_ref[...]) * (1.0 + m[1:2]) + m[0:1]).astype(BF16)

    z = _dot(h, w_ref[:, OFF_A:OFF_A + 768])
    cosa, sina = cosa_ref[...], sina_ref[...]
    qa_ref[0] = (_rope(z[:, 0:256], cosa, sina, 16) * (A_HEAD_DIM ** -0.5)).astype(BF16)
    ka_ref[0] = _rope(z[:, 256:512], cosa, sina, 16).astype(BF16)
    va_ref[0] = z[:, 512:768].astype(BF16)

    z = _dot(h, w_ref[:, OFF_B:OFF_B + 1280])
    bq_ref[0] = z[:, 0:256].astype(BF16)
    bff_ref[0] = z[:, 256:512]
    bfb_ref[0] = z[:, 512:768]
    bv_ref[0] = z[:, 768:1024].astype(BF16)
    bg_ref[0] = z[:, 1024:1280].astype(BF16)

    z = _dot(h, w_ref[:, OFF_C:OFF_C + 896])
    cosc, sinc = cosc_ref[...], sinc_ref[...]
    cq = _rms(z[:, 0:256], qn_ref[...]).astype(BF16)
    q = _rope(_dot(cq, wuq_ref[...]), cosc, sinc, 8)
    cq_ref[0] = (q * ((C_NOPE + C_ROPE) ** -0.5 * LOG2E)).astype(BF16)
    ckv = _rms(z[:, 256:384], kvn_ref[...]).astype(BF16)
    kv = _dot(ckv, wukv_ref[...])
    ck_ref[0] = (kv[:, 0:512] + _rope(z[:, 384:896], cosc, sinc, 8)).astype(BF16)
    cv_ref[0] = kv[:, 512:768].astype(BF16)

    z = _dot(h, w_ref[:, OFF_D:OFF_D + 768])
    cosd, sind = cosd_ref[...], sind_ref[...]
    dq_ref[0] = (_rope(z[:, 0:256], cosd, sind, 8) * (D_HEAD_DIM ** -0.5 * LOG2E)).astype(BF16)
    dk_ref[0] = _rope(z[:, 256:512], cosd, sind, 8).astype(BF16)
    dv_ref[0] = z[:, 512:768].astype(BF16)


def _inproj(x, other, mod_prev, mod, n1, w_perm, qn, kvn, wuq, wukv, tabs, n_lat_tiles):
    has_prev = mod_prev is not None
    b, _, d = x.shape
    t = x.shape[1] if has_prev else x.shape[1] + other.shape[1]
    tm = TM_PROJ
    nt = pl.cdiv(t, tm)
    tile = lambda w: pl.BlockSpec((1, tm, w), lambda bi, i: (bi, i, 0))
    modspec = pl.BlockSpec((1, 1, 8, d), lambda bi, i: (bi, i // n_lat_tiles, 0, 0))
    tab = lambda w: pl.BlockSpec((tm, w), lambda bi, i: (i, 0))
    if has_prev:
        in_specs, args = [tile(d), tile(d), modspec], [x, other, mod_prev]
    else:
        lat = pl.BlockSpec((1, tm, d), lambda bi, i: (bi, jnp.minimum(i, n_lat_tiles - 1), 0))
        ctx = pl.BlockSpec((1, min(tm, other.shape[1]), d),
                           lambda bi, i: (bi, jnp.maximum(i - n_lat_tiles, 0), 0))
        in_specs, args = [lat, ctx], [x, other]
    in_specs += [modspec, _const_spec((1, d)), _const_spec((d, N_PERM)), _const_spec((1, 256)),
                 _const_spec((1, 128)), _const_spec((256, 512)), _const_spec((128, 768)),
                 tab(256), tab(256), tab(512), tab(512), tab(256), tab(256)]
    args += [mod, n1, w_perm, qn, kvn, wuq, wukv, *tabs]
    widths = [(256, BF16), (256, BF16), (256, BF16),
              (256, BF16), (256, F32), (256, F32), (256, BF16), (256, BF16),
              (512, BF16), (512, BF16), (256, BF16),
              (256, BF16), (256, BF16), (256, BF16)]
    widths = [(d, F32)] + widths
    out_shape = [jax.ShapeDtypeStruct((b, t, w), dt) for w, dt in widths]
    out_specs = [tile(w) for w, _ in widths]
    res = pl.pallas_call(
        functools.partial(_inproj_kernel, has_prev=has_prev, n_lat_tiles=n_lat_tiles),
        out_shape=out_shape, grid=(b, nt), in_specs=in_specs, out_specs=out_specs,
        compiler_params=_cparams(("parallel", "parallel")),
        name="inproj",
    )(*args)
    return res[0], res[1:]


def _win_kernel(sink_ref, q_ref, k_ref, v_ref, o_ref, *, n_lat, t_all):
    w = A_WINDOW
    qb = WIN_Q
    kb = qb + 2 * w
    n = pl.program_id(1)
    start = pl.multiple_of(jnp.clip(n * qb - w, 0, n_lat - kb), w)
    q = q_ref[0]
    lane = lax.broadcasted_iota(jnp.int32, (qb, LANES), 1)
    row = lax.broadcasted_iota(jnp.int32, (2 * qb, kb), 0)
    col = lax.broadcasted_iota(jnp.int32, (2 * qb, kb), 1)
    rel = (start + col) - (n * qb + row % qb)
    valid = (jnp.abs(rel) <= w) & (n < n_lat // qb)
    row1 = lax.broadcasted_iota(jnp.int32, (2 * qb, 1), 0)
    for j in range(A_KV_HEADS):
        sl = slice(LANES * j, LANES * (j + 1))
        qp = q[:, sl]
        zero = jnp.zeros_like(qp)
        lhs = jnp.concatenate([jnp.where(lane < 64, qp, zero), jnp.where(lane >= 64, qp, zero)], axis=0)
        sb = _dot_nt(lhs, k_ref[0, pl.ds(start, kb), sl])
        sc = _dot_nt(lhs, k_ref[0, n_lat:t_all, sl])
        sb = jnp.where(valid, sb, MASK_VALUE)
        sink = jnp.where(row1 < qb, sink_ref[2 * j], sink_ref[2 * j + 1])
        m = jnp.maximum(jnp.maximum(jnp.max(sb, axis=1, keepdims=True), jnp.max(sc, axis=1, keepdims=True)), sink)
        pb = jnp.exp(sb - m)
        pc = jnp.exp(sc - m)
        den = jnp.sum(pb, axis=1, keepdims=True) + jnp.sum(pc, axis=1, keepdims=True) + jnp.exp(sink - m)
        o = (_dot(pb.astype(BF16), v_ref[0, pl.ds(start, kb), sl])
             + _dot(pc.astype(BF16), v_ref[0, n_lat:t_all, sl])) / den
        o_ref[0, :, sl] = jnp.where(lane < 64, o[:qb], o[qb:]).astype(o_ref.dtype)


def _win_attention(sink, q, k, v, n_lat):
    b, t, _ = q.shape
    qb = WIN_Q
    return pl.pallas_call(
        functools.partial(_win_kernel, n_lat=n_lat, t_all=t),
        out_shape=jax.ShapeDtypeStruct((b, t, 256), BF16),
        grid=(b, t // qb),
        in_specs=[pl.BlockSpec(memory_space=pltpu.SMEM),
                  pl.BlockSpec((1, qb, 256), lambda bi, i: (bi, i, 0)),
                  pl.BlockSpec((1, t, 256), lambda bi, i: (bi, 0, 0)),
                  pl.BlockSpec((1, t, 256), lambda bi, i: (bi, 0, 0))],
        out_specs=pl.BlockSpec((1, qb, 256), lambda bi, i: (bi, i, 0)),
        compiler_params=_cparams(("parallel", "arbitrary")),
        name="win_attention",
    )(sink, q, k, v)


def _hgrn_level_matrix(rev):
    c = HG_CHUNK
    m = np.zeros((8, c, c), np.float32)
    for lvl in range(HG_LEVELS):
        size = c >> lvl
        for t in range(c):
            mid = (t // size) * size + size // 2
            upper = t >= mid
            if not rev:
                rng = range(mid, t + 1) if upper else range(t + 1, mid)
            else:
                rng = range(mid, t) if upper else range(t, mid)
            m[lvl, t, list(rng)] = 1.0
    for t in range(c):
        if not rev:
            m[6, t, :t + 1] = 1.0
            m[7, t, t + 1:] = 1.0
        else:
            m[6, t, t:] = 1.0
            m[7, t, :t] = 1.0
    m = m.reshape(8 * c, c)
    return np.concatenate([m, m], axis=1)


def _hgrn_chunk(q, k, logf, v, mlev, rev):
    c = HG_CHUNK
    hi = logf.astype(BF16)
    lo = (logf - hi.astype(F32)).astype(BF16)
    wgt = jnp.exp(_dot(mlev, jnp.concatenate([hi, lo], axis=0)))
    t = lax.broadcasted_iota(jnp.int32, (c, 256), 0)
    s_col = lax.broadcasted_iota(jnp.int32, (c, 256), 1) % c
    r_bd = lax.broadcasted_iota(jnp.int32, (256, 256), 0) // 64
    c_bd = lax.broadcasted_iota(jnp.int32, (256, 256), 1) // 64
    bd = r_bd == c_bd

    def block_diag(a):
        return jnp.where(bd, jnp.concatenate([a] * 4, axis=0), 0.0).astype(BF16)

    scores = jnp.where(t == s_col, _dot_nt(q.astype(BF16), block_diag(k)), 0.0)
    for lvl in range(HG_LEVELS):
        wl = wgt[c * lvl:c * (lvl + 1)]
        q_side = ((t >> (HG_LEVELS - 1 - lvl)) & 1) == (0 if rev else 1)
        ql = jnp.where(q_side, q * wl, 0.0).astype(BF16)
        kl = jnp.where(q_side, 0.0, k * wl)
        s = _dot_nt(ql, block_diag(kl))
        if lvl > 0:
            sh = HG_LEVELS - lvl
            s = jnp.where((t >> sh) == (s_col >> sh), s, 0.0)
        scores = scores + s
    q_in = (q * wgt[6 * c:7 * c]).astype(BF16)
    k_end = (k * wgt[7 * c:8 * c]).astype(BF16)
    o_intra = _dot(scores.astype(BF16), block_diag(v))
    edge = 6 * c if rev else 7 * c - 1
    total = wgt[edge:edge + 1]
    update = jnp.where(bd, _dot_tn(v.astype(BF16), k_end), 0.0)
    return o_intra, q_in, update, total


def _hgrn_kernel(mf_ref, mb_ref, lb_ref, qf_ref, zf_ref, vf_ref, qb_ref, zb_ref, vb_ref,
                 of_ref, ob_ref, stf_ref, stb_ref):
    c = HG_CHUNK
    nc = TM // c

    @pl.when(pl.program_id(1) == 0)
    def _():
        stf_ref[...] = jnp.zeros_like(stf_ref)
        stb_ref[...] = jnp.zeros_like(stb_ref)

    def scan(q_ref, z_ref, v_ref, lb, mlev, o_ref, st_ref, rev):
        sig = _sigmoid(z_ref[0])
        logf = jnp.log(jnp.maximum(lb + (1.0 - lb) * sig, GATE_FLOOR))
        k = (1.0 - lb) * (1.0 - sig)
        order = range(nc - 1, -1, -1) if rev else range(nc)
        rows = [slice(ci * c, (ci + 1) * c) for ci in order]
        parts = [_hgrn_chunk(q_ref[0, r, :].astype(F32), k[r], logf[r], v_ref[0, r, :].astype(F32), mlev, rev)
                 for r in rows]
        st = st_ref[...]
        for r, (o_intra, q_in, update, total) in zip(rows, parts):
            o_ref[0, r, :] = o_intra + _dot_nt(q_in, st.astype(BF16))
            st = st * total + update
        st_ref[...] = st

    scan(qf_ref, zf_ref, vf_ref, lb_ref[0:1], mf_ref[...], of_ref, stf_ref, False)
    scan(qb_ref, zb_ref, vb_ref, lb_ref[1:2], mb_ref[...], ob_ref, stb_ref, True)


def _hgrn(lb, q, zff, zfb, v, n_lat):
    b, t, _ = q.shape
    nlb = n_lat // TM
    ncb = (t - n_lat) // TM
    fwd = lambda bi, i: (bi, jnp.where(i < ncb, nlb + i, i - ncb), 0)
    bwd = lambda bi, i: (bi, nlb + ncb - 1 - i, 0)
    blk = lambda im: pl.BlockSpec((1, TM, 256), im)
    mf = jnp.asarray(_hgrn_level_matrix(False), BF16)
    mb = jnp.asarray(_hgrn_level_matrix(True), BF16)
    return pl.pallas_call(
        _hgrn_kernel,
        out_shape=[jax.ShapeDtypeStruct((b, t, 256), F32)] * 2,
        grid=(b, t // TM),
        in_specs=[_const_spec((8 * HG_CHUNK, 2 * HG_CHUNK)), _const_spec((8 * HG_CHUNK, 2 * HG_CHUNK)),
                  _const_spec((2, 256)),
                  blk(fwd), blk(fwd), blk(fwd), blk(bwd), blk(bwd), blk(bwd)],
        out_specs=[blk(fwd), blk(bwd)],
        scratch_shapes=[pltpu.VMEM((256, 256), F32), pltpu.VMEM((256, 256), F32)],
        compiler_params=_cparams(("parallel", "arbitrary")),
        name="hgrn",
    )(mf, mb, lb, q, zff, v, q, zfb, v)


def _key_chunks(n_lat, t_all):
    chunks = [(lo, min(lo + KV_CHUNK, t_all)) for lo in range(0, t_all, KV_CHUNK)]
    return chunks, [(n_lat, t_all)]


def _online_softmax(lhs, k_ref, v_ref, chunks):
    rows = lhs.shape[0]
    m = jnp.full((rows, 1), -jnp.inf, F32)
    l = jnp.zeros((rows, 1), F32)
    acc = jnp.zeros((rows, LANES), F32)
    for lo, hi in chunks:
        s = _dot_nt(lhs, k_ref[0, lo:hi, :])
        m_new = jnp.maximum(m, jnp.max(s, axis=1, keepdims=True))
        alpha = jnp.exp2(m - m_new)
        p = jnp.exp2(s - m_new)
        l = alpha * l + jnp.sum(p, axis=1, keepdims=True)
        acc = alpha * acc + _dot(p.astype(BF16), v_ref[0, lo:hi, :])
        m = m_new
    return acc / l


def _mla_kernel(q_ref, k_ref, v_ref, o_ref, *, chunks):
    tq = q_ref.shape[1]
    q = q_ref[0]
    zero = jnp.zeros_like(q)
    lane2 = lax.broadcasted_iota(jnp.int32, q.shape, 1)
    lhs = jnp.concatenate([jnp.where(lane2 < LANES, q, zero), jnp.where(lane2 >= LANES, q, zero)], axis=0)
    o = _online_softmax(lhs, k_ref, v_ref, chunks)
    lane = lax.broadcasted_iota(jnp.int32, (tq, LANES), 1)
    o_ref[0] = jnp.where(lane < 64, o[:tq], o[tq:]).astype(o_ref.dtype)


def _diff_kernel(lam_ref, g_ref, q_ref, k_ref, v_ref, o_ref, *, chunks, lam_init):
    tq = q_ref.shape[1]
    q = q_ref[0]
    zero = jnp.zeros_like(q)
    lane = lax.broadcasted_iota(jnp.int32, (tq, LANES), 1)
    lhs = jnp.concatenate([jnp.where(lane // D_HEAD_DIM == r, q, zero) for r in range(4)], axis=0)
    o = _online_softmax(lhs, k_ref, v_ref, chunks)
    lam = lam_ref[0]
    o0 = o[0:tq] - lam * o[tq:2 * tq]
    o1 = o[2 * tq:3 * tq] - lam * o[3 * tq:4 * tq]
    o = jnp.where(lane < 64, o0, o1)
    o_ref[0] = (_head_rms(o, g_ref[...], LANES) * (1.0 - lam_init)).astype(o_ref.dtype)


def _attention_call(body, name, extras, extra_specs, q, k, v, tq, row0, n_rows, out_init):
    b, t, qw = q.shape
    blk0 = row0 // tq
    n_extra = len(extras)
    in_specs = list(extra_specs) + [
        pl.BlockSpec((1, tq, qw // 2), lambda bi, p, i: (bi, blk0 + i, p)),
        pl.BlockSpec((1, t, k.shape[2] // 2), lambda bi, p, i: (bi, 0, p)),
        pl.BlockSpec((1, t, LANES), lambda bi, p, i: (bi, 0, p))]
    args = list(extras) + [q, k, v]
    aliases = {}
    if out_init is not None:
        in_specs.append(pl.BlockSpec(memory_space=pl.ANY))
        aliases = {len(args): 0}
        args.append(out_init)

    def kern(*refs):
        body(*refs[:n_extra + 3], refs[-1])

    return pl.pallas_call(
        kern,
        out_shape=jax.ShapeDtypeStruct((b, t, 256), BF16),
        grid=(b, 2, n_rows // tq),
        in_specs=in_specs,
        out_specs=pl.BlockSpec((1, tq, LANES), lambda bi, p, i: (bi, blk0 + i, p)),
        input_output_aliases=aliases,
        compiler_params=_cparams(("parallel", "parallel", "arbitrary")),
        name=name,
    )(*args)


def _two_pass_attention(body_of, name, extras, extra_specs, q, k, v, n_lat, tq_lat, with_ctx):
    t = q.shape[1]
    lat_chunks, ctx_chunks = _key_chunks(n_lat, t)
    out = jnp.zeros((q.shape[0], t, 256), BF16)
    out = _attention_call(body_of(lat_chunks), name, extras, extra_specs, q, k, v, tq_lat, 0, n_lat, out)
    if not with_ctx:
        return out
    return _attention_call(body_of(ctx_chunks), name + "_ctx", extras, extra_specs, q, k, v,
                           TQ, n_lat, t - n_lat, out)


def _mla_attention(q, k, v, n_lat, with_ctx):
    body_of = lambda chunks: functools.partial(_mla_kernel, chunks=chunks)
    return _two_pass_attention(body_of, "mla_attention", [], [], q, k, v, n_lat, TQ_MLA, with_ctx)


def _diff_attention(lam, q, k, v, gain, n_lat, lam_init, with_ctx):
    body_of = lambda chunks: functools.partial(_diff_kernel, chunks=chunks, lam_init=lam_init)
    specs = [pl.BlockSpec(memory_space=pltpu.SMEM), _const_spec((1, LANES))]
    return _two_pass_attention(body_of, "diff_attention", [lam, gain], specs, q, k, v, n_lat, TQ, with_ctx)


def _merge_kernel(x_ref, mod_ref, n1_ref, n2_ref, a_ref, of_ref, ob_ref, bg_ref, hg_ref, c_ref, d_ref,
                  wg_ref, wb_ref, wo_ref, wr_ref, br_ref, xo_ref, h2_ref, ri_ref, rw_ref):
    d = D_MODEL
    tm = x_ref.shape[1]
    x = x_ref[0]
    m = mod_ref[0, 0]
    h = (_rms(x, n1_ref[...]) * (1.0 + m[1:2]) + m[0:1]).astype(BF16)
    g = bg_ref[0].astype(F32)
    b_out = _head_rms(of_ref[0] + ob_ref[0], hg_ref[...], 256) * (g * _sigmoid(g))
    branches = (a_ref[0], b_out.astype(BF16), c_ref[0], d_ref[0])
    y = jnp.zeros((tm, d), F32)
    for i, br in enumerate(branches):
        gate = _sigmoid(_dot(h, wg_ref[:, d * i:d * (i + 1)]))
        y = y + gate * _dot(br, wb_ref[i])
    x = x + m[2:3] * _dot(y.astype(BF16), wo_ref[...])
    xo_ref[0] = x
    h2 = _rms(x, n2_ref[...]) * (1.0 + m[4:5]) + m[3:4]
    h2b = h2.astype(BF16)
    h2_ref[0] = _rows_first(h2)

    h2lo = (h2 - h2b.astype(F32)).astype(BF16)
    r = _dot(jnp.concatenate([h2b, h2lo], axis=0), wr_ref[...])
    logits = r[:tm, :LANES] + r[:tm, LANES:] + r[tm:, :LANES] + br_ref[...]
    lane = lax.broadcasted_iota(jnp.int32, (tm, LANES), 1)
    neg = -jnp.inf
    is_g = (lane >= N_EXPERTS) & (lane < N_EXPERTS + N_GROUPS)
    lg = jnp.where(is_g, logits, neg)
    mg = jnp.max(lg, axis=1, keepdims=True)
    g_val = 1.0 / jnp.sum(jnp.exp(lg - mg), axis=1, keepdims=True)
    g_idx = jnp.min(jnp.where(lg == mg, lane, 4 * LANES), axis=1, keepdims=True) - N_EXPERTS
    in_group = (lane >= g_idx * EXPERTS_PER_GROUP) & (lane < (g_idx + 1) * EXPERTS_PER_GROUP)
    le = jnp.where(in_group, logits, neg)
    m1 = jnp.max(le, axis=1, keepdims=True)
    e1 = jnp.min(jnp.where(le == m1, lane, 4 * LANES), axis=1, keepdims=True)
    le2 = jnp.where(lane == e1, neg, le)
    m2 = jnp.max(le2, axis=1, keepdims=True)
    e2 = jnp.min(jnp.where(le2 == m2, lane, 4 * LANES), axis=1, keepdims=True)
    r2 = jnp.exp(m2 - m1)
    v1 = g_val / (1.0 + r2)
    ri_ref[0] = jnp.where(lane == 0, e1, jnp.where(lane == 1, e2, 0))
    rw_ref[0] = jnp.where(lane == 0, v1, jnp.where(lane == 1, v1 * r2, 0.0))


def _merge(x, mod, n1, n2, a_o, o_f, o_b, bg, hg, c_o, d_o, wg, wb, wo, wr, br, n_lat_tiles, t):
    b, _, d = x.shape
    tm = TM_PROJ
    tile = lambda w: pl.BlockSpec((1, tm, w), lambda bi, i: (bi, i, 0))
    return pl.pallas_call(
        _merge_kernel,
        out_shape=[jax.ShapeDtypeStruct((b, t, d), F32), jax.ShapeDtypeStruct((b, t, ROW_SLABS, LANES), F32),
                   jax.ShapeDtypeStruct((b, t, LANES), jnp.int32), jax.ShapeDtypeStruct((b, t, LANES), F32)],
        grid=(b, pl.cdiv(t, tm)),
        in_specs=[tile(d), pl.BlockSpec((1, 1, 8, d), lambda bi, i: (bi, i // n_lat_tiles, 0, 0)),
                  _const_spec((1, d)), _const_spec((1, d)),
                  tile(256), tile(256), tile(256), tile(256), _const_spec((1, 256)), tile(256), tile(256),
                  _const_spec((d, 4 * d)), _const_spec((4, 256, d)), _const_spec((d, d)),
                  _const_spec((d, 256)), _const_spec((1, LANES))],
        out_specs=[tile(d), pl.BlockSpec((1, tm, ROW_SLABS, LANES), lambda bi, i: (bi, i, 0, 0)),
                   tile(LANES), tile(LANES)],
        compiler_params=_cparams(("parallel", "parallel")),
        name="merge",
    )(x, mod, n1, n2, a_o, o_f, o_b, bg, hg, c_o, d_o, wg, wb, wo, wr, br)


def _route_positions(ri, n_tok):
    ef = ri[:, :2].reshape(-1)
    rb = 2 * TM
    oh = (ef[:, None] == jnp.arange(N_EXPERTS, dtype=jnp.int32)[None, :]).astype(F32).reshape(-1, rb, N_EXPERTS)
    tri = (jnp.arange(rb)[:, None] >= jnp.arange(rb)[None, :]).astype(F32)
    within = jnp.einsum("ij,gje->gie", tri, oh)
    tot = within[:, -1, :]
    before = jnp.cumsum(tot, axis=0) - tot
    cnt = jnp.sum(tot, axis=0).astype(jnp.int32)
    pcnt = ((cnt + MOE_TILE - 1) // MOE_TILE) * MOE_TILE
    end = jnp.cumsum(pcnt)
    start = (end - pcnt).astype(F32)
    pos = jnp.sum((within + (before + start[None, :])[:, None, :]) * oh, axis=2) - 1.0
    pos = pos.reshape(-1)
    n_tiles = 2 * n_tok // MOE_TILE + N_EXPERTS
    n_act = (end[-1] // MOE_TILE).astype(jnp.int32)
    tile_row = jnp.minimum(jnp.arange(n_tiles, dtype=jnp.int32), n_act - 1) * MOE_TILE
    tile_expert = jnp.sum((end[None, :] <= tile_row[:, None]).astype(jnp.int32), axis=1)
    tile_expert = jnp.minimum(tile_expert, N_EXPERTS - 1)
    return pos.reshape(n_tok // TM, 1, 2 * TM).astype(jnp.int32), tile_expert, n_act.reshape(1)


def _dispatch_kernel(pos_ref, h_ref, xs_in_ref, xs_ref, sem):
    del xs_in_ref

    def body(r, carry):
        for s in range(2):
            p = pos_ref[0, 0, 2 * r + s]
            pltpu.make_async_copy(h_ref.at[pl.ds(r, 1)], xs_ref.at[pl.ds(p, 1)], sem.at[s]).start(priority=s)
        return carry

    lax.fori_loop(0, TM, body, 0)
    for s in range(2):
        pltpu.make_async_copy(h_ref, h_ref, sem.at[s]).wait()


def _dispatch(pos, h2, xs_init):
    n = h2.shape[0]
    return pl.pallas_call(
        _dispatch_kernel,
        out_shape=jax.ShapeDtypeStruct(xs_init.shape, F32),
        grid=(n // TM,),
        in_specs=[pl.BlockSpec((1, 1, 2 * TM), lambda t: (t, 0, 0), memory_space=pltpu.SMEM),
                  pl.BlockSpec((TM, ROW_SLABS, LANES), lambda t: (t, 0, 0)),
                  pl.BlockSpec(memory_space=pl.ANY)],
        out_specs=pl.BlockSpec(memory_space=pl.ANY),
        scratch_shapes=[pltpu.SemaphoreType.DMA((2,))],
        input_output_aliases={2: 0},
        compiler_params=_cparams(("arbitrary",)),
        name="moe_dispatch",
    )(pos, h2, xs_init)


def _experts_kernel(te_ref, na_ref, x_ref, wgu_ref, wd_ref, o_ref):
    del te_ref
    t = pl.program_id(0)

    @pl.when(t < na_ref[0])
    def _():
        x = _slabs_first(x_ref[...])
        gu = _dot(x[0].astype(BF16), wgu_ref[0, 0:LANES, :])
        for s in range(1, ROW_SLABS):
            gu = gu + _dot(x[s].astype(BF16), wgu_ref[0, LANES * s:LANES * (s + 1), :])
        gt, up = gu[:, :EXPERT_HIDDEN], gu[:, EXPERT_HIDDEN:]
        o_ref[...] = _rows_first(_dot((gt * _sigmoid(gt) * up).astype(BF16), wd_ref[0]))

    @pl.when(t >= na_ref[0])
    def _():
        o_ref[...] = jnp.zeros_like(o_ref)


def _experts(tile_expert, n_act, xs, wgu, wd):
    n_tiles = tile_expert.shape[0]
    d = D_MODEL
    rows = pl.BlockSpec((MOE_TILE, ROW_SLABS, LANES), lambda t, te, na: (t, 0, 0))
    return pl.pallas_call(
        _experts_kernel,
        out_shape=jax.ShapeDtypeStruct((n_tiles * MOE_TILE, ROW_SLABS, LANES), F32),
        grid_spec=pltpu.PrefetchScalarGridSpec(
            num_scalar_prefetch=2, grid=(n_tiles,),
            in_specs=[rows,
                      pl.BlockSpec((1, d, 2 * EXPERT_HIDDEN), lambda t, te, na: (te[t], 0, 0)),
                      pl.BlockSpec((1, EXPERT_HIDDEN, d), lambda t, te, na: (te[t], 0, 0))],
            out_specs=rows),
        compiler_params=_cparams(("arbitrary",)),
        name="moe_experts",
    )(tile_expert, n_act, xs, wgu, wd)


def _combine_kernel(pos_ref, ys_ref, w_ref, o_ref, buf, sem):
    def body(r, carry):
        for s in range(2):
            p = pos_ref[0, 0, 2 * r + s]
            pltpu.make_async_copy(ys_ref.at[pl.ds(p, 1)], buf.at[s, pl.ds(r, 1)], sem.at[s]).start(priority=s)
        return carry

    lax.fori_loop(0, TM, body, 0)
    for s in range(2):
        pltpu.make_async_copy(buf.at[s], buf.at[s], sem.at[s]).wait()
    w = w_ref[...]
    y0, y1 = _slabs_first(buf[0]), _slabs_first(buf[1])
    for k in range(ROW_SLABS):
        o_ref[:, LANES * k:LANES * (k + 1)] = w[:, 0:1] * y0[k] + w[:, 1:2] * y1[k]


def _combine(pos, ys, rwt):
    n = rwt.shape[0]
    d = D_MODEL
    return pl.pallas_call(
        _combine_kernel,
        out_shape=jax.ShapeDtypeStruct((n, d), F32),
        grid=(n // TM,),
        in_specs=[pl.BlockSpec((1, 1, 2 * TM), lambda t: (t, 0, 0), memory_space=pltpu.SMEM),
                  pl.BlockSpec(memory_space=pl.ANY),
                  pl.BlockSpec((TM, LANES), lambda t: (t, 0))],
        out_specs=pl.BlockSpec((TM, d), lambda t: (t, 0)),
        scratch_shapes=[pltpu.VMEM((2, TM, ROW_SLABS, LANES), F32), pltpu.SemaphoreType.DMA((2,))],
        compiler_params=_cparams(("arbitrary",)),
        name="moe_combine",
    )(pos, ys, rwt)


def _moe(h2, ri, rwt, wgu, wd, xs_init):
    n = h2.shape[0]
    pos, tile_expert, n_act = _route_positions(ri, n)
    xs = _dispatch(pos, h2, xs_init)
    ys = _experts(tile_expert, n_act, xs, wgu, wd)
    return _combine(pos, ys, rwt), xs


def _final_kernel(x_ref, f_ref, mod_ref, g_ref, o_ref):
    x = x_ref[0] + mod_ref[0, 0, 5:6, :] * f_ref[0]
    o_ref[0] = _rms(x, g_ref[...])


def _final(x, f, mod, gain, n_lat):
    b, t, d = x.shape
    tile = pl.BlockSpec((1, TM, d), lambda bi, i: (bi, i, 0))
    return pl.pallas_call(
        _final_kernel,
        out_shape=jax.ShapeDtypeStruct((b, n_lat, d), F32),
        grid=(b, n_lat // TM),
        in_specs=[tile, tile, pl.BlockSpec((1, 1, 8, d), lambda bi, i: (bi, 0, 0, 0)), _const_spec((1, d))],
        out_specs=tile,
        compiler_params=_cparams(("parallel", "parallel")),
        name="final_norm",
    )(x, f, mod, gain)


def _rope_tables(n_lat, t_all, lane_rope, lane_off, dr):
    half, quarter = dr // 2, dr // 4
    inv_freq = 1.0 / (ROPE_BASE ** (jnp.arange(quarter, dtype=F32) / quarter))
    off = np.asarray(lane_off)
    use_col = off >= half
    j = (off % half) % quarter
    first = (off % half) < quarter
    tok = jnp.arange(n_lat, dtype=jnp.int32)
    row = (tok // GRID_W).astype(F32)
    col = (tok % GRID_W).astype(F32)
    pos = jnp.where(jnp.asarray(use_col)[None, :], col[:, None], row[:, None])
    ang = pos * inv_freq[jnp.asarray(j)][None, :]
    rope = jnp.asarray(lane_rope)[None, :]
    cos = jnp.where(rope, jnp.cos(ang), 1.0)
    sin = jnp.where(rope, jnp.sin(ang) * jnp.where(jnp.asarray(first), -1.0, 1.0)[None, :], 0.0)
    pad = t_all - n_lat
    w = off.shape[0]
    cos = jnp.concatenate([cos, jnp.ones((pad, w), F32)], axis=0)
    sin = jnp.concatenate([sin, jnp.zeros((pad, w), F32)], axis=0)
    return cos, sin


def _all_rope_tables(n_lat, t_all):
    la = np.arange(256)
    ta = _rope_tables(n_lat, t_all, np.ones(256, bool), la % 64, 64)
    lc = np.arange(512) % 128
    tc = _rope_tables(n_lat, t_all, (lc >= 64) & (lc < 96), np.clip(lc - 64, 0, 31), 32)
    td = _rope_tables(n_lat, t_all, np.ones(256, bool), la % 32, 32)
    return (*ta, *tc, *td)


def _permute_w_in(w):
    s = np.cumsum([0, 256, 128, 128, 256, 256, 256, 256, 256, 256, 128, 32, 256, 256, 256])
    seg = lambda i: w[:, s[i]:s[i + 1]]
    dup = lambda a: jnp.concatenate([a[:, 0:64], a[:, 0:64], a[:, 64:128], a[:, 64:128]], axis=1)
    z64 = jnp.zeros((w.shape[0], 64), w.dtype)
    z32 = jnp.zeros((w.shape[0], 32), w.dtype)
    kr = jnp.concatenate([z64, seg(10), z32] * C_HEADS, axis=1)
    cols = [seg(0), dup(seg(1)), dup(seg(2)),
            seg(3), seg(4), seg(5), seg(6), seg(7),
            seg(8), seg(9), kr,
            seg(11), seg(12), seg(13)]
    return jnp.concatenate(cols, axis=1).astype(BF16)


def _permute_mla(w_uq, w_ukv):
    z32 = jnp.zeros((w_uq.shape[0], 32), w_uq.dtype)
    qd = C_NOPE + C_ROPE
    uq = jnp.concatenate([a for h in range(C_HEADS) for a in (w_uq[:, qd * h:qd * (h + 1)], z32)], axis=1)
    z64 = jnp.zeros((w_ukv.shape[0], 64), w_ukv.dtype)
    kd = C_NOPE + C_V
    uk = jnp.concatenate([a for h in range(C_HEADS) for a in (w_ukv[:, kd * h:kd * h + C_NOPE], z64)], axis=1)
    uv = jnp.concatenate([w_ukv[:, kd * h + C_NOPE:kd * (h + 1)] for h in range(C_HEADS)], axis=1)
    return uq.astype(BF16), jnp.concatenate([uk, uv], axis=1).astype(BF16)


def kernel(x, c, ctx, c_ctx, w_mod, b_mod, norm1, norm2, w_in, w_gate, w_branch, w_out, attn_sink, hgrn_lb_logits, hgrn_norm, mla_q_norm, mla_kv_norm, mla_w_uq, mla_w_ukv, diff_lambda, diff_subln, w_router_group, b_router_group, w_router_expert, b_router_expert, w_expert_gate, w_expert_up, w_expert_down, final_norm):
    b, n_lat, d = x.shape
    n_ctx = ctx.shape[1]
    depth = w_mod.shape[0]
    t_all = n_lat + n_ctx
    assert d == D_MODEL and n_lat % TM == 0 and n_ctx % TM == 0 and n_lat % GRID_W == 0 and n_ctx <= n_lat
    assert n_lat >= WIN_Q + 2 * A_WINDOW and n_lat % TQ_MLA == 0 and n_lat % TM_PROJ == 0 and b <= 15
    n_lat_tiles = n_lat // TM_PROJ

    c_all = jnp.concatenate([c, c_ctx[None], jnp.zeros((15 - b, d), F32)], axis=0)
    mod = _modulation(c_all, w_mod, b_mod).reshape(depth, 16, 6, d)
    mod = jnp.pad(mod, ((0, 0), (0, 0), (0, 2), (0, 0)))
    mod = jnp.stack([mod[:, :b], jnp.broadcast_to(mod[:, b:b + 1], (depth, b, 8, d))], axis=2)

    sm = jax.nn.softmax(hgrn_lb_logits.astype(F32), axis=0)
    lower_bounds = jnp.cumsum(sm, axis=0) - sm[0]
    tabs = _all_rope_tables(n_lat, t_all)

    n_tok = b * t_all
    moe_buf = jnp.zeros((2 * n_tok + N_EXPERTS * MOE_TILE, ROW_SLABS, LANES), F32)
    xs, f_prev, mod_prev = x, ctx, None
    for li in range(depth):
        lam_init = 0.8 - 0.6 * math.exp(-0.3 * li)
        lp = diff_lambda[li].astype(F32)
        lam = (jnp.exp(jnp.sum(lp[0] * lp[1])) - jnp.exp(jnp.sum(lp[2] * lp[3])) + lam_init).reshape(1)
        w_perm = _permute_w_in(w_in[li])
        wuq, wukv = _permute_mla(mla_w_uq[li], mla_w_ukv[li])
        xs, z = _inproj(xs, f_prev, mod_prev, mod[li], norm1[li][None], w_perm,
                        mla_q_norm[li][None], mla_kv_norm[li][None], wuq, wukv, tabs, n_lat_tiles)
        qa, ka, va, bq, bff, bfb, bv, bg, cq, ck, cv, dq, dk, dv = z
        a_o = _win_attention(attn_sink[li].astype(F32), qa, ka, va, n_lat)
        o_f, o_b = _hgrn(lower_bounds[li], bq, bff, bfb, bv, n_lat)
        with_ctx = li < depth - 1
        c_o = _mla_attention(cq, ck, cv, n_lat, with_ctx)
        d_o = _diff_attention(lam, dq, dk, dv, jnp.tile(diff_subln[li], 2)[None], n_lat, lam_init, with_ctx)
        wr = jnp.concatenate([w_router_expert[li], w_router_group[li],
                              jnp.zeros((d, LANES - N_EXPERTS - N_GROUPS), F32)], axis=1)
        wr_hi = wr.astype(BF16)
        wr_lo = (wr - wr_hi.astype(F32)).astype(BF16)
        br = jnp.concatenate([b_router_expert[li], b_router_group[li],
                              jnp.zeros((LANES - N_EXPERTS - N_GROUPS,), F32)])[None]
        t_keep = t_all if with_ctx else n_lat
        n_keep = b * t_keep
        xs, h2, ri, rwt = _merge(xs, mod[li], norm1[li][None], norm2[li][None], a_o, o_f, o_b, bg,
                                 jnp.tile(hgrn_norm[li], B_HEADS)[None], c_o, d_o,
                                 w_gate[li].astype(BF16), w_branch[li].astype(BF16), w_out[li].astype(BF16),
                                 jnp.concatenate([wr_hi, wr_lo], axis=1), br, n_lat_tiles, t_keep)
        wgu = jnp.concatenate([w_expert_gate[li], w_expert_up[li]], axis=2).astype(BF16)
        f, moe_buf = _moe(h2.reshape(n_keep, ROW_SLABS, LANES), ri.reshape(n_keep, LANES),
                          rwt.reshape(n_keep, LANES), wgu, w_expert_down[li].astype(BF16), moe_buf)
        f_prev, mod_prev = f.reshape(b, t_keep, d), mod[li]
    return _final(xs, f_prev, mod_prev, final_norm[None], n_lat)
```

```python
import functools
import math

import numpy as np
import jax
import jax.numpy as jnp
from jax import lax
from jax.experimental import pallas as pl
from jax.experimental.pallas import tpu as pltpu

D_MODEL = 1024
GRID_W = 64
ROPE_BASE = 10000.0
NORM_EPS = 1e-6
MASK_VALUE = -1e30
GATE_FLOOR = 1e-30

A_HEADS, A_KV_HEADS, A_HEAD_DIM, A_WINDOW = 4, 2, 64, 128
B_HEADS, B_KEY_DIM, B_VAL_DIM = 4, 64, 64
C_HEADS, C_Q_LORA, C_KV_LORA, C_NOPE, C_ROPE, C_V = 4, 256, 128, 64, 32, 64
D_HEADS, D_HEAD_DIM = 4, 32
N_GROUPS, EXPERTS_PER_GROUP, EXPERT_HIDDEN = 4, 8, 256
N_EXPERTS = N_GROUPS * EXPERTS_PER_GROUP

TM = 256
TM_PROJ = 256
WIN_Q = 256
TQ = 256
TQ_MLA = 512
LOG2E = 1.4426950408889634
KV_CHUNK = 4096
MOE_TILE = 512
HG_CHUNK = 64
HG_LEVELS = 6
LANES = 128
ROW_SLABS = D_MODEL // LANES
VMEM_LIMIT = 52 * 1024 * 1024

OFF_A, OFF_B, OFF_C, OFF_D, N_PERM = 0, 768, 2048, 2944, 3712

F32 = jnp.float32
BF16 = jnp.bfloat16


def _dot(a, b):
    return jnp.dot(a, b, preferred_element_type=F32)


def _dot_nt(a, b):
    return lax.dot_general(a, b, (((1,), (1,)), ((), ())), preferred_element_type=F32)


def _dot_tn(a, b):
    return lax.dot_general(a, b, (((0,), (0,)), ((), ())), preferred_element_type=F32)


def _dot_hi(a, b):
    return jnp.dot(a, b, preferred_element_type=F32, precision=lax.Precision.HIGHEST)


def _sigmoid(x):
    return 1.0 / (1.0 + jnp.exp(-x))


def _rms(x, gain):
    return x * lax.rsqrt(jnp.mean(x * x, axis=-1, keepdims=True) + NORM_EPS) * gain


def _head_rms(o, gain, width):
    r = lax.broadcasted_iota(jnp.int32, (width, width), 0) // 64
    c = lax.broadcasted_iota(jnp.int32, (width, width), 1) // 64
    ones = jnp.where(r == c, 1.0 / 64.0, 0.0).astype(F32)
    ms = _dot_hi(o * o, ones)
    return o * lax.rsqrt(ms + NORM_EPS) * gain


def _rope(x, cos, sin, quarter):
    w = x.shape[-1]
    lane = lax.broadcasted_iota(jnp.int32, x.shape, 1)
    first = (lane % (2 * quarter)) < quarter
    sw = jnp.where(first, pltpu.roll(x, w - quarter, 1), pltpu.roll(x, quarter, 1))
    return x * cos + sw * sin


def _slabs_first(x):
    return pltpu.einshape("tsl->stl", x)


def _rows_first(y):
    slabs = jnp.stack([y[:, LANES * s:LANES * (s + 1)] for s in range(ROW_SLABS)], axis=0)
    return pltpu.einshape("stl->tsl", slabs)


def _cparams(sem):
    return pltpu.CompilerParams(dimension_semantics=sem, vmem_limit_bytes=VMEM_LIMIT)


def _const_spec(shape):
    n = len(shape)
    return pl.BlockSpec(shape, lambda *_: (0,) * n)


def _mod_kernel(c_ref, w_ref, b_ref, o_ref):
    c = c_ref[...]
    o_ref[0] = _dot_hi(c * _sigmoid(c), w_ref[0]) + b_ref[0]


def _modulation(c_all, w_mod, b_mod):
    depth, d, n = w_mod.shape
    nb = 1536
    return pl.pallas_call(
        _mod_kernel,
        out_shape=jax.ShapeDtypeStruct((depth, 16, n), F32),
        grid=(depth, n // nb),
        in_specs=[pl.BlockSpec((16, d), lambda l, j: (0, 0)),
                  pl.BlockSpec((1, d, nb), lambda l, j: (l, 0, j)),
                  pl.BlockSpec((1, 1, nb), lambda l, j: (l, 0, j))],
        out_specs=pl.BlockSpec((1, 16, nb), lambda l, j: (l, 0, j)),
        compiler_params=_cparams(("arbitrary", "arbitrary")),
        name="modulation",
    )(c_all, w_mod, b_mod.reshape(depth, 1, n))


def _inproj_kernel(*refs, first, n_lat_tiles):
    if not first:
        x = refs[0][0]
        refs = refs[1:]
    else:
        x_ref, ctx_ref = refs[:2]
        refs = refs[2:]
        ctx = ctx_ref[0]
        if ctx.shape[0] < x_ref.shape[1]:
            ctx = jnp.concatenate([ctx] * (x_ref.shape[1] // ctx.shape[0]), axis=0)
        x = jnp.where(pl.program_id(1) < n_lat_tiles, x_ref[0], ctx)
    (mod_ref, n1_ref, w_ref, qn_ref, kvn_ref, wuq_ref, wukv_ref,
     cosa_ref, sina_ref, cosc_ref, sinc_ref, cosd_ref, sind_ref) = refs[:13]
    outs = refs[13:]
    if first:
        outs[0][0] = x
        outs = outs[1:]
    (qa_ref, ka_ref, va_ref, bq_ref, bff_ref, bfb_ref, bv_ref, bg_ref,
     cq_ref, ck_ref, cv_ref, dq_ref, dk_ref, dv_ref) = outs
    m = mod_ref[0, 0]
    h = (_rms(x, n1_ref[...]) * (1.0 + m[1:2]) + m[0:1]).astype(BF16)

    z = _dot(h, w_ref[:, OFF_A:OFF_A + 768])
    cosa, sina = cosa_ref[...], sina_ref[...]
    qa_ref[0] = (_rope(z[:, 0:256], cosa, sina, 16) * (A_HEAD_DIM ** -0.5)).astype(BF16)
    ka_ref[0] = _rope(z[:, 256:512], cosa, sina, 16).astype(BF16)
    va_ref[0] = z[:, 512:768].astype(BF16)

    z = _dot(h, w_ref[:, OFF_B:OFF_B + 1280])
    bq_ref[0] = z[:, 0:256].astype(BF16)
    bff_ref[0] = z[:, 256:512]
    bfb_ref[0] = z[:, 512:768]
    bv_ref[0] = z[:, 768:1024].astype(BF16)
    bg_ref[0] = z[:, 1024:1280].astype(BF16)

    z = _dot(h, w_ref[:, OFF_C:OFF_C + 896])
    cosc, sinc = cosc_ref[...], sinc_ref[...]
    cq = _rms(z[:, 0:256], qn_ref[...]).astype(BF16)
    q = _rope(_dot(cq, wuq_ref[...]), cosc, sinc, 8)
    cq_ref[0] = (q * ((C_NOPE + C_ROPE) ** -0.5 * LOG2E)).astype(BF16)
    ckv = _rms(z[:, 256:384], kvn_ref[...]).astype(BF16)
    kv = _dot(ckv, wukv_ref[...])
    ck_ref[0] = (kv[:, 0:512] + _rope(z[:, 384:896], cosc, sinc, 8)).astype(BF16)
    cv_ref[0] = kv[:, 512:768].astype(BF16)

    z = _dot(h, w_ref[:, OFF_D:OFF_D + 768])
    cosd, sind = cosd_ref[...], sind_ref[...]
    dq_ref[0] = (_rope(z[:, 0:256], cosd, sind, 8) * (D_HEAD_DIM ** -0.5 * LOG2E)).astype(BF16)
    dk_ref[0] = _rope(z[:, 256:512], cosd, sind, 8).astype(BF16)
    dv_ref[0] = z[:, 512:768].astype(BF16)


def _inproj(x, ctx, mod, n1, w_perm, qn, kvn, wuq, wukv, tabs, n_lat_tiles):
    first = ctx is not None
    b, _, d = x.shape
    t = x.shape[1] + (ctx.shape[1] if first else 0)
    tm = TM_PROJ
    nt = pl.cdiv(t, tm)
    tile = lambda w: pl.BlockSpec((1, tm, w), lambda bi, i: (bi, i, 0))
    modspec = pl.BlockSpec((1, 1, 8, d), lambda bi, i: (bi, i // n_lat_tiles, 0, 0))
    tab = lambda w: pl.BlockSpec((tm, w), lambda bi, i: (i, 0))
    if first:
        lat = pl.BlockSpec((1, tm, d), lambda bi, i: (bi, jnp.minimum(i, n_lat_tiles - 1), 0))
        ctx_spec = pl.BlockSpec((1, min(tm, ctx.shape[1]), d),
                                lambda bi, i: (bi, jnp.maximum(i - n_lat_tiles, 0), 0))
        in_specs, args = [lat, ctx_spec], [x, ctx]
    else:
        in_specs, args = [tile(d)], [x]
    in_specs += [modspec, _const_spec((1, d)), _const_spec((d, N_PERM)), _const_spec((1, 256)),
                 _const_spec((1, 128)), _const_spec((256, 512)), _const_spec((128, 768)),
                 tab(256), tab(256), tab(512), tab(512), tab(256), tab(256)]
    args += [mod, n1, w_perm, qn, kvn, wuq, wukv, *tabs]
    widths = [(256, BF16), (256, BF16), (256, BF16),
              (256, BF16), (256, F32), (256, F32), (256, BF16), (256, BF16),
              (512, BF16), (512, BF16), (256, BF16),
              (256, BF16), (256, BF16), (256, BF16)]
    if first:
        widths = [(d, F32)] + widths
    out_shape = [jax.ShapeDtypeStruct((b, t, w), dt) for w, dt in widths]
    out_specs = [tile(w) for w, _ in widths]
    res = pl.pallas_call(
        functools.partial(_inproj_kernel, first=first, n_lat_tiles=n_lat_tiles),
        out_shape=out_shape, grid=(b, nt), in_specs=in_specs, out_specs=out_specs,
        compiler_params=_cparams(("parallel", "parallel")),
        name="inproj",
    )(*args)
    return (res[0], res[1:]) if first else (x, res)


def _win_kernel(sink_ref, q_ref, k_ref, v_ref, o_ref, *, n_lat, t_all):
    w = A_WINDOW
    qb = WIN_Q
    kb = qb + 2 * w
    n = pl.program_id(1)
    start = pl.multiple_of(jnp.clip(n * qb - w, 0, n_lat - kb), w)
    q = q_ref[0]
    lane = lax.broadcasted_iota(jnp.int32, (qb, LANES), 1)
    row = lax.broadcasted_iota(jnp.int32, (2 * qb, kb), 0)
    col = lax.broadcasted_iota(jnp.int32, (2 * qb, kb), 1)
    rel = (start + col) - (n * qb + row % qb)
    valid = (jnp.abs(rel) <= w) & (n < n_lat // qb)
    row1 = lax.broadcasted_iota(jnp.int32, (2 * qb, 1), 0)
    for j in range(A_KV_HEADS):
        sl = slice(LANES * j, LANES * (j + 1))
        qp = q[:, sl]
        zero = jnp.zeros_like(qp)
        lhs = jnp.concatenate([jnp.where(lane < 64, qp, zero), jnp.where(lane >= 64, qp, zero)], axis=0)
        sb = _dot_nt(lhs, k_ref[0, pl.ds(start, kb), sl])
        sc = _dot_nt(lhs, k_ref[0, n_lat:t_all, sl])
        sb = jnp.where(valid, sb, MASK_VALUE)
        sink = jnp.where(row1 < qb, sink_ref[2 * j], sink_ref[2 * j + 1])
        m = jnp.maximum(jnp.maximum(jnp.max(sb, axis=1, keepdims=True), jnp.max(sc, axis=1, keepdims=True)), sink)
        pb = jnp.exp(sb - m)
        pc = jnp.exp(sc - m)
        den = jnp.sum(pb, axis=1, keepdims=True) + jnp.sum(pc, axis=1, keepdims=True) + jnp.exp(sink - m)
        o = (_dot(pb.astype(BF16), v_ref[0, pl.ds(start, kb), sl])
             + _dot(pc.astype(BF16), v_ref[0, n_lat:t_all, sl])) / den
        o_ref[0, :, sl] = jnp.where(lane < 64, o[:qb], o[qb:]).astype(o_ref.dtype)


def _win_attention(sink, q, k, v, n_lat):
    b, t, _ = q.shape
    qb = WIN_Q
    return pl.pallas_call(
        functools.partial(_win_kernel, n_lat=n_lat, t_all=t),
        out_shape=jax.ShapeDtypeStruct((b, t, 256), BF16),
        grid=(b, t // qb),
        in_specs=[pl.BlockSpec(memory_space=pltpu.SMEM),
                  pl.BlockSpec((1, qb, 256), lambda bi, i: (bi, i, 0)),
                  pl.BlockSpec((1, t, 256), lambda bi, i: (bi, 0, 0)),
                  pl.BlockSpec((1, t, 256), lambda bi, i: (bi, 0, 0))],
        out_specs=pl.BlockSpec((1, qb, 256), lambda bi, i: (bi, i, 0)),
        compiler_params=_cparams(("parallel", "arbitrary")),
        name="win_attention",
    )(sink, q, k, v)


def _hgrn_level_matrix(rev):
    c = HG_CHUNK
    m = np.zeros((8, c, c), np.float32)
    for lvl in range(HG_LEVELS):
        size = c >> lvl
        for t in range(c):
            mid = (t // size) * size + size // 2
            upper = t >= mid
            if not rev:
                rng = range(mid, t + 1) if upper else range(t + 1, mid)
            else:
                rng = range(mid, t) if upper else range(t, mid)
            m[lvl, t, list(rng)] = 1.0
    for t in range(c):
        if not rev:
            m[6, t, :t + 1] = 1.0
            m[7, t, t + 1:] = 1.0
        else:
            m[6, t, t:] = 1.0
            m[7, t, :t] = 1.0
    m = m.reshape(8 * c, c)
    return np.concatenate([m, m], axis=1)


def _hgrn_chunk(q, k, logf, v, mlev, rev):
    c = HG_CHUNK
    hi = logf.astype(BF16)
    lo = (logf - hi.astype(F32)).astype(BF16)
    wgt = jnp.exp(_dot(mlev, jnp.concatenate([hi, lo], axis=0)))
    t = lax.broadcasted_iota(jnp.int32, (c, 256), 0)
    s_col = lax.broadcasted_iota(jnp.int32, (c, 256), 1) % c
    r_bd = lax.broadcasted_iota(jnp.int32, (256, 256), 0) // 64
    c_bd = lax.broadcasted_iota(jnp.int32, (256, 256), 1) // 64
    bd = r_bd == c_bd

    def block_diag(a):
        return jnp.where(bd, jnp.concatenate([a] * 4, axis=0), 0.0).astype(BF16)

    scores = jnp.where(t == s_col, _dot_nt(q.astype(BF16), block_diag(k)), 0.0)
    for lvl in range(HG_LEVELS):
        wl = wgt[c * lvl:c * (lvl + 1)]
        q_side = ((t >> (HG_LEVELS - 1 - lvl)) & 1) == (0 if rev else 1)
        ql = jnp.where(q_side, q * wl, 0.0).astype(BF16)
        kl = jnp.where(q_side, 0.0, k * wl)
        s = _dot_nt(ql, block_diag(kl))
        if lvl > 0:
            sh = HG_LEVELS - lvl
            s = jnp.where((t >> sh) == (s_col >> sh), s, 0.0)
        scores = scores + s
    q_in = (q * wgt[6 * c:7 * c]).astype(BF16)
    k_end = (k * wgt[7 * c:8 * c]).astype(BF16)
    o_intra = _dot(scores.astype(BF16), block_diag(v))
    edge = 6 * c if rev else 7 * c - 1
    total = wgt[edge:edge + 1]
    update = jnp.where(bd, _dot_tn(v.astype(BF16), k_end), 0.0)
    return o_intra, q_in, update, total


def _hgrn_kernel(mf_ref, mb_ref, lb_ref, qf_ref, zf_ref, vf_ref, qb_ref, zb_ref, vb_ref,
                 of_ref, ob_ref, stf_ref, stb_ref):
    c = HG_CHUNK
    nc = TM // c

    @pl.when(pl.program_id(1) == 0)
    def _():
        stf_ref[...] = jnp.zeros_like(stf_ref)
        stb_ref[...] = jnp.zeros_like(stb_ref)

    def scan(q_ref, z_ref, v_ref, lb, mlev, o_ref, st_ref, rev):
        sig = _sigmoid(z_ref[0])
        logf = jnp.log(jnp.maximum(lb + (1.0 - lb) * sig, GATE_FLOOR))
        k = (1.0 - lb) * (1.0 - sig)
        order = range(nc - 1, -1, -1) if rev else range(nc)
        rows = [slice(ci * c, (ci + 1) * c) for ci in order]
        parts = [_hgrn_chunk(q_ref[0, r, :].astype(F32), k[r], logf[r], v_ref[0, r, :].astype(F32), mlev, rev)
                 for r in rows]
        st = st_ref[...]
        for r, (o_intra, q_in, update, total) in zip(rows, parts):
            o_ref[0, r, :] = o_intra + _dot_nt(q_in, st.astype(BF16))
            st = st * total + update
        st_ref[...] = st

    scan(qf_ref, zf_ref, vf_ref, lb_ref[0:1], mf_ref[...], of_ref, stf_ref, False)
    scan(qb_ref, zb_ref, vb_ref, lb_ref[1:2], mb_ref[...], ob_ref, stb_ref, True)


def _hgrn(lb, q, zff, zfb, v, n_lat):
    b, t, _ = q.shape
    nlb = n_lat // TM
    ncb = (t - n_lat) // TM
    fwd = lambda bi, i: (bi, jnp.where(i < ncb, nlb + i, i - ncb), 0)
    bwd = lambda bi, i: (bi, nlb + ncb - 1 - i, 0)
    blk = lambda im: pl.BlockSpec((1, TM, 256), im)
    mf = jnp.asarray(_hgrn_level_matrix(False), BF16)
    mb = jnp.asarray(_hgrn_level_matrix(True), BF16)
    return pl.pallas_call(
        _hgrn_kernel,
        out_shape=[jax.ShapeDtypeStruct((b, t, 256), F32)] * 2,
        grid=(b, t // TM),
        in_specs=[_const_spec((8 * HG_CHUNK, 2 * HG_CHUNK)), _const_spec((8 * HG_CHUNK, 2 * HG_CHUNK)),
                  _const_spec((2, 256)),
                  blk(fwd), blk(fwd), blk(fwd), blk(bwd), blk(bwd), blk(bwd)],
        out_specs=[blk(fwd), blk(bwd)],
        scratch_shapes=[pltpu.VMEM((256, 256), F32), pltpu.VMEM((256, 256), F32)],
        compiler_params=_cparams(("parallel", "arbitrary")),
        name="hgrn",
    )(mf, mb, lb, q, zff, v, q, zfb, v)


def _key_chunks(n_lat, t_all):
    chunks = [(lo, min(lo + KV_CHUNK, t_all)) for lo in range(0, t_all, KV_CHUNK)]
    return chunks, [(n_lat, t_all)]


def _online_softmax(lhs, k_ref, v_ref, chunks):
    rows = lhs.shape[0]
    m = jnp.full((rows, 1), -jnp.inf, F32)
    l = jnp.zeros((rows, 1), F32)
    acc = jnp.zeros((rows, LANES), F32)
    for lo, hi in chunks:
        s = _dot_nt(lhs, k_ref[0, lo:hi, :])
        m_new = jnp.maximum(m, jnp.max(s, axis=1, keepdims=True))
        alpha = jnp.exp2(m - m_new)
        p = jnp.exp2(s - m_new)
        l = alpha * l + jnp.sum(p, axis=1, keepdims=True)
        acc = alpha * acc + _dot(p.astype(BF16), v_ref[0, lo:hi, :])
        m = m_new
    return acc / l


def _mla_kernel(q_ref, k_ref, v_ref, o_ref, *, chunks):
    tq = q_ref.shape[1]
    q = q_ref[0]
    zero = jnp.zeros_like(q)
    lane2 = lax.broadcasted_iota(jnp.int32, q.shape, 1)
    lhs = jnp.concatenate([jnp.where(lane2 < LANES, q, zero), jnp.where(lane2 >= LANES, q, zero)], axis=0)
    o = _online_softmax(lhs, k_ref, v_ref, chunks)
    lane = lax.broadcasted_iota(jnp.int32, (tq, LANES), 1)
    o_ref[0] = jnp.where(lane < 64, o[:tq], o[tq:]).astype(o_ref.dtype)


def _diff_kernel(lam_ref, g_ref, q_ref, k_ref, v_ref, o_ref, *, chunks, lam_init):
    tq = q_ref.shape[1]
    q = q_ref[0]
    zero = jnp.zeros_like(q)
    lane = lax.broadcasted_iota(jnp.int32, (tq, LANES), 1)
    lhs = jnp.concatenate([jnp.where(lane // D_HEAD_DIM == r, q, zero) for r in range(4)], axis=0)
    o = _online_softmax(lhs, k_ref, v_ref, chunks)
    lam = lam_ref[0]
    o0 = o[0:tq] - lam * o[tq:2 * tq]
    o1 = o[2 * tq:3 * tq] - lam * o[3 * tq:4 * tq]
    o = jnp.where(lane < 64, o0, o1)
    o_ref[0] = (_head_rms(o, g_ref[...], LANES) * (1.0 - lam_init)).astype(o_ref.dtype)


def _attention_call(body, name, extras, extra_specs, q, k, v, tq, row0, n_rows, out_init):
    b, t, qw = q.shape
    blk0 = row0 // tq
    n_extra = len(extras)
    in_specs = list(extra_specs) + [
        pl.BlockSpec((1, tq, qw // 2), lambda bi, p, i: (bi, blk0 + i, p)),
        pl.BlockSpec((1, t, k.shape[2] // 2), lambda bi, p, i: (bi, 0, p)),
        pl.BlockSpec((1, t, LANES), lambda bi, p, i: (bi, 0, p))]
    args = list(extras) + [q, k, v]
    aliases = {}
    if out_init is not None:
        in_specs.append(pl.BlockSpec(memory_space=pl.ANY))
        aliases = {len(args): 0}
        args.append(out_init)

    def kern(*refs):
        body(*refs[:n_extra + 3], refs[-1])

    return pl.pallas_call(
        kern,
        out_shape=jax.ShapeDtypeStruct((b, t, 256), BF16),
        grid=(b, 2, n_rows // tq),
        in_specs=in_specs,
        out_specs=pl.BlockSpec((1, tq, LANES), lambda bi, p, i: (bi, blk0 + i, p)),
        input_output_aliases=aliases,
        compiler_params=_cparams(("parallel", "parallel", "arbitrary")),
        name=name,
    )(*args)


def _two_pass_attention(body_of, name, extras, extra_specs, q, k, v, n_lat, tq_lat, with_ctx):
    t = q.shape[1]
    lat_chunks, ctx_chunks = _key_chunks(n_lat, t)
    out = jnp.zeros((q.shape[0], t, 256), BF16)
    out = _attention_call(body_of(lat_chunks), name, extras, extra_specs, q, k, v, tq_lat, 0, n_lat, out)
    if not with_ctx:
        return out
    return _attention_call(body_of(ctx_chunks), name + "_ctx", extras, extra_specs, q, k, v,
                           TQ, n_lat, t - n_lat, out)


def _mla_attention(q, k, v, n_lat, with_ctx):
    body_of = lambda chunks: functools.partial(_mla_kernel, chunks=chunks)
    return _two_pass_attention(body_of, "mla_attention", [], [], q, k, v, n_lat, TQ_MLA, with_ctx)


def _diff_attention(lam, q, k, v, gain, n_lat, lam_init, with_ctx):
    body_of = lambda chunks: functools.partial(_diff_kernel, chunks=chunks, lam_init=lam_init)
    specs = [pl.BlockSpec(memory_space=pltpu.SMEM), _const_spec((1, LANES))]
    return _two_pass_attention(body_of, "diff_attention", [lam, gain], specs, q, k, v, n_lat, TQ, with_ctx)


def _merge_kernel(x_ref, mod_ref, n1_ref, n2_ref, a_ref, of_ref, ob_ref, bg_ref, hg_ref, c_ref, d_ref,
                  wg_ref, wb_ref, wo_ref, wr_ref, br_ref, xo_ref, h2_ref, ri_ref, rw_ref):
    d = D_MODEL
    tm = x_ref.shape[1]
    x = x_ref[0]
    m = mod_ref[0, 0]
    h = (_rms(x, n1_ref[...]) * (1.0 + m[1:2]) + m[0:1]).astype(BF16)
    g = bg_ref[0].astype(F32)
    b_out = _head_rms(of_ref[0] + ob_ref[0], hg_ref[...], 256) * (g * _sigmoid(g))
    branches = (a_ref[0], b_out.astype(BF16), c_ref[0], d_ref[0])
    y = jnp.zeros((tm, d), F32)
    for i, br in enumerate(branches):
        gate = _sigmoid(_dot(h, wg_ref[:, d * i:d * (i + 1)]))
        y = y + gate * _dot(br, wb_ref[i])
    x = x + m[2:3] * _dot(y.astype(BF16), wo_ref[...])
    xo_ref[0] = x
    h2 = _rms(x, n2_ref[...]) * (1.0 + m[4:5]) + m[3:4]
    h2b = h2.astype(BF16)
    h2_ref[0] = _rows_first(h2)

    h2lo = (h2 - h2b.astype(F32)).astype(BF16)
    r = _dot(jnp.concatenate([h2b, h2lo], axis=0), wr_ref[...])
    logits = r[:tm, :LANES] + r[:tm, LANES:] + r[tm:, :LANES] + br_ref[...]
    lane = lax.broadcasted_iota(jnp.int32, (tm, LANES), 1)
    neg = -jnp.inf
    is_g = (lane >= N_EXPERTS) & (lane < N_EXPERTS + N_GROUPS)
    lg = jnp.where(is_g, logits, neg)
    mg = jnp.max(lg, axis=1, keepdims=True)
    g_val = 1.0 / jnp.sum(jnp.exp(lg - mg), axis=1, keepdims=True)
    g_idx = jnp.min(jnp.where(lg == mg, lane, 4 * LANES), axis=1, keepdims=True) - N_EXPERTS
    in_group = (lane >= g_idx * EXPERTS_PER_GROUP) & (lane < (g_idx + 1) * EXPERTS_PER_GROUP)
    le = jnp.where(in_group, logits, neg)
    m1 = jnp.max(le, axis=1, keepdims=True)
    e1 = jnp.min(jnp.where(le == m1, lane, 4 * LANES), axis=1, keepdims=True)
    le2 = jnp.where(lane == e1, neg, le)
    m2 = jnp.max(le2, axis=1, keepdims=True)
    e2 = jnp.min(jnp.where(le2 == m2, lane, 4 * LANES), axis=1, keepdims=True)
    r2 = jnp.exp(m2 - m1)
    v1 = g_val / (1.0 + r2)
    ri_ref[0] = jnp.where(lane == 0, e1, jnp.where(lane == 1, e2, 0))
    rw_ref[0] = jnp.where(lane == 0, v1, jnp.where(lane == 1, v1 * r2, 0.0))


def _merge(x, mod, n1, n2, a_o, o_f, o_b, bg, hg, c_o, d_o, wg, wb, wo, wr, br, n_lat_tiles, t):
    b, _, d = x.shape
    tm = TM_PROJ
    tile = lambda w: pl.BlockSpec((1, tm, w), lambda bi, i: (bi, i, 0))
    return pl.pallas_call(
        _merge_kernel,
        out_shape=[jax.ShapeDtypeStruct((b, t, d), F32), jax.ShapeDtypeStruct((b, t, ROW_SLABS, LANES), F32),
                   jax.ShapeDtypeStruct((b, t, LANES), jnp.int32), jax.ShapeDtypeStruct((b, t, LANES), F32)],
        grid=(b, pl.cdiv(t, tm)),
        in_specs=[tile(d), pl.BlockSpec((1, 1, 8, d), lambda bi, i: (bi, i // n_lat_tiles, 0, 0)),
                  _const_spec((1, d)), _const_spec((1, d)),
                  tile(256), tile(256), tile(256), tile(256), _const_spec((1, 256)), tile(256), tile(256),
                  _const_spec((d, 4 * d)), _const_spec((4, 256, d)), _const_spec((d, d)),
                  _const_spec((d, 256)), _const_spec((1, LANES))],
        out_specs=[tile(d), pl.BlockSpec((1, tm, ROW_SLABS, LANES), lambda bi, i: (bi, i, 0, 0)),
                   tile(LANES), tile(LANES)],
        compiler_params=_cparams(("parallel", "parallel")),
        name="merge",
    )(x, mod, n1, n2, a_o, o_f, o_b, bg, hg, c_o, d_o, wg, wb, wo, wr, br)


def _route_positions(ri, n_tok):
    ef = ri[:, :2].reshape(-1)
    rb = 2 * TM
    oh = (ef[:, None] == jnp.arange(N_EXPERTS, dtype=jnp.int32)[None, :]).astype(F32).reshape(-1, rb, N_EXPERTS)
    tri = (jnp.arange(rb)[:, None] >= jnp.arange(rb)[None, :]).astype(F32)
    within = jnp.einsum("ij,gje->gie", tri, oh)
    tot = within[:, -1, :]
    before = jnp.cumsum(tot, axis=0) - tot
    cnt = jnp.sum(tot, axis=0).astype(jnp.int32)
    pcnt = ((cnt + MOE_TILE - 1) // MOE_TILE) * MOE_TILE
    end = jnp.cumsum(pcnt)
    start = (end - pcnt).astype(F32)
    pos = jnp.sum((within + (before + start[None, :])[:, None, :]) * oh, axis=2) - 1.0
    pos = pos.reshape(-1)
    n_tiles = 2 * n_tok // MOE_TILE + N_EXPERTS
    n_act = (end[-1] // MOE_TILE).astype(jnp.int32)
    tile_row = jnp.minimum(jnp.arange(n_tiles, dtype=jnp.int32), n_act - 1) * MOE_TILE
    tile_expert = jnp.sum((end[None, :] <= tile_row[:, None]).astype(jnp.int32), axis=1)
    tile_expert = jnp.minimum(tile_expert, N_EXPERTS - 1)
    return pos.reshape(n_tok // TM, 1, 2 * TM).astype(jnp.int32), tile_expert, n_act.reshape(1)


def _dispatch_kernel(pos_ref, h_ref, xs_in_ref, xs_ref, sem):
    del xs_in_ref

    def body(r, carry):
        for s in range(2):
            p = pos_ref[0, 0, 2 * r + s]
            pltpu.make_async_copy(h_ref.at[pl.ds(r, 1)], xs_ref.at[pl.ds(p, 1)], sem.at[s]).start(priority=s)
        return carry

    lax.fori_loop(0, TM, body, 0)
    for s in range(2):
        pltpu.make_async_copy(h_ref, h_ref, sem.at[s]).wait()


def _dispatch(pos, h2, xs_init):
    n = h2.shape[0]
    return pl.pallas_call(
        _dispatch_kernel,
        out_shape=jax.ShapeDtypeStruct(xs_init.shape, F32),
        grid=(n // TM,),
        in_specs=[pl.BlockSpec((1, 1, 2 * TM), lambda t: (t, 0, 0), memory_space=pltpu.SMEM),
                  pl.BlockSpec((TM, ROW_SLABS, LANES), lambda t: (t, 0, 0)),
                  pl.BlockSpec(memory_space=pl.ANY)],
        out_specs=pl.BlockSpec(memory_space=pl.ANY),
        scratch_shapes=[pltpu.SemaphoreType.DMA((2,))],
        input_output_aliases={2: 0},
        compiler_params=_cparams(("arbitrary",)),
        name="moe_dispatch",
    )(pos, h2, xs_init)


def _experts_kernel(te_ref, na_ref, x_ref, wgu_ref, wd_ref, o_ref):
    del te_ref
    t = pl.program_id(0)

    @pl.when(t < na_ref[0])
    def _():
        x = _slabs_first(x_ref[...])
        gu = _dot(x[0].astype(BF16), wgu_ref[0, 0:LANES, :])
        for s in range(1, ROW_SLABS):
            gu = gu + _dot(x[s].astype(BF16), wgu_ref[0, LANES * s:LANES * (s + 1), :])
        gt, up = gu[:, :EXPERT_HIDDEN], gu[:, EXPERT_HIDDEN:]
        o_ref[...] = _rows_first(_dot((gt * _sigmoid(gt) * up).astype(BF16), wd_ref[0]))

    @pl.when(t >= na_ref[0])
    def _():
        o_ref[...] = jnp.zeros_like(o_ref)


def _experts(tile_expert, n_act, xs, wgu, wd):
    n_tiles = tile_expert.shape[0]
    d = D_MODEL
    rows = pl.BlockSpec((MOE_TILE, ROW_SLABS, LANES), lambda t, te, na: (t, 0, 0))
    return pl.pallas_call(
        _experts_kernel,
        out_shape=jax.ShapeDtypeStruct((n_tiles * MOE_TILE, ROW_SLABS, LANES), F32),
        grid_spec=pltpu.PrefetchScalarGridSpec(
            num_scalar_prefetch=2, grid=(n_tiles,),
            in_specs=[rows,
                      pl.BlockSpec((1, d, 2 * EXPERT_HIDDEN), lambda t, te, na: (te[t], 0, 0)),
                      pl.BlockSpec((1, EXPERT_HIDDEN, d), lambda t, te, na: (te[t], 0, 0))],
            out_specs=rows),
        compiler_params=_cparams(("arbitrary",)),
        name="moe_experts",
    )(tile_expert, n_act, xs, wgu, wd)


def _combine_kernel(pos_ref, ys_ref, w_ref, x_ref, mod_ref, g_ref, o_ref, buf, sem, *, final):
    def body(r, carry):
        for s in range(2):
            p = pos_ref[0, 0, 2 * r + s]
            pltpu.make_async_copy(ys_ref.at[pl.ds(p, 1)], buf.at[s, pl.ds(r, 1)], sem.at[s]).start(priority=s)
        return carry

    lax.fori_loop(0, TM, body, 0)
    for s in range(2):
        pltpu.make_async_copy(buf.at[s], buf.at[s], sem.at[s]).wait()
    w = w_ref[0]
    y0, y1 = _slabs_first(buf[0]), _slabs_first(buf[1])
    f = jnp.concatenate([w[:, 0:1] * y0[k] + w[:, 1:2] * y1[k] for k in range(ROW_SLABS)], axis=1)
    x = x_ref[0] + mod_ref[0, 0, 5:6, :] * f
    o_ref[0] = _rms(x, g_ref[...]) if final else x


def _combine(pos, ys, rwt, x, mod, gain, n_lat):
    b, t, d = x.shape
    nt = t // TM
    tile = lambda w: pl.BlockSpec((1, TM, w), lambda bi, i: (bi, i, 0))
    final = gain is not None
    return pl.pallas_call(
        functools.partial(_combine_kernel, final=final),
        out_shape=jax.ShapeDtypeStruct((b, t, d), F32),
        grid=(b, nt),
        in_specs=[pl.BlockSpec((1, 1, 2 * TM), lambda bi, i: (bi * nt + i, 0, 0), memory_space=pltpu.SMEM),
                  pl.BlockSpec(memory_space=pl.ANY),
                  tile(LANES), tile(d),
                  pl.BlockSpec((1, 1, 8, d), lambda bi, i: (bi, i // (n_lat // TM), 0, 0)),
                  _const_spec((1, d))],
        out_specs=tile(d),
        scratch_shapes=[pltpu.VMEM((2, TM, ROW_SLABS, LANES), F32), pltpu.SemaphoreType.DMA((2,))],
        compiler_params=_cparams(("arbitrary", "arbitrary")),
        name="moe_combine",
    )(pos, ys, rwt, x, mod, gain if final else jnp.ones((1, d), F32))


def _moe(x, h2, ri, rwt, wgu, wd, xs_init, mod, gain, n_lat):
    b, t, _ = x.shape
    n = b * t
    pos, tile_expert, n_act = _route_positions(ri.reshape(n, LANES), n)
    xs = _dispatch(pos, h2.reshape(n, ROW_SLABS, LANES), xs_init)
    ys = _experts(tile_expert, n_act, xs, wgu, wd)
    return _combine(pos, ys, rwt, x, mod, gain, n_lat), xs


def _rope_tables(n_lat, t_all, lane_rope, lane_off, dr):
    half, quarter = dr // 2, dr // 4
    inv_freq = 1.0 / (ROPE_BASE ** (jnp.arange(quarter, dtype=F32) / quarter))
    off = np.asarray(lane_off)
    use_col = off >= half
    j = (off % half) % quarter
    first = (off % half) < quarter
    tok = jnp.arange(n_lat, dtype=jnp.int32)
    row = (tok // GRID_W).astype(F32)
    col = (tok % GRID_W).astype(F32)
    pos = jnp.where(jnp.asarray(use_col)[None, :], col[:, None], row[:, None])
    ang = pos * inv_freq[jnp.asarray(j)][None, :]
    rope = jnp.asarray(lane_rope)[None, :]
    cos = jnp.where(rope, jnp.cos(ang), 1.0)
    sin = jnp.where(rope, jnp.sin(ang) * jnp.where(jnp.asarray(first), -1.0, 1.0)[None, :], 0.0)
    pad = t_all - n_lat
    w = off.shape[0]
    cos = jnp.concatenate([cos, jnp.ones((pad, w), F32)], axis=0)
    sin = jnp.concatenate([sin, jnp.zeros((pad, w), F32)], axis=0)
    return cos, sin


def _all_rope_tables(n_lat, t_all):
    la = np.arange(256)
    ta = _rope_tables(n_lat, t_all, np.ones(256, bool), la % 64, 64)
    lc = np.arange(512) % 128
    tc = _rope_tables(n_lat, t_all, (lc >= 64) & (lc < 96), np.clip(lc - 64, 0, 31), 32)
    td = _rope_tables(n_lat, t_all, np.ones(256, bool), la % 32, 32)
    return (*ta, *tc, *td)


def _permute_w_in(w):
    s = np.cumsum([0, 256, 128, 128, 256, 256, 256, 256, 256, 256, 128, 32, 256, 256, 256])
    seg = lambda i: w[:, s[i]:s[i + 1]]
    dup = lambda a: jnp.concatenate([a[:, 0:64], a[:, 0:64], a[:, 64:128], a[:, 64:128]], axis=1)
    z64 = jnp.zeros((w.shape[0], 64), w.dtype)
    z32 = jnp.zeros((w.shape[0], 32), w.dtype)
    kr = jnp.concatenate([z64, seg(10), z32] * C_HEADS, axis=1)
    cols = [seg(0), dup(seg(1)), dup(seg(2)),
            seg(3), seg(4), seg(5), seg(6), seg(7),
            seg(8), seg(9), kr,
            seg(11), seg(12), seg(13)]
    return jnp.concatenate(cols, axis=1).astype(BF16)


def _permute_mla(w_uq, w_ukv):
    z32 = jnp.zeros((w_uq.shape[0], 32), w_uq.dtype)
    qd = C_NOPE + C_ROPE
    uq = jnp.concatenate([a for h in range(C_HEADS) for a in (w_uq[:, qd * h:qd * (h + 1)], z32)], axis=1)
    z64 = jnp.zeros((w_ukv.shape[0], 64), w_ukv.dtype)
    kd = C_NOPE + C_V
    uk = jnp.concatenate([a for h in range(C_HEADS) for a in (w_ukv[:, kd * h:kd * h + C_NOPE], z64)], axis=1)
    uv = jnp.concatenate([w_ukv[:, kd * h + C_NOPE:kd * (h + 1)] for h in range(C_HEADS)], axis=1)
    return uq.astype(BF16), jnp.concatenate([uk, uv], axis=1).astype(BF16)


def kernel(x, c, ctx, c_ctx, w_mod, b_mod, norm1, norm2, w_in, w_gate, w_branch, w_out, attn_sink, hgrn_lb_logits, hgrn_norm, mla_q_norm, mla_kv_norm, mla_w_uq, mla_w_ukv, diff_lambda, diff_subln, w_router_group, b_router_group, w_router_expert, b_router_expert, w_expert_gate, w_expert_up, w_expert_down, final_norm):
    b, n_lat, d = x.shape
    n_ctx = ctx.shape[1]
    depth = w_mod.shape[0]
    t_all = n_lat + n_ctx
    assert d == D_MODEL and n_lat % TM == 0 and n_ctx % TM == 0 and n_lat % GRID_W == 0 and n_ctx <= n_lat
    assert n_lat >= WIN_Q + 2 * A_WINDOW and n_lat % TQ_MLA == 0 and n_lat % TM_PROJ == 0 and b <= 15
    n_lat_tiles = n_lat // TM_PROJ

    c_all = jnp.concatenate([c, c_ctx[None], jnp.zeros((15 - b, d), F32)], axis=0)
    mod = _modulation(c_all, w_mod, b_mod).reshape(depth, 16, 6, d)
    mod = jnp.pad(mod, ((0, 0), (0, 0), (0, 2), (0, 0)))
    mod = jnp.stack([mod[:, :b], jnp.broadcast_to(mod[:, b:b + 1], (depth, b, 8, d))], axis=2)

    sm = jax.nn.softmax(hgrn_lb_logits.astype(F32), axis=0)
    lower_bounds = jnp.cumsum(sm, axis=0) - sm[0]
    tabs = _all_rope_tables(n_lat, t_all)

    n_tok = b * t_all
    moe_buf = jnp.zeros((2 * n_tok + N_EXPERTS * MOE_TILE, ROW_SLABS, LANES), F32)
    xs = x
    for li in range(depth):
        lam_init = 0.8 - 0.6 * math.exp(-0.3 * li)
        lp = diff_lambda[li].astype(F32)
        lam = (jnp.exp(jnp.sum(lp[0] * lp[1])) - jnp.exp(jnp.sum(lp[2] * lp[3])) + lam_init).reshape(1)
        w_perm = _permute_w_in(w_in[li])
        wuq, wukv = _permute_mla(mla_w_uq[li], mla_w_ukv[li])
        xs, z = _inproj(xs, ctx if li == 0 else None, mod[li], norm1[li][None], w_perm,
                        mla_q_norm[li][None], mla_kv_norm[li][None], wuq, wukv, tabs, n_lat_tiles)
        qa, ka, va, bq, bff, bfb, bv, bg, cq, ck, cv, dq, dk, dv = z
        a_o = _win_attention(attn_sink[li].astype(F32), qa, ka, va, n_lat)
        o_f, o_b = _hgrn(lower_bounds[li], bq, bff, bfb, bv, n_lat)
        with_ctx = li < depth - 1
        c_o = _mla_attention(cq, ck, cv, n_lat, with_ctx)
        d_o = _diff_attention(lam, dq, dk, dv, jnp.tile(diff_subln[li], 2)[None], n_lat, lam_init, with_ctx)
        wr = jnp.concatenate([w_router_expert[li], w_router_group[li],
                              jnp.zeros((d, LANES - N_EXPERTS - N_GROUPS), F32)], axis=1)
        wr_hi = wr.astype(BF16)
        wr_lo = (wr - wr_hi.astype(F32)).astype(BF16)
        br = jnp.concatenate([b_router_expert[li], b_router_group[li],
                              jnp.zeros((LANES - N_EXPERTS - N_GROUPS,), F32)])[None]
        t_keep = t_all if with_ctx else n_lat
        xs, h2, ri, rwt = _merge(xs, mod[li], norm1[li][None], norm2[li][None], a_o, o_f, o_b, bg,
                                 jnp.tile(hgrn_norm[li], B_HEADS)[None], c_o, d_o,
                                 w_gate[li].astype(BF16), w_branch[li].astype(BF16), w_out[li].astype(BF16),
                                 jnp.concatenate([wr_hi, wr_lo], axis=1), br, n_lat_tiles, t_keep)
        wgu = jnp.concatenate([w_expert_gate[li], w_expert_up[li]], axis=2).astype(BF16)
        xs, moe_buf = _moe(xs, h2, ri, rwt, wgu, w_expert_down[li].astype(BF16), moe_buf, mod[li],
                           None if with_ctx else final_norm[None], n_lat)
    return xs
```

```python
import functools
import math

import numpy as np
import jax
import jax.numpy as jnp
from jax import lax
from jax.experimental import pallas as pl
from jax.experimental.pallas import tpu as pltpu

D_MODEL = 1024
GRID_W = 64
ROPE_BASE = 10000.0
NORM_EPS = 1e-6
MASK_VALUE = -1e30
GATE_FLOOR = 1e-30

A_HEADS, A_KV_HEADS, A_HEAD_DIM, A_WINDOW = 4, 2, 64, 128
B_HEADS, B_KEY_DIM, B_VAL_DIM = 4, 64, 64
C_HEADS, C_Q_LORA, C_KV_LORA, C_NOPE, C_ROPE, C_V = 4, 256, 128, 64, 32, 64
D_HEADS, D_HEAD_DIM = 4, 32
N_GROUPS, EXPERTS_PER_GROUP, EXPERT_HIDDEN = 4, 8, 256
N_EXPERTS = N_GROUPS * EXPERTS_PER_GROUP

TM = 256
TM_PROJ = 256
WIN_Q = 256
TQ = 256
TQ_MLA = 512
LOG2E = 1.4426950408889634
KV_CHUNK = 4096
MOE_TILE = 512
HG_BATCH = 2
HG_CHUNK = 64
HG_LEVELS = 6
LANES = 128
ROW_SLABS = D_MODEL // LANES
VMEM_LIMIT = 52 * 1024 * 1024

OFF_A, OFF_B, OFF_C, OFF_D, N_PERM = 0, 768, 2048, 2944, 3712

F32 = jnp.float32
BF16 = jnp.bfloat16


def _dot(a, b):
    return jnp.dot(a, b, preferred_element_type=F32)


def _dot_nt(a, b):
    return lax.dot_general(a, b, (((1,), (1,)), ((), ())), preferred_element_type=F32)


def _dot_tn(a, b):
    return lax.dot_general(a, b, (((0,), (0,)), ((), ())), preferred_element_type=F32)


def _dot_hi(a, b):
    return jnp.dot(a, b, preferred_element_type=F32, precision=lax.Precision.HIGHEST)


def _sigmoid(x):
    return 1.0 / (1.0 + jnp.exp(-x))


def _rms(x, gain):
    return x * lax.rsqrt(jnp.mean(x * x, axis=-1, keepdims=True) + NORM_EPS) * gain


def _head_rms(o, gain, width):
    r = lax.broadcasted_iota(jnp.int32, (width, width), 0) // 64
    c = lax.broadcasted_iota(jnp.int32, (width, width), 1) // 64
    ones = jnp.where(r == c, 1.0 / 64.0, 0.0).astype(F32)
    ms = _dot_hi(o * o, ones)
    return o * lax.rsqrt(ms + NORM_EPS) * gain


def _rope(x, cos, sin, quarter):
    w = x.shape[-1]
    lane = lax.broadcasted_iota(jnp.int32, x.shape, 1)
    first = (lane % (2 * quarter)) < quarter
    sw = jnp.where(first, pltpu.roll(x, w - quarter, 1), pltpu.roll(x, quarter, 1))
    return x * cos + sw * sin


def _slabs_first(x):
    return pltpu.einshape("tsl->stl", x)


def _rows_first(y):
    slabs = jnp.stack([y[:, LANES * s:LANES * (s + 1)] for s in range(ROW_SLABS)], axis=0)
    return pltpu.einshape("stl->tsl", slabs)


def _cparams(sem):
    return pltpu.CompilerParams(dimension_semantics=sem, vmem_limit_bytes=VMEM_LIMIT)


def _const_spec(shape):
    n = len(shape)
    return pl.BlockSpec(shape, lambda *_: (0,) * n)


def _mod_kernel(c_ref, w_ref, b_ref, o_ref):
    c = c_ref[...]
    o_ref[0] = _dot_hi(c * _sigmoid(c), w_ref[0]) + b_ref[0]


def _modulation(c_all, w_mod, b_mod):
    depth, d, n = w_mod.shape
    nb = 1536
    return pl.pallas_call(
        _mod_kernel,
        out_shape=jax.ShapeDtypeStruct((depth, 16, n), F32),
        grid=(depth, n // nb),
        in_specs=[pl.BlockSpec((16, d), lambda l, j: (0, 0)),
                  pl.BlockSpec((1, d, nb), lambda l, j: (l, 0, j)),
                  pl.BlockSpec((1, 1, nb), lambda l, j: (l, 0, j))],
        out_specs=pl.BlockSpec((1, 16, nb), lambda l, j: (l, 0, j)),
        compiler_params=_cparams(("arbitrary", "arbitrary")),
        name="modulation",
    )(c_all, w_mod, b_mod.reshape(depth, 1, n))


def _inproj_kernel(*refs, first, n_lat_tiles):
    if not first:
        x = refs[0][0]
        refs = refs[1:]
    else:
        x_ref, ctx_ref = refs[:2]
        refs = refs[2:]
        ctx = ctx_ref[0]
        if ctx.shape[0] < x_ref.shape[1]:
            ctx = jnp.concatenate([ctx] * (x_ref.shape[1] // ctx.shape[0]), axis=0)
        x = jnp.where(pl.program_id(1) < n_lat_tiles, x_ref[0], ctx)
    (mod_ref, n1_ref, w_ref, qn_ref, kvn_ref, wuq_ref, wukv_ref,
     cosa_ref, sina_ref, cosc_ref, sinc_ref, cosd_ref, sind_ref) = refs[:13]
    outs = refs[13:]
    if first:
        outs[0][0] = x
        outs = outs[1:]
    (qa_ref, ka_ref, va_ref, bq_ref, bff_ref, bfb_ref, bv_ref, bg_ref,
     cq_ref, ck_ref, cv_ref, dq_ref, dk_ref, dv_ref) = outs
    m = mod_ref[0, 0]
    h = (_rms(x, n1_ref[...]) * (1.0 + m[1:2]) + m[0:1]).astype(BF16)

    z = _dot(h, w_ref[:, OFF_A:OFF_A + 768])
    cosa, sina = cosa_ref[...], sina_ref[...]
    qa_ref[0] = (_rope(z[:, 0:256], cosa, sina, 16) * (A_HEAD_DIM ** -0.5)).astype(BF16)
    ka_ref[0] = _rope(z[:, 256:512], cosa, sina, 16).astype(BF16)
    va_ref[0] = z[:, 512:768].astype(BF16)

    z = _dot(h, w_ref[:, OFF_B:OFF_B + 1280])
    bq_ref[0] = z[:, 0:256].astype(BF16)
    bff_ref[0] = z[:, 256:512]
    bfb_ref[0] = z[:, 512:768]
    bv_ref[0] = z[:, 768:1024].astype(BF16)
    bg_ref[0] = z[:, 1024:1280].astype(BF16)

    z = _dot(h, w_ref[:, OFF_C:OFF_C + 896])
    cosc, sinc = cosc_ref[...], sinc_ref[...]
    cq = _rms(z[:, 0:256], qn_ref[...]).astype(BF16)
    q = _rope(_dot(cq, wuq_ref[...]), cosc, sinc, 8)
    cq_ref[0] = (q * ((C_NOPE + C_ROPE) ** -0.5 * LOG2E)).astype(BF16)
    ckv = _rms(z[:, 256:384], kvn_ref[...]).astype(BF16)
    kv = _dot(ckv, wukv_ref[...])
    ck_ref[0] = (kv[:, 0:512] + _rope(z[:, 384:896], cosc, sinc, 8)).astype(BF16)
    cv_ref[0] = kv[:, 512:768].astype(BF16)

    z = _dot(h, w_ref[:, OFF_D:OFF_D + 768])
    cosd, sind = cosd_ref[...], sind_ref[...]
    dq_ref[0] = (_rope(z[:, 0:256], cosd, sind, 8) * (D_HEAD_DIM ** -0.5 * LOG2E)).astype(BF16)
    dk_ref[0] = _rope(z[:, 256:512], cosd, sind, 8).astype(BF16)
    dv_ref[0] = z[:, 512:768].astype(BF16)


def _inproj(x, ctx, mod, n1, w_perm, qn, kvn, wuq, wukv, tabs, n_lat_tiles):
    first = ctx is not None
    b, _, d = x.shape
    t = x.shape[1] + (ctx.shape[1] if first else 0)
    tm = TM_PROJ
    nt = pl.cdiv(t, tm)
    tile = lambda w: pl.BlockSpec((1, tm, w), lambda bi, i: (bi, i, 0))
    modspec = pl.BlockSpec((1, 1, 8, d), lambda bi, i: (bi, i // n_lat_tiles, 0, 0))
    tab = lambda w: pl.BlockSpec((tm, w), lambda bi, i: (i, 0))
    if first:
        lat = pl.BlockSpec((1, tm, d), lambda bi, i: (bi, jnp.minimum(i, n_lat_tiles - 1), 0))
        ctx_spec = pl.BlockSpec((1, min(tm, ctx.shape[1]), d),
                                lambda bi, i: (bi, jnp.maximum(i - n_lat_tiles, 0), 0))
        in_specs, args = [lat, ctx_spec], [x, ctx]
    else:
        in_specs, args = [tile(d)], [x]
    in_specs += [modspec, _const_spec((1, d)), _const_spec((d, N_PERM)), _const_spec((1, 256)),
                 _const_spec((1, 128)), _const_spec((256, 512)), _const_spec((128, 768)),
                 tab(256), tab(256), tab(512), tab(512), tab(256), tab(256)]
    args += [mod, n1, w_perm, qn, kvn, wuq, wukv, *tabs]
    widths = [(256, BF16), (256, BF16), (256, BF16),
              (256, BF16), (256, F32), (256, F32), (256, BF16), (256, BF16),
              (512, BF16), (512, BF16), (256, BF16),
              (256, BF16), (256, BF16), (256, BF16)]
    if first:
        widths = [(d, F32)] + widths
    out_shape = [jax.ShapeDtypeStruct((b, t, w), dt) for w, dt in widths]
    out_specs = [tile(w) for w, _ in widths]
    res = pl.pallas_call(
        functools.partial(_inproj_kernel, first=first, n_lat_tiles=n_lat_tiles),
        out_shape=out_shape, grid=(b, nt), in_specs=in_specs, out_specs=out_specs,
        compiler_params=_cparams(("parallel", "parallel")),
        name="inproj",
    )(*args)
    return (res[0], res[1:]) if first else (x, res)


def _win_kernel(sink_ref, q_ref, k_ref, v_ref, o_ref, *, n_lat, t_all):
    w = A_WINDOW
    qb = WIN_Q
    kb = qb + 2 * w
    n = pl.program_id(1)
    start = pl.multiple_of(jnp.clip(n * qb - w, 0, n_lat - kb), w)
    q = q_ref[0]
    lane = lax.broadcasted_iota(jnp.int32, (qb, LANES), 1)
    row = lax.broadcasted_iota(jnp.int32, (2 * qb, kb), 0)
    col = lax.broadcasted_iota(jnp.int32, (2 * qb, kb), 1)
    rel = (start + col) - (n * qb + row % qb)
    valid = (jnp.abs(rel) <= w) & (n < n_lat // qb)
    row1 = lax.broadcasted_iota(jnp.int32, (2 * qb, 1), 0)
    for j in range(A_KV_HEADS):
        sl = slice(LANES * j, LANES * (j + 1))
        qp = q[:, sl]
        zero = jnp.zeros_like(qp)
        lhs = jnp.concatenate([jnp.where(lane < 64, qp, zero), jnp.where(lane >= 64, qp, zero)], axis=0)
        sb = _dot_nt(lhs, k_ref[0, pl.ds(start, kb), sl])
        sc = _dot_nt(lhs, k_ref[0, n_lat:t_all, sl])
        sb = jnp.where(valid, sb, MASK_VALUE)
        sink = jnp.where(row1 < qb, sink_ref[2 * j], sink_ref[2 * j + 1])
        m = jnp.maximum(jnp.maximum(jnp.max(sb, axis=1, keepdims=True), jnp.max(sc, axis=1, keepdims=True)), sink)
        pb = jnp.exp(sb - m)
        pc = jnp.exp(sc - m)
        den = jnp.sum(pb, axis=1, keepdims=True) + jnp.sum(pc, axis=1, keepdims=True) + jnp.exp(sink - m)
        o = (_dot(pb.astype(BF16), v_ref[0, pl.ds(start, kb), sl])
             + _dot(pc.astype(BF16), v_ref[0, n_lat:t_all, sl])) / den
        o_ref[0, :, sl] = jnp.where(lane < 64, o[:qb], o[qb:]).astype(o_ref.dtype)


def _win_attention(sink, q, k, v, n_lat):
    b, t, _ = q.shape
    qb = WIN_Q
    return pl.pallas_call(
        functools.partial(_win_kernel, n_lat=n_lat, t_all=t),
        out_shape=jax.ShapeDtypeStruct((b, t, 256), BF16),
        grid=(b, t // qb),
        in_specs=[pl.BlockSpec(memory_space=pltpu.SMEM),
                  pl.BlockSpec((1, qb, 256), lambda bi, i: (bi, i, 0)),
                  pl.BlockSpec((1, t, 256), lambda bi, i: (bi, 0, 0)),
                  pl.BlockSpec((1, t, 256), lambda bi, i: (bi, 0, 0))],
        out_specs=pl.BlockSpec((1, qb, 256), lambda bi, i: (bi, i, 0)),
        compiler_params=_cparams(("parallel", "arbitrary")),
        name="win_attention",
    )(sink, q, k, v)


def _hgrn_level_matrix(rev):
    c = HG_CHUNK
    m = np.zeros((8, c, c), np.float32)
    for lvl in range(HG_LEVELS):
        size = c >> lvl
        for t in range(c):
            mid = (t // size) * size + size // 2
            upper = t >= mid
            if not rev:
                rng = range(mid, t + 1) if upper else range(t + 1, mid)
            else:
                rng = range(mid, t) if upper else range(t, mid)
            m[lvl, t, list(rng)] = 1.0
    for t in range(c):
        if not rev:
            m[6, t, :t + 1] = 1.0
            m[7, t, t + 1:] = 1.0
        else:
            m[6, t, t:] = 1.0
            m[7, t, :t] = 1.0
    m = m.reshape(8 * c, c)
    return np.concatenate([m, m], axis=1)


def _hgrn_chunk(q, k, logf, v, mlev, rev):
    c = HG_CHUNK
    hi = logf.astype(BF16)
    lo = (logf - hi.astype(F32)).astype(BF16)
    wgt = jnp.exp(_dot(mlev, jnp.concatenate([hi, lo], axis=0)))
    t = lax.broadcasted_iota(jnp.int32, (c, 256), 0)
    s_col = lax.broadcasted_iota(jnp.int32, (c, 256), 1) % c
    r_bd = lax.broadcasted_iota(jnp.int32, (256, 256), 0) // 64
    c_bd = lax.broadcasted_iota(jnp.int32, (256, 256), 1) // 64
    bd = r_bd == c_bd

    def block_diag(a):
        return jnp.where(bd, jnp.concatenate([a] * 4, axis=0), 0.0).astype(BF16)

    scores = jnp.where(t == s_col, _dot_nt(q.astype(BF16), block_diag(k)), 0.0)
    for lvl in range(HG_LEVELS):
        wl = wgt[c * lvl:c * (lvl + 1)]
        q_side = ((t >> (HG_LEVELS - 1 - lvl)) & 1) == (0 if rev else 1)
        ql = jnp.where(q_side, q * wl, 0.0).astype(BF16)
        kl = jnp.where(q_side, 0.0, k * wl)
        s = _dot_nt(ql, block_diag(kl))
        if lvl > 0:
            sh = HG_LEVELS - lvl
            s = jnp.where((t >> sh) == (s_col >> sh), s, 0.0)
        scores = scores + s
    q_in = (q * wgt[6 * c:7 * c]).astype(BF16)
    k_end = (k * wgt[7 * c:8 * c]).astype(BF16)
    o_intra = _dot(scores.astype(BF16), block_diag(v))
    edge = 6 * c if rev else 7 * c - 1
    total = wgt[edge:edge + 1]
    update = jnp.where(bd, _dot_tn(v.astype(BF16), k_end), 0.0)
    return o_intra, q_in, update, total


def _hgrn_kernel(mf_ref, mb_ref, lb_ref, qf_ref, zf_ref, vf_ref, qb_ref, zb_ref, vb_ref,
                 of_ref, ob_ref, stf_ref, stb_ref):
    c = HG_CHUNK
    nc = TM // c

    @pl.when(pl.program_id(1) == 0)
    def _():
        stf_ref[...] = jnp.zeros_like(stf_ref)
        stb_ref[...] = jnp.zeros_like(stb_ref)

    def scan(s, q_ref, z_ref, v_ref, lb, mlev, o_ref, st_ref, rev):
        sig = _sigmoid(z_ref[s])
        logf = jnp.log(jnp.maximum(lb + (1.0 - lb) * sig, GATE_FLOOR))
        k = (1.0 - lb) * (1.0 - sig)
        order = range(nc - 1, -1, -1) if rev else range(nc)
        rows = [slice(ci * c, (ci + 1) * c) for ci in order]
        parts = [_hgrn_chunk(q_ref[s, r, :].astype(F32), k[r], logf[r], v_ref[s, r, :].astype(F32), mlev, rev)
                 for r in rows]
        st = st_ref[s]
        for r, (o_intra, q_in, update, total) in zip(rows, parts):
            o_ref[s, r, :] = o_intra + _dot_nt(q_in, st.astype(BF16))
            st = st * total + update
        st_ref[s] = st

    for s in range(qf_ref.shape[0]):
        scan(s, qf_ref, zf_ref, vf_ref, lb_ref[0:1], mf_ref[...], of_ref, stf_ref, False)
        scan(s, qb_ref, zb_ref, vb_ref, lb_ref[1:2], mb_ref[...], ob_ref, stb_ref, True)


def _hgrn(lb, q, zff, zfb, v, n_lat):
    b, t, _ = q.shape
    nlb = n_lat // TM
    ncb = (t - n_lat) // TM
    fwd = lambda bi, i: (bi, jnp.where(i < ncb, nlb + i, i - ncb), 0)
    bwd = lambda bi, i: (bi, nlb + ncb - 1 - i, 0)
    nb = HG_BATCH if b % HG_BATCH == 0 else 1
    blk = lambda im: pl.BlockSpec((nb, TM, 256), im)
    mf = jnp.asarray(_hgrn_level_matrix(False), BF16)
    mb = jnp.asarray(_hgrn_level_matrix(True), BF16)
    return pl.pallas_call(
        _hgrn_kernel,
        out_shape=[jax.ShapeDtypeStruct((b, t, 256), F32)] * 2,
        grid=(b // nb, t // TM),
        in_specs=[_const_spec((8 * HG_CHUNK, 2 * HG_CHUNK)), _const_spec((8 * HG_CHUNK, 2 * HG_CHUNK)),
                  _const_spec((2, 256)),
                  blk(fwd), blk(fwd), blk(fwd), blk(bwd), blk(bwd), blk(bwd)],
        out_specs=[blk(fwd), blk(bwd)],
        scratch_shapes=[pltpu.VMEM((nb, 256, 256), F32), pltpu.VMEM((nb, 256, 256), F32)],
        compiler_params=_cparams(("parallel", "arbitrary")),
        name="hgrn",
    )(mf, mb, lb, q, zff, v, q, zfb, v)


def _key_chunks(n_lat, t_all):
    chunks = [(lo, min(lo + KV_CHUNK, t_all)) for lo in range(0, t_all, KV_CHUNK)]
    return chunks, [(n_lat, t_all)]


def _online_softmax(lhs, k_ref, v_ref, chunks):
    rows = lhs.shape[0]
    m = jnp.full((rows, 1), -jnp.inf, F32)
    l = jnp.zeros((rows, 1), F32)
    acc = jnp.zeros((rows, LANES), F32)
    for lo, hi in chunks:
        s = _dot_nt(lhs, k_ref[0, lo:hi, :])
        m_new = jnp.maximum(m, jnp.max(s, axis=1, keepdims=True))
        alpha = jnp.exp2(m - m_new)
        p = jnp.exp2(s - m_new)
        l = alpha * l + jnp.sum(p, axis=1, keepdims=True)
        acc = alpha * acc + _dot(p.astype(BF16), v_ref[0, lo:hi, :])
        m = m_new
    return acc / l


def _mla_kernel(q_ref, k_ref, v_ref, o_ref, *, chunks):
    tq = q_ref.shape[1]
    q = q_ref[0]
    zero = jnp.zeros_like(q)
    lane2 = lax.broadcasted_iota(jnp.int32, q.shape, 1)
    lhs = jnp.concatenate([jnp.where(lane2 < LANES, q, zero), jnp.where(lane2 >= LANES, q, zero)], axis=0)
    o = _online_softmax(lhs, k_ref, v_ref, chunks)
    lane = lax.broadcasted_iota(jnp.int32, (tq, LANES), 1)
    o_ref[0] = jnp.where(lane < 64, o[:tq], o[tq:]).astype(o_ref.dtype)


def _diff_kernel(lam_ref, g_ref, q_ref, k_ref, v_ref, o_ref, *, chunks, lam_init):
    tq = q_ref.shape[1]
    q = q_ref[0]
    zero = jnp.zeros_like(q)
    lane = lax.broadcasted_iota(jnp.int32, (tq, LANES), 1)
    lhs = jnp.concatenate([jnp.where(lane // D_HEAD_DIM == r, q, zero) for r in range(4)], axis=0)
    o = _online_softmax(lhs, k_ref, v_ref, chunks)
    lam = lam_ref[0]
    o0 = o[0:tq] - lam * o[tq:2 * tq]
    o1 = o[2 * tq:3 * tq] - lam * o[3 * tq:4 * tq]
    o = jnp.where(lane < 64, o0, o1)
    o_ref[0] = (_head_rms(o, g_ref[...], LANES) * (1.0 - lam_init)).astype(o_ref.dtype)


def _attention_call(body, name, extras, extra_specs, q, k, v, tq, row0, n_rows, out_init):
    b, t, qw = q.shape
    blk0 = row0 // tq
    n_extra = len(extras)
    in_specs = list(extra_specs) + [
        pl.BlockSpec((1, tq, qw // 2), lambda bi, p, i: (bi, blk0 + i, p)),
        pl.BlockSpec((1, t, k.shape[2] // 2), lambda bi, p, i: (bi, 0, p)),
        pl.BlockSpec((1, t, LANES), lambda bi, p, i: (bi, 0, p))]
    args = list(extras) + [q, k, v]
    aliases = {}
    if out_init is not None:
        in_specs.append(pl.BlockSpec(memory_space=pl.ANY))
        aliases = {len(args): 0}
        args.append(out_init)

    def kern(*refs):
        body(*refs[:n_extra + 3], refs[-1])

    return pl.pallas_call(
        kern,
        out_shape=jax.ShapeDtypeStruct((b, t, 256), BF16),
        grid=(b, 2, n_rows // tq),
        in_specs=in_specs,
        out_specs=pl.BlockSpec((1, tq, LANES), lambda bi, p, i: (bi, blk0 + i, p)),
        input_output_aliases=aliases,
        compiler_params=_cparams(("parallel", "parallel", "arbitrary")),
        name=name,
    )(*args)


def _two_pass_attention(body_of, name, extras, extra_specs, q, k, v, n_lat, tq_lat, with_ctx):
    t = q.shape[1]
    lat_chunks, ctx_chunks = _key_chunks(n_lat, t)
    out = jnp.zeros((q.shape[0], t, 256), BF16)
    out = _attention_call(body_of(lat_chunks), name, extras, extra_specs, q, k, v, tq_lat, 0, n_lat, out)
    if not with_ctx:
        return out
    return _attention_call(body_of(ctx_chunks), name + "_ctx", extras, extra_specs, q, k, v,
                           TQ, n_lat, t - n_lat, out)


def _mla_attention(q, k, v, n_lat, with_ctx):
    body_of = lambda chunks: functools.partial(_mla_kernel, chunks=chunks)
    return _two_pass_attention(body_of, "mla_attention", [], [], q, k, v, n_lat, TQ_MLA, with_ctx)


def _diff_attention(lam, q, k, v, gain, n_lat, lam_init, with_ctx):
    body_of = lambda chunks: functools.partial(_diff_kernel, chunks=chunks, lam_init=lam_init)
    specs = [pl.BlockSpec(memory_space=pltpu.SMEM), _const_spec((1, LANES))]
    return _two_pass_attention(body_of, "diff_attention", [lam, gain], specs, q, k, v, n_lat, TQ, with_ctx)


def _merge_kernel(x_ref, mod_ref, n1_ref, n2_ref, a_ref, of_ref, ob_ref, bg_ref, hg_ref, c_ref, d_ref,
                  wg_ref, wb_ref, wo_ref, wr_ref, br_ref, xo_ref, h2_ref, ri_ref, rw_ref):
    d = D_MODEL
    tm = x_ref.shape[1]
    x = x_ref[0]
    m = mod_ref[0, 0]
    h = (_rms(x, n1_ref[...]) * (1.0 + m[1:2]) + m[0:1]).astype(BF16)
    g = bg_ref[0].astype(F32)
    b_out = _head_rms(of_ref[0] + ob_ref[0], hg_ref[...], 256) * (g * _sigmoid(g))
    branches = (a_ref[0], b_out.astype(BF16), c_ref[0], d_ref[0])
    y = jnp.zeros((tm, d), F32)
    for i, br in enumerate(branches):
        gate = _sigmoid(_dot(h, wg_ref[:, d * i:d * (i + 1)]))
        y = y + gate * _dot(br, wb_ref[i])
    x = x + m[2:3] * _dot(y.astype(BF16), wo_ref[...])
    xo_ref[0] = x
    h2 = _rms(x, n2_ref[...]) * (1.0 + m[4:5]) + m[3:4]
    h2b = h2.astype(BF16)
    h2_ref[0] = _rows_first(h2)

    h2lo = (h2 - h2b.astype(F32)).astype(BF16)
    r = _dot(jnp.concatenate([h2b, h2lo], axis=0), wr_ref[...])
    logits = r[:tm, :LANES] + r[:tm, LANES:] + r[tm:, :LANES] + br_ref[...]
    lane = lax.broadcasted_iota(jnp.int32, (tm, LANES), 1)
    neg = -jnp.inf
    is_g = (lane >= N_EXPERTS) & (lane < N_EXPERTS + N_GROUPS)
    lg = jnp.where(is_g, logits, neg)
    mg = jnp.max(lg, axis=1, keepdims=True)
    g_val = 1.0 / jnp.sum(jnp.exp(lg - mg), axis=1, keepdims=True)
    g_idx = jnp.min(jnp.where(lg == mg, lane, 4 * LANES), axis=1, keepdims=True) - N_EXPERTS
    in_group = (lane >= g_idx * EXPERTS_PER_GROUP) & (lane < (g_idx + 1) * EXPERTS_PER_GROUP)
    le = jnp.where(in_group, logits, neg)
    m1 = jnp.max(le, axis=1, keepdims=True)
    e1 = jnp.min(jnp.where(le == m1, lane, 4 * LANES), axis=1, keepdims=True)
    le2 = jnp.where(lane == e1, neg, le)
    m2 = jnp.max(le2, axis=1, keepdims=True)
    e2 = jnp.min(jnp.where(le2 == m2, lane, 4 * LANES), axis=1, keepdims=True)
    r2 = jnp.exp(m2 - m1)
    v1 = g_val / (1.0 + r2)
    ri_ref[0] = jnp.where(lane == 0, e1, jnp.where(lane == 1, e2, 0))
    rw_ref[0] = jnp.where(lane == 0, v1, jnp.where(lane == 1, v1 * r2, 0.0))


def _merge(x, mod, n1, n2, a_o, o_f, o_b, bg, hg, c_o, d_o, wg, wb, wo, wr, br, n_lat_tiles, t):
    b, _, d = x.shape
    tm = TM_PROJ
    tile = lambda w: pl.BlockSpec((1, tm, w), lambda bi, i: (bi, i, 0))
    return pl.pallas_call(
        _merge_kernel,
        out_shape=[jax.ShapeDtypeStruct((b, t, d), F32), jax.ShapeDtypeStruct((b, t, ROW_SLABS, LANES), F32),
                   jax.ShapeDtypeStruct((b, t, LANES), jnp.int32), jax.ShapeDtypeStruct((b, t, LANES), F32)],
        grid=(b, pl.cdiv(t, tm)),
        in_specs=[tile(d), pl.BlockSpec((1, 1, 8, d), lambda bi, i: (bi, i // n_lat_tiles, 0, 0)),
                  _const_spec((1, d)), _const_spec((1, d)),
                  tile(256), tile(256), tile(256), tile(256), _const_spec((1, 256)), tile(256), tile(256),
                  _const_spec((d, 4 * d)), _const_spec((4, 256, d)), _const_spec((d, d)),
                  _const_spec((d, 256)), _const_spec((1, LANES))],
        out_specs=[tile(d), pl.BlockSpec((1, tm, ROW_SLABS, LANES), lambda bi, i: (bi, i, 0, 0)),
                   tile(LANES), tile(LANES)],
        compiler_params=_cparams(("parallel", "parallel")),
        name="merge",
    )(x, mod, n1, n2, a_o, o_f, o_b, bg, hg, c_o, d_o, wg, wb, wo, wr, br)


def _route_positions(ri, n_tok):
    ef = ri[:, :2].reshape(-1)
    rb = 2 * TM
    oh = (ef[:, None] == jnp.arange(N_EXPERTS, dtype=jnp.int32)[None, :]).astype(F32).reshape(-1, rb, N_EXPERTS)
    tri = (jnp.arange(rb)[:, None] >= jnp.arange(rb)[None, :]).astype(F32)
    within = jnp.einsum("ij,gje->gie", tri, oh)
    tot = within[:, -1, :]
    before = jnp.cumsum(tot, axis=0) - tot
    cnt = jnp.sum(tot, axis=0).astype(jnp.int32)
    pcnt = ((cnt + MOE_TILE - 1) // MOE_TILE) * MOE_TILE
    end = jnp.cumsum(pcnt)
    start = (end - pcnt).astype(F32)
    pos = jnp.sum((within + (before + start[None, :])[:, None, :]) * oh, axis=2) - 1.0
    pos = pos.reshape(-1)
    n_tiles = 2 * n_tok // MOE_TILE + N_EXPERTS
    n_act = (end[-1] // MOE_TILE).astype(jnp.int32)
    tile_row = jnp.minimum(jnp.arange(n_tiles, dtype=jnp.int32), n_act - 1) * MOE_TILE
    tile_expert = jnp.sum((end[None, :] <= tile_row[:, None]).astype(jnp.int32), axis=1)
    tile_expert = jnp.minimum(tile_expert, N_EXPERTS - 1)
    return pos.reshape(n_tok // TM, 1, 2 * TM).astype(jnp.int32), tile_expert, n_act.reshape(1)


def _dispatch_kernel(pos_ref, h_ref, xs_in_ref, xs_ref, sem):
    del xs_in_ref

    def body(r, carry):
        for s in range(2):
            p = pos_ref[0, 0, 2 * r + s]
            pltpu.make_async_copy(h_ref.at[pl.ds(r, 1)], xs_ref.at[pl.ds(p, 1)], sem.at[s]).start(priority=s)
        return carry

    lax.fori_loop(0, TM, body, 0)
    for s in range(2):
        pltpu.make_async_copy(h_ref, h_ref, sem.at[s]).wait()


def _dispatch(pos, h2, xs_init):
    n = h2.shape[0]
    return pl.pallas_call(
        _dispatch_kernel,
        out_shape=jax.ShapeDtypeStruct(xs_init.shape, F32),
        grid=(n // TM,),
        in_specs=[pl.BlockSpec((1, 1, 2 * TM), lambda t: (t, 0, 0), memory_space=pltpu.SMEM),
                  pl.BlockSpec((TM, ROW_SLABS, LANES), lambda t: (t, 0, 0)),
                  pl.BlockSpec(memory_space=pl.ANY)],
        out_specs=pl.BlockSpec(memory_space=pl.ANY),
        scratch_shapes=[pltpu.SemaphoreType.DMA((2,))],
        input_output_aliases={2: 0},
        compiler_params=_cparams(("arbitrary",)),
        name="moe_dispatch",
    )(pos, h2, xs_init)


def _experts_kernel(te_ref, na_ref, x_ref, wg_ref, wu_ref, wd_ref, o_ref):
    del te_ref
    t = pl.program_id(0)

    @pl.when(t < na_ref[0])
    def _():
        x = _slabs_first(x_ref[...])
        xb = [x[s].astype(BF16) for s in range(ROW_SLABS)]
        gt = sum(_dot(xb[s], wg_ref[0, LANES * s:LANES * (s + 1), :]) for s in range(ROW_SLABS))
        up = sum(_dot(xb[s], wu_ref[0, LANES * s:LANES * (s + 1), :]) for s in range(ROW_SLABS))
        o_ref[...] = _rows_first(_dot((gt * _sigmoid(gt) * up).astype(BF16), wd_ref[0]))

    @pl.when(t >= na_ref[0])
    def _():
        o_ref[...] = jnp.zeros_like(o_ref)


def _experts(tile_expert, n_act, xs, wg, wu, wd):
    n_tiles = tile_expert.shape[0]
    d = D_MODEL
    rows = pl.BlockSpec((MOE_TILE, ROW_SLABS, LANES), lambda t, te, na: (t, 0, 0))
    w_in = pl.BlockSpec((1, d, EXPERT_HIDDEN), lambda t, te, na: (te[t], 0, 0))
    return pl.pallas_call(
        _experts_kernel,
        out_shape=jax.ShapeDtypeStruct((n_tiles * MOE_TILE, ROW_SLABS, LANES), F32),
        grid_spec=pltpu.PrefetchScalarGridSpec(
            num_scalar_prefetch=2, grid=(n_tiles,),
            in_specs=[rows, w_in, w_in,
                      pl.BlockSpec((1, EXPERT_HIDDEN, d), lambda t, te, na: (te[t], 0, 0))],
            out_specs=rows),
        compiler_params=_cparams(("arbitrary",)),
        name="moe_experts",
    )(tile_expert, n_act, xs, wg, wu, wd)


def _combine_kernel(pos_ref, ys_ref, w_ref, x_ref, mod_ref, g_ref, o_ref, buf, sem, *, final):
    def body(r, carry):
        for s in range(2):
            p = pos_ref[0, 0, 2 * r + s]
            pltpu.make_async_copy(ys_ref.at[pl.ds(p, 1)], buf.at[s, pl.ds(r, 1)], sem.at[s]).start(priority=s)
        return carry

    lax.fori_loop(0, TM, body, 0)
    for s in range(2):
        pltpu.make_async_copy(buf.at[s], buf.at[s], sem.at[s]).wait()
    w = w_ref[0]
    y0, y1 = _slabs_first(buf[0]), _slabs_first(buf[1])
    f = jnp.concatenate([w[:, 0:1] * y0[k] + w[:, 1:2] * y1[k] for k in range(ROW_SLABS)], axis=1)
    x = x_ref[0] + mod_ref[0, 0, 5:6, :] * f
    o_ref[0] = _rms(x, g_ref[...]) if final else x


def _combine(pos, ys, rwt, x, mod, gain, n_lat):
    b, t, d = x.shape
    nt = t // TM
    tile = lambda w: pl.BlockSpec((1, TM, w), lambda bi, i: (bi, i, 0))
    final = gain is not None
    return pl.pallas_call(
        functools.partial(_combine_kernel, final=final),
        out_shape=jax.ShapeDtypeStruct((b, t, d), F32),
        grid=(b, nt),
        in_specs=[pl.BlockSpec((1, 1, 2 * TM), lambda bi, i: (bi * nt + i, 0, 0), memory_space=pltpu.SMEM),
                  pl.BlockSpec(memory_space=pl.ANY),
                  tile(LANES), tile(d),
                  pl.BlockSpec((1, 1, 8, d), lambda bi, i: (bi, i // (n_lat // TM), 0, 0)),
                  _const_spec((1, d))],
        out_specs=tile(d),
        scratch_shapes=[pltpu.VMEM((2, TM, ROW_SLABS, LANES), F32), pltpu.SemaphoreType.DMA((2,))],
        compiler_params=_cparams(("arbitrary", "arbitrary")),
        name="moe_combine",
    )(pos, ys, rwt, x, mod, gain if final else jnp.ones((1, d), F32))


def _moe(x, h2, ri, rwt, wg, wu, wd, xs_init, mod, gain, n_lat):
    b, t, _ = x.shape
    n = b * t
    pos, tile_expert, n_act = _route_positions(ri.reshape(n, LANES), n)
    xs = _dispatch(pos, h2.reshape(n, ROW_SLABS, LANES), xs_init)
    ys = _experts(tile_expert, n_act, xs, wg, wu, wd)
    return _combine(pos, ys, rwt, x, mod, gain, n_lat), xs


def _rope_tables(n_lat, t_all, lane_rope, lane_off, dr):
    half, quarter = dr // 2, dr // 4
    inv_freq = 1.0 / (ROPE_BASE ** (jnp.arange(quarter, dtype=F32) / quarter))
    off = np.asarray(lane_off)
    use_col = off >= half
    j = (off % half) % quarter
    first = (off % half) < quarter
    tok = jnp.arange(n_lat, dtype=jnp.int32)
    row = (tok // GRID_W).astype(F32)
    col = (tok % GRID_W).astype(F32)
    pos = jnp.where(jnp.asarray(use_col)[None, :], col[:, None], row[:, None])
    ang = pos * inv_freq[jnp.asarray(j)][None, :]
    rope = jnp.asarray(lane_rope)[None, :]
    cos = jnp.where(rope, jnp.cos(ang), 1.0)
    sin = jnp.where(rope, jnp.sin(ang) * jnp.where(jnp.asarray(first), -1.0, 1.0)[None, :], 0.0)
    pad = t_all - n_lat
    w = off.shape[0]
    cos = jnp.concatenate([cos, jnp.ones((pad, w), F32)], axis=0)
    sin = jnp.concatenate([sin, jnp.zeros((pad, w), F32)], axis=0)
    return cos, sin


def _all_rope_tables(n_lat, t_all):
    rep = lambda tabs, n: tuple(jnp.tile(tb, (1, n)) for tb in tabs)
    ta = rep(_rope_tables(n_lat, t_all, np.ones(64, bool), np.arange(64), 64), 4)
    lc = np.arange(128)
    tc = rep(_rope_tables(n_lat, t_all, (lc >= 64) & (lc < 96), np.clip(lc - 64, 0, 31), 32), 4)
    td = rep(_rope_tables(n_lat, t_all, np.ones(32, bool), np.arange(32), 32), 8)
    return (*ta, *tc, *td)


def _permute_w_in(w):
    s = np.cumsum([0, 256, 128, 128, 256, 256, 256, 256, 256, 256, 128, 32, 256, 256, 256])
    w = w.astype(BF16)
    seg = lambda i: w[:, s[i]:s[i + 1]]
    dup = lambda a: jnp.concatenate([a[:, 0:64], a[:, 0:64], a[:, 64:128], a[:, 64:128]], axis=1)
    z64 = jnp.zeros((w.shape[0], 64), w.dtype)
    z32 = jnp.zeros((w.shape[0], 32), w.dtype)
    kr = jnp.concatenate([z64, seg(10), z32] * C_HEADS, axis=1)
    cols = [seg(0), dup(seg(1)), dup(seg(2)),
            seg(3), seg(4), seg(5), seg(6), seg(7),
            seg(8), seg(9), kr,
            seg(11), seg(12), seg(13)]
    return jnp.concatenate(cols, axis=1)


def _permute_mla(w_uq, w_ukv):
    z32 = jnp.zeros((w_uq.shape[0], 32), w_uq.dtype)
    qd = C_NOPE + C_ROPE
    uq = jnp.concatenate([a for h in range(C_HEADS) for a in (w_uq[:, qd * h:qd * (h + 1)], z32)], axis=1)
    z64 = jnp.zeros((w_ukv.shape[0], 64), w_ukv.dtype)
    kd = C_NOPE + C_V
    uk = jnp.concatenate([a for h in range(C_HEADS) for a in (w_ukv[:, kd * h:kd * h + C_NOPE], z64)], axis=1)
    uv = jnp.concatenate([w_ukv[:, kd * h + C_NOPE:kd * (h + 1)] for h in range(C_HEADS)], axis=1)
    return uq.astype(BF16), jnp.concatenate([uk, uv], axis=1).astype(BF16)


def kernel(x, c, ctx, c_ctx, w_mod, b_mod, norm1, norm2, w_in, w_gate, w_branch, w_out, attn_sink, hgrn_lb_logits, hgrn_norm, mla_q_norm, mla_kv_norm, mla_w_uq, mla_w_ukv, diff_lambda, diff_subln, w_router_group, b_router_group, w_router_expert, b_router_expert, w_expert_gate, w_expert_up, w_expert_down, final_norm):
    b, n_lat, d = x.shape
    n_ctx = ctx.shape[1]
    depth = w_mod.shape[0]
    t_all = n_lat + n_ctx
    assert d == D_MODEL and n_lat % TM == 0 and n_ctx % TM == 0 and n_lat % GRID_W == 0 and n_ctx <= n_lat
    assert n_lat >= WIN_Q + 2 * A_WINDOW and n_lat % TQ_MLA == 0 and n_lat % TM_PROJ == 0 and b <= 15
    n_lat_tiles = n_lat // TM_PROJ

    c_all = jnp.concatenate([c, c_ctx[None], jnp.zeros((15 - b, d), F32)], axis=0)
    mod = _modulation(c_all, w_mod, b_mod).reshape(depth, 16, 6, d)
    mod = jnp.pad(mod, ((0, 0), (0, 0), (0, 2), (0, 0)))
    mod = jnp.stack([mod[:, :b], jnp.broadcast_to(mod[:, b:b + 1], (depth, b, 8, d))], axis=2)

    sm = jax.nn.softmax(hgrn_lb_logits.astype(F32), axis=0)
    lower_bounds = jnp.cumsum(sm, axis=0) - sm[0]
    tabs = _all_rope_tables(n_lat, t_all)

    n_tok = b * t_all
    moe_buf = jnp.zeros((2 * n_tok + N_EXPERTS * MOE_TILE, ROW_SLABS, LANES), F32)
    xs = x
    for li in range(depth):
        lam_init = 0.8 - 0.6 * math.exp(-0.3 * li)
        lp = diff_lambda[li].astype(F32)
        lam = (jnp.exp(jnp.sum(lp[0] * lp[1])) - jnp.exp(jnp.sum(lp[2] * lp[3])) + lam_init).reshape(1)
        w_perm = _permute_w_in(w_in[li])
        wuq, wukv = _permute_mla(mla_w_uq[li], mla_w_ukv[li])
        xs, z = _inproj(xs, ctx if li == 0 else None, mod[li], norm1[li][None], w_perm,
                        mla_q_norm[li][None], mla_kv_norm[li][None], wuq, wukv, tabs, n_lat_tiles)
        qa, ka, va, bq, bff, bfb, bv, bg, cq, ck, cv, dq, dk, dv = z
        a_o = _win_attention(attn_sink[li].astype(F32), qa, ka, va, n_lat)
        o_f, o_b = _hgrn(lower_bounds[li], bq, bff, bfb, bv, n_lat)
        with_ctx = li < depth - 1
        c_o = _mla_attention(cq, ck, cv, n_lat, with_ctx)
        d_o = _diff_attention(lam, dq, dk, dv, jnp.tile(diff_subln[li], 2)[None], n_lat, lam_init, with_ctx)
        wr = jnp.concatenate([w_router_expert[li], w_router_group[li],
                              jnp.zeros((d, LANES - N_EXPERTS - N_GROUPS), F32)], axis=1)
        wr_hi = wr.astype(BF16)
        wr_lo = (wr - wr_hi.astype(F32)).astype(BF16)
        br = jnp.concatenate([b_router_expert[li], b_router_group[li],
                              jnp.zeros((LANES - N_EXPERTS - N_GROUPS,), F32)])[None]
        t_keep = t_all if with_ctx else n_lat
        xs, h2, ri, rwt = _merge(xs, mod[li], norm1[li][None], norm2[li][None], a_o, o_f, o_b, bg,
                                 jnp.tile(hgrn_norm[li], B_HEADS)[None], c_o, d_o,
                                 w_gate[li].astype(BF16), w_branch[li].astype(BF16), w_out[li].astype(BF16),
                                 jnp.concatenate([wr_hi, wr_lo], axis=1), br, n_lat_tiles, t_keep)
        xs, moe_buf = _moe(xs, h2, ri, rwt, w_expert_gate[li].astype(BF16), w_expert_up[li].astype(BF16),
                           w_expert_down[li].astype(BF16), moe_buf, mod[li],
                           None if with_ctx else final_norm[None], n_lat)
    return xs
```

```python
import functools
import math

import numpy as np
import jax
import jax.numpy as jnp
from jax import lax
from jax.experimental import pallas as pl
from jax.experimental.pallas import tpu as pltpu

D_MODEL = 1024
GRID_W = 64
ROPE_BASE = 10000.0
NORM_EPS = 1e-6
MASK_VALUE = -1e30
GATE_FLOOR = 1e-30

A_HEADS, A_KV_HEADS, A_HEAD_DIM, A_WINDOW = 4, 2, 64, 128
B_HEADS, B_KEY_DIM, B_VAL_DIM = 4, 64, 64
C_HEADS, C_Q_LORA, C_KV_LORA, C_NOPE, C_ROPE, C_V = 4, 256, 128, 64, 32, 64
D_HEADS, D_HEAD_DIM = 4, 32
N_GROUPS, EXPERTS_PER_GROUP, EXPERT_HIDDEN = 4, 8, 256
N_EXPERTS = N_GROUPS * EXPERTS_PER_GROUP

TM = 256
TM_PROJ = 256
WIN_Q = 256
TQ = 256
TQ_MLA = 512
LOG2E = 1.4426950408889634
KV_CHUNK_MLA = 2048
KV_CHUNK_DIFF = 1024
MOE_TILE = 512
HG_BATCH = 2
HG_CHUNK = 64
HG_LEVELS = 6
LANES = 128
ROW_SLABS = D_MODEL // LANES
VMEM_LIMIT = 52 * 1024 * 1024

OFF_A, OFF_B, OFF_C, OFF_D, N_PERM = 0, 768, 2048, 2944, 3712

F32 = jnp.float32
BF16 = jnp.bfloat16


def _dot(a, b):
    return jnp.dot(a, b, preferred_element_type=F32)


def _dot_nt(a, b):
    return lax.dot_general(a, b, (((1,), (1,)), ((), ())), preferred_element_type=F32)


def _dot_tn(a, b):
    return lax.dot_general(a, b, (((0,), (0,)), ((), ())), preferred_element_type=F32)


def _dot_hi(a, b):
    return jnp.dot(a, b, preferred_element_type=F32, precision=lax.Precision.HIGHEST)


def _sigmoid(x):
    return 1.0 / (1.0 + jnp.exp(-x))


def _rms(x, gain):
    return x * lax.rsqrt(jnp.mean(x * x, axis=-1, keepdims=True) + NORM_EPS) * gain


def _head_rms(o, gain, width):
    r = lax.broadcasted_iota(jnp.int32, (width, width), 0) // 64
    c = lax.broadcasted_iota(jnp.int32, (width, width), 1) // 64
    ones = jnp.where(r == c, 1.0 / 64.0, 0.0).astype(F32)
    ms = _dot_hi(o * o, ones)
    return o * lax.rsqrt(ms + NORM_EPS) * gain


def _rope(x, cos, sin, quarter):
    w = x.shape[-1]
    lane = lax.broadcasted_iota(jnp.int32, x.shape, 1)
    first = (lane % (2 * quarter)) < quarter
    sw = jnp.where(first, pltpu.roll(x, w - quarter, 1), pltpu.roll(x, quarter, 1))
    return x * cos + sw * sin


def _slabs_first(x):
    return pltpu.einshape("tsl->stl", x)


def _rows_first(y):
    slabs = jnp.stack([y[:, LANES * s:LANES * (s + 1)] for s in range(ROW_SLABS)], axis=0)
    return pltpu.einshape("stl->tsl", slabs)


def _cparams(sem):
    return pltpu.CompilerParams(dimension_semantics=sem, vmem_limit_bytes=VMEM_LIMIT)


def _const_spec(shape):
    n = len(shape)
    return pl.BlockSpec(shape, lambda *_: (0,) * n)


def _mod_kernel(c_ref, w_ref, b_ref, o_ref):
    c = c_ref[...]
    o_ref[0] = _dot_hi(c * _sigmoid(c), w_ref[0]) + b_ref[0]


def _modulation(c_all, w_mod, b_mod):
    depth, d, n = w_mod.shape
    nb = 1536
    return pl.pallas_call(
        _mod_kernel,
        out_shape=jax.ShapeDtypeStruct((depth, 16, n), F32),
        grid=(depth, n // nb),
        in_specs=[pl.BlockSpec((16, d), lambda l, j: (0, 0)),
                  pl.BlockSpec((1, d, nb), lambda l, j: (l, 0, j)),
                  pl.BlockSpec((1, 1, nb), lambda l, j: (l, 0, j))],
        out_specs=pl.BlockSpec((1, 16, nb), lambda l, j: (l, 0, j)),
        compiler_params=_cparams(("arbitrary", "arbitrary")),
        name="modulation",
    )(c_all, w_mod, b_mod.reshape(depth, 1, n))


def _inproj_kernel(*refs, first, n_lat_tiles):
    if not first:
        x = refs[0][0]
        refs = refs[1:]
    else:
        x_ref, ctx_ref = refs[:2]
        refs = refs[2:]
        ctx = ctx_ref[0]
        if ctx.shape[0] < x_ref.shape[1]:
            ctx = jnp.concatenate([ctx] * (x_ref.shape[1] // ctx.shape[0]), axis=0)
        x = jnp.where(pl.program_id(1) < n_lat_tiles, x_ref[0], ctx)
    (mod_ref, n1_ref, w_ref, qn_ref, kvn_ref, wuq_ref, wukv_ref,
     cosa_ref, sina_ref, cosc_ref, sinc_ref, cosd_ref, sind_ref) = refs[:13]
    outs = refs[13:]
    if first:
        outs[0][0] = x
        outs = outs[1:]
    (qa_ref, ka_ref, va_ref, bq_ref, bff_ref, bfb_ref, bv_ref, bg_ref,
     cq_ref, ck_ref, cv_ref, dq_ref, dk_ref, dv_ref) = outs
    m = mod_ref[0, 0]
    h = (_rms(x, n1_ref[...]) * (1.0 + m[1:2]) + m[0:1]).astype(BF16)

    z = _dot(h, w_ref[:, OFF_A:OFF_A + 768])
    cosa, sina = cosa_ref[...], sina_ref[...]
    qa_ref[0] = (_rope(z[:, 0:256], cosa, sina, 16) * (A_HEAD_DIM ** -0.5)).astype(BF16)
    ka_ref[0] = _rope(z[:, 256:512], cosa, sina, 16).astype(BF16)
    va_ref[0] = z[:, 512:768].astype(BF16)

    z = _dot(h, w_ref[:, OFF_B:OFF_B + 1280])
    bq_ref[0] = z[:, 0:256].astype(BF16)
    bff_ref[0] = z[:, 256:512]
    bfb_ref[0] = z[:, 512:768]
    bv_ref[0] = z[:, 768:1024].astype(BF16)
    bg_ref[0] = z[:, 1024:1280].astype(BF16)

    z = _dot(h, w_ref[:, OFF_C:OFF_C + 896])
    cosc, sinc = cosc_ref[...], sinc_ref[...]
    cq = _rms(z[:, 0:256], qn_ref[...]).astype(BF16)
    q = _rope(_dot(cq, wuq_ref[...]), cosc, sinc, 8)
    cq_ref[0] = (q * ((C_NOPE + C_ROPE) ** -0.5 * LOG2E)).astype(BF16)
    ckv = _rms(z[:, 256:384], kvn_ref[...]).astype(BF16)
    kv = _dot(ckv, wukv_ref[...])
    ck_ref[0] = (kv[:, 0:512] + _rope(z[:, 384:896], cosc, sinc, 8)).astype(BF16)
    cv_ref[0] = kv[:, 512:768].T.astype(BF16)

    z = _dot(h, w_ref[:, OFF_D:OFF_D + 768])
    cosd, sind = cosd_ref[...], sind_ref[...]
    dq_ref[0] = (_rope(z[:, 0:256], cosd, sind, 8) * (D_HEAD_DIM ** -0.5 * LOG2E)).astype(BF16)
    dk_ref[0] = _rope(z[:, 256:512], cosd, sind, 8).astype(BF16)
    dv_ref[0] = z[:, 512:768].T.astype(BF16)


def _inproj(x, ctx, mod, n1, w_perm, qn, kvn, wuq, wukv, tabs, n_lat_tiles):
    first = ctx is not None
    b, _, d = x.shape
    t = x.shape[1] + (ctx.shape[1] if first else 0)
    tm = TM_PROJ
    nt = pl.cdiv(t, tm)
    tile = lambda w: pl.BlockSpec((1, tm, w), lambda bi, i: (bi, i, 0))
    modspec = pl.BlockSpec((1, 1, 8, d), lambda bi, i: (bi, i // n_lat_tiles, 0, 0))
    tab = lambda w: pl.BlockSpec((tm, w), lambda bi, i: (i, 0))
    if first:
        lat = pl.BlockSpec((1, tm, d), lambda bi, i: (bi, jnp.minimum(i, n_lat_tiles - 1), 0))
        ctx_spec = pl.BlockSpec((1, min(tm, ctx.shape[1]), d),
                                lambda bi, i: (bi, jnp.maximum(i - n_lat_tiles, 0), 0))
        in_specs, args = [lat, ctx_spec], [x, ctx]
    else:
        in_specs, args = [tile(d)], [x]
    in_specs += [modspec, _const_spec((1, d)), _const_spec((d, N_PERM)), _const_spec((1, 256)),
                 _const_spec((1, 128)), _const_spec((256, 512)), _const_spec((128, 768)),
                 tab(256), tab(256), tab(512), tab(512), tab(256), tab(256)]
    args += [mod, n1, w_perm, qn, kvn, wuq, wukv, *tabs]
    widths = [(256, BF16), (256, BF16), (256, BF16),
              (256, BF16), (256, F32), (256, F32), (256, BF16), (256, BF16),
              (512, BF16), (512, BF16), (-256, BF16),
              (256, BF16), (256, BF16), (-256, BF16)]
    if first:
        widths = [(d, F32)] + widths
    ttile = lambda w: pl.BlockSpec((1, w, tm), lambda bi, i: (bi, 0, i))
    out_shape = [jax.ShapeDtypeStruct((b, t, w) if w > 0 else (b, -w, t), dt) for w, dt in widths]
    out_specs = [tile(w) if w > 0 else ttile(-w) for w, _ in widths]
    res = pl.pallas_call(
        functools.partial(_inproj_kernel, first=first, n_lat_tiles=n_lat_tiles),
        out_shape=out_shape, grid=(b, nt), in_specs=in_specs, out_specs=out_specs,
        compiler_params=_cparams(("parallel", "parallel")),
        name="inproj",
    )(*args)
    return (res[0], res[1:]) if first else (x, res)


def _win_kernel(sink_ref, q_ref, k_ref, v_ref, o_ref, *, n_lat, t_all):
    w = A_WINDOW
    qb = WIN_Q
    kb = qb + 2 * w
    n = pl.program_id(1)
    start = pl.multiple_of(jnp.clip(n * qb - w, 0, n_lat - kb), w)
    q = q_ref[0]
    lane = lax.broadcasted_iota(jnp.int32, (qb, LANES), 1)
    row = lax.broadcasted_iota(jnp.int32, (2 * qb, kb), 0)
    col = lax.broadcasted_iota(jnp.int32, (2 * qb, kb), 1)
    rel = (start + col) - (n * qb + row % qb)
    valid = (jnp.abs(rel) <= w) & (n < n_lat // qb)
    row1 = lax.broadcasted_iota(jnp.int32, (2 * qb, 1), 0)
    for j in range(A_KV_HEADS):
        sl = slice(LANES * j, LANES * (j + 1))
        qp = q[:, sl]
        zero = jnp.zeros_like(qp)
        lhs = jnp.concatenate([jnp.where(lane < 64, qp, zero), jnp.where(lane >= 64, qp, zero)], axis=0)
        sb = _dot_nt(lhs, k_ref[0, pl.ds(start, kb), sl])
        sc = _dot_nt(lhs, k_ref[0, n_lat:t_all, sl])
        sb = jnp.where(valid, sb, MASK_VALUE)
        sink = jnp.where(row1 < qb, sink_ref[2 * j], sink_ref[2 * j + 1])
        m = jnp.maximum(jnp.maximum(jnp.max(sb, axis=1, keepdims=True), jnp.max(sc, axis=1, keepdims=True)), sink)
        pb = jnp.exp(sb - m)
        pc = jnp.exp(sc - m)
        den = jnp.sum(pb, axis=1, keepdims=True) + jnp.sum(pc, axis=1, keepdims=True) + jnp.exp(sink - m)
        o = (_dot(pb.astype(BF16), v_ref[0, pl.ds(start, kb), sl])
             + _dot(pc.astype(BF16), v_ref[0, n_lat:t_all, sl])) / den
        o_ref[0, :, sl] = jnp.where(lane < 64, o[:qb], o[qb:]).astype(o_ref.dtype)


def _win_attention(sink, q, k, v, n_lat):
    b, t, _ = q.shape
    qb = WIN_Q
    return pl.pallas_call(
        functools.partial(_win_kernel, n_lat=n_lat, t_all=t),
        out_shape=jax.ShapeDtypeStruct((b, t, 256), BF16),
        grid=(b, t // qb),
        in_specs=[pl.BlockSpec(memory_space=pltpu.SMEM),
                  pl.BlockSpec((1, qb, 256), lambda bi, i: (bi, i, 0)),
                  pl.BlockSpec((1, t, 256), lambda bi, i: (bi, 0, 0)),
                  pl.BlockSpec((1, t, 256), lambda bi, i: (bi, 0, 0))],
        out_specs=pl.BlockSpec((1, qb, 256), lambda bi, i: (bi, i, 0)),
        compiler_params=_cparams(("parallel", "arbitrary")),
        name="win_attention",
    )(sink, q, k, v)


def _hgrn_level_matrix(rev):
    c = HG_CHUNK
    m = np.zeros((8, c, c), np.float32)
    for lvl in range(HG_LEVELS):
        size = c >> lvl
        for t in range(c):
            mid = (t // size) * size + size // 2
            upper = t >= mid
            if not rev:
                rng = range(mid, t + 1) if upper else range(t + 1, mid)
            else:
                rng = range(mid, t) if upper else range(t, mid)
            m[lvl, t, list(rng)] = 1.0
    for t in range(c):
        if not rev:
            m[6, t, :t + 1] = 1.0
            m[7, t, t + 1:] = 1.0
        else:
            m[6, t, t:] = 1.0
            m[7, t, :t] = 1.0
    m = m.reshape(8 * c, c)
    return np.concatenate([m, m], axis=1)


def _hgrn_chunk(q, k, logf, v, mlev, rev):
    c = HG_CHUNK
    hi = logf.astype(BF16)
    lo = (logf - hi.astype(F32)).astype(BF16)
    wgt = jnp.exp(_dot(mlev, jnp.concatenate([hi, lo], axis=0)))
    t = lax.broadcasted_iota(jnp.int32, (c, 256), 0)
    s_col = lax.broadcasted_iota(jnp.int32, (c, 256), 1) % c
    r_bd = lax.broadcasted_iota(jnp.int32, (256, 256), 0) // 64
    c_bd = lax.broadcasted_iota(jnp.int32, (256, 256), 1) // 64
    bd = r_bd == c_bd

    def block_diag(a):
        return jnp.where(bd, jnp.concatenate([a] * 4, axis=0), 0.0).astype(BF16)

    scores = jnp.where(t == s_col, _dot_nt(q.astype(BF16), block_diag(k)), 0.0)
    for lvl in range(HG_LEVELS):
        wl = wgt[c * lvl:c * (lvl + 1)]
        q_side = ((t >> (HG_LEVELS - 1 - lvl)) & 1) == (0 if rev else 1)
        ql = jnp.where(q_side, q * wl, 0.0).astype(BF16)
        kl = jnp.where(q_side, 0.0, k * wl)
        s = _dot_nt(ql, block_diag(kl))
        if lvl > 0:
            sh = HG_LEVELS - lvl
            s = jnp.where((t >> sh) == (s_col >> sh), s, 0.0)
        scores = scores + s
    q_in = (q * wgt[6 * c:7 * c]).astype(BF16)
    k_end = (k * wgt[7 * c:8 * c]).astype(BF16)
    o_intra = _dot(scores.astype(BF16), block_diag(v))
    edge = 6 * c if rev else 7 * c - 1
    total = wgt[edge:edge + 1]
    update = jnp.where(bd, _dot_tn(v.astype(BF16), k_end), 0.0)
    return o_intra, q_in, update, total


def _hgrn_kernel(mf_ref, mb_ref, lb_ref, qf_ref, zf_ref, vf_ref, qb_ref, zb_ref, vb_ref,
                 of_ref, ob_ref, stf_ref, stb_ref):
    c = HG_CHUNK
    nc = TM // c

    @pl.when(pl.program_id(1) == 0)
    def _():
        stf_ref[...] = jnp.zeros_like(stf_ref)
        stb_ref[...] = jnp.zeros_like(stb_ref)

    def scan(s, q_ref, z_ref, v_ref, lb, mlev, o_ref, st_ref, rev):
        sig = _sigmoid(z_ref[s])
        logf = jnp.log(jnp.maximum(lb + (1.0 - lb) * sig, GATE_FLOOR))
        k = (1.0 - lb) * (1.0 - sig)
        order = range(nc - 1, -1, -1) if rev else range(nc)
        rows = [slice(ci * c, (ci + 1) * c) for ci in order]
        parts = [_hgrn_chunk(q_ref[s, r, :].astype(F32), k[r], logf[r], v_ref[s, r, :].astype(F32), mlev, rev)
                 for r in rows]
        st = st_ref[s]
        for r, (o_intra, q_in, update, total) in zip(rows, parts):
            o_ref[s, r, :] = o_intra + _dot_nt(q_in, st.astype(BF16))
            st = st * total + update
        st_ref[s] = st

    for s in range(qf_ref.shape[0]):
        scan(s, qf_ref, zf_ref, vf_ref, lb_ref[0:1], mf_ref[...], of_ref, stf_ref, False)
        scan(s, qb_ref, zb_ref, vb_ref, lb_ref[1:2], mb_ref[...], ob_ref, stb_ref, True)


def _hgrn(lb, q, zff, zfb, v, n_lat):
    b, t, _ = q.shape
    nlb = n_lat // TM
    ncb = (t - n_lat) // TM
    fwd = lambda bi, i: (bi, jnp.where(i < ncb, nlb + i, i - ncb), 0)
    bwd = lambda bi, i: (bi, nlb + ncb - 1 - i, 0)
    nb = HG_BATCH if b % HG_BATCH == 0 else 1
    blk = lambda im: pl.BlockSpec((nb, TM, 256), im)
    mf = jnp.asarray(_hgrn_level_matrix(False), BF16)
    mb = jnp.asarray(_hgrn_level_matrix(True), BF16)
    return pl.pallas_call(
        _hgrn_kernel,
        out_shape=[jax.ShapeDtypeStruct((b, t, 256), F32)] * 2,
        grid=(b // nb, t // TM),
        in_specs=[_const_spec((8 * HG_CHUNK, 2 * HG_CHUNK)), _const_spec((8 * HG_CHUNK, 2 * HG_CHUNK)),
                  _const_spec((2, 256)),
                  blk(fwd), blk(fwd), blk(fwd), blk(bwd), blk(bwd), blk(bwd)],
        out_specs=[blk(fwd), blk(bwd)],
        scratch_shapes=[pltpu.VMEM((nb, 256, 256), F32), pltpu.VMEM((nb, 256, 256), F32)],
        compiler_params=_cparams(("parallel", "arbitrary")),
        name="hgrn",
    )(mf, mb, lb, q, zff, v, q, zfb, v)


def _key_chunks(n_lat, t_all, size):
    chunks = [(lo, min(lo + size, t_all)) for lo in range(0, t_all, size)]
    return chunks, [(n_lat, t_all)]


def _online_softmax(lhs, k_ref, vt_ref, chunks):
    rows = lhs.shape[0]
    m = jnp.full((1, rows), -jnp.inf, F32)
    l = jnp.zeros((1, rows), F32)
    acc = jnp.zeros((LANES, rows), F32)
    for lo, hi in chunks:
        s = _dot_nt(k_ref[0, lo:hi, :], lhs)
        m_new = jnp.maximum(m, jnp.max(s, axis=0, keepdims=True))
        alpha = jnp.exp2(m - m_new)
        p = jnp.exp2(s - m_new)
        l = alpha * l + jnp.sum(p, axis=0, keepdims=True)
        acc = alpha * acc + _dot(vt_ref[0, :, lo:hi], p.astype(BF16))
        m = m_new
    return (acc / l).T


def _mla_kernel(q_ref, k_ref, v_ref, o_ref, *, chunks):
    tq = q_ref.shape[1]
    q = q_ref[0]
    zero = jnp.zeros_like(q)
    lane2 = lax.broadcasted_iota(jnp.int32, q.shape, 1)
    lhs = jnp.concatenate([jnp.where(lane2 < LANES, q, zero), jnp.where(lane2 >= LANES, q, zero)], axis=0)
    o = _online_softmax(lhs, k_ref, v_ref, chunks)
    lane = lax.broadcasted_iota(jnp.int32, (tq, LANES), 1)
    o_ref[0] = jnp.where(lane < 64, o[:tq], o[tq:]).astype(o_ref.dtype)


def _diff_kernel(lam_ref, g_ref, q_ref, k_ref, v_ref, o_ref, *, chunks, lam_init):
    tq = q_ref.shape[1]
    q = q_ref[0]
    zero = jnp.zeros_like(q)
    lane = lax.broadcasted_iota(jnp.int32, (tq, LANES), 1)
    lhs = jnp.concatenate([jnp.where(lane // D_HEAD_DIM == r, q, zero) for r in range(4)], axis=0)
    o = _online_softmax(lhs, k_ref, v_ref, chunks)
    lam = lam_ref[0]
    o0 = o[0:tq] - lam * o[tq:2 * tq]
    o1 = o[2 * tq:3 * tq] - lam * o[3 * tq:4 * tq]
    o = jnp.where(lane < 64, o0, o1)
    o_ref[0] = (_head_rms(o, g_ref[...], LANES) * (1.0 - lam_init)).astype(o_ref.dtype)


def _attention_call(body, name, extras, extra_specs, q, k, v, tq, row0, n_rows, out_init):
    b, t, qw = q.shape
    blk0 = row0 // tq
    n_extra = len(extras)
    in_specs = list(extra_specs) + [
        pl.BlockSpec((1, tq, qw // 2), lambda bi, p, i: (bi, blk0 + i, p)),
        pl.BlockSpec((1, t, k.shape[2] // 2), lambda bi, p, i: (bi, 0, p)),
        pl.BlockSpec((1, LANES, t), lambda bi, p, i: (bi, p, 0))]
    args = list(extras) + [q, k, v]
    aliases = {}
    if out_init is not None:
        in_specs.append(pl.BlockSpec(memory_space=pl.ANY))
        aliases = {len(args): 0}
        args.append(out_init)

    def kern(*refs):
        body(*refs[:n_extra + 3], refs[-1])

    return pl.pallas_call(
        kern,
        out_shape=jax.ShapeDtypeStruct((b, t, 256), BF16),
        grid=(b, 2, n_rows // tq),
        in_specs=in_specs,
        out_specs=pl.BlockSpec((1, tq, LANES), lambda bi, p, i: (bi, blk0 + i, p)),
        input_output_aliases=aliases,
        compiler_params=_cparams(("parallel", "parallel", "arbitrary")),
        name=name,
    )(*args)


def _two_pass_attention(body_of, name, extras, extra_specs, q, k, v, n_lat, tq_lat, kv_chunk, with_ctx):
    t = q.shape[1]
    lat_chunks, ctx_chunks = _key_chunks(n_lat, t, kv_chunk)
    out = jnp.zeros((q.shape[0], t, 256), BF16)
    out = _attention_call(body_of(lat_chunks), name, extras, extra_specs, q, k, v, tq_lat, 0, n_lat, out)
    if not with_ctx:
        return out
    return _attention_call(body_of(ctx_chunks), name + "_ctx", extras, extra_specs, q, k, v,
                           TQ, n_lat, t - n_lat, out)


def _mla_attention(q, k, v, n_lat, with_ctx):
    body_of = lambda chunks: functools.partial(_mla_kernel, chunks=chunks)
    return _two_pass_attention(body_of, "mla_attention", [], [], q, k, v, n_lat, TQ_MLA, KV_CHUNK_MLA, with_ctx)


def _diff_attention(lam, q, k, v, gain, n_lat, lam_init, with_ctx):
    body_of = lambda chunks: functools.partial(_diff_kernel, chunks=chunks, lam_init=lam_init)
    specs = [pl.BlockSpec(memory_space=pltpu.SMEM), _const_spec((1, LANES))]
    return _two_pass_attention(body_of, "diff_attention", [lam, gain], specs, q, k, v, n_lat, TQ,
                               KV_CHUNK_DIFF, with_ctx)


def _merge_kernel(x_ref, mod_ref, n1_ref, n2_ref, a_ref, of_ref, ob_ref, bg_ref, hg_ref, c_ref, d_ref,
                  wg_ref, wb_ref, wo_ref, wr_ref, br_ref, xo_ref, h2_ref, ri_ref, rw_ref):
    d = D_MODEL
    tm = x_ref.shape[1]
    x = x_ref[0]
    m = mod_ref[0, 0]
    h = (_rms(x, n1_ref[...]) * (1.0 + m[1:2]) + m[0:1]).astype(BF16)
    g = bg_ref[0].astype(F32)
    b_out = _head_rms(of_ref[0] + ob_ref[0], hg_ref[...], 256) * (g * _sigmoid(g))
    branches = (a_ref[0], b_out.astype(BF16), c_ref[0], d_ref[0])
    y = jnp.zeros((tm, d), F32)
    for i, br in enumerate(branches):
        gate = _sigmoid(_dot(h, wg_ref[:, d * i:d * (i + 1)]))
        y = y + gate * _dot(br, wb_ref[i])
    x = x + m[2:3] * _dot(y.astype(BF16), wo_ref[...])
    xo_ref[0] = x
    h2 = _rms(x, n2_ref[...]) * (1.0 + m[4:5]) + m[3:4]
    h2b = h2.astype(BF16)
    h2_ref[0] = _rows_first(h2)

    h2lo = (h2 - h2b.astype(F32)).astype(BF16)
    r = _dot(jnp.concatenate([h2b, h2lo], axis=0), wr_ref[...])
    logits = r[:tm, :LANES] + r[:tm, LANES:] + r[tm:, :LANES] + br_ref[...]
    lane = lax.broadcasted_iota(jnp.int32, (tm, LANES), 1)
    neg = -jnp.inf
    is_g = (lane >= N_EXPERTS) & (lane < N_EXPERTS + N_GROUPS)
    lg = jnp.where(is_g, logits, neg)
    mg = jnp.max(lg, axis=1, keepdims=True)
    g_val = 1.0 / jnp.sum(jnp.exp(lg - mg), axis=1, keepdims=True)
    g_idx = jnp.min(jnp.where(lg == mg, lane, 4 * LANES), axis=1, keepdims=True) - N_EXPERTS
    in_group = (lane >= g_idx * EXPERTS_PER_GROUP) & (lane < (g_idx + 1) * EXPERTS_PER_GROUP)
    le = jnp.where(in_group, logits, neg)
    m1 = jnp.max(le, axis=1, keepdims=True)
    e1 = jnp.min(jnp.where(le == m1, lane, 4 * LANES), axis=1, keepdims=True)
    le2 = jnp.where(lane == e1, neg, le)
    m2 = jnp.max(le2, axis=1, keepdims=True)
    e2 = jnp.min(jnp.where(le2 == m2, lane, 4 * LANES), axis=1, keepdims=True)
    r2 = jnp.exp(m2 - m1)
    v1 = g_val / (1.0 + r2)
    ri_ref[0] = jnp.where(lane == 0, e1, jnp.where(lane == 1, e2, 0))
    rw_ref[0] = jnp.where(lane == 0, v1, jnp.where(lane == 1, v1 * r2, 0.0))


def _merge(x, mod, n1, n2, a_o, o_f, o_b, bg, hg, c_o, d_o, wg, wb, wo, wr, br, n_lat_tiles, t):
    b, _, d = x.shape
    tm = TM_PROJ
    tile = lambda w: pl.BlockSpec((1, tm, w), lambda bi, i: (bi, i, 0))
    return pl.pallas_call(
        _merge_kernel,
        out_shape=[jax.ShapeDtypeStruct((b, t, d), F32), jax.ShapeDtypeStruct((b, t, ROW_SLABS, LANES), F32),
                   jax.ShapeDtypeStruct((b, t, LANES), jnp.int32), jax.ShapeDtypeStruct((b, t, LANES), F32)],
        grid=(b, pl.cdiv(t, tm)),
        in_specs=[tile(d), pl.BlockSpec((1, 1, 8, d), lambda bi, i: (bi, i // n_lat_tiles, 0, 0)),
                  _const_spec((1, d)), _const_spec((1, d)),
                  tile(256), tile(256), tile(256), tile(256), _const_spec((1, 256)), tile(256), tile(256),
                  _const_spec((d, 4 * d)), _const_spec((4, 256, d)), _const_spec((d, d)),
                  _const_spec((d, 256)), _const_spec((1, LANES))],
        out_specs=[tile(d), pl.BlockSpec((1, tm, ROW_SLABS, LANES), lambda bi, i: (bi, i, 0, 0)),
                   tile(LANES), tile(LANES)],
        compiler_params=_cparams(("parallel", "parallel")),
        name="merge",
    )(x, mod, n1, n2, a_o, o_f, o_b, bg, hg, c_o, d_o, wg, wb, wo, wr, br)


def _route_positions(ri, n_tok):
    ef = ri[:, :2].reshape(-1)
    rb = 2 * TM
    oh = (ef[:, None] == jnp.arange(N_EXPERTS, dtype=jnp.int32)[None, :]).astype(F32).reshape(-1, rb, N_EXPERTS)
    tri = (jnp.arange(rb)[:, None] >= jnp.arange(rb)[None, :]).astype(F32)
    within = jnp.einsum("ij,gje->gie", tri, oh)
    tot = within[:, -1, :]
    before = jnp.cumsum(tot, axis=0) - tot
    cnt = jnp.sum(tot, axis=0).astype(jnp.int32)
    pcnt = ((cnt + MOE_TILE - 1) // MOE_TILE) * MOE_TILE
    end = jnp.cumsum(pcnt)
    start = (end - pcnt).astype(F32)
    pos = jnp.sum((within + (before + start[None, :])[:, None, :]) * oh, axis=2) - 1.0
    pos = pos.reshape(-1)
    n_tiles = 2 * n_tok // MOE_TILE + N_EXPERTS
    n_act = (end[-1] // MOE_TILE).astype(jnp.int32)
    tile_row = jnp.minimum(jnp.arange(n_tiles, dtype=jnp.int32), n_act - 1) * MOE_TILE
    tile_expert = jnp.sum((end[None, :] <= tile_row[:, None]).astype(jnp.int32), axis=1)
    tile_expert = jnp.minimum(tile_expert, N_EXPERTS - 1)
    return pos.reshape(n_tok // TM, 1, 2 * TM).astype(jnp.int32), tile_expert, n_act.reshape(1)


def _dispatch_kernel(pos_ref, h_ref, xs_in_ref, xs_ref, sem):
    del xs_in_ref

    def body(r, carry):
        for s in range(2):
            p = pos_ref[0, 0, 2 * r + s]
            pltpu.make_async_copy(h_ref.at[pl.ds(r, 1)], xs_ref.at[pl.ds(p, 1)], sem.at[s]).start(priority=s)
        return carry

    lax.fori_loop(0, TM, body, 0)
    for s in range(2):
        pltpu.make_async_copy(h_ref, h_ref, sem.at[s]).wait()


def _dispatch(pos, h2, xs_init):
    n = h2.shape[0]
    return pl.pallas_call(
        _dispatch_kernel,
        out_shape=jax.ShapeDtypeStruct(xs_init.shape, F32),
        grid=(n // TM,),
        in_specs=[pl.BlockSpec((1, 1, 2 * TM), lambda t: (t, 0, 0), memory_space=pltpu.SMEM),
                  pl.BlockSpec((TM, ROW_SLABS, LANES), lambda t: (t, 0, 0)),
                  pl.BlockSpec(memory_space=pl.ANY)],
        out_specs=pl.BlockSpec(memory_space=pl.ANY),
        scratch_shapes=[pltpu.SemaphoreType.DMA((2,))],
        input_output_aliases={2: 0},
        compiler_params=_cparams(("arbitrary",)),
        name="moe_dispatch",
    )(pos, h2, xs_init)


def _experts_kernel(te_ref, na_ref, x_ref, wg_ref, wu_ref, wd_ref, o_ref):
    del te_ref
    t = pl.program_id(0)

    @pl.when(t < na_ref[0])
    def _():
        x = _slabs_first(x_ref[...])
        xb = [x[s].astype(BF16) for s in range(ROW_SLABS)]
        gt = sum(_dot(xb[s], wg_ref[0, LANES * s:LANES * (s + 1), :]) for s in range(ROW_SLABS))
        up = sum(_dot(xb[s], wu_ref[0, LANES * s:LANES * (s + 1), :]) for s in range(ROW_SLABS))
        o_ref[...] = _rows_first(_dot((gt * _sigmoid(gt) * up).astype(BF16), wd_ref[0]))

    @pl.when(t >= na_ref[0])
    def _():
        o_ref[...] = jnp.zeros_like(o_ref)


def _experts(tile_expert, n_act, xs, wg, wu, wd):
    n_tiles = tile_expert.shape[0]
    d = D_MODEL
    rows = pl.BlockSpec((MOE_TILE, ROW_SLABS, LANES), lambda t, te, na: (t, 0, 0))
    w_in = pl.BlockSpec((1, d, EXPERT_HIDDEN), lambda t, te, na: (te[t], 0, 0))
    return pl.pallas_call(
        _experts_kernel,
        out_shape=jax.ShapeDtypeStruct((n_tiles * MOE_TILE, ROW_SLABS, LANES), F32),
        grid_spec=pltpu.PrefetchScalarGridSpec(
            num_scalar_prefetch=2, grid=(n_tiles,),
            in_specs=[rows, w_in, w_in,
                      pl.BlockSpec((1, EXPERT_HIDDEN, d), lambda t, te, na: (te[t], 0, 0))],
            out_specs=rows),
        compiler_params=_cparams(("arbitrary",)),
        name="moe_experts",
    )(tile_expert, n_act, xs, wg, wu, wd)


def _combine_kernel(pos_ref, ys_ref, w_ref, x_ref, mod_ref, g_ref, o_ref, buf, sem, *, final):
    def body(r, carry):
        for s in range(2):
            p = pos_ref[0, 0, 2 * r + s]
            pltpu.make_async_copy(ys_ref.at[pl.ds(p, 1)], buf.at[s, pl.ds(r, 1)], sem.at[s]).start(priority=s)
        return carry

    lax.fori_loop(0, TM, body, 0)
    for s in range(2):
        pltpu.make_async_copy(buf.at[s], buf.at[s], sem.at[s]).wait()
    w = w_ref[0]
    y0, y1 = _slabs_first(buf[0]), _slabs_first(buf[1])
    f = jnp.concatenate([w[:, 0:1] * y0[k] + w[:, 1:2] * y1[k] for k in range(ROW_SLABS)], axis=1)
    x = x_ref[0] + mod_ref[0, 0, 5:6, :] * f
    o_ref[0] = _rms(x, g_ref[...]) if final else x


def _combine(pos, ys, rwt, x, mod, gain, n_lat):
    b, t, d = x.shape
    nt = t // TM
    tile = lambda w: pl.BlockSpec((1, TM, w), lambda bi, i: (bi, i, 0))
    final = gain is not None
    return pl.pallas_call(
        functools.partial(_combine_kernel, final=final),
        out_shape=jax.ShapeDtypeStruct((b, t, d), F32),
        grid=(b, nt),
        in_specs=[pl.BlockSpec((1, 1, 2 * TM), lambda bi, i: (bi * nt + i, 0, 0), memory_space=pltpu.SMEM),
                  pl.BlockSpec(memory_space=pl.ANY),
                  tile(LANES), tile(d),
                  pl.BlockSpec((1, 1, 8, d), lambda bi, i: (bi, i // (n_lat // TM), 0, 0)),
                  _const_spec((1, d))],
        out_specs=tile(d),
        scratch_shapes=[pltpu.VMEM((2, TM, ROW_SLABS, LANES), F32), pltpu.SemaphoreType.DMA((2,))],
        compiler_params=_cparams(("arbitrary", "arbitrary")),
        name="moe_combine",
    )(pos, ys, rwt, x, mod, gain if final else jnp.ones((1, d), F32))


def _moe(x, h2, ri, rwt, wg, wu, wd, xs_init, mod, gain, n_lat):
    b, t, _ = x.shape
    n = b * t
    pos, tile_expert, n_act = _route_positions(ri.reshape(n, LANES), n)
    xs = _dispatch(pos, h2.reshape(n, ROW_SLABS, LANES), xs_init)
    ys = _experts(tile_expert, n_act, xs, wg, wu, wd)
    return _combine(pos, ys, rwt, x, mod, gain, n_lat), xs


def _rope_tables(n_lat, t_all, lane_rope, lane_off, dr):
    half, quarter = dr // 2, dr // 4
    inv_freq = 1.0 / (ROPE_BASE ** (jnp.arange(quarter, dtype=F32) / quarter))
    off = np.asarray(lane_off)
    use_col = off >= half
    j = (off % half) % quarter
    first = (off % half) < quarter
    tok = jnp.arange(n_lat, dtype=jnp.int32)
    row = (tok // GRID_W).astype(F32)
    col = (tok % GRID_W).astype(F32)
    pos = jnp.where(jnp.asarray(use_col)[None, :], col[:, None], row[:, None])
    ang = pos * inv_freq[jnp.asarray(j)][None, :]
    rope = jnp.asarray(lane_rope)[None, :]
    cos = jnp.where(rope, jnp.cos(ang), 1.0)
    sin = jnp.where(rope, jnp.sin(ang) * jnp.where(jnp.asarray(first), -1.0, 1.0)[None, :], 0.0)
    pad = t_all - n_lat
    w = off.shape[0]
    cos = jnp.concatenate([cos, jnp.ones((pad, w), F32)], axis=0)
    sin = jnp.concatenate([sin, jnp.zeros((pad, w), F32)], axis=0)
    return cos, sin


def _all_rope_tables(n_lat, t_all):
    rep = lambda tabs, n: tuple(jnp.tile(tb, (1, n)) for tb in tabs)
    ta = rep(_rope_tables(n_lat, t_all, np.ones(64, bool), np.arange(64), 64), 4)
    lc = np.arange(128)
    tc = rep(_rope_tables(n_lat, t_all, (lc >= 64) & (lc < 96), np.clip(lc - 64, 0, 31), 32), 4)
    td = rep(_rope_tables(n_lat, t_all, np.ones(32, bool), np.arange(32), 32), 8)
    return (*ta, *tc, *td)


def _permute_w_in(w):
    s = np.cumsum([0, 256, 128, 128, 256, 256, 256, 256, 256, 256, 128, 32, 256, 256, 256])
    w = w.astype(BF16)
    seg = lambda i: w[:, s[i]:s[i + 1]]
    dup = lambda a: jnp.concatenate([a[:, 0:64], a[:, 0:64], a[:, 64:128], a[:, 64:128]], axis=1)
    z64 = jnp.zeros((w.shape[0], 64), w.dtype)
    z32 = jnp.zeros((w.shape[0], 32), w.dtype)
    kr = jnp.concatenate([z64, seg(10), z32] * C_HEADS, axis=1)
    cols = [seg(0), dup(seg(1)), dup(seg(2)),
            seg(3), seg(4), seg(5), seg(6), seg(7),
            seg(8), seg(9), kr,
            seg(11), seg(12), seg(13)]
    return jnp.concatenate(cols, axis=1)


def _permute_mla(w_uq, w_ukv):
    z32 = jnp.zeros((w_uq.shape[0], 32), w_uq.dtype)
    qd = C_NOPE + C_ROPE
    uq = jnp.concatenate([a for h in range(C_HEADS) for a in (w_uq[:, qd * h:qd * (h + 1)], z32)], axis=1)
    z64 = jnp.zeros((w_ukv.shape[0], 64), w_ukv.dtype)
    kd = C_NOPE + C_V
    uk = jnp.concatenate([a for h in range(C_HEADS) for a in (w_ukv[:, kd * h:kd * h + C_NOPE], z64)], axis=1)
    uv = jnp.concatenate([w_ukv[:, kd * h + C_NOPE:kd * (h + 1)] for h in range(C_HEADS)], axis=1)
    return uq.astype(BF16), jnp.concatenate([uk, uv], axis=1).astype(BF16)


def kernel(x, c, ctx, c_ctx, w_mod, b_mod, norm1, norm2, w_in, w_gate, w_branch, w_out, attn_sink, hgrn_lb_logits, hgrn_norm, mla_q_norm, mla_kv_norm, mla_w_uq, mla_w_ukv, diff_lambda, diff_subln, w_router_group, b_router_group, w_router_expert, b_router_expert, w_expert_gate, w_expert_up, w_expert_down, final_norm):
    b, n_lat, d = x.shape
    n_ctx = ctx.shape[1]
    depth = w_mod.shape[0]
    t_all = n_lat + n_ctx
    assert d == D_MODEL and n_lat % TM == 0 and n_ctx % TM == 0 and n_lat % GRID_W == 0 and n_ctx <= n_lat
    assert n_lat >= WIN_Q + 2 * A_WINDOW and n_lat % TQ_MLA == 0 and n_lat % TM_PROJ == 0 and b <= 15
    n_lat_tiles = n_lat // TM_PROJ

    c_all = jnp.concatenate([c, c_ctx[None], jnp.zeros((15 - b, d), F32)], axis=0)
    mod = _modulation(c_all, w_mod, b_mod).reshape(depth, 16, 6, d)
    mod = jnp.pad(mod, ((0, 0), (0, 0), (0, 2), (0, 0)))
    mod = jnp.stack([mod[:, :b], jnp.broadcast_to(mod[:, b:b + 1], (depth, b, 8, d))], axis=2)

    sm = jax.nn.softmax(hgrn_lb_logits.astype(F32), axis=0)
    lower_bounds = jnp.cumsum(sm, axis=0) - sm[0]
    tabs = _all_rope_tables(n_lat, t_all)

    n_tok = b * t_all
    moe_buf = jnp.zeros((2 * n_tok + N_EXPERTS * MOE_TILE, ROW_SLABS, LANES), F32)
    xs = x
    for li in range(depth):
        lam_init = 0.8 - 0.6 * math.exp(-0.3 * li)
        lp = diff_lambda[li].astype(F32)
        lam = (jnp.exp(jnp.sum(lp[0] * lp[1])) - jnp.exp(jnp.sum(lp[2] * lp[3])) + lam_init).reshape(1)
        w_perm = _permute_w_in(w_in[li])
        wuq, wukv = _permute_mla(mla_w_uq[li], mla_w_ukv[li])
        xs, z = _inproj(xs, ctx if li == 0 else None, mod[li], norm1[li][None], w_perm,
                        mla_q_norm[li][None], mla_kv_norm[li][None], wuq, wukv, tabs, n_lat_tiles)
        qa, ka, va, bq, bff, bfb, bv, bg, cq, ck, cv, dq, dk, dv = z
        a_o = _win_attention(attn_sink[li].astype(F32), qa, ka, va, n_lat)
        o_f, o_b = _hgrn(lower_bounds[li], bq, bff, bfb, bv, n_lat)
        with_ctx = li < depth - 1
        c_o = _mla_attention(cq, ck, cv, n_lat, with_ctx)
        d_o = _diff_attention(lam, dq, dk, dv, jnp.tile(diff_subln[li], 2)[None], n_lat, lam_init, with_ctx)
        wr = jnp.concatenate([w_router_expert[li], w_router_group[li],
                              jnp.zeros((d, LANES - N_EXPERTS - N_GROUPS), F32)], axis=1)
        wr_hi = wr.astype(BF16)
        wr_lo = (wr - wr_hi.astype(F32)).astype(BF16)
        br = jnp.concatenate([b_router_expert[li], b_router_group[li],
                              jnp.zeros((LANES - N_EXPERTS - N_GROUPS,), F32)])[None]
        t_keep = t_all if with_ctx else n_lat
        xs, h2, ri, rwt = _merge(xs, mod[li], norm1[li][None], norm2[li][None], a_o, o_f, o_b, bg,
                                 jnp.tile(hgrn_norm[li], B_HEADS)[None], c_o, d_o,
                                 w_gate[li].astype(BF16), w_branch[li].astype(BF16), w_out[li].astype(BF16),
                                 jnp.concatenate([wr_hi, wr_lo], axis=1), br, n_lat_tiles, t_keep)
        xs, moe_buf = _moe(xs, h2, ri, rwt, w_expert_gate[li].astype(BF16), w_expert_up[li].astype(BF16),
                           w_expert_down[li].astype(BF16), moe_buf, mod[li],
                           None if with_ctx else final_norm[None], n_lat)
    return xs
```

```python
import functools
import math

import numpy as np
import jax
import jax.numpy as jnp
from jax import lax
from jax.experimental import pallas as pl
from jax.experimental.pallas import tpu as pltpu

D_MODEL = 1024
GRID_W = 64
ROPE_BASE = 10000.0
NORM_EPS = 1e-6
MASK_VALUE = -1e30
GATE_FLOOR = 1e-30

A_HEADS, A_KV_HEADS, A_HEAD_DIM, A_WINDOW = 4, 2, 64, 128
B_HEADS, B_KEY_DIM, B_VAL_DIM = 4, 64, 64
C_HEADS, C_Q_LORA, C_KV_LORA, C_NOPE, C_ROPE, C_V = 4, 256, 128, 64, 32, 64
D_HEADS, D_HEAD_DIM = 4, 32
N_GROUPS, EXPERTS_PER_GROUP, EXPERT_HIDDEN = 4, 8, 256
N_EXPERTS = N_GROUPS * EXPERTS_PER_GROUP

TM = 256
TM_PROJ = 256
WIN_Q = 256
TQ = 256
TQ_MLA = 512
LOG2E = 1.4426950408889634
KV_CHUNK = 4096
MOE_TILE = 512
HG_BATCH = 2
HG_CHUNK = 64
HG_LEVELS = 6
LANES = 128
ROW_SLABS = D_MODEL // LANES
VMEM_LIMIT = 52 * 1024 * 1024

OFF_A, OFF_B, OFF_C, OFF_D, N_PERM = 0, 512, 1792, 2304, 3072

F32 = jnp.float32
BF16 = jnp.bfloat16


def _dot(a, b):
    return jnp.dot(a, b, preferred_element_type=F32)


def _dot_nt(a, b):
    return lax.dot_general(a, b, (((1,), (1,)), ((), ())), preferred_element_type=F32)


def _dot_tn(a, b):
    return lax.dot_general(a, b, (((0,), (0,)), ((), ())), preferred_element_type=F32)


def _dot_hi(a, b):
    return jnp.dot(a, b, preferred_element_type=F32, precision=lax.Precision.HIGHEST)


def _sigmoid(x):
    return 1.0 / (1.0 + jnp.exp(-x))


def _rms(x, gain):
    return x * lax.rsqrt(jnp.mean(x * x, axis=-1, keepdims=True) + NORM_EPS) * gain


def _head_rms(o, gain, width):
    rows = o.shape[0]
    r = lax.broadcasted_iota(jnp.int32, (width, width), 0) // 64
    c = lax.broadcasted_iota(jnp.int32, (width, width), 1) // 64
    ones = jnp.where(r == c, 1.0 / 64.0, 0.0).astype(BF16)
    sq = o * o
    hi = sq.astype(BF16)
    lo = (sq - hi.astype(F32)).astype(BF16)
    ms = _dot(jnp.concatenate([hi, lo], axis=0), ones)
    return o * lax.rsqrt(ms[:rows] + ms[rows:] + NORM_EPS) * gain


def _rope(x, cos, sin, quarter):
    w = x.shape[-1]
    lane = lax.broadcasted_iota(jnp.int32, x.shape, 1)
    first = (lane % (2 * quarter)) < quarter
    sw = jnp.where(first, pltpu.roll(x, w - quarter, 1), pltpu.roll(x, quarter, 1))
    return x * cos + sw * sin


def _dup_heads(x):
    lane = lax.broadcasted_iota(jnp.int32, x.shape, 1)
    sw = pltpu.roll(x, 64, 1)
    return jnp.concatenate([jnp.where(lane < 64, x, sw), jnp.where(lane < 64, sw, x)], axis=1)


def _slabs_first(x):
    return pltpu.einshape("tsl->stl", x)


def _rows_first(y):
    slabs = jnp.stack([y[:, LANES * s:LANES * (s + 1)] for s in range(ROW_SLABS)], axis=0)
    return pltpu.einshape("stl->tsl", slabs)


def _cparams(sem):
    return pltpu.CompilerParams(dimension_semantics=sem, vmem_limit_bytes=VMEM_LIMIT)


def _const_spec(shape):
    n = len(shape)
    return pl.BlockSpec(shape, lambda *_: (0,) * n)


def _mod_kernel(c_ref, w_ref, b_ref, o_ref):
    c = c_ref[...]
    o_ref[0] = _dot_hi(c * _sigmoid(c), w_ref[0]) + b_ref[0]


def _modulation(c_all, w_mod, b_mod):
    depth, d, n = w_mod.shape
    nb = 1536
    return pl.pallas_call(
        _mod_kernel,
        out_shape=jax.ShapeDtypeStruct((depth, 16, n), F32),
        grid=(depth, n // nb),
        in_specs=[pl.BlockSpec((16, d), lambda l, j: (0, 0)),
                  pl.BlockSpec((1, d, nb), lambda l, j: (l, 0, j)),
                  pl.BlockSpec((1, 1, nb), lambda l, j: (l, 0, j))],
        out_specs=pl.BlockSpec((1, 16, nb), lambda l, j: (l, 0, j)),
        compiler_params=_cparams(("arbitrary", "arbitrary")),
        name="modulation",
    )(c_all, w_mod, b_mod.reshape(depth, 1, n))


def _inproj_kernel(*refs, first, n_lat_tiles):
    if not first:
        x = refs[0][0]
        refs = refs[1:]
    else:
        x_ref, ctx_ref = refs[:2]
        refs = refs[2:]
        ctx = ctx_ref[0]
        if ctx.shape[0] < x_ref.shape[1]:
            ctx = jnp.concatenate([ctx] * (x_ref.shape[1] // ctx.shape[0]), axis=0)
        x = jnp.where(pl.program_id(1) < n_lat_tiles, x_ref[0], ctx)
    (mod_ref, n1_ref, w_ref, qn_ref, kvn_ref, wuq_ref, wukv_ref,
     cosa_ref, sina_ref, cosc_ref, sinc_ref, cosd_ref, sind_ref) = refs[:13]
    outs = refs[13:]
    if first:
        outs[0][0] = x
        outs = outs[1:]
    (qa_ref, ka_ref, va_ref, bq_ref, bff_ref, bfb_ref, bv_ref, bg_ref,
     cq_ref, ck_ref, cv_ref, dq_ref, dk_ref, dv_ref) = outs
    m = mod_ref[0, 0]
    h = (_rms(x, n1_ref[...]) * (1.0 + m[1:2]) + m[0:1]).astype(BF16)

    z = _dot(h, w_ref[:, OFF_A:OFF_A + 512])
    cosa, sina = cosa_ref[...], sina_ref[...]
    qa_ref[0] = (_rope(z[:, 0:256], cosa, sina, 16) * (A_HEAD_DIM ** -0.5)).astype(BF16)
    ka_ref[0] = _rope(_dup_heads(z[:, 256:384]), cosa, sina, 16).astype(BF16)
    va_ref[0] = _dup_heads(z[:, 384:512]).astype(BF16)

    z = _dot(h, w_ref[:, OFF_B:OFF_B + 1280])
    bq_ref[0] = z[:, 0:256].astype(BF16)
    bff_ref[0] = z[:, 256:512]
    bfb_ref[0] = z[:, 512:768]
    bv_ref[0] = z[:, 768:1024].astype(BF16)
    bg_ref[0] = z[:, 1024:1280].astype(BF16)

    z = _dot(h, w_ref[:, OFF_C:OFF_C + 512])
    cosc, sinc = cosc_ref[...], sinc_ref[...]
    cq = _rms(z[:, 0:256], qn_ref[...]).astype(BF16)
    q = _rope(_dot(cq, wuq_ref[...]), cosc, sinc, 8)
    cq_ref[0] = (q * ((C_NOPE + C_ROPE) ** -0.5 * LOG2E)).astype(BF16)
    ckv = _rms(z[:, 256:384], kvn_ref[...]).astype(BF16)
    kv = _dot(ckv, wukv_ref[...])
    k_rope = jnp.concatenate([z[:, 384:512]] * C_HEADS, axis=1)
    ck_ref[0] = (kv[:, 0:512] + _rope(k_rope, cosc, sinc, 8)).astype(BF16)
    cv_ref[0] = kv[:, 512:768].astype(BF16)

    z = _dot(h, w_ref[:, OFF_D:OFF_D + 768])
    cosd, sind = cosd_ref[...], sind_ref[...]
    dq_ref[0] = (_rope(z[:, 0:256], cosd, sind, 8) * (D_HEAD_DIM ** -0.5 * LOG2E)).astype(BF16)
    dk_ref[0] = _rope(z[:, 256:512], cosd, sind, 8).astype(BF16)
    dv_ref[0] = z[:, 512:768].astype(BF16)


def _inproj(x, ctx, mod, n1, w_perm, qn, kvn, wuq, wukv, tabs, n_lat_tiles):
    first = ctx is not None
    b, _, d = x.shape
    t = x.shape[1] + (ctx.shape[1] if first else 0)
    tm = TM_PROJ
    nt = pl.cdiv(t, tm)
    tile = lambda w: pl.BlockSpec((1, tm, w), lambda bi, i: (bi, i, 0))
    modspec = pl.BlockSpec((1, 1, 8, d), lambda bi, i: (bi, i // n_lat_tiles, 0, 0))
    tab = lambda w: pl.BlockSpec((tm, w), lambda bi, i: (i, 0))
    if first:
        lat = pl.BlockSpec((1, tm, d), lambda bi, i: (bi, jnp.minimum(i, n_lat_tiles - 1), 0))
        ctx_spec = pl.BlockSpec((1, min(tm, ctx.shape[1]), d),
                                lambda bi, i: (bi, jnp.maximum(i - n_lat_tiles, 0), 0))
        in_specs, args = [lat, ctx_spec], [x, ctx]
    else:
        in_specs, args = [tile(d)], [x]
    in_specs += [modspec, _const_spec((1, d)), _const_spec((d, N_PERM)), _const_spec((1, 256)),
                 _const_spec((1, 128)), _const_spec((256, 512)), _const_spec((128, 768)),
                 tab(256), tab(256), tab(512), tab(512), tab(256), tab(256)]
    args += [mod, n1, w_perm, qn, kvn, wuq, wukv, *tabs]
    widths = [(256, BF16), (256, BF16), (256, BF16),
              (256, BF16), (256, F32), (256, F32), (256, BF16), (256, BF16),
              (512, BF16), (512, BF16), (256, BF16),
              (256, BF16), (256, BF16), (256, BF16)]
    if first:
        widths = [(d, F32)] + widths
    out_shape = [jax.ShapeDtypeStruct((b, t, w), dt) for w, dt in widths]
    out_specs = [tile(w) for w, _ in widths]
    res = pl.pallas_call(
        functools.partial(_inproj_kernel, first=first, n_lat_tiles=n_lat_tiles),
        out_shape=out_shape, grid=(b, nt), in_specs=in_specs, out_specs=out_specs,
        compiler_params=_cparams(("parallel", "parallel")),
        name="inproj",
    )(*args)
    return (res[0], res[1:]) if first else (x, res)


def _win_kernel(sink_ref, q_ref, k_ref, v_ref, o_ref, *, n_lat, t_all):
    w = A_WINDOW
    qb = WIN_Q
    kb = qb + 2 * w
    n = pl.program_id(1)
    start = pl.multiple_of(jnp.clip(n * qb - w, 0, n_lat - kb), w)
    q = q_ref[0]
    lane = lax.broadcasted_iota(jnp.int32, (qb, LANES), 1)
    row = lax.broadcasted_iota(jnp.int32, (2 * qb, kb), 0)
    col = lax.broadcasted_iota(jnp.int32, (2 * qb, kb), 1)
    rel = (start + col) - (n * qb + row % qb)
    valid = (jnp.abs(rel) <= w) & (n < n_lat // qb)
    row1 = lax.broadcasted_iota(jnp.int32, (2 * qb, 1), 0)
    for j in range(A_KV_HEADS):
        sl = slice(LANES * j, LANES * (j + 1))
        qp = q[:, sl]
        zero = jnp.zeros_like(qp)
        lhs = jnp.concatenate([jnp.where(lane < 64, qp, zero), jnp.where(lane >= 64, qp, zero)], axis=0)
        sb = _dot_nt(lhs, k_ref[0, pl.ds(start, kb), sl])
        sc = _dot_nt(lhs, k_ref[0, n_lat:t_all, sl])
        sb = jnp.where(valid, sb, MASK_VALUE)
        sink = jnp.where(row1 < qb, sink_ref[2 * j], sink_ref[2 * j + 1])
        m = jnp.maximum(jnp.maximum(jnp.max(sb, axis=1, keepdims=True), jnp.max(sc, axis=1, keepdims=True)), sink)
        pb = jnp.exp(sb - m)
        pc = jnp.exp(sc - m)
        den = jnp.sum(pb, axis=1, keepdims=True) + jnp.sum(pc, axis=1, keepdims=True) + jnp.exp(sink - m)
        o = (_dot(pb.astype(BF16), v_ref[0, pl.ds(start, kb), sl])
             + _dot(pc.astype(BF16), v_ref[0, n_lat:t_all, sl])) / den
        o_ref[0, :, sl] = jnp.where(lane < 64, o[:qb], o[qb:]).astype(o_ref.dtype)


def _win_attention(sink, q, k, v, n_lat):
    b, t, _ = q.shape
    qb = WIN_Q
    return pl.pallas_call(
        functools.partial(_win_kernel, n_lat=n_lat, t_all=t),
        out_shape=jax.ShapeDtypeStruct((b, t, 256), BF16),
        grid=(b, t // qb),
        in_specs=[pl.BlockSpec(memory_space=pltpu.SMEM),
                  pl.BlockSpec((1, qb, 256), lambda bi, i: (bi, i, 0)),
                  pl.BlockSpec((1, t, 256), lambda bi, i: (bi, 0, 0)),
                  pl.BlockSpec((1, t, 256), lambda bi, i: (bi, 0, 0))],
        out_specs=pl.BlockSpec((1, qb, 256), lambda bi, i: (bi, i, 0)),
        compiler_params=_cparams(("parallel", "arbitrary")),
        name="win_attention",
    )(sink, q, k, v)


def _hgrn_level_matrix(rev):
    c = HG_CHUNK
    m = np.zeros((8, c, c), np.float32)
    for lvl in range(HG_LEVELS):
        size = c >> lvl
        for t in range(c):
            mid = (t // size) * size + size // 2
            upper = t >= mid
            if not rev:
                rng = range(mid, t + 1) if upper else range(t + 1, mid)
            else:
                rng = range(mid, t) if upper else range(t, mid)
            m[lvl, t, list(rng)] = 1.0
    for t in range(c):
        if not rev:
            m[6, t, :t + 1] = 1.0
            m[7, t, t + 1:] = 1.0
        else:
            m[6, t, t:] = 1.0
            m[7, t, :t] = 1.0
    m = m.reshape(8 * c, c)
    return np.concatenate([m, m], axis=1)


def _hgrn_chunk(q, k, logf, v, mlev, rev):
    c = HG_CHUNK
    hi = logf.astype(BF16)
    lo = (logf - hi.astype(F32)).astype(BF16)
    wgt = jnp.exp(_dot(mlev, jnp.concatenate([hi, lo], axis=0)))
    t = lax.broadcasted_iota(jnp.int32, (c, 256), 0)
    s_col = lax.broadcasted_iota(jnp.int32, (c, 256), 1) % c
    r_bd = lax.broadcasted_iota(jnp.int32, (256, 256), 0) // 64
    c_bd = lax.broadcasted_iota(jnp.int32, (256, 256), 1) // 64
    bd = r_bd == c_bd

    def block_diag(a):
        return jnp.where(bd, jnp.concatenate([a] * 4, axis=0), 0.0).astype(BF16)

    scores = jnp.where(t == s_col, _dot_nt(q.astype(BF16), block_diag(k)), 0.0)
    for lvl in range(HG_LEVELS):
        wl = wgt[c * lvl:c * (lvl + 1)]
        q_side = ((t >> (HG_LEVELS - 1 - lvl)) & 1) == (0 if rev else 1)
        ql = jnp.where(q_side, q * wl, 0.0).astype(BF16)
        kl = jnp.where(q_side, 0.0, k * wl)
        s = _dot_nt(ql, block_diag(kl))
        if lvl > 0:
            sh = HG_LEVELS - lvl
            s = jnp.where((t >> sh) == (s_col >> sh), s, 0.0)
        scores = scores + s
    q_in = (q * wgt[6 * c:7 * c]).astype(BF16)
    k_end = (k * wgt[7 * c:8 * c]).astype(BF16)
    o_intra = _dot(scores.astype(BF16), block_diag(v))
    edge = 6 * c if rev else 7 * c - 1
    total = wgt[edge:edge + 1]
    update = jnp.where(bd, _dot_tn(v.astype(BF16), k_end), 0.0)
    return o_intra, q_in, update, total


def _hgrn_kernel(mf_ref, mb_ref, lb_ref, qf_ref, zf_ref, vf_ref, qb_ref, zb_ref, vb_ref,
                 of_ref, ob_ref, stf_ref, stb_ref):
    c = HG_CHUNK
    nc = TM // c

    @pl.when(pl.program_id(1) == 0)
    def _():
        stf_ref[...] = jnp.zeros_like(stf_ref)
        stb_ref[...] = jnp.zeros_like(stb_ref)

    def scan(s, q_ref, z_ref, v_ref, lb, mlev, o_ref, st_ref, rev):
        sig = _sigmoid(z_ref[s])
        logf = jnp.log(jnp.maximum(lb + (1.0 - lb) * sig, GATE_FLOOR))
        k = (1.0 - lb) * (1.0 - sig)
        order = range(nc - 1, -1, -1) if rev else range(nc)
        rows = [slice(ci * c, (ci + 1) * c) for ci in order]
        parts = [_hgrn_chunk(q_ref[s, r, :].astype(F32), k[r], logf[r], v_ref[s, r, :].astype(F32), mlev, rev)
                 for r in rows]
        st = st_ref[s]
        for r, (o_intra, q_in, update, total) in zip(rows, parts):
            o_ref[s, r, :] = o_intra + _dot_nt(q_in, st.astype(BF16))
            st = st * total + update
        st_ref[s] = st

    for s in range(qf_ref.shape[0]):
        scan(s, qf_ref, zf_ref, vf_ref, lb_ref[0:1], mf_ref[...], of_ref, stf_ref, False)
        scan(s, qb_ref, zb_ref, vb_ref, lb_ref[1:2], mb_ref[...], ob_ref, stb_ref, True)


def _hgrn(lb, q, zff, zfb, v, n_lat):
    b, t, _ = q.shape
    nlb = n_lat // TM
    ncb = (t - n_lat) // TM
    fwd = lambda bi, i: (bi, jnp.where(i < ncb, nlb + i, i - ncb), 0)
    bwd = lambda bi, i: (bi, nlb + ncb - 1 - i, 0)
    nb = HG_BATCH if b % HG_BATCH == 0 else 1
    blk = lambda im: pl.BlockSpec((nb, TM, 256), im)
    mf = jnp.asarray(_hgrn_level_matrix(False), BF16)
    mb = jnp.asarray(_hgrn_level_matrix(True), BF16)
    return pl.pallas_call(
        _hgrn_kernel,
        out_shape=[jax.ShapeDtypeStruct((b, t, 256), F32)] * 2,
        grid=(b // nb, t // TM),
        in_specs=[_const_spec((8 * HG_CHUNK, 2 * HG_CHUNK)), _const_spec((8 * HG_CHUNK, 2 * HG_CHUNK)),
                  _const_spec((2, 256)),
                  blk(fwd), blk(fwd), blk(fwd), blk(bwd), blk(bwd), blk(bwd)],
        out_specs=[blk(fwd), blk(bwd)],
        scratch_shapes=[pltpu.VMEM((nb, 256, 256), F32), pltpu.VMEM((nb, 256, 256), F32)],
        compiler_params=_cparams(("parallel", "arbitrary")),
        name="hgrn",
    )(mf, mb, lb, q, zff, v, q, zfb, v)


def _key_chunks(n_lat, t_all):
    chunks = [(lo, min(lo + KV_CHUNK, t_all)) for lo in range(0, t_all, KV_CHUNK)]
    return chunks, [(n_lat, t_all)]


def _online_softmax(lhs, k_ref, v_ref, chunks):
    rows = lhs.shape[0]
    m = jnp.full((rows, 1), -jnp.inf, F32)
    l = jnp.zeros((rows, 1), F32)
    acc = jnp.zeros((rows, LANES), F32)
    for lo, hi in chunks:
        s = _dot_nt(lhs, k_ref[0, lo:hi, :])
        m_new = jnp.maximum(m, jnp.max(s, axis=1, keepdims=True))
        alpha = jnp.exp2(m - m_new)
        p = jnp.exp2(s - m_new)
        l = alpha * l + jnp.sum(p, axis=1, keepdims=True)
        acc = alpha * acc + _dot(p.astype(BF16), v_ref[0, lo:hi, :])
        m = m_new
    return acc / l


def _mla_kernel(q_ref, k_ref, v_ref, o_ref, *, chunks):
    tq = q_ref.shape[1]
    q = q_ref[0]
    zero = jnp.zeros_like(q)
    lane2 = lax.broadcasted_iota(jnp.int32, q.shape, 1)
    lhs = jnp.concatenate([jnp.where(lane2 < LANES, q, zero), jnp.where(lane2 >= LANES, q, zero)], axis=0)
    o = _online_softmax(lhs, k_ref, v_ref, chunks)
    lane = lax.broadcasted_iota(jnp.int32, (tq, LANES), 1)
    o_ref[0] = jnp.where(lane < 64, o[:tq], o[tq:]).astype(o_ref.dtype)


def _diff_kernel(lam_ref, g_ref, q_ref, k_ref, v_ref, o_ref, *, chunks, lam_init):
    tq = q_ref.shape[1]
    q = q_ref[0]
    zero = jnp.zeros_like(q)
    lane = lax.broadcasted_iota(jnp.int32, (tq, LANES), 1)
    lhs = jnp.concatenate([jnp.where(lane // D_HEAD_DIM == r, q, zero) for r in range(4)], axis=0)
    o = _online_softmax(lhs, k_ref, v_ref, chunks)
    lam = lam_ref[0]
    o0 = o[0:tq] - lam * o[tq:2 * tq]
    o1 = o[2 * tq:3 * tq] - lam * o[3 * tq:4 * tq]
    o = jnp.where(lane < 64, o0, o1)
    o_ref[0] = (_head_rms(o, g_ref[...], LANES) * (1.0 - lam_init)).astype(o_ref.dtype)


def _attention_call(body, name, extras, extra_specs, q, k, v, tq, row0, n_rows, out_init):
    b, t, qw = q.shape
    blk0 = row0 // tq
    n_extra = len(extras)
    in_specs = list(extra_specs) + [
        pl.BlockSpec((1, tq, qw // 2), lambda bi, p, i: (bi, blk0 + i, p)),
        pl.BlockSpec((1, t, k.shape[2] // 2), lambda bi, p, i: (bi, 0, p)),
        pl.BlockSpec((1, t, LANES), lambda bi, p, i: (bi, 0, p))]
    args = list(extras) + [q, k, v]
    aliases = {}
    if out_init is not None:
        in_specs.append(pl.BlockSpec(memory_space=pl.ANY))
        aliases = {len(args): 0}
        args.append(out_init)

    def kern(*refs):
        body(*refs[:n_extra + 3], refs[-1])

    return pl.pallas_call(
        kern,
        out_shape=jax.ShapeDtypeStruct((b, t, 256), BF16),
        grid=(b, 2, n_rows // tq),
        in_specs=in_specs,
        out_specs=pl.BlockSpec((1, tq, LANES), lambda bi, p, i: (bi, blk0 + i, p)),
        input_output_aliases=aliases,
        compiler_params=_cparams(("parallel", "parallel", "arbitrary")),
        name=name,
    )(*args)


def _two_pass_attention(body_of, name, extras, extra_specs, q, k, v, n_lat, tq_lat, with_ctx):
    t = q.shape[1]
    lat_chunks, ctx_chunks = _key_chunks(n_lat, t)
    out = jnp.zeros((q.shape[0], t, 256), BF16)
    out = _attention_call(body_of(lat_chunks), name, extras, extra_specs, q, k, v, tq_lat, 0, n_lat, out)
    if not with_ctx:
        return out
    return _attention_call(body_of(ctx_chunks), name + "_ctx", extras, extra_specs, q, k, v,
                           TQ, n_lat, t - n_lat, out)


def _mla_attention(q, k, v, n_lat, with_ctx):
    body_of = lambda chunks: functools.partial(_mla_kernel, chunks=chunks)
    return _two_pass_attention(body_of, "mla_attention", [], [], q, k, v, n_lat, TQ_MLA, with_ctx)


def _diff_attention(lam, q, k, v, gain, n_lat, lam_init, with_ctx):
    body_of = lambda chunks: functools.partial(_diff_kernel, chunks=chunks, lam_init=lam_init)
    specs = [pl.BlockSpec(memory_space=pltpu.SMEM), _const_spec((1, LANES))]
    return _two_pass_attention(body_of, "diff_attention", [lam, gain], specs, q, k, v, n_lat, TQ, with_ctx)


def _merge_kernel(x_ref, mod_ref, n1_ref, n2_ref, a_ref, of_ref, ob_ref, bg_ref, hg_ref, c_ref, d_ref,
                  wg_ref, wb_ref, wo_ref, wr_ref, br_ref, xo_ref, h2_ref, ri_ref, rw_ref):
    d = D_MODEL
    tm = x_ref.shape[1]
    x = x_ref[0]
    m = mod_ref[0, 0]
    h = (_rms(x, n1_ref[...]) * (1.0 + m[1:2]) + m[0:1]).astype(BF16)
    g = bg_ref[0].astype(F32)
    b_out = _head_rms(of_ref[0] + ob_ref[0], hg_ref[...], 256) * (g * _sigmoid(g))
    branches = (a_ref[0], b_out.astype(BF16), c_ref[0], d_ref[0])
    y = jnp.zeros((tm, d), F32)
    for i, br in enumerate(branches):
        gate = _sigmoid(_dot(h, wg_ref[:, d * i:d * (i + 1)]))
        y = y + gate * _dot(br, wb_ref[i])
    x = x + m[2:3] * _dot(y.astype(BF16), wo_ref[...])
    xo_ref[0] = x
    h2 = _rms(x, n2_ref[...]) * (1.0 + m[4:5]) + m[3:4]
    h2b = h2.astype(BF16)
    h2_ref[0] = _rows_first(h2)

    h2lo = (h2 - h2b.astype(F32)).astype(BF16)
    r = _dot(jnp.concatenate([h2b, h2lo], axis=0), wr_ref[...])
    logits = r[:tm, :LANES] + r[:tm, LANES:] + r[tm:, :LANES] + br_ref[...]
    lane = lax.broadcasted_iota(jnp.int32, (tm, LANES), 1)
    neg = -jnp.inf
    is_g = (lane >= N_EXPERTS) & (lane < N_EXPERTS + N_GROUPS)
    lg = jnp.where(is_g, logits, neg)
    mg = jnp.max(lg, axis=1, keepdims=True)
    g_val = 1.0 / jnp.sum(jnp.exp(lg - mg), axis=1, keepdims=True)
    g_idx = jnp.min(jnp.where(lg == mg, lane, 4 * LANES), axis=1, keepdims=True) - N_EXPERTS
    in_group = (lane >= g_idx * EXPERTS_PER_GROUP) & (lane < (g_idx + 1) * EXPERTS_PER_GROUP)
    le = jnp.where(in_group, logits, neg)
    m1 = jnp.max(le, axis=1, keepdims=True)
    e1 = jnp.min(jnp.where(le == m1, lane, 4 * LANES), axis=1, keepdims=True)
    le2 = jnp.where(lane == e1, neg, le)
    m2 = jnp.max(le2, axis=1, keepdims=True)
    e2 = jnp.min(jnp.where(le2 == m2, lane, 4 * LANES), axis=1, keepdims=True)
    r2 = jnp.exp(m2 - m1)
    v1 = g_val / (1.0 + r2)
    ri_ref[0] = jnp.where(lane == 0, e1, jnp.where(lane == 1, e2, 0))
    rw_ref[0] = jnp.where(lane == 0, v1, jnp.where(lane == 1, v1 * r2, 0.0))


def _merge(x, mod, n1, n2, a_o, o_f, o_b, bg, hg, c_o, d_o, wg, wb, wo, wr, br, n_lat_tiles, t):
    b, _, d = x.shape
    tm = TM_PROJ
    tile = lambda w: pl.BlockSpec((1, tm, w), lambda bi, i: (bi, i, 0))
    return pl.pallas_call(
        _merge_kernel,
        out_shape=[jax.ShapeDtypeStruct((b, t, d), F32), jax.ShapeDtypeStruct((b, t, ROW_SLABS, LANES), F32),
                   jax.ShapeDtypeStruct((b, t, LANES), jnp.int32), jax.ShapeDtypeStruct((b, t, LANES), F32)],
        grid=(b, pl.cdiv(t, tm)),
        in_specs=[tile(d), pl.BlockSpec((1, 1, 8, d), lambda bi, i: (bi, i // n_lat_tiles, 0, 0)),
                  _const_spec((1, d)), _const_spec((1, d)),
                  tile(256), tile(256), tile(256), tile(256), _const_spec((1, 256)), tile(256), tile(256),
                  _const_spec((d, 4 * d)), _const_spec((4, 256, d)), _const_spec((d, d)),
                  _const_spec((d, 256)), _const_spec((1, LANES))],
        out_specs=[tile(d), pl.BlockSpec((1, tm, ROW_SLABS, LANES), lambda bi, i: (bi, i, 0, 0)),
                   tile(LANES), tile(LANES)],
        compiler_params=_cparams(("parallel", "parallel")),
        name="merge",
    )(x, mod, n1, n2, a_o, o_f, o_b, bg, hg, c_o, d_o, wg, wb, wo, wr, br)


def _route_positions(ri, n_tok):
    ef = ri[:, :2].reshape(-1)
    rb = 2 * TM
    oh = (ef[:, None] == jnp.arange(N_EXPERTS, dtype=jnp.int32)[None, :]).astype(F32).reshape(-1, rb, N_EXPERTS)
    tri = (jnp.arange(rb)[:, None] >= jnp.arange(rb)[None, :]).astype(F32)
    within = jnp.einsum("ij,gje->gie", tri, oh)
    tot = within[:, -1, :]
    before = jnp.cumsum(tot, axis=0) - tot
    cnt = jnp.sum(tot, axis=0).astype(jnp.int32)
    pcnt = ((cnt + MOE_TILE - 1) // MOE_TILE) * MOE_TILE
    end = jnp.cumsum(pcnt)
    start = (end - pcnt).astype(F32)
    pos = jnp.sum((within + (before + start[None, :])[:, None, :]) * oh, axis=2) - 1.0
    pos = pos.reshape(-1)
    n_tiles = 2 * n_tok // MOE_TILE + N_EXPERTS
    n_act = (end[-1] // MOE_TILE).astype(jnp.int32)
    tile_row = jnp.minimum(jnp.arange(n_tiles, dtype=jnp.int32), n_act - 1) * MOE_TILE
    tile_expert = jnp.sum((end[None, :] <= tile_row[:, None]).astype(jnp.int32), axis=1)
    tile_expert = jnp.minimum(tile_expert, N_EXPERTS - 1)
    return pos.reshape(n_tok // TM, 1, 2 * TM).astype(jnp.int32), tile_expert, n_act.reshape(1)


def _dispatch_kernel(pos_ref, h_ref, xs_in_ref, xs_ref, sem):
    del xs_in_ref

    def body(r, carry):
        for s in range(2):
            p = pos_ref[0, 0, 2 * r + s]
            pltpu.make_async_copy(h_ref.at[pl.ds(r, 1)], xs_ref.at[pl.ds(p, 1)], sem.at[s]).start(priority=s)
        return carry

    lax.fori_loop(0, TM, body, 0)
    for s in range(2):
        pltpu.make_async_copy(h_ref, h_ref, sem.at[s]).wait()


def _dispatch(pos, h2, xs_init):
    n = h2.shape[0]
    return pl.pallas_call(
        _dispatch_kernel,
        out_shape=jax.ShapeDtypeStruct(xs_init.shape, F32),
        grid=(n // TM,),
        in_specs=[pl.BlockSpec((1, 1, 2 * TM), lambda t: (t, 0, 0), memory_space=pltpu.SMEM),
                  pl.BlockSpec((TM, ROW_SLABS, LANES), lambda t: (t, 0, 0)),
                  pl.BlockSpec(memory_space=pl.ANY)],
        out_specs=pl.BlockSpec(memory_space=pl.ANY),
        scratch_shapes=[pltpu.SemaphoreType.DMA((2,))],
        input_output_aliases={2: 0},
        compiler_params=_cparams(("arbitrary",)),
        name="moe_dispatch",
    )(pos, h2, xs_init)


def _experts_kernel(te_ref, na_ref, x_ref, wg_ref, wu_ref, wd_ref, o_ref):
    del te_ref
    t = pl.program_id(0)

    @pl.when(t < na_ref[0])
    def _():
        x = _slabs_first(x_ref[...])
        xb = [x[s].astype(BF16) for s in range(ROW_SLABS)]
        gt = sum(_dot(xb[s], wg_ref[0, LANES * s:LANES * (s + 1), :]) for s in range(ROW_SLABS))
        up = sum(_dot(xb[s], wu_ref[0, LANES * s:LANES * (s + 1), :]) for s in range(ROW_SLABS))
        o_ref[...] = _rows_first(_dot((gt * _sigmoid(gt) * up).astype(BF16), wd_ref[0]))

    @pl.when(t >= na_ref[0])
    def _():
        o_ref[...] = jnp.zeros_like(o_ref)


def _experts(tile_expert, n_act, xs, wg, wu, wd):
    n_tiles = tile_expert.shape[0]
    d = D_MODEL
    rows = pl.BlockSpec((MOE_TILE, ROW_SLABS, LANES), lambda t, te, na: (t, 0, 0))
    w_in = pl.BlockSpec((1, d, EXPERT_HIDDEN), lambda t, te, na: (te[t], 0, 0))
    return pl.pallas_call(
        _experts_kernel,
        out_shape=jax.ShapeDtypeStruct((n_tiles * MOE_TILE, ROW_SLABS, LANES), F32),
        grid_spec=pltpu.PrefetchScalarGridSpec(
            num_scalar_prefetch=2, grid=(n_tiles,),
            in_specs=[rows, w_in, w_in,
                      pl.BlockSpec((1, EXPERT_HIDDEN, d), lambda t, te, na: (te[t], 0, 0))],
            out_specs=rows),
        compiler_params=_cparams(("arbitrary",)),
        name="moe_experts",
    )(tile_expert, n_act, xs, wg, wu, wd)


def _combine_kernel(pos_ref, ys_ref, w_ref, x_ref, mod_ref, g_ref, o_ref, buf, sem, *, final):
    def body(r, carry):
        for s in range(2):
            p = pos_ref[0, 0, 2 * r + s]
            pltpu.make_async_copy(ys_ref.at[pl.ds(p, 1)], buf.at[s, pl.ds(r, 1)], sem.at[s]).start(priority=s)
        return carry

    lax.fori_loop(0, TM, body, 0)
    for s in range(2):
        pltpu.make_async_copy(buf.at[s], buf.at[s], sem.at[s]).wait()
    w = w_ref[0]
    y0, y1 = _slabs_first(buf[0]), _slabs_first(buf[1])
    f = jnp.concatenate([w[:, 0:1] * y0[k] + w[:, 1:2] * y1[k] for k in range(ROW_SLABS)], axis=1)
    x = x_ref[0] + mod_ref[0, 0, 5:6, :] * f
    o_ref[0] = _rms(x, g_ref[...]) if final else x


def _combine(pos, ys, rwt, x, mod, gain, n_lat):
    b, t, d = x.shape
    nt = t // TM
    tile = lambda w: pl.BlockSpec((1, TM, w), lambda bi, i: (bi, i, 0))
    final = gain is not None
    return pl.pallas_call(
        functools.partial(_combine_kernel, final=final),
        out_shape=jax.ShapeDtypeStruct((b, t, d), F32),
        grid=(b, nt),
        in_specs=[pl.BlockSpec((1, 1, 2 * TM), lambda bi, i: (bi * nt + i, 0, 0), memory_space=pltpu.SMEM),
                  pl.BlockSpec(memory_space=pl.ANY),
                  tile(LANES), tile(d),
                  pl.BlockSpec((1, 1, 8, d), lambda bi, i: (bi, i // (n_lat // TM), 0, 0)),
                  _const_spec((1, d))],
        out_specs=tile(d),
        scratch_shapes=[pltpu.VMEM((2, TM, ROW_SLABS, LANES), F32), pltpu.SemaphoreType.DMA((2,))],
        compiler_params=_cparams(("arbitrary", "arbitrary")),
        name="moe_combine",
    )(pos, ys, rwt, x, mod, gain if final else jnp.ones((1, d), F32))


def _moe(x, h2, ri, rwt, wg, wu, wd, xs_init, mod, gain, n_lat):
    b, t, _ = x.shape
    n = b * t
    pos, tile_expert, n_act = _route_positions(ri.reshape(n, LANES), n)
    xs = _dispatch(pos, h2.reshape(n, ROW_SLABS, LANES), xs_init)
    ys = _experts(tile_expert, n_act, xs, wg, wu, wd)
    return _combine(pos, ys, rwt, x, mod, gain, n_lat), xs


def _rope_tables(n_lat, t_all, lane_rope, lane_off, dr):
    half, quarter = dr // 2, dr // 4
    inv_freq = 1.0 / (ROPE_BASE ** (jnp.arange(quarter, dtype=F32) / quarter))
    off = np.asarray(lane_off)
    use_col = off >= half
    j = (off % half) % quarter
    first = (off % half) < quarter
    tok = jnp.arange(n_lat, dtype=jnp.int32)
    row = (tok // GRID_W).astype(F32)
    col = (tok % GRID_W).astype(F32)
    pos = jnp.where(jnp.asarray(use_col)[None, :], col[:, None], row[:, None])
    ang = pos * inv_freq[jnp.asarray(j)][None, :]
    rope = jnp.asarray(lane_rope)[None, :]
    cos = jnp.where(rope, jnp.cos(ang), 1.0)
    sin = jnp.where(rope, jnp.sin(ang) * jnp.where(jnp.asarray(first), -1.0, 1.0)[None, :], 0.0)
    pad = t_all - n_lat
    w = off.shape[0]
    cos = jnp.concatenate([cos, jnp.ones((pad, w), F32)], axis=0)
    sin = jnp.concatenate([sin, jnp.zeros((pad, w), F32)], axis=0)
    return cos, sin


def _all_rope_tables(n_lat, t_all):
    rep = lambda tabs, n: tuple(jnp.tile(tb, (1, n)) for tb in tabs)
    ta = rep(_rope_tables(n_lat, t_all, np.ones(64, bool), np.arange(64), 64), 4)
    lc = np.arange(128)
    tc = rep(_rope_tables(n_lat, t_all, (lc >= 64) & (lc < 96), np.clip(lc - 64, 0, 31), 32), 4)
    td = rep(_rope_tables(n_lat, t_all, np.ones(32, bool), np.arange(32), 32), 8)
    return (*ta, *tc, *td)


def _permute_w_in(w):
    s = np.cumsum([0, 256, 128, 128, 256, 256, 256, 256, 256, 256, 128, 32, 256, 256, 256])
    w = w.astype(BF16)
    seg = lambda i: w[:, s[i]:s[i + 1]]
    z64 = jnp.zeros((w.shape[0], 64), w.dtype)
    z32 = jnp.zeros((w.shape[0], 32), w.dtype)
    kr = jnp.concatenate([z64, seg(10), z32], axis=1)
    cols = [seg(0), seg(1), seg(2),
            seg(3), seg(4), seg(5), seg(6), seg(7),
            seg(8), seg(9), kr,
            seg(11), seg(12), seg(13)]
    return jnp.concatenate(cols, axis=1)


def _permute_mla(w_uq, w_ukv):
    z32 = jnp.zeros((w_uq.shape[0], 32), w_uq.dtype)
    qd = C_NOPE + C_ROPE
    uq = jnp.concatenate([a for h in range(C_HEADS) for a in (w_uq[:, qd * h:qd * (h + 1)], z32)], axis=1)
    z64 = jnp.zeros((w_ukv.shape[0], 64), w_ukv.dtype)
    kd = C_NOPE + C_V
    uk = jnp.concatenate([a for h in range(C_HEADS) for a in (w_ukv[:, kd * h:kd * h + C_NOPE], z64)], axis=1)
    uv = jnp.concatenate([w_ukv[:, kd * h + C_NOPE:kd * (h + 1)] for h in range(C_HEADS)], axis=1)
    return uq.astype(BF16), jnp.concatenate([uk, uv], axis=1).astype(BF16)


def kernel(x, c, ctx, c_ctx, w_mod, b_mod, norm1, norm2, w_in, w_gate, w_branch, w_out, attn_sink, hgrn_lb_logits, hgrn_norm, mla_q_norm, mla_kv_norm, mla_w_uq, mla_w_ukv, diff_lambda, diff_subln, w_router_group, b_router_group, w_router_expert, b_router_expert, w_expert_gate, w_expert_up, w_expert_down, final_norm):
    b, n_lat, d = x.shape
    n_ctx = ctx.shape[1]
    depth = w_mod.shape[0]
    t_all = n_lat + n_ctx
    assert d == D_MODEL and n_lat % TM == 0 and n_ctx % TM == 0 and n_lat % GRID_W == 0 and n_ctx <= n_lat
    assert n_lat >= WIN_Q + 2 * A_WINDOW and n_lat % TQ_MLA == 0 and n_lat % TM_PROJ == 0 and b <= 15
    n_lat_tiles = n_lat // TM_PROJ

    c_all = jnp.concatenate([c, c_ctx[None], jnp.zeros((15 - b, d), F32)], axis=0)
    mod = _modulation(c_all, w_mod, b_mod).reshape(depth, 16, 6, d)
    mod = jnp.pad(mod, ((0, 0), (0, 0), (0, 2), (0, 0)))
    mod = jnp.stack([mod[:, :b], jnp.broadcast_to(mod[:, b:b + 1], (depth, b, 8, d))], axis=2)

    sm = jax.nn.softmax(hgrn_lb_logits.astype(F32), axis=0)
    lower_bounds = jnp.cumsum(sm, axis=0) - sm[0]
    tabs = _all_rope_tables(n_lat, t_all)

    n_tok = b * t_all
    moe_buf = jnp.zeros((2 * n_tok + N_EXPERTS * MOE_TILE, ROW_SLABS, LANES), F32)
    xs = x
    for li in range(depth):
        lam_init = 0.8 - 0.6 * math.exp(-0.3 * li)
        lp = diff_lambda[li].astype(F32)
        lam = (jnp.exp(jnp.sum(lp[0] * lp[1])) - jnp.exp(jnp.sum(lp[2] * lp[3])) + lam_init).reshape(1)
        w_perm = _permute_w_in(w_in[li])
        wuq, wukv = _permute_mla(mla_w_uq[li], mla_w_ukv[li])
        xs, z = _inproj(xs, ctx if li == 0 else None, mod[li], norm1[li][None], w_perm,
                        mla_q_norm[li][None], mla_kv_norm[li][None], wuq, wukv, tabs, n_lat_tiles)
        qa, ka, va, bq, bff, bfb, bv, bg, cq, ck, cv, dq, dk, dv = z
        a_o = _win_attention(attn_sink[li].astype(F32), qa, ka, va, n_lat)
        o_f, o_b = _hgrn(lower_bounds[li], bq, bff, bfb, bv, n_lat)
        with_ctx = li < depth - 1
        c_o = _mla_attention(cq, ck, cv, n_lat, with_ctx)
        d_o = _diff_attention(lam, dq, dk, dv, jnp.tile(diff_subln[li], 2)[None], n_lat, lam_init, with_ctx)
        wr = jnp.concatenate([w_router_expert[li], w_router_group[li],
                              jnp.zeros((d, LANES - N_EXPERTS - N_GROUPS), F32)], axis=1)
        wr_hi = wr.astype(BF16)
        wr_lo = (wr - wr_hi.astype(F32)).astype(BF16)
        br = jnp.concatenate([b_router_expert[li], b_router_group[li],
                              jnp.zeros((LANES - N_EXPERTS - N_GROUPS,), F32)])[None]
        t_keep = t_all if with_ctx else n_lat
        xs, h2, ri, rwt = _merge(xs, mod[li], norm1[li][None], norm2[li][None], a_o, o_f, o_b, bg,
                                 jnp.tile(hgrn_norm[li], B_HEADS)[None], c_o, d_o,
                                 w_gate[li].astype(BF16), w_branch[li].astype(BF16), w_out[li].astype(BF16),
                                 jnp.concatenate([wr_hi, wr_lo], axis=1), br, n_lat_tiles, t_keep)
        xs, moe_buf = _moe(xs, h2, ri, rwt, w_expert_gate[li].astype(BF16), w_expert_up[li].astype(BF16),
                           w_expert_down[li].astype(BF16), moe_buf, mod[li],
                           None if with_ctx else final_norm[None], n_lat)
    return xs
```

```python
import functools
import math

import numpy as np
import jax
import jax.numpy as jnp
from jax import lax
from jax.experimental import pallas as pl
from jax.experimental.pallas import tpu as pltpu

D_MODEL = 1024
GRID_W = 64
ROPE_BASE = 10000.0
NORM_EPS = 1e-6
MASK_VALUE = -1e30
GATE_FLOOR = 1e-30

A_HEADS, A_KV_HEADS, A_HEAD_DIM, A_WINDOW = 4, 2, 64, 128
B_HEADS, B_KEY_DIM, B_VAL_DIM = 4, 64, 64
C_HEADS, C_Q_LORA, C_KV_LORA, C_NOPE, C_ROPE, C_V = 4, 256, 128, 64, 32, 64
D_HEADS, D_HEAD_DIM = 4, 32
N_GROUPS, EXPERTS_PER_GROUP, EXPERT_HIDDEN = 4, 8, 256
N_EXPERTS = N_GROUPS * EXPERTS_PER_GROUP

TM = 256
TM_PROJ = 256
WIN_Q = 256
TQ = 256
TQ_MLA = 512
LOG2E = 1.4426950408889634
KV_CHUNK = 4096
MOE_TILE = 512
HG_BATCH = 2
MERGE_BATCH = 2
HG_CHUNK = 64
HG_LEVELS = 6
LANES = 128
ROW_SLABS = D_MODEL // LANES
VMEM_LIMIT = 52 * 1024 * 1024

OFF_A, OFF_B, OFF_C, OFF_D, N_PERM = 0, 512, 1792, 2304, 3072

F32 = jnp.float32
BF16 = jnp.bfloat16


def _dot(a, b):
    return jnp.dot(a, b, preferred_element_type=F32)


def _dot_nt(a, b):
    return lax.dot_general(a, b, (((1,), (1,)), ((), ())), preferred_element_type=F32)


def _dot_tn(a, b):
    return lax.dot_general(a, b, (((0,), (0,)), ((), ())), preferred_element_type=F32)


def _dot_hi(a, b):
    return jnp.dot(a, b, preferred_element_type=F32, precision=lax.Precision.HIGHEST)


def _sigmoid(x):
    return 1.0 / (1.0 + jnp.exp(-x))


def _rms(x, gain):
    return x * lax.rsqrt(jnp.mean(x * x, axis=-1, keepdims=True) + NORM_EPS) * gain


def _head_rms(o, gain, width):
    rows = o.shape[0]
    r = lax.broadcasted_iota(jnp.int32, (width, width), 0) // 64
    c = lax.broadcasted_iota(jnp.int32, (width, width), 1) // 64
    ones = jnp.where(r == c, 1.0 / 64.0, 0.0).astype(BF16)
    sq = o * o
    hi = sq.astype(BF16)
    lo = (sq - hi.astype(F32)).astype(BF16)
    ms = _dot(jnp.concatenate([hi, lo], axis=0), ones)
    return o * lax.rsqrt(ms[:rows] + ms[rows:] + NORM_EPS) * gain


def _rope(x, cos, sin, quarter):
    w = x.shape[-1]
    lane = lax.broadcasted_iota(jnp.int32, x.shape, 1)
    first = (lane % (2 * quarter)) < quarter
    sw = jnp.where(first, pltpu.roll(x, w - quarter, 1), pltpu.roll(x, quarter, 1))
    return x * cos + sw * sin


def _dup_heads(x):
    lane = lax.broadcasted_iota(jnp.int32, x.shape, 1)
    sw = pltpu.roll(x, 64, 1)
    return jnp.concatenate([jnp.where(lane < 64, x, sw), jnp.where(lane < 64, sw, x)], axis=1)


def _slabs_first(x):
    return pltpu.einshape("tsl->stl", x)


def _rows_first(y):
    slabs = jnp.stack([y[:, LANES * s:LANES * (s + 1)] for s in range(ROW_SLABS)], axis=0)
    return pltpu.einshape("stl->tsl", slabs)


def _cparams(sem):
    return pltpu.CompilerParams(dimension_semantics=sem, vmem_limit_bytes=VMEM_LIMIT)


def _const_spec(shape):
    n = len(shape)
    return pl.BlockSpec(shape, lambda *_: (0,) * n)


def _mod_kernel(c_ref, w_ref, b_ref, o_ref):
    c = c_ref[...]
    o_ref[0] = _dot_hi(c * _sigmoid(c), w_ref[0]) + b_ref[0]


def _modulation(c_all, w_mod, b_mod):
    depth, d, n = w_mod.shape
    nb = 1536
    return pl.pallas_call(
        _mod_kernel,
        out_shape=jax.ShapeDtypeStruct((depth, 16, n), F32),
        grid=(depth, n // nb),
        in_specs=[pl.BlockSpec((16, d), lambda l, j: (0, 0)),
                  pl.BlockSpec((1, d, nb), lambda l, j: (l, 0, j)),
                  pl.BlockSpec((1, 1, nb), lambda l, j: (l, 0, j))],
        out_specs=pl.BlockSpec((1, 16, nb), lambda l, j: (l, 0, j)),
        compiler_params=_cparams(("arbitrary", "arbitrary")),
        name="modulation",
    )(c_all, w_mod, b_mod.reshape(depth, 1, n))


def _inproj_kernel(*refs, first, n_lat_tiles):
    if not first:
        x = refs[0][...]
        refs = refs[1:]
    else:
        x_ref, ctx_ref = refs[:2]
        refs = refs[2:]
        ctx = ctx_ref[...]
        if ctx.shape[1] < x_ref.shape[1]:
            ctx = jnp.concatenate([ctx] * (x_ref.shape[1] // ctx.shape[1]), axis=1)
        x = jnp.where(pl.program_id(1) < n_lat_tiles, x_ref[...], ctx)
    (mod_ref, n1_ref, w_ref, qn_ref, kvn_ref, wuq_ref, wukv_ref,
     cosa_ref, sina_ref, cosc_ref, sinc_ref, cosd_ref, sind_ref) = refs[:13]
    outs = refs[13:]
    nb, tm, d = x.shape

    def put(ref, val):
        ref[...] = val.reshape(ref.shape)

    if first:
        put(outs[0], x)
        outs = outs[1:]
    (qa_ref, ka_ref, va_ref, bq_ref, bff_ref, bfb_ref, bv_ref, bg_ref,
     cq_ref, ck_ref, cv_ref, dq_ref, dk_ref, dv_ref) = outs
    m = mod_ref[:, 0]
    h = (_rms(x, n1_ref[...]) * (1.0 + m[:, 1:2, :]) + m[:, 0:1, :]).astype(BF16).reshape(nb * tm, d)
    tab = lambda ref: jnp.concatenate([ref[...]] * nb, axis=0)

    z = _dot(h, w_ref[:, OFF_A:OFF_A + 512])
    cosa, sina = tab(cosa_ref), tab(sina_ref)
    put(qa_ref, (_rope(z[:, 0:256], cosa, sina, 16) * (A_HEAD_DIM ** -0.5 * LOG2E)).astype(BF16))
    put(ka_ref, _dup_heads(_rope(z[:, 256:384], cosa[:, :LANES], sina[:, :LANES], 16)).astype(BF16))
    put(va_ref, _dup_heads(z[:, 384:512]).astype(BF16))

    z = _dot(h, w_ref[:, OFF_B:OFF_B + 1280])
    put(bq_ref, z[:, 0:256].astype(BF16))
    put(bff_ref, z[:, 256:512])
    put(bfb_ref, z[:, 512:768])
    put(bv_ref, z[:, 768:1024].astype(BF16))
    put(bg_ref, z[:, 1024:1280].astype(BF16))

    z = _dot(h, w_ref[:, OFF_C:OFF_C + 512])
    cosc, sinc = tab(cosc_ref), tab(sinc_ref)
    cq = _rms(z[:, 0:256], qn_ref[...]).astype(BF16)
    q = _rope(_dot(cq, wuq_ref[...]), cosc, sinc, 8)
    put(cq_ref, (q * ((C_NOPE + C_ROPE) ** -0.5 * LOG2E)).astype(BF16))
    ckv = _rms(z[:, 256:384], kvn_ref[...]).astype(BF16)
    kv = _dot(ckv, wukv_ref[...])
    k_rope = _rope(z[:, 384:512], cosc[:, :LANES], sinc[:, :LANES], 8)
    put(ck_ref, (kv[:, 0:512] + jnp.concatenate([k_rope] * C_HEADS, axis=1)).astype(BF16))
    put(cv_ref, kv[:, 512:768].astype(BF16))

    z = _dot(h, w_ref[:, OFF_D:OFF_D + 768])
    cosd, sind = tab(cosd_ref), tab(sind_ref)
    put(dq_ref, (_rope(z[:, 0:256], cosd, sind, 8) * (D_HEAD_DIM ** -0.5 * LOG2E)).astype(BF16))
    put(dk_ref, _rope(z[:, 256:512], cosd, sind, 8).astype(BF16))
    put(dv_ref, z[:, 512:768].astype(BF16))


def _inproj(x, ctx, mod, n1, w_perm, qn, kvn, wuq, wukv, tabs, n_lat_tiles):
    first = ctx is not None
    b, _, d = x.shape
    t = x.shape[1] + (ctx.shape[1] if first else 0)
    tm = TM_PROJ
    nt = pl.cdiv(t, tm)
    nb = MERGE_BATCH if b % MERGE_BATCH == 0 else 1
    tile = lambda w: pl.BlockSpec((nb, tm, w), lambda bi, i: (bi, i, 0))
    modspec = pl.BlockSpec((nb, 1, 8, d), lambda bi, i: (bi, i // n_lat_tiles, 0, 0))
    tab = lambda w: pl.BlockSpec((tm, w), lambda bi, i: (i, 0))
    if first:
        lat = pl.BlockSpec((nb, tm, d), lambda bi, i: (bi, jnp.minimum(i, n_lat_tiles - 1), 0))
        ctx_spec = pl.BlockSpec((nb, min(tm, ctx.shape[1]), d),
                                lambda bi, i: (bi, jnp.maximum(i - n_lat_tiles, 0), 0))
        in_specs, args = [lat, ctx_spec], [x, ctx]
    else:
        in_specs, args = [tile(d)], [x]
    in_specs += [modspec, _const_spec((1, d)), _const_spec((d, N_PERM)), _const_spec((1, 256)),
                 _const_spec((1, 128)), _const_spec((256, 512)), _const_spec((128, 768)),
                 tab(256), tab(256), tab(512), tab(512), tab(256), tab(256)]
    args += [mod, n1, w_perm, qn, kvn, wuq, wukv, *tabs]
    widths = [(256, BF16), (256, BF16), (256, BF16),
              (256, BF16), (256, F32), (256, F32), (256, BF16), (256, BF16),
              (512, BF16), (512, BF16), (256, BF16),
              (256, BF16), (256, BF16), (256, BF16)]
    if first:
        widths = [(d, F32)] + widths
    out_shape = [jax.ShapeDtypeStruct((b, t, w), dt) for w, dt in widths]
    out_specs = [tile(w) for w, _ in widths]
    res = pl.pallas_call(
        functools.partial(_inproj_kernel, first=first, n_lat_tiles=n_lat_tiles),
        out_shape=out_shape, grid=(b // nb, nt), in_specs=in_specs, out_specs=out_specs,
        compiler_params=_cparams(("parallel", "parallel")),
        name="inproj",
    )(*args)
    return (res[0], res[1:]) if first else (x, res)


def _win_kernel(sink_ref, q_ref, k_ref, v_ref, o_ref, *, n_lat, t_all):
    w = A_WINDOW
    qb = WIN_Q
    kb = qb + 2 * w
    n = pl.program_id(1)
    start = pl.multiple_of(jnp.clip(n * qb - w, 0, n_lat - kb), w)
    q = q_ref[0]
    lane = lax.broadcasted_iota(jnp.int32, (qb, LANES), 1)
    row = lax.broadcasted_iota(jnp.int32, (2 * qb, kb), 0)
    col = lax.broadcasted_iota(jnp.int32, (2 * qb, kb), 1)
    rel = (start + col) - (n * qb + row % qb)
    valid = (jnp.abs(rel) <= w) & (n < n_lat // qb)
    row1 = lax.broadcasted_iota(jnp.int32, (2 * qb, 1), 0)
    for j in range(A_KV_HEADS):
        sl = slice(LANES * j, LANES * (j + 1))
        qp = q[:, sl]
        zero = jnp.zeros_like(qp)
        lhs = jnp.concatenate([jnp.where(lane < 64, qp, zero), jnp.where(lane >= 64, qp, zero)], axis=0)
        sb = _dot_nt(lhs, k_ref[0, pl.ds(start, kb), sl])
        sc = _dot_nt(lhs, k_ref[0, n_lat:t_all, sl])
        sb = jnp.where(valid, sb, MASK_VALUE)
        sink = jnp.where(row1 < qb, sink_ref[2 * j], sink_ref[2 * j + 1]) * LOG2E
        m = jnp.maximum(jnp.maximum(jnp.max(sb, axis=1, keepdims=True), jnp.max(sc, axis=1, keepdims=True)), sink)
        pb = jnp.exp2(sb - m)
        pc = jnp.exp2(sc - m)
        den = jnp.sum(pb, axis=1, keepdims=True) + jnp.sum(pc, axis=1, keepdims=True) + jnp.exp2(sink - m)
        o = (_dot(pb.astype(BF16), v_ref[0, pl.ds(start, kb), sl])
             + _dot(pc.astype(BF16), v_ref[0, n_lat:t_all, sl])) / den
        o_ref[0, :, sl] = jnp.where(lane < 64, o[:qb], o[qb:]).astype(o_ref.dtype)


def _win_attention(sink, q, k, v, n_lat):
    b, t, _ = q.shape
    qb = WIN_Q
    return pl.pallas_call(
        functools.partial(_win_kernel, n_lat=n_lat, t_all=t),
        out_shape=jax.ShapeDtypeStruct((b, t, 256), BF16),
        grid=(b, t // qb),
        in_specs=[pl.BlockSpec(memory_space=pltpu.SMEM),
                  pl.BlockSpec((1, qb, 256), lambda bi, i: (bi, i, 0)),
                  pl.BlockSpec((1, t, 256), lambda bi, i: (bi, 0, 0)),
                  pl.BlockSpec((1, t, 256), lambda bi, i: (bi, 0, 0))],
        out_specs=pl.BlockSpec((1, qb, 256), lambda bi, i: (bi, i, 0)),
        compiler_params=_cparams(("parallel", "arbitrary")),
        name="win_attention",
    )(sink, q, k, v)


def _hgrn_level_matrix(rev):
    c = HG_CHUNK
    m = np.zeros((8, c, c), np.float32)
    for lvl in range(HG_LEVELS):
        size = c >> lvl
        for t in range(c):
            mid = (t // size) * size + size // 2
            upper = t >= mid
            if not rev:
                rng = range(mid, t + 1) if upper else range(t + 1, mid)
            else:
                rng = range(mid, t) if upper else range(t, mid)
            m[lvl, t, list(rng)] = 1.0
    for t in range(c):
        if not rev:
            m[6, t, :t + 1] = 1.0
            m[7, t, t + 1:] = 1.0
        else:
            m[6, t, t:] = 1.0
            m[7, t, :t] = 1.0
    m = m.reshape(8 * c, c)
    return np.concatenate([m, m], axis=1)


def _hgrn_chunk(q, k, logf, v, mlev, rev):
    c = HG_CHUNK
    hi = logf.astype(BF16)
    lo = (logf - hi.astype(F32)).astype(BF16)
    wgt = jnp.exp(_dot(mlev, jnp.concatenate([hi, lo], axis=0)))
    t = lax.broadcasted_iota(jnp.int32, (c, 256), 0)
    s_col = lax.broadcasted_iota(jnp.int32, (c, 256), 1) % c
    r_bd = lax.broadcasted_iota(jnp.int32, (256, 256), 0) // 64
    c_bd = lax.broadcasted_iota(jnp.int32, (256, 256), 1) // 64
    bd = r_bd == c_bd

    def block_diag(a):
        return jnp.where(bd, jnp.concatenate([a] * 4, axis=0), 0.0).astype(BF16)

    scores = jnp.where(t == s_col, _dot_nt(q.astype(BF16), block_diag(k)), 0.0)
    for lvl in range(HG_LEVELS):
        wl = wgt[c * lvl:c * (lvl + 1)]
        q_side = ((t >> (HG_LEVELS - 1 - lvl)) & 1) == (0 if rev else 1)
        ql = jnp.where(q_side, q * wl, 0.0).astype(BF16)
        kl = jnp.where(q_side, 0.0, k * wl)
        s = _dot_nt(ql, block_diag(kl))
        if lvl > 0:
            sh = HG_LEVELS - lvl
            s = jnp.where((t >> sh) == (s_col >> sh), s, 0.0)
        scores = scores + s
    q_in = (q * wgt[6 * c:7 * c]).astype(BF16)
    k_end = (k * wgt[7 * c:8 * c]).astype(BF16)
    o_intra = _dot(scores.astype(BF16), block_diag(v))
    edge = 6 * c if rev else 7 * c - 1
    total = wgt[edge:edge + 1]
    update = jnp.where(bd, _dot_tn(v.astype(BF16), k_end), 0.0)
    return o_intra, q_in, update, total


def _hgrn_kernel(mf_ref, mb_ref, lb_ref, qf_ref, zf_ref, vf_ref, qb_ref, zb_ref, vb_ref,
                 of_ref, ob_ref, stf_ref, stb_ref):
    c = HG_CHUNK
    nc = TM // c

    @pl.when(pl.program_id(1) == 0)
    def _():
        stf_ref[...] = jnp.zeros_like(stf_ref)
        stb_ref[...] = jnp.zeros_like(stb_ref)

    def scan(s, q_ref, z_ref, v_ref, lb, mlev, o_ref, st_ref, rev):
        sig = _sigmoid(z_ref[s])
        logf = jnp.log(jnp.maximum(lb + (1.0 - lb) * sig, GATE_FLOOR))
        k = (1.0 - lb) * (1.0 - sig)
        order = range(nc - 1, -1, -1) if rev else range(nc)
        rows = [slice(ci * c, (ci + 1) * c) for ci in order]
        parts = [_hgrn_chunk(q_ref[s, r, :].astype(F32), k[r], logf[r], v_ref[s, r, :].astype(F32), mlev, rev)
                 for r in rows]
        st = st_ref[s]
        for r, (o_intra, q_in, update, total) in zip(rows, parts):
            o_ref[s, r, :] = o_intra + _dot_nt(q_in, st.astype(BF16))
            st = st * total + update
        st_ref[s] = st

    for s in range(qf_ref.shape[0]):
        scan(s, qf_ref, zf_ref, vf_ref, lb_ref[0:1], mf_ref[...], of_ref, stf_ref, False)
        scan(s, qb_ref, zb_ref, vb_ref, lb_ref[1:2], mb_ref[...], ob_ref, stb_ref, True)


def _hgrn(lb, q, zff, zfb, v, n_lat):
    b, t, _ = q.shape
    nlb = n_lat // TM
    ncb = (t - n_lat) // TM
    fwd = lambda bi, i: (bi, jnp.where(i < ncb, nlb + i, i - ncb), 0)
    bwd = lambda bi, i: (bi, nlb + ncb - 1 - i, 0)
    nb = HG_BATCH if b % HG_BATCH == 0 else 1
    blk = lambda im: pl.BlockSpec((nb, TM, 256), im)
    mf = jnp.asarray(_hgrn_level_matrix(False), BF16)
    mb = jnp.asarray(_hgrn_level_matrix(True), BF16)
    return pl.pallas_call(
        _hgrn_kernel,
        out_shape=[jax.ShapeDtypeStruct((b, t, 256), F32)] * 2,
        grid=(b // nb, t // TM),
        in_specs=[_const_spec((8 * HG_CHUNK, 2 * HG_CHUNK)), _const_spec((8 * HG_CHUNK, 2 * HG_CHUNK)),
                  _const_spec((2, 256)),
                  blk(fwd), blk(fwd), blk(fwd), blk(bwd), blk(bwd), blk(bwd)],
        out_specs=[blk(fwd), blk(bwd)],
        scratch_shapes=[pltpu.VMEM((nb, 256, 256), F32), pltpu.VMEM((nb, 256, 256), F32)],
        compiler_params=_cparams(("parallel", "arbitrary")),
        name="hgrn",
    )(mf, mb, lb, q, zff, v, q, zfb, v)


def _key_chunks(n_lat, t_all):
    chunks = [(lo, min(lo + KV_CHUNK, t_all)) for lo in range(0, t_all, KV_CHUNK)]
    return chunks, [(n_lat, t_all)]


def _online_softmax(lhs, k_ref, v_ref, chunks):
    rows = lhs.shape[0]
    m = jnp.full((rows, 1), -jnp.inf, F32)
    l = jnp.zeros((rows, 1), F32)
    acc = jnp.zeros((rows, LANES), F32)
    for lo, hi in chunks:
        s = _dot_nt(lhs, k_ref[0, lo:hi, :])
        m_new = jnp.maximum(m, jnp.max(s, axis=1, keepdims=True))
        alpha = jnp.exp2(m - m_new)
        p = jnp.exp2(s - m_new)
        l = alpha * l + jnp.sum(p, axis=1, keepdims=True)
        acc = alpha * acc + _dot(p.astype(BF16), v_ref[0, lo:hi, :])
        m = m_new
    return acc / l


def _mla_kernel(q_ref, k_ref, v_ref, o_ref, *, chunks):
    tq = q_ref.shape[1]
    q = q_ref[0]
    zero = jnp.zeros_like(q)
    lane2 = lax.broadcasted_iota(jnp.int32, q.shape, 1)
    lhs = jnp.concatenate([jnp.where(lane2 < LANES, q, zero), jnp.where(lane2 >= LANES, q, zero)], axis=0)
    o = _online_softmax(lhs, k_ref, v_ref, chunks)
    lane = lax.broadcasted_iota(jnp.int32, (tq, LANES), 1)
    o_ref[0] = jnp.where(lane < 64, o[:tq], o[tq:]).astype(o_ref.dtype)


def _diff_kernel(lam_ref, g_ref, q_ref, k_ref, v_ref, o_ref, *, chunks, lam_init):
    tq = q_ref.shape[1]
    q = q_ref[0]
    zero = jnp.zeros_like(q)
    lane = lax.broadcasted_iota(jnp.int32, (tq, LANES), 1)
    lhs = jnp.concatenate([jnp.where(lane // D_HEAD_DIM == r, q, zero) for r in range(4)], axis=0)
    o = _online_softmax(lhs, k_ref, v_ref, chunks)
    lam = lam_ref[0]
    o0 = o[0:tq] - lam * o[tq:2 * tq]
    o1 = o[2 * tq:3 * tq] - lam * o[3 * tq:4 * tq]
    o = jnp.where(lane < 64, o0, o1)
    o_ref[0] = (_head_rms(o, g_ref[...], LANES) * (1.0 - lam_init)).astype(o_ref.dtype)


def _attention_call(body, name, extras, extra_specs, q, k, v, tq, row0, n_rows, out_init):
    b, t, qw = q.shape
    blk0 = row0 // tq
    n_extra = len(extras)
    in_specs = list(extra_specs) + [
        pl.BlockSpec((1, tq, qw // 2), lambda bi, p, i: (bi, blk0 + i, p)),
        pl.BlockSpec((1, t, k.shape[2] // 2), lambda bi, p, i: (bi, 0, p)),
        pl.BlockSpec((1, t, LANES), lambda bi, p, i: (bi, 0, p))]
    args = list(extras) + [q, k, v]
    aliases = {}
    if out_init is not None:
        in_specs.append(pl.BlockSpec(memory_space=pl.ANY))
        aliases = {len(args): 0}
        args.append(out_init)

    def kern(*refs):
        body(*refs[:n_extra + 3], refs[-1])

    return pl.pallas_call(
        kern,
        out_shape=jax.ShapeDtypeStruct((b, t, 256), BF16),
        grid=(b, 2, n_rows // tq),
        in_specs=in_specs,
        out_specs=pl.BlockSpec((1, tq, LANES), lambda bi, p, i: (bi, blk0 + i, p)),
        input_output_aliases=aliases,
        compiler_params=_cparams(("parallel", "parallel", "arbitrary")),
        name=name,
    )(*args)


def _two_pass_attention(body_of, name, extras, extra_specs, q, k, v, n_lat, tq_lat, with_ctx):
    t = q.shape[1]
    lat_chunks, ctx_chunks = _key_chunks(n_lat, t)
    out = jnp.zeros((q.shape[0], t, 256), BF16)
    out = _attention_call(body_of(lat_chunks), name, extras, extra_specs, q, k, v, tq_lat, 0, n_lat, out)
    if not with_ctx:
        return out
    return _attention_call(body_of(ctx_chunks), name + "_ctx", extras, extra_specs, q, k, v,
                           TQ, n_lat, t - n_lat, out)


def _mla_attention(q, k, v, n_lat, with_ctx):
    body_of = lambda chunks: functools.partial(_mla_kernel, chunks=chunks)
    return _two_pass_attention(body_of, "mla_attention", [], [], q, k, v, n_lat, TQ_MLA, with_ctx)


def _diff_attention(lam, q, k, v, gain, n_lat, lam_init, with_ctx):
    body_of = lambda chunks: functools.partial(_diff_kernel, chunks=chunks, lam_init=lam_init)
    specs = [pl.BlockSpec(memory_space=pltpu.SMEM), _const_spec((1, LANES))]
    return _two_pass_attention(body_of, "diff_attention", [lam, gain], specs, q, k, v, n_lat, TQ, with_ctx)


def _merge_kernel(x_ref, mod_ref, n1_ref, n2_ref, a_ref, of_ref, ob_ref, bg_ref, hg_ref, c_ref, d_ref,
                  wg_ref, wb_ref, wo_ref, wr_ref, br_ref, xo_ref, h2_ref, ri_ref, rw_ref):
    d = D_MODEL
    nb, tm, _ = x_ref.shape
    rows = nb * tm
    flat = lambda a: a.reshape(rows, a.shape[-1])
    x = x_ref[...]
    m = mod_ref[:, 0]
    mrow = lambda k: m[:, k:k + 1, :]
    h = flat((_rms(x, n1_ref[...]) * (1.0 + mrow(1)) + mrow(0)).astype(BF16))
    g = flat(bg_ref[...]).astype(F32)
    b_out = _head_rms(flat(of_ref[...] + ob_ref[...]), hg_ref[...], 256) * (g * _sigmoid(g))
    branches = (flat(a_ref[...]), b_out.astype(BF16), flat(c_ref[...]), flat(d_ref[...]))
    y = jnp.zeros((rows, d), F32)
    for i, br in enumerate(branches):
        gate = _sigmoid(_dot(h, wg_ref[:, d * i:d * (i + 1)]))
        y = y + gate * _dot(br, wb_ref[i])
    x = x + mrow(2) * _dot(y.astype(BF16), wo_ref[...]).reshape(nb, tm, d)
    xo_ref[...] = x
    h2 = flat(_rms(x, n2_ref[...]) * (1.0 + mrow(4)) + mrow(3))
    h2b = h2.astype(BF16)
    h2_ref[...] = _rows_first(h2).reshape(nb, tm, ROW_SLABS, LANES)

    h2lo = (h2 - h2b.astype(F32)).astype(BF16)
    r = _dot(jnp.concatenate([h2b, h2lo], axis=0), wr_ref[...])
    logits = r[:rows, :LANES] + r[:rows, LANES:] + r[rows:, :LANES] + br_ref[...]
    lane = lax.broadcasted_iota(jnp.int32, (rows, LANES), 1)
    neg = -jnp.inf
    is_g = (lane >= N_EXPERTS) & (lane < N_EXPERTS + N_GROUPS)
    lg = jnp.where(is_g, logits, neg)
    mg = jnp.max(lg, axis=1, keepdims=True)
    g_val = 1.0 / jnp.sum(jnp.exp(lg - mg), axis=1, keepdims=True)
    g_idx = jnp.min(jnp.where(lg == mg, lane, 4 * LANES), axis=1, keepdims=True) - N_EXPERTS
    in_group = (lane >= g_idx * EXPERTS_PER_GROUP) & (lane < (g_idx + 1) * EXPERTS_PER_GROUP)
    le = jnp.where(in_group, logits, neg)
    m1 = jnp.max(le, axis=1, keepdims=True)
    e1 = jnp.min(jnp.where(le == m1, lane, 4 * LANES), axis=1, keepdims=True)
    le2 = jnp.where(lane == e1, neg, le)
    m2 = jnp.max(le2, axis=1, keepdims=True)
    e2 = jnp.min(jnp.where(le2 == m2, lane, 4 * LANES), axis=1, keepdims=True)
    r2 = jnp.exp(m2 - m1)
    v1 = g_val / (1.0 + r2)
    ri_ref[...] = jnp.where(lane == 0, e1, jnp.where(lane == 1, e2, 0)).reshape(nb, tm, LANES)
    rw_ref[...] = jnp.where(lane == 0, v1, jnp.where(lane == 1, v1 * r2, 0.0)).reshape(nb, tm, LANES)


def _merge(x, mod, n1, n2, a_o, o_f, o_b, bg, hg, c_o, d_o, wg, wb, wo, wr, br, n_lat_tiles, t):
    b, _, d = x.shape
    tm = TM_PROJ
    nb = MERGE_BATCH if b % MERGE_BATCH == 0 else 1
    tile = lambda w: pl.BlockSpec((nb, tm, w), lambda bi, i: (bi, i, 0))
    return pl.pallas_call(
        _merge_kernel,
        out_shape=[jax.ShapeDtypeStruct((b, t, d), F32), jax.ShapeDtypeStruct((b, t, ROW_SLABS, LANES), F32),
                   jax.ShapeDtypeStruct((b, t, LANES), jnp.int32), jax.ShapeDtypeStruct((b, t, LANES), F32)],
        grid=(b // nb, pl.cdiv(t, tm)),
        in_specs=[tile(d), pl.BlockSpec((nb, 1, 8, d), lambda bi, i: (bi, i // n_lat_tiles, 0, 0)),
                  _const_spec((1, d)), _const_spec((1, d)),
                  tile(256), tile(256), tile(256), tile(256), _const_spec((1, 256)), tile(256), tile(256),
                  _const_spec((d, 4 * d)), _const_spec((4, 256, d)), _const_spec((d, d)),
                  _const_spec((d, 256)), _const_spec((1, LANES))],
        out_specs=[tile(d), pl.BlockSpec((nb, tm, ROW_SLABS, LANES), lambda bi, i: (bi, i, 0, 0)),
                   tile(LANES), tile(LANES)],
        compiler_params=_cparams(("parallel", "parallel")),
        name="merge",
    )(x, mod, n1, n2, a_o, o_f, o_b, bg, hg, c_o, d_o, wg, wb, wo, wr, br)


def _route_positions(ri, n_tok):
    ef = ri[:, :2].reshape(-1)
    rb = 2 * TM
    oh = (ef[:, None] == jnp.arange(N_EXPERTS, dtype=jnp.int32)[None, :]).astype(F32).reshape(-1, rb, N_EXPERTS)
    tri = (jnp.arange(rb)[:, None] >= jnp.arange(rb)[None, :]).astype(F32)
    within = jnp.einsum("ij,gje->gie", tri, oh)
    tot = within[:, -1, :]
    before = jnp.cumsum(tot, axis=0) - tot
    cnt = jnp.sum(tot, axis=0).astype(jnp.int32)
    pcnt = ((cnt + MOE_TILE - 1) // MOE_TILE) * MOE_TILE
    end = jnp.cumsum(pcnt)
    start = (end - pcnt).astype(F32)
    pos = jnp.sum((within + (before + start[None, :])[:, None, :]) * oh, axis=2) - 1.0
    pos = pos.reshape(-1)
    n_tiles = 2 * n_tok // MOE_TILE + N_EXPERTS
    n_act = (end[-1] // MOE_TILE).astype(jnp.int32)
    tile_row = jnp.minimum(jnp.arange(n_tiles, dtype=jnp.int32), n_act - 1) * MOE_TILE
    tile_expert = jnp.sum((end[None, :] <= tile_row[:, None]).astype(jnp.int32), axis=1)
    tile_expert = jnp.minimum(tile_expert, N_EXPERTS - 1)
    return pos.reshape(n_tok // TM, 1, 2 * TM).astype(jnp.int32), tile_expert, n_act.reshape(1)


def _dispatch_kernel(pos_ref, h_ref, xs_in_ref, xs_ref, sem):
    del xs_in_ref

    def body(r, carry):
        for s in range(2):
            p = pos_ref[0, 0, 2 * r + s]
            pltpu.make_async_copy(h_ref.at[pl.ds(r, 1)], xs_ref.at[pl.ds(p, 1)], sem.at[s]).start(priority=s)
        return carry

    lax.fori_loop(0, TM, body, 0)
    for s in range(2):
        pltpu.make_async_copy(h_ref, h_ref, sem.at[s]).wait()


def _dispatch(pos, h2, xs_init):
    n = h2.shape[0]
    return pl.pallas_call(
        _dispatch_kernel,
        out_shape=jax.ShapeDtypeStruct(xs_init.shape, F32),
        grid=(n // TM,),
        in_specs=[pl.BlockSpec((1, 1, 2 * TM), lambda t: (t, 0, 0), memory_space=pltpu.SMEM),
                  pl.BlockSpec((TM, ROW_SLABS, LANES), lambda t: (t, 0, 0)),
                  pl.BlockSpec(memory_space=pl.ANY)],
        out_specs=pl.BlockSpec(memory_space=pl.ANY),
        scratch_shapes=[pltpu.SemaphoreType.DMA((2,))],
        input_output_aliases={2: 0},
        compiler_params=_cparams(("arbitrary",)),
        name="moe_dispatch",
    )(pos, h2, xs_init)


def _experts_kernel(te_ref, na_ref, x_ref, wg_ref, wu_ref, wd_ref, o_ref):
    del te_ref
    t = pl.program_id(0)

    @pl.when(t < na_ref[0])
    def _():
        x = _slabs_first(x_ref[...])
        xb = [x[s].astype(BF16) for s in range(ROW_SLABS)]
        gt = sum(_dot(xb[s], wg_ref[0, LANES * s:LANES * (s + 1), :]) for s in range(ROW_SLABS))
        up = sum(_dot(xb[s], wu_ref[0, LANES * s:LANES * (s + 1), :]) for s in range(ROW_SLABS))
        o_ref[...] = _rows_first(_dot((gt * _sigmoid(gt) * up).astype(BF16), wd_ref[0]))

    @pl.when(t >= na_ref[0])
    def _():
        o_ref[...] = jnp.zeros_like(o_ref)


def _experts(tile_expert, n_act, xs, wg, wu, wd):
    n_tiles = tile_expert.shape[0]
    d = D_MODEL
    rows = pl.BlockSpec((MOE_TILE, ROW_SLABS, LANES), lambda t, te, na: (t, 0, 0))
    w_in = pl.BlockSpec((1, d, EXPERT_HIDDEN), lambda t, te, na: (te[t], 0, 0))
    return pl.pallas_call(
        _experts_kernel,
        out_shape=jax.ShapeDtypeStruct((n_tiles * MOE_TILE, ROW_SLABS, LANES), F32),
        grid_spec=pltpu.PrefetchScalarGridSpec(
            num_scalar_prefetch=2, grid=(n_tiles,),
            in_specs=[rows, w_in, w_in,
                      pl.BlockSpec((1, EXPERT_HIDDEN, d), lambda t, te, na: (te[t], 0, 0))],
            out_specs=rows),
        compiler_params=_cparams(("arbitrary",)),
        name="moe_experts",
    )(tile_expert, n_act, xs, wg, wu, wd)


def _combine_kernel(pos_ref, ys_ref, w_ref, x_ref, mod_ref, g_ref, o_ref, buf, sem, *, final):
    def body(r, carry):
        for s in range(2):
            p = pos_ref[0, 0, 2 * r + s]
            pltpu.make_async_copy(ys_ref.at[pl.ds(p, 1)], buf.at[s, pl.ds(r, 1)], sem.at[s]).start(priority=s)
        return carry

    lax.fori_loop(0, TM, body, 0)
    for s in range(2):
        pltpu.make_async_copy(buf.at[s], buf.at[s], sem.at[s]).wait()
    w = w_ref[0]
    y0, y1 = _slabs_first(buf[0]), _slabs_first(buf[1])
    f = jnp.concatenate([w[:, 0:1] * y0[k] + w[:, 1:2] * y1[k] for k in range(ROW_SLABS)], axis=1)
    x = x_ref[0] + mod_ref[0, 0, 5:6, :] * f
    o_ref[0] = _rms(x, g_ref[...]) if final else x


def _combine(pos, ys, rwt, x, mod, gain, n_lat):
    b, t, d = x.shape
    nt = t // TM
    tile = lambda w: pl.BlockSpec((1, TM, w), lambda bi, i: (bi, i, 0))
    final = gain is not None
    return pl.pallas_call(
        functools.partial(_combine_kernel, final=final),
        out_shape=jax.ShapeDtypeStruct((b, t, d), F32),
        grid=(b, nt),
        in_specs=[pl.BlockSpec((1, 1, 2 * TM), lambda bi, i: (bi * nt + i, 0, 0), memory_space=pltpu.SMEM),
                  pl.BlockSpec(memory_space=pl.ANY),
                  tile(LANES), tile(d),
                  pl.BlockSpec((1, 1, 8, d), lambda bi, i: (bi, i // (n_lat // TM), 0, 0)),
                  _const_spec((1, d))],
        out_specs=tile(d),
        scratch_shapes=[pltpu.VMEM((2, TM, ROW_SLABS, LANES), F32), pltpu.SemaphoreType.DMA((2,))],
        compiler_params=_cparams(("arbitrary", "arbitrary")),
        name="moe_combine",
    )(pos, ys, rwt, x, mod, gain if final else jnp.ones((1, d), F32))


def _moe(x, h2, ri, rwt, wg, wu, wd, xs_init, mod, gain, n_lat):
    b, t, _ = x.shape
    n = b * t
    pos, tile_expert, n_act = _route_positions(ri.reshape(n, LANES), n)
    xs = _dispatch(pos, h2.reshape(n, ROW_SLABS, LANES), xs_init)
    ys = _experts(tile_expert, n_act, xs, wg, wu, wd)
    return _combine(pos, ys, rwt, x, mod, gain, n_lat), xs


def _rope_tables(n_lat, t_all, lane_rope, lane_off, dr):
    half, quarter = dr // 2, dr // 4
    inv_freq = 1.0 / (ROPE_BASE ** (jnp.arange(quarter, dtype=F32) / quarter))
    off = np.asarray(lane_off)
    use_col = off >= half
    j = (off % half) % quarter
    first = (off % half) < quarter
    tok = jnp.arange(n_lat, dtype=jnp.int32)
    row = (tok // GRID_W).astype(F32)
    col = (tok % GRID_W).astype(F32)
    pos = jnp.where(jnp.asarray(use_col)[None, :], col[:, None], row[:, None])
    ang = pos * inv_freq[jnp.asarray(j)][None, :]
    rope = jnp.asarray(lane_rope)[None, :]
    cos = jnp.where(rope, jnp.cos(ang), 1.0)
    sin = jnp.where(rope, jnp.sin(ang) * jnp.where(jnp.asarray(first), -1.0, 1.0)[None, :], 0.0)
    pad = t_all - n_lat
    w = off.shape[0]
    cos = jnp.concatenate([cos, jnp.ones((pad, w), F32)], axis=0)
    sin = jnp.concatenate([sin, jnp.zeros((pad, w), F32)], axis=0)
    return cos, sin


def _all_rope_tables(n_lat, t_all):
    rep = lambda tabs, n: tuple(jnp.tile(tb, (1, n)) for tb in tabs)
    ta = rep(_rope_tables(n_lat, t_all, np.ones(64, bool), np.arange(64), 64), 4)
    lc = np.arange(128)
    tc = rep(_rope_tables(n_lat, t_all, (lc >= 64) & (lc < 96), np.clip(lc - 64, 0, 31), 32), 4)
    td = rep(_rope_tables(n_lat, t_all, np.ones(32, bool), np.arange(32), 32), 8)
    return (*ta, *tc, *td)


def _permute_w_in(w):
    s = np.cumsum([0, 256, 128, 128, 256, 256, 256, 256, 256, 256, 128, 32, 256, 256, 256])
    w = w.astype(BF16)
    seg = lambda i: w[:, s[i]:s[i + 1]]
    z64 = jnp.zeros((w.shape[0], 64), w.dtype)
    z32 = jnp.zeros((w.shape[0], 32), w.dtype)
    kr = jnp.concatenate([z64, seg(10), z32], axis=1)
    cols = [seg(0), seg(1), seg(2),
            seg(3), seg(4), seg(5), seg(6), seg(7),
            seg(8), seg(9), kr,
            seg(11), seg(12), seg(13)]
    return jnp.concatenate(cols, axis=1)


def _permute_mla(w_uq, w_ukv):
    z32 = jnp.zeros((w_uq.shape[0], 32), w_uq.dtype)
    qd = C_NOPE + C_ROPE
    uq = jnp.concatenate([a for h in range(C_HEADS) for a in (w_uq[:, qd * h:qd * (h + 1)], z32)], axis=1)
    z64 = jnp.zeros((w_ukv.shape[0], 64), w_ukv.dtype)
    kd = C_NOPE + C_V
    uk = jnp.concatenate([a for h in range(C_HEADS) for a in (w_ukv[:, kd * h:kd * h + C_NOPE], z64)], axis=1)
    uv = jnp.concatenate([w_ukv[:, kd * h + C_NOPE:kd * (h + 1)] for h in range(C_HEADS)], axis=1)
    return uq.astype(BF16), jnp.concatenate([uk, uv], axis=1).astype(BF16)


def kernel(x, c, ctx, c_ctx, w_mod, b_mod, norm1, norm2, w_in, w_gate, w_branch, w_out, attn_sink, hgrn_lb_logits, hgrn_norm, mla_q_norm, mla_kv_norm, mla_w_uq, mla_w_ukv, diff_lambda, diff_subln, w_router_group, b_router_group, w_router_expert, b_router_expert, w_expert_gate, w_expert_up, w_expert_down, final_norm):
    b, n_lat, d = x.shape
    n_ctx = ctx.shape[1]
    depth = w_mod.shape[0]
    t_all = n_lat + n_ctx
    assert d == D_MODEL and n_lat % TM == 0 and n_ctx % TM == 0 and n_lat % GRID_W == 0 and n_ctx <= n_lat
    assert n_lat >= WIN_Q + 2 * A_WINDOW and n_lat % TQ_MLA == 0 and n_lat % TM_PROJ == 0 and b <= 15
    n_lat_tiles = n_lat // TM_PROJ

    c_all = jnp.concatenate([c, c_ctx[None], jnp.zeros((15 - b, d), F32)], axis=0)
    mod = _modulation(c_all, w_mod, b_mod).reshape(depth, 16, 6, d)
    mod = jnp.pad(mod, ((0, 0), (0, 0), (0, 2), (0, 0)))
    mod = jnp.stack([mod[:, :b], jnp.broadcast_to(mod[:, b:b + 1], (depth, b, 8, d))], axis=2)

    sm = jax.nn.softmax(hgrn_lb_logits.astype(F32), axis=0)
    lower_bounds = jnp.cumsum(sm, axis=0) - sm[0]
    tabs = _all_rope_tables(n_lat, t_all)

    n_tok = b * t_all
    moe_buf = jnp.zeros((2 * n_tok + N_EXPERTS * MOE_TILE, ROW_SLABS, LANES), F32)
    xs = x
    for li in range(depth):
        lam_init = 0.8 - 0.6 * math.exp(-0.3 * li)
        lp = diff_lambda[li].astype(F32)
        lam = (jnp.exp(jnp.sum(lp[0] * lp[1])) - jnp.exp(jnp.sum(lp[2] * lp[3])) + lam_init).reshape(1)
        w_perm = _permute_w_in(w_in[li])
        wuq, wukv = _permute_mla(mla_w_uq[li], mla_w_ukv[li])
        xs, z = _inproj(xs, ctx if li == 0 else None, mod[li], norm1[li][None], w_perm,
                        mla_q_norm[li][None], mla_kv_norm[li][None], wuq, wukv, tabs, n_lat_tiles)
        qa, ka, va, bq, bff, bfb, bv, bg, cq, ck, cv, dq, dk, dv = z
        a_o = _win_attention(attn_sink[li].astype(F32), qa, ka, va, n_lat)
        o_f, o_b = _hgrn(lower_bounds[li], bq, bff, bfb, bv, n_lat)
        with_ctx = li < depth - 1
        c_o = _mla_attention(cq, ck, cv, n_lat, with_ctx)
        d_o = _diff_attention(lam, dq, dk, dv, jnp.tile(diff_subln[li], 2)[None], n_lat, lam_init, with_ctx)
        wr = jnp.concatenate([w_router_expert[li], w_router_group[li],
                              jnp.zeros((d, LANES - N_EXPERTS - N_GROUPS), F32)], axis=1)
        wr_hi = wr.astype(BF16)
        wr_lo = (wr - wr_hi.astype(F32)).astype(BF16)
        br = jnp.concatenate([b_router_expert[li], b_router_group[li],
                              jnp.zeros((LANES - N_EXPERTS - N_GROUPS,), F32)])[None]
        t_keep = t_all if with_ctx else n_lat
        xs, h2, ri, rwt = _merge(xs, mod[li], norm1[li][None], norm2[li][None], a_o, o_f, o_b, bg,
                                 jnp.tile(hgrn_norm[li], B_HEADS)[None], c_o, d_o,
                                 w_gate[li].astype(BF16), w_branch[li].astype(BF16), w_out[li].astype(BF16),
                                 jnp.concatenate([wr_hi, wr_lo], axis=1), br, n_lat_tiles, t_keep)
        xs, moe_buf = _moe(xs, h2, ri, rwt, w_expert_gate[li].astype(BF16), w_expert_up[li].astype(BF16),
                           w_expert_down[li].astype(BF16), moe_buf, mod[li],
                           None if with_ctx else final_norm[None], n_lat)
    return xs
```

```python
import functools
import math

import numpy as np
import jax
import jax.numpy as jnp
from jax import lax
from jax.experimental import pallas as pl
from jax.experimental.pallas import tpu as pltpu

D_MODEL = 1024
GRID_W = 64
ROPE_BASE = 10000.0
NORM_EPS = 1e-6
MASK_VALUE = -1e30
GATE_FLOOR = 1e-30

A_HEADS, A_KV_HEADS, A_HEAD_DIM, A_WINDOW = 4, 2, 64, 128
B_HEADS, B_KEY_DIM, B_VAL_DIM = 4, 64, 64
C_HEADS, C_Q_LORA, C_KV_LORA, C_NOPE, C_ROPE, C_V = 4, 256, 128, 64, 32, 64
D_HEADS, D_HEAD_DIM = 4, 32
N_GROUPS, EXPERTS_PER_GROUP, EXPERT_HIDDEN = 4, 8, 256
N_EXPERTS = N_GROUPS * EXPERTS_PER_GROUP

TM = 256
TM_PROJ = 256
WIN_Q = 256
TQ = 256
TQ_MLA = 512
LOG2E = 1.4426950408889634
KV_CHUNK = 4096
MOE_TILE = 512
HG_BATCH = 2
MERGE_BATCH = 2
HG_CHUNK = 64
HG_LEVELS = 6
LANES = 128
ROW_SLABS = D_MODEL // LANES
VMEM_LIMIT = 52 * 1024 * 1024

OFF_A, OFF_B, OFF_C, OFF_D, N_PERM = 0, 512, 1792, 2304, 3072

F32 = jnp.float32
BF16 = jnp.bfloat16


def _dot(a, b):
    return jnp.dot(a, b, preferred_element_type=F32)


def _dot_nt(a, b):
    return lax.dot_general(a, b, (((1,), (1,)), ((), ())), preferred_element_type=F32)


def _dot_tn(a, b):
    return lax.dot_general(a, b, (((0,), (0,)), ((), ())), preferred_element_type=F32)


def _dot_hi(a, b):
    return jnp.dot(a, b, preferred_element_type=F32, precision=lax.Precision.HIGHEST)


def _sigmoid(x):
    return 1.0 / (1.0 + jnp.exp(-x))


def _rms(x, gain):
    return x * lax.rsqrt(jnp.mean(x * x, axis=-1, keepdims=True) + NORM_EPS) * gain


def _head_rms(o, gain, width):
    rows = o.shape[0]
    r = lax.broadcasted_iota(jnp.int32, (width, width), 0) // 64
    c = lax.broadcasted_iota(jnp.int32, (width, width), 1) // 64
    ones = jnp.where(r == c, 1.0 / 64.0, 0.0).astype(BF16)
    sq = o * o
    hi = sq.astype(BF16)
    lo = (sq - hi.astype(F32)).astype(BF16)
    ms = _dot(jnp.concatenate([hi, lo], axis=0), ones)
    return o * lax.rsqrt(ms[:rows] + ms[rows:] + NORM_EPS) * gain


def _rope(x, cos, sin, quarter):
    w = x.shape[-1]
    lane = lax.broadcasted_iota(jnp.int32, x.shape, 1)
    first = (lane % (2 * quarter)) < quarter
    sw = jnp.where(first, pltpu.roll(x, w - quarter, 1), pltpu.roll(x, quarter, 1))
    return x * cos + sw * sin


def _dup_heads(x):
    lane = lax.broadcasted_iota(jnp.int32, x.shape, 1)
    sw = pltpu.roll(x, 64, 1)
    return jnp.concatenate([jnp.where(lane < 64, x, sw), jnp.where(lane < 64, sw, x)], axis=1)


def _slabs_first(x):
    return pltpu.einshape("tsl->stl", x)


def _rows_first(y):
    slabs = jnp.stack([y[:, LANES * s:LANES * (s + 1)] for s in range(ROW_SLABS)], axis=0)
    return pltpu.einshape("stl->tsl", slabs)


def _cparams(sem):
    return pltpu.CompilerParams(dimension_semantics=sem, vmem_limit_bytes=VMEM_LIMIT)


def _const_spec(shape):
    n = len(shape)
    return pl.BlockSpec(shape, lambda *_: (0,) * n)


def _mod_kernel(c_ref, w_ref, b_ref, o_ref):
    c = c_ref[...]
    o_ref[0] = _dot_hi(c * _sigmoid(c), w_ref[0]) + b_ref[0]


def _modulation(c_all, w_mod, b_mod):
    depth, d, n = w_mod.shape
    nb = 1536
    return pl.pallas_call(
        _mod_kernel,
        out_shape=jax.ShapeDtypeStruct((depth, 16, n), F32),
        grid=(depth, n // nb),
        in_specs=[pl.BlockSpec((16, d), lambda l, j: (0, 0)),
                  pl.BlockSpec((1, d, nb), lambda l, j: (l, 0, j)),
                  pl.BlockSpec((1, 1, nb), lambda l, j: (l, 0, j))],
        out_specs=pl.BlockSpec((1, 16, nb), lambda l, j: (l, 0, j)),
        compiler_params=_cparams(("arbitrary", "arbitrary")),
        name="modulation",
    )(c_all, w_mod, b_mod.reshape(depth, 1, n))


def _inproj_kernel(*refs, first, n_lat_tiles):
    if not first:
        x = refs[0][...]
        refs = refs[1:]
    else:
        x_ref, ctx_ref = refs[:2]
        refs = refs[2:]
        ctx = ctx_ref[...]
        if ctx.shape[1] < x_ref.shape[1]:
            ctx = jnp.concatenate([ctx] * (x_ref.shape[1] // ctx.shape[1]), axis=1)
        x = jnp.where(pl.program_id(1) < n_lat_tiles, x_ref[...], ctx)
    (mod_ref, n1_ref, w_ref, qn_ref, kvn_ref, wuq_ref, wukv_ref,
     cosa_ref, sina_ref, cosc_ref, sinc_ref, cosd_ref, sind_ref) = refs[:13]
    outs = refs[13:]
    nb, tm, d = x.shape

    def put(ref, val):
        ref[...] = val.reshape(ref.shape)

    if first:
        put(outs[0], x)
        outs = outs[1:]
    (qa_ref, ka_ref, va_ref, bq_ref, bff_ref, bfb_ref, bv_ref, bg_ref,
     cq_ref, ck_ref, cv_ref, dq_ref, dk_ref, dv_ref) = outs
    m = mod_ref[:, 0]
    h = (_rms(x, n1_ref[...]) * (1.0 + m[:, 1:2, :]) + m[:, 0:1, :]).astype(BF16).reshape(nb * tm, d)
    tab = lambda ref: jnp.concatenate([ref[...]] * nb, axis=0)

    z = _dot(h, w_ref[:, OFF_A:OFF_A + 512])
    cosa, sina = tab(cosa_ref), tab(sina_ref)
    put(qa_ref, (_rope(z[:, 0:256], cosa, sina, 16) * (A_HEAD_DIM ** -0.5 * LOG2E)).astype(BF16))
    put(ka_ref, _dup_heads(_rope(z[:, 256:384], cosa[:, :LANES], sina[:, :LANES], 16)).astype(BF16))
    put(va_ref, _dup_heads(z[:, 384:512]).astype(BF16))

    z = _dot(h, w_ref[:, OFF_B:OFF_B + 1280])
    put(bq_ref, z[:, 0:256].astype(BF16))
    put(bff_ref, z[:, 256:512])
    put(bfb_ref, z[:, 512:768])
    put(bv_ref, z[:, 768:1024].astype(BF16))
    put(bg_ref, z[:, 1024:1280].astype(BF16))

    z = _dot(h, w_ref[:, OFF_C:OFF_C + 512])
    cosc, sinc = tab(cosc_ref), tab(sinc_ref)
    cq = _rms(z[:, 0:256], qn_ref[...]).astype(BF16)
    q = _rope(_dot(cq, wuq_ref[...]), cosc, sinc, 8)
    put(cq_ref, (q * ((C_NOPE + C_ROPE) ** -0.5 * LOG2E)).astype(BF16))
    ckv = _rms(z[:, 256:384], kvn_ref[...]).astype(BF16)
    kv = _dot(ckv, wukv_ref[...])
    k_rope = _rope(z[:, 384:512], cosc[:, :LANES], sinc[:, :LANES], 8)
    put(ck_ref, (kv[:, 0:512] + jnp.concatenate([k_rope] * C_HEADS, axis=1)).astype(BF16))
    put(cv_ref, kv[:, 512:768].astype(BF16))

    z = _dot(h, w_ref[:, OFF_D:OFF_D + 768])
    cosd, sind = tab(cosd_ref), tab(sind_ref)
    put(dq_ref, (_rope(z[:, 0:256], cosd, sind, 8) * (D_HEAD_DIM ** -0.5 * LOG2E)).astype(BF16))
    put(dk_ref, _rope(z[:, 256:512], cosd, sind, 8).astype(BF16))
    put(dv_ref, z[:, 512:768].astype(BF16))


def _inproj(x, ctx, mod, n1, w_perm, qn, kvn, wuq, wukv, tabs, n_lat_tiles):
    first = ctx is not None
    b, _, d = x.shape
    t = x.shape[1] + (ctx.shape[1] if first else 0)
    tm = TM_PROJ
    nt = pl.cdiv(t, tm)
    nb = MERGE_BATCH if b % MERGE_BATCH == 0 else 1
    tile = lambda w: pl.BlockSpec((nb, tm, w), lambda bi, i: (bi, i, 0))
    modspec = pl.BlockSpec((nb, 1, 8, d), lambda bi, i: (bi, i // n_lat_tiles, 0, 0))
    tab = lambda w: pl.BlockSpec((tm, w), lambda bi, i: (i, 0))
    if first:
        lat = pl.BlockSpec((nb, tm, d), lambda bi, i: (bi, jnp.minimum(i, n_lat_tiles - 1), 0))
        ctx_spec = pl.BlockSpec((nb, min(tm, ctx.shape[1]), d),
                                lambda bi, i: (bi, jnp.maximum(i - n_lat_tiles, 0), 0))
        in_specs, args = [lat, ctx_spec], [x, ctx]
    else:
        in_specs, args = [tile(d)], [x]
    in_specs += [modspec, _const_spec((1, d)), _const_spec((d, N_PERM)), _const_spec((1, 256)),
                 _const_spec((1, 128)), _const_spec((256, 512)), _const_spec((128, 768)),
                 tab(256), tab(256), tab(512), tab(512), tab(256), tab(256)]
    args += [mod, n1, w_perm, qn, kvn, wuq, wukv, *tabs]
    widths = [(256, BF16), (256, BF16), (256, BF16),
              (256, BF16), (256, F32), (256, F32), (256, BF16), (256, BF16),
              (512, BF16), (512, BF16), (256, BF16),
              (256, BF16), (256, BF16), (256, BF16)]
    if first:
        widths = [(d, F32)] + widths
    out_shape = [jax.ShapeDtypeStruct((b, t, w), dt) for w, dt in widths]
    out_specs = [tile(w) for w, _ in widths]
    res = pl.pallas_call(
        functools.partial(_inproj_kernel, first=first, n_lat_tiles=n_lat_tiles),
        out_shape=out_shape, grid=(b // nb, nt), in_specs=in_specs, out_specs=out_specs,
        compiler_params=_cparams(("parallel", "parallel")),
        name="inproj",
    )(*args)
    return (res[0], res[1:]) if first else (x, res)


def _win_kernel(sink_ref, q_ref, k_ref, v_ref, o_ref, *, n_lat, t_all):
    w = A_WINDOW
    qb = WIN_Q
    kb = qb + 2 * w
    n = pl.program_id(1)
    start = pl.multiple_of(jnp.clip(n * qb - w, 0, n_lat - kb), w)
    q = q_ref[0]
    lane = lax.broadcasted_iota(jnp.int32, (qb, LANES), 1)
    row = lax.broadcasted_iota(jnp.int32, (2 * qb, kb), 0)
    col = lax.broadcasted_iota(jnp.int32, (2 * qb, kb), 1)
    rel = (start + col) - (n * qb + row % qb)
    valid = (jnp.abs(rel) <= w) & (n < n_lat // qb)
    row1 = lax.broadcasted_iota(jnp.int32, (2 * qb, 1), 0)
    for j in range(A_KV_HEADS):
        sl = slice(LANES * j, LANES * (j + 1))
        qp = q[:, sl]
        zero = jnp.zeros_like(qp)
        lhs = jnp.concatenate([jnp.where(lane < 64, qp, zero), jnp.where(lane >= 64, qp, zero)], axis=0)
        sb = _dot_nt(lhs, k_ref[0, pl.ds(start, kb), sl])
        sc = _dot_nt(lhs, k_ref[0, n_lat:t_all, sl])
        sb = jnp.where(valid, sb, MASK_VALUE)
        sink = jnp.where(row1 < qb, sink_ref[2 * j], sink_ref[2 * j + 1]) * LOG2E
        m = jnp.maximum(jnp.maximum(jnp.max(sb, axis=1, keepdims=True), jnp.max(sc, axis=1, keepdims=True)), sink)
        pb = jnp.exp2(sb - m)
        pc = jnp.exp2(sc - m)
        den = jnp.sum(pb, axis=1, keepdims=True) + jnp.sum(pc, axis=1, keepdims=True) + jnp.exp2(sink - m)
        o = (_dot(pb.astype(BF16), v_ref[0, pl.ds(start, kb), sl])
             + _dot(pc.astype(BF16), v_ref[0, n_lat:t_all, sl])) / den
        o_ref[0, :, sl] = jnp.where(lane < 64, o[:qb], o[qb:]).astype(o_ref.dtype)


def _win_attention(sink, q, k, v, n_lat):
    b, t, _ = q.shape
    qb = WIN_Q
    return pl.pallas_call(
        functools.partial(_win_kernel, n_lat=n_lat, t_all=t),
        out_shape=jax.ShapeDtypeStruct((b, t, 256), BF16),
        grid=(b, t // qb),
        in_specs=[pl.BlockSpec(memory_space=pltpu.SMEM),
                  pl.BlockSpec((1, qb, 256), lambda bi, i: (bi, i, 0)),
                  pl.BlockSpec((1, t, 256), lambda bi, i: (bi, 0, 0)),
                  pl.BlockSpec((1, t, 256), lambda bi, i: (bi, 0, 0))],
        out_specs=pl.BlockSpec((1, qb, 256), lambda bi, i: (bi, i, 0)),
        compiler_params=_cparams(("parallel", "arbitrary")),
        name="win_attention",
    )(sink, q, k, v)


def _hgrn_level_matrix(rev):
    c = HG_CHUNK
    m = np.zeros((8, c, c), np.float32)
    for lvl in range(HG_LEVELS):
        size = c >> lvl
        for t in range(c):
            mid = (t // size) * size + size // 2
            upper = t >= mid
            if not rev:
                rng = range(mid, t + 1) if upper else range(t + 1, mid)
            else:
                rng = range(mid, t) if upper else range(t, mid)
            m[lvl, t, list(rng)] = 1.0
    for t in range(c):
        if not rev:
            m[6, t, :t + 1] = 1.0
            m[7, t, t + 1:] = 1.0
        else:
            m[6, t, t:] = 1.0
            m[7, t, :t] = 1.0
    m = m.reshape(8 * c, c)
    return np.concatenate([m, m], axis=1)


def _hgrn_chunk(q, k, logf, v, mlev, rev):
    c = HG_CHUNK
    hi = logf.astype(BF16)
    lo = (logf - hi.astype(F32)).astype(BF16)
    wgt = jnp.exp(_dot(mlev, jnp.concatenate([hi, lo], axis=0)))
    t = lax.broadcasted_iota(jnp.int32, (c, 256), 0)
    s_col = lax.broadcasted_iota(jnp.int32, (c, 256), 1) % c
    r_bd = lax.broadcasted_iota(jnp.int32, (256, 256), 0) // 64
    c_bd = lax.broadcasted_iota(jnp.int32, (256, 256), 1) // 64
    bd = r_bd == c_bd

    def block_diag(a):
        return jnp.where(bd, jnp.concatenate([a] * 4, axis=0), 0.0).astype(BF16)

    scores = jnp.where(t == s_col, _dot_nt(q.astype(BF16), block_diag(k)), 0.0)
    for lvl in range(HG_LEVELS):
        wl = wgt[c * lvl:c * (lvl + 1)]
        q_side = ((t >> (HG_LEVELS - 1 - lvl)) & 1) == (0 if rev else 1)
        ql = jnp.where(q_side, q * wl, 0.0).astype(BF16)
        kl = jnp.where(q_side, 0.0, k * wl)
        s = _dot_nt(ql, block_diag(kl))
        if lvl > 0:
            sh = HG_LEVELS - lvl
            s = jnp.where((t >> sh) == (s_col >> sh), s, 0.0)
        scores = scores + s
    q_in = (q * wgt[6 * c:7 * c]).astype(BF16)
    k_end = (k * wgt[7 * c:8 * c]).astype(BF16)
    o_intra = _dot(scores.astype(BF16), block_diag(v))
    edge = 6 * c if rev else 7 * c - 1
    total = wgt[edge:edge + 1]
    update = jnp.where(bd, _dot_tn(v.astype(BF16), k_end), 0.0)
    return o_intra, q_in, update, total


def _hgrn_kernel(mf_ref, mb_ref, lb_ref, qf_ref, zf_ref, vf_ref, qb_ref, zb_ref, vb_ref,
                 of_ref, ob_ref, stf_ref, stb_ref):
    c = HG_CHUNK
    nc = TM // c

    @pl.when(pl.program_id(1) == 0)
    def _():
        stf_ref[...] = jnp.zeros_like(stf_ref)
        stb_ref[...] = jnp.zeros_like(stb_ref)

    def scan(s, q_ref, z_ref, v_ref, lb, mlev, o_ref, st_ref, rev):
        sig = _sigmoid(z_ref[s])
        logf = jnp.log(jnp.maximum(lb + (1.0 - lb) * sig, GATE_FLOOR))
        k = (1.0 - lb) * (1.0 - sig)
        order = range(nc - 1, -1, -1) if rev else range(nc)
        rows = [slice(ci * c, (ci + 1) * c) for ci in order]
        parts = [_hgrn_chunk(q_ref[s, r, :].astype(F32), k[r], logf[r], v_ref[s, r, :].astype(F32), mlev, rev)
                 for r in rows]
        st = st_ref[s]
        for r, (o_intra, q_in, update, total) in zip(rows, parts):
            o_ref[s, r, :] = o_intra + _dot_nt(q_in, st.astype(BF16))
            st = st * total + update
        st_ref[s] = st

    for s in range(qf_ref.shape[0]):
        scan(s, qf_ref, zf_ref, vf_ref, lb_ref[0:1], mf_ref[...], of_ref, stf_ref, False)
        scan(s, qb_ref, zb_ref, vb_ref, lb_ref[1:2], mb_ref[...], ob_ref, stb_ref, True)


def _hgrn(lb, q, zff, zfb, v, n_lat):
    b, t, _ = q.shape
    nlb = n_lat // TM
    ncb = (t - n_lat) // TM
    fwd = lambda bi, i: (bi, jnp.where(i < ncb, nlb + i, i - ncb), 0)
    bwd = lambda bi, i: (bi, nlb + ncb - 1 - i, 0)
    nb = HG_BATCH if b % HG_BATCH == 0 else 1
    blk = lambda im: pl.BlockSpec((nb, TM, 256), im)
    mf = jnp.asarray(_hgrn_level_matrix(False), BF16)
    mb = jnp.asarray(_hgrn_level_matrix(True), BF16)
    return pl.pallas_call(
        _hgrn_kernel,
        out_shape=[jax.ShapeDtypeStruct((b, t, 256), F32)] * 2,
        grid=(b // nb, t // TM),
        in_specs=[_const_spec((8 * HG_CHUNK, 2 * HG_CHUNK)), _const_spec((8 * HG_CHUNK, 2 * HG_CHUNK)),
                  _const_spec((2, 256)),
                  blk(fwd), blk(fwd), blk(fwd), blk(bwd), blk(bwd), blk(bwd)],
        out_specs=[blk(fwd), blk(bwd)],
        scratch_shapes=[pltpu.VMEM((nb, 256, 256), F32), pltpu.VMEM((nb, 256, 256), F32)],
        compiler_params=_cparams(("parallel", "arbitrary")),
        name="hgrn",
    )(mf, mb, lb, q, zff, v, q, zfb, v)


def _key_chunks(n_lat, t_all):
    chunks = [(lo, min(lo + KV_CHUNK, t_all)) for lo in range(0, t_all, KV_CHUNK)]
    return chunks, [(n_lat, t_all)]


def _online_softmax(lhs, k_ref, v_ref, chunks):
    rows = lhs.shape[0]
    m = jnp.full((rows, 1), -jnp.inf, F32)
    l = jnp.zeros((rows, 1), F32)
    acc = jnp.zeros((rows, LANES), F32)
    for lo, hi in chunks:
        s = _dot_nt(lhs, k_ref[0, lo:hi, :])
        m_new = jnp.maximum(m, jnp.max(s, axis=1, keepdims=True))
        alpha = jnp.exp2(m - m_new)
        p = jnp.exp2(s - m_new)
        l = alpha * l + jnp.sum(p, axis=1, keepdims=True)
        acc = alpha * acc + _dot(p.astype(BF16), v_ref[0, lo:hi, :])
        m = m_new
    return acc / l


def _mla_kernel(q_ref, k_ref, v_ref, o_ref, *, chunks):
    tq = q_ref.shape[1]
    q = q_ref[0]
    zero = jnp.zeros_like(q)
    lane2 = lax.broadcasted_iota(jnp.int32, q.shape, 1)
    lhs = jnp.concatenate([jnp.where(lane2 < LANES, q, zero), jnp.where(lane2 >= LANES, q, zero)], axis=0)
    o = _online_softmax(lhs, k_ref, v_ref, chunks)
    lane = lax.broadcasted_iota(jnp.int32, (tq, LANES), 1)
    o_ref[0] = jnp.where(lane < 64, o[:tq], o[tq:]).astype(o_ref.dtype)


def _diff_kernel(lam_ref, g_ref, q_ref, k_ref, v_ref, o_ref, *, chunks, lam_init):
    tq = q_ref.shape[1]
    q = q_ref[0]
    zero = jnp.zeros_like(q)
    lane = lax.broadcasted_iota(jnp.int32, (tq, LANES), 1)
    lhs = jnp.concatenate([jnp.where(lane // D_HEAD_DIM == r, q, zero) for r in range(4)], axis=0)
    o = _online_softmax(lhs, k_ref, v_ref, chunks)
    lam = lam_ref[0]
    o0 = o[0:tq] - lam * o[tq:2 * tq]
    o1 = o[2 * tq:3 * tq] - lam * o[3 * tq:4 * tq]
    o = jnp.where(lane < 64, o0, o1)
    o_ref[0] = (_head_rms(o, g_ref[...], LANES) * (1.0 - lam_init)).astype(o_ref.dtype)


def _attention_call(body, name, extras, extra_specs, q, k, v, tq, row0, n_rows, out_init):
    b, t, qw = q.shape
    blk0 = row0 // tq
    n_extra = len(extras)
    in_specs = list(extra_specs) + [
        pl.BlockSpec((1, tq, qw // 2), lambda bi, p, i: (bi, blk0 + i, p)),
        pl.BlockSpec((1, t, k.shape[2] // 2), lambda bi, p, i: (bi, 0, p)),
        pl.BlockSpec((1, t, LANES), lambda bi, p, i: (bi, 0, p))]
    args = list(extras) + [q, k, v]
    aliases = {}
    if out_init is not None:
        in_specs.append(pl.BlockSpec(memory_space=pl.ANY))
        aliases = {len(args): 0}
        args.append(out_init)

    def kern(*refs):
        body(*refs[:n_extra + 3], refs[-1])

    return pl.pallas_call(
        kern,
        out_shape=jax.ShapeDtypeStruct((b, t, 256), BF16),
        grid=(b, 2, n_rows // tq),
        in_specs=in_specs,
        out_specs=pl.BlockSpec((1, tq, LANES), lambda bi, p, i: (bi, blk0 + i, p)),
        input_output_aliases=aliases,
        compiler_params=_cparams(("parallel", "parallel", "arbitrary")),
        name=name,
    )(*args)


def _two_pass_attention(body_of, name, extras, extra_specs, q, k, v, n_lat, tq_lat, with_ctx):
    t = q.shape[1]
    lat_chunks, ctx_chunks = _key_chunks(n_lat, t)
    out = jnp.zeros((q.shape[0], t, 256), BF16)
    out = _attention_call(body_of(lat_chunks), name, extras, extra_specs, q, k, v, tq_lat, 0, n_lat, out)
    if not with_ctx:
        return out
    return _attention_call(body_of(ctx_chunks), name + "_ctx", extras, extra_specs, q, k, v,
                           TQ, n_lat, t - n_lat, out)


def _mla_attention(q, k, v, n_lat, with_ctx):
    body_of = lambda chunks: functools.partial(_mla_kernel, chunks=chunks)
    return _two_pass_attention(body_of, "mla_attention", [], [], q, k, v, n_lat, TQ_MLA, with_ctx)


def _diff_attention(lam, q, k, v, gain, n_lat, lam_init, with_ctx):
    body_of = lambda chunks: functools.partial(_diff_kernel, chunks=chunks, lam_init=lam_init)
    specs = [pl.BlockSpec(memory_space=pltpu.SMEM), _const_spec((1, LANES))]
    return _two_pass_attention(body_of, "diff_attention", [lam, gain], specs, q, k, v, n_lat, TQ, with_ctx)


def _merge_kernel(x_ref, mod_ref, n1_ref, n2_ref, a_ref, of_ref, ob_ref, bg_ref, hg_ref, c_ref, d_ref,
                  wg_ref, wb_ref, wo_ref, wr_ref, br_ref, xo_ref, h2_ref, ri_ref, rw_ref):
    d = D_MODEL
    nb, tm, _ = x_ref.shape
    rows = nb * tm
    flat = lambda a: a.reshape(rows, a.shape[-1])
    x = x_ref[...]
    m = mod_ref[:, 0]
    mrow = lambda k: m[:, k:k + 1, :]
    h = flat((_rms(x, n1_ref[...]) * (1.0 + mrow(1)) + mrow(0)).astype(BF16))
    g = flat(bg_ref[...]).astype(F32)
    b_out = _head_rms(flat(of_ref[...] + ob_ref[...]), hg_ref[...], 256) * (g * _sigmoid(g))
    branches = (flat(a_ref[...]), b_out.astype(BF16), flat(c_ref[...]), flat(d_ref[...]))
    y = jnp.zeros((rows, d), F32)
    for i, br in enumerate(branches):
        gate = _sigmoid(_dot(h, wg_ref[:, d * i:d * (i + 1)]))
        y = y + gate * _dot(br, wb_ref[i])
    x = x + mrow(2) * _dot(y.astype(BF16), wo_ref[...]).reshape(nb, tm, d)
    xo_ref[...] = x
    h2 = flat(_rms(x, n2_ref[...]) * (1.0 + mrow(4)) + mrow(3))
    h2b = h2.astype(BF16)
    h2_ref[...] = _rows_first(h2).reshape(nb, tm, ROW_SLABS, LANES)

    h2lo = (h2 - h2b.astype(F32)).astype(BF16)
    r = _dot(jnp.concatenate([h2b, h2lo], axis=0), wr_ref[...])
    logits = r[:rows, :LANES] + r[:rows, LANES:] + r[rows:, :LANES] + br_ref[...]
    lane = lax.broadcasted_iota(jnp.int32, (rows, LANES), 1)
    neg = -jnp.inf
    is_g = (lane >= N_EXPERTS) & (lane < N_EXPERTS + N_GROUPS)
    lg = jnp.where(is_g, logits, neg)
    mg = jnp.max(lg, axis=1, keepdims=True)
    g_val = 1.0 / jnp.sum(jnp.exp(lg - mg), axis=1, keepdims=True)
    g_idx = jnp.min(jnp.where(lg == mg, lane, 4 * LANES), axis=1, keepdims=True) - N_EXPERTS
    in_group = (lane >= g_idx * EXPERTS_PER_GROUP) & (lane < (g_idx + 1) * EXPERTS_PER_GROUP)
    le = jnp.where(in_group, logits, neg)
    m1 = jnp.max(le, axis=1, keepdims=True)
    e1 = jnp.min(jnp.where(le == m1, lane, 4 * LANES), axis=1, keepdims=True)
    le2 = jnp.where(lane == e1, neg, le)
    m2 = jnp.max(le2, axis=1, keepdims=True)
    e2 = jnp.min(jnp.where(le2 == m2, lane, 4 * LANES), axis=1, keepdims=True)
    r2 = jnp.exp(m2 - m1)
    v1 = g_val / (1.0 + r2)
    ri_ref[...] = jnp.where(lane == 0, e1, jnp.where(lane == 1, e2, 0)).reshape(nb, tm, LANES)
    rw_ref[...] = jnp.where(lane == 0, v1, jnp.where(lane == 1, v1 * r2, 0.0)).reshape(nb, tm, LANES)


def _merge(x, mod, n1, n2, a_o, o_f, o_b, bg, hg, c_o, d_o, wg, wb, wo, wr, br, n_lat_tiles, t):
    b, _, d = x.shape
    tm = TM_PROJ
    nb = MERGE_BATCH if b % MERGE_BATCH == 0 else 1
    tile = lambda w: pl.BlockSpec((nb, tm, w), lambda bi, i: (bi, i, 0))
    return pl.pallas_call(
        _merge_kernel,
        out_shape=[jax.ShapeDtypeStruct((b, t, d), F32), jax.ShapeDtypeStruct((b, t, ROW_SLABS, LANES), F32),
                   jax.ShapeDtypeStruct((b, t, LANES), jnp.int32), jax.ShapeDtypeStruct((b, t, LANES), F32)],
        grid=(b // nb, pl.cdiv(t, tm)),
        in_specs=[tile(d), pl.BlockSpec((nb, 1, 8, d), lambda bi, i: (bi, i // n_lat_tiles, 0, 0)),
                  _const_spec((1, d)), _const_spec((1, d)),
                  tile(256), tile(256), tile(256), tile(256), _const_spec((1, 256)), tile(256), tile(256),
                  _const_spec((d, 4 * d)), _const_spec((4, 256, d)), _const_spec((d, d)),
                  _const_spec((d, 256)), _const_spec((1, LANES))],
        out_specs=[tile(d), pl.BlockSpec((nb, tm, ROW_SLABS, LANES), lambda bi, i: (bi, i, 0, 0)),
                   tile(LANES), tile(LANES)],
        compiler_params=_cparams(("parallel", "parallel")),
        name="merge",
    )(x, mod, n1, n2, a_o, o_f, o_b, bg, hg, c_o, d_o, wg, wb, wo, wr, br)


def _route_positions(ri, n_tok):
    ef = ri[:, :2].reshape(-1)
    rb = 2 * TM
    oh = (ef[:, None] == jnp.arange(N_EXPERTS, dtype=jnp.int32)[None, :]).astype(F32).reshape(-1, rb, N_EXPERTS)
    tri = (jnp.arange(rb)[:, None] >= jnp.arange(rb)[None, :]).astype(F32)
    within = jnp.einsum("ij,gje->gie", tri, oh)
    tot = within[:, -1, :]
    before = jnp.cumsum(tot, axis=0) - tot
    cnt = jnp.sum(tot, axis=0).astype(jnp.int32)
    pcnt = ((cnt + MOE_TILE - 1) // MOE_TILE) * MOE_TILE
    end = jnp.cumsum(pcnt)
    start = (end - pcnt).astype(F32)
    pos = jnp.sum((within + (before + start[None, :])[:, None, :]) * oh, axis=2) - 1.0
    pos = pos.reshape(-1)
    n_tiles = 2 * n_tok // MOE_TILE + N_EXPERTS
    n_act = (end[-1] // MOE_TILE).astype(jnp.int32)
    tile_row = jnp.minimum(jnp.arange(n_tiles, dtype=jnp.int32), n_act - 1) * MOE_TILE
    tile_expert = jnp.sum((end[None, :] <= tile_row[:, None]).astype(jnp.int32), axis=1)
    tile_expert = jnp.minimum(tile_expert, N_EXPERTS - 1)
    return pos.reshape(n_tok // TM, 1, 2 * TM).astype(jnp.int32), tile_expert, n_act.reshape(1)


def _dispatch_kernel(pos_ref, h_ref, xs_in_ref, xs_ref, sem):
    del xs_in_ref

    def body(r, carry):
        for s in range(2):
            p = pos_ref[0, 0, 2 * r + s]
            pltpu.make_async_copy(h_ref.at[pl.ds(r, 1)], xs_ref.at[pl.ds(p, 1)], sem.at[s]).start(priority=s)
        return carry

    lax.fori_loop(0, TM, body, 0)
    for s in range(2):
        pltpu.make_async_copy(h_ref, h_ref, sem.at[s]).wait()


def _dispatch(pos, h2, xs_init):
    n = h2.shape[0]
    return pl.pallas_call(
        _dispatch_kernel,
        out_shape=jax.ShapeDtypeStruct(xs_init.shape, F32),
        grid=(n // TM,),
        in_specs=[pl.BlockSpec((1, 1, 2 * TM), lambda t: (t, 0, 0), memory_space=pltpu.SMEM),
                  pl.BlockSpec((TM, ROW_SLABS, LANES), lambda t: (t, 0, 0)),
                  pl.BlockSpec(memory_space=pl.ANY)],
        out_specs=pl.BlockSpec(memory_space=pl.ANY),
        scratch_shapes=[pltpu.SemaphoreType.DMA((2,))],
        input_output_aliases={2: 0},
        compiler_params=_cparams(("arbitrary",)),
        name="moe_dispatch",
    )(pos, h2, xs_init)


def _experts_kernel(te_ref, na_ref, x_ref, wg_ref, wu_ref, wd_ref, o_ref):
    del te_ref
    t = pl.program_id(0)

    @pl.when(t < na_ref[0])
    def _():
        x = _slabs_first(x_ref[...])
        xb = [x[s].astype(BF16) for s in range(ROW_SLABS)]
        gt = sum(_dot(xb[s], wg_ref[0, LANES * s:LANES * (s + 1), :]) for s in range(ROW_SLABS))
        up = sum(_dot(xb[s], wu_ref[0, LANES * s:LANES * (s + 1), :]) for s in range(ROW_SLABS))
        o_ref[...] = _rows_first(_dot((gt * _sigmoid(gt) * up).astype(BF16), wd_ref[0]))

    @pl.when(t >= na_ref[0])
    def _():
        o_ref[...] = jnp.zeros_like(o_ref)


def _experts(tile_expert, n_act, xs, wg, wu, wd):
    n_tiles = tile_expert.shape[0]
    d = D_MODEL
    rows = pl.BlockSpec((MOE_TILE, ROW_SLABS, LANES), lambda t, te, na: (t, 0, 0))
    w_in = pl.BlockSpec((1, d, EXPERT_HIDDEN), lambda t, te, na: (te[t], 0, 0))
    return pl.pallas_call(
        _experts_kernel,
        out_shape=jax.ShapeDtypeStruct((n_tiles * MOE_TILE, ROW_SLABS, LANES), F32),
        grid_spec=pltpu.PrefetchScalarGridSpec(
            num_scalar_prefetch=2, grid=(n_tiles,),
            in_specs=[rows, w_in, w_in,
                      pl.BlockSpec((1, EXPERT_HIDDEN, d), lambda t, te, na: (te[t], 0, 0))],
            out_specs=rows),
        compiler_params=_cparams(("arbitrary",)),
        name="moe_experts",
    )(tile_expert, n_act, xs, wg, wu, wd)


def _combine_kernel(pos_ref, posn_ref, ys_ref, w_ref, x_ref, mod_ref, g_ref, o_ref, buf, sem, *, final, n_steps):
    g = pl.program_id(0) * pl.num_programs(1) + pl.program_id(1)
    cur = g % 2

    def start_gather(p_ref, par):
        def body(r, carry):
            for s in range(2):
                p = p_ref[0, 0, 2 * r + s]
                pltpu.make_async_copy(ys_ref.at[pl.ds(p, 1)], buf.at[par, s, pl.ds(r, 1)],
                                      sem.at[par, s]).start(priority=s)
            return carry

        lax.fori_loop(0, TM, body, 0)

    @pl.when(g == 0)
    def _():
        start_gather(pos_ref, 0)

    @pl.when(g + 1 < n_steps)
    def _():
        start_gather(posn_ref, 1 - cur)

    for s in range(2):
        pltpu.make_async_copy(buf.at[cur, s], buf.at[cur, s], sem.at[cur, s]).wait()
    w = w_ref[0]
    y0, y1 = _slabs_first(buf[cur, 0]), _slabs_first(buf[cur, 1])
    f = jnp.concatenate([w[:, 0:1] * y0[k] + w[:, 1:2] * y1[k] for k in range(ROW_SLABS)], axis=1)
    x = x_ref[0] + mod_ref[0, 0, 5:6, :] * f
    o_ref[0] = _rms(x, g_ref[...]) if final else x


def _combine(pos, ys, rwt, x, mod, gain, n_lat):
    b, t, d = x.shape
    nt = t // TM
    n_steps = b * nt
    tile = lambda w: pl.BlockSpec((1, TM, w), lambda bi, i: (bi, i, 0))
    final = gain is not None
    pos_spec = lambda ahead: pl.BlockSpec(
        (1, 1, 2 * TM), lambda bi, i: (jnp.minimum(bi * nt + i + ahead, n_steps - 1), 0, 0),
        memory_space=pltpu.SMEM)
    return pl.pallas_call(
        functools.partial(_combine_kernel, final=final, n_steps=n_steps),
        out_shape=jax.ShapeDtypeStruct((b, t, d), F32),
        grid=(b, nt),
        in_specs=[pos_spec(0), pos_spec(1),
                  pl.BlockSpec(memory_space=pl.ANY),
                  tile(LANES), tile(d),
                  pl.BlockSpec((1, 1, 8, d), lambda bi, i: (bi, i // (n_lat // TM), 0, 0)),
                  _const_spec((1, d))],
        out_specs=tile(d),
        scratch_shapes=[pltpu.VMEM((2, 2, TM, ROW_SLABS, LANES), F32), pltpu.SemaphoreType.DMA((2, 2))],
        compiler_params=_cparams(("arbitrary", "arbitrary")),
        name="moe_combine",
    )(pos, pos, ys, rwt, x, mod, gain if final else jnp.ones((1, d), F32))


def _moe(x, h2, ri, rwt, wg, wu, wd, xs_init, mod, gain, n_lat):
    b, t, _ = x.shape
    n = b * t
    pos, tile_expert, n_act = _route_positions(ri.reshape(n, LANES), n)
    xs = _dispatch(pos, h2.reshape(n, ROW_SLABS, LANES), xs_init)
    ys = _experts(tile_expert, n_act, xs, wg, wu, wd)
    return _combine(pos, ys, rwt, x, mod, gain, n_lat), xs


def _rope_tables(n_lat, t_all, lane_rope, lane_off, dr):
    half, quarter = dr // 2, dr // 4
    inv_freq = 1.0 / (ROPE_BASE ** (jnp.arange(quarter, dtype=F32) / quarter))
    off = np.asarray(lane_off)
    use_col = off >= half
    j = (off % half) % quarter
    first = (off % half) < quarter
    tok = jnp.arange(n_lat, dtype=jnp.int32)
    row = (tok // GRID_W).astype(F32)
    col = (tok % GRID_W).astype(F32)
    pos = jnp.where(jnp.asarray(use_col)[None, :], col[:, None], row[:, None])
    ang = pos * inv_freq[jnp.asarray(j)][None, :]
    rope = jnp.asarray(lane_rope)[None, :]
    cos = jnp.where(rope, jnp.cos(ang), 1.0)
    sin = jnp.where(rope, jnp.sin(ang) * jnp.where(jnp.asarray(first), -1.0, 1.0)[None, :], 0.0)
    pad = t_all - n_lat
    w = off.shape[0]
    cos = jnp.concatenate([cos, jnp.ones((pad, w), F32)], axis=0)
    sin = jnp.concatenate([sin, jnp.zeros((pad, w), F32)], axis=0)
    return cos, sin


def _all_rope_tables(n_lat, t_all):
    rep = lambda tabs, n: tuple(jnp.tile(tb, (1, n)) for tb in tabs)
    ta = rep(_rope_tables(n_lat, t_all, np.ones(64, bool), np.arange(64), 64), 4)
    lc = np.arange(128)
    tc = rep(_rope_tables(n_lat, t_all, (lc >= 64) & (lc < 96), np.clip(lc - 64, 0, 31), 32), 4)
    td = rep(_rope_tables(n_lat, t_all, np.ones(32, bool), np.arange(32), 32), 8)
    return (*ta, *tc, *td)


def _permute_w_in(w):
    s = np.cumsum([0, 256, 128, 128, 256, 256, 256, 256, 256, 256, 128, 32, 256, 256, 256])
    w = w.astype(BF16)
    seg = lambda i: w[:, s[i]:s[i + 1]]
    z64 = jnp.zeros((w.shape[0], 64), w.dtype)
    z32 = jnp.zeros((w.shape[0], 32), w.dtype)
    kr = jnp.concatenate([z64, seg(10), z32], axis=1)
    cols = [seg(0), seg(1), seg(2),
            seg(3), seg(4), seg(5), seg(6), seg(7),
            seg(8), seg(9), kr,
            seg(11), seg(12), seg(13)]
    return jnp.concatenate(cols, axis=1)


def _permute_mla(w_uq, w_ukv):
    z32 = jnp.zeros((w_uq.shape[0], 32), w_uq.dtype)
    qd = C_NOPE + C_ROPE
    uq = jnp.concatenate([a for h in range(C_HEADS) for a in (w_uq[:, qd * h:qd * (h + 1)], z32)], axis=1)
    z64 = jnp.zeros((w_ukv.shape[0], 64), w_ukv.dtype)
    kd = C_NOPE + C_V
    uk = jnp.concatenate([a for h in range(C_HEADS) for a in (w_ukv[:, kd * h:kd * h + C_NOPE], z64)], axis=1)
    uv = jnp.concatenate([w_ukv[:, kd * h + C_NOPE:kd * (h + 1)] for h in range(C_HEADS)], axis=1)
    return uq.astype(BF16), jnp.concatenate([uk, uv], axis=1).astype(BF16)


def kernel(x, c, ctx, c_ctx, w_mod, b_mod, norm1, norm2, w_in, w_gate, w_branch, w_out, attn_sink, hgrn_lb_logits, hgrn_norm, mla_q_norm, mla_kv_norm, mla_w_uq, mla_w_ukv, diff_lambda, diff_subln, w_router_group, b_router_group, w_router_expert, b_router_expert, w_expert_gate, w_expert_up, w_expert_down, final_norm):
    b, n_lat, d = x.shape
    n_ctx = ctx.shape[1]
    depth = w_mod.shape[0]
    t_all = n_lat + n_ctx
    assert d == D_MODEL and n_lat % TM == 0 and n_ctx % TM == 0 and n_lat % GRID_W == 0 and n_ctx <= n_lat
    assert n_lat >= WIN_Q + 2 * A_WINDOW and n_lat % TQ_MLA == 0 and n_lat % TM_PROJ == 0 and b <= 15
    n_lat_tiles = n_lat // TM_PROJ

    c_all = jnp.concatenate([c, c_ctx[None], jnp.zeros((15 - b, d), F32)], axis=0)
    mod = _modulation(c_all, w_mod, b_mod).reshape(depth, 16, 6, d)
    mod = jnp.pad(mod, ((0, 0), (0, 0), (0, 2), (0, 0)))
    mod = jnp.stack([mod[:, :b], jnp.broadcast_to(mod[:, b:b + 1], (depth, b, 8, d))], axis=2)

    sm = jax.nn.softmax(hgrn_lb_logits.astype(F32), axis=0)
    lower_bounds = jnp.cumsum(sm, axis=0) - sm[0]
    tabs = _all_rope_tables(n_lat, t_all)

    n_tok = b * t_all
    moe_buf = jnp.zeros((2 * n_tok + N_EXPERTS * MOE_TILE, ROW_SLABS, LANES), F32)
    xs = x
    for li in range(depth):
        lam_init = 0.8 - 0.6 * math.exp(-0.3 * li)
        lp = diff_lambda[li].astype(F32)
        lam = (jnp.exp(jnp.sum(lp[0] * lp[1])) - jnp.exp(jnp.sum(lp[2] * lp[3])) + lam_init).reshape(1)
        w_perm = _permute_w_in(w_in[li])
        wuq, wukv = _permute_mla(mla_w_uq[li], mla_w_ukv[li])
        xs, z = _inproj(xs, ctx if li == 0 else None, mod[li], norm1[li][None], w_perm,
                        mla_q_norm[li][None], mla_kv_norm[li][None], wuq, wukv, tabs, n_lat_tiles)
        qa, ka, va, bq, bff, bfb, bv, bg, cq, ck, cv, dq, dk, dv = z
        a_o = _win_attention(attn_sink[li].astype(F32), qa, ka, va, n_lat)
        o_f, o_b = _hgrn(lower_bounds[li], bq, bff, bfb, bv, n_lat)
        with_ctx = li < depth - 1
        c_o = _mla_attention(cq, ck, cv, n_lat, with_ctx)
        d_o = _diff_attention(lam, dq, dk, dv, jnp.tile(diff_subln[li], 2)[None], n_lat, lam_init, with_ctx)
        wr = jnp.concatenate([w_router_expert[li], w_router_group[li],
                              jnp.zeros((d, LANES - N_EXPERTS - N_GROUPS), F32)], axis=1)
        wr_hi = wr.astype(BF16)
        wr_lo = (wr - wr_hi.astype(F32)).astype(BF16)
        br = jnp.concatenate([b_router_expert[li], b_router_group[li],
                              jnp.zeros((LANES - N_EXPERTS - N_GROUPS,), F32)])[None]
        t_keep = t_all if with_ctx else n_lat
        xs, h2, ri, rwt = _merge(xs, mod[li], norm1[li][None], norm2[li][None], a_o, o_f, o_b, bg,
                                 jnp.tile(hgrn_norm[li], B_HEADS)[None], c_o, d_o,
                                 w_gate[li].astype(BF16), w_branch[li].astype(BF16), w_out[li].astype(BF16),
                                 jnp.concatenate([wr_hi, wr_lo], axis=1), br, n_lat_tiles, t_keep)
        xs, moe_buf = _moe(xs, h2, ri, rwt, w_expert_gate[li].astype(BF16), w_expert_up[li].astype(BF16),
                           w_expert_down[li].astype(BF16), moe_buf, mod[li],
                           None if with_ctx else final_norm[None], n_lat)
    return xs
```

```python
import functools
import math

import numpy as np
import jax
import jax.numpy as jnp
from jax import lax
from jax.experimental import pallas as pl
from jax.experimental.pallas import tpu as pltpu

D_MODEL = 1024
GRID_W = 64
ROPE_BASE = 10000.0
NORM_EPS = 1e-6
MASK_VALUE = -1e30
GATE_FLOOR = 1e-30

A_HEADS, A_KV_HEADS, A_HEAD_DIM, A_WINDOW = 4, 2, 64, 128
B_HEADS, B_KEY_DIM, B_VAL_DIM = 4, 64, 64
C_HEADS, C_Q_LORA, C_KV_LORA, C_NOPE, C_ROPE, C_V = 4, 256, 128, 64, 32, 64
D_HEADS, D_HEAD_DIM = 4, 32
N_GROUPS, EXPERTS_PER_GROUP, EXPERT_HIDDEN = 4, 8, 256
N_EXPERTS = N_GROUPS * EXPERTS_PER_GROUP

TM = 256
TM_PROJ = 256
WIN_Q = 256
TQ = 256
TQ_MLA = 512
LOG2E = 1.4426950408889634
KV_CHUNK = 4096
MOE_TILE = 512
HG_BATCH = 2
MERGE_BATCH = 2
HG_CHUNK = 64
HG_LEVELS = 6
LANES = 128
ROW_SLABS = D_MODEL // LANES
VMEM_LIMIT = 52 * 1024 * 1024

OFF_A, OFF_B, OFF_C, OFF_D, N_PERM = 0, 512, 1792, 2304, 3072

F32 = jnp.float32
BF16 = jnp.bfloat16


def _dot(a, b):
    return jnp.dot(a, b, preferred_element_type=F32)


def _dot_nt(a, b):
    return lax.dot_general(a, b, (((1,), (1,)), ((), ())), preferred_element_type=F32)


def _dot_tn(a, b):
    return lax.dot_general(a, b, (((0,), (0,)), ((), ())), preferred_element_type=F32)


def _dot_hi(a, b):
    return jnp.dot(a, b, preferred_element_type=F32, precision=lax.Precision.HIGHEST)


def _sigmoid(x):
    return 1.0 / (1.0 + jnp.exp(-x))


def _rms(x, gain):
    return x * lax.rsqrt(jnp.mean(x * x, axis=-1, keepdims=True) + NORM_EPS) * gain


def _head_rms(o, gain, width):
    rows = o.shape[0]
    r = lax.broadcasted_iota(jnp.int32, (width, width), 0) // 64
    c = lax.broadcasted_iota(jnp.int32, (width, width), 1) // 64
    ones = jnp.where(r == c, 1.0 / 64.0, 0.0).astype(BF16)
    sq = o * o
    hi = sq.astype(BF16)
    lo = (sq - hi.astype(F32)).astype(BF16)
    ms = _dot(jnp.concatenate([hi, lo], axis=0), ones)
    return o * lax.rsqrt(ms[:rows] + ms[rows:] + NORM_EPS) * gain


def _rope(x, cos, sin, quarter):
    w = x.shape[-1]
    lane = lax.broadcasted_iota(jnp.int32, x.shape, 1)
    first = (lane % (2 * quarter)) < quarter
    sw = jnp.where(first, pltpu.roll(x, w - quarter, 1), pltpu.roll(x, quarter, 1))
    return x * cos + sw * sin


def _dup_heads(x):
    lane = lax.broadcasted_iota(jnp.int32, x.shape, 1)
    sw = pltpu.roll(x, 64, 1)
    return jnp.concatenate([jnp.where(lane < 64, x, sw), jnp.where(lane < 64, sw, x)], axis=1)


def _slabs_first(x):
    return pltpu.einshape("tsl->stl", x)


def _rows_first(y):
    slabs = jnp.stack([y[:, LANES * s:LANES * (s + 1)] for s in range(ROW_SLABS)], axis=0)
    return pltpu.einshape("stl->tsl", slabs)


def _cparams(sem):
    return pltpu.CompilerParams(dimension_semantics=sem, vmem_limit_bytes=VMEM_LIMIT)


def _const_spec(shape):
    n = len(shape)
    return pl.BlockSpec(shape, lambda *_: (0,) * n)


def _mod_kernel(c_ref, w_ref, b_ref, o_ref):
    c = c_ref[...]
    o_ref[0] = _dot_hi(c * _sigmoid(c), w_ref[0]) + b_ref[0]


def _modulation(c_all, w_mod, b_mod):
    depth, d, n = w_mod.shape
    nb = 1536
    return pl.pallas_call(
        _mod_kernel,
        out_shape=jax.ShapeDtypeStruct((depth, 16, n), F32),
        grid=(depth, n // nb),
        in_specs=[pl.BlockSpec((16, d), lambda l, j: (0, 0)),
                  pl.BlockSpec((1, d, nb), lambda l, j: (l, 0, j)),
                  pl.BlockSpec((1, 1, nb), lambda l, j: (l, 0, j))],
        out_specs=pl.BlockSpec((1, 16, nb), lambda l, j: (l, 0, j)),
        compiler_params=_cparams(("arbitrary", "arbitrary")),
        name="modulation",
    )(c_all, w_mod, b_mod.reshape(depth, 1, n))


def _inproj_kernel(*refs, first, n_lat_tiles):
    if not first:
        x = refs[0][...]
        refs = refs[1:]
    else:
        x_ref, ctx_ref = refs[:2]
        refs = refs[2:]
        ctx = ctx_ref[...]
        if ctx.shape[1] < x_ref.shape[1]:
            ctx = jnp.concatenate([ctx] * (x_ref.shape[1] // ctx.shape[1]), axis=1)
        x = jnp.where(pl.program_id(1) < n_lat_tiles, x_ref[...], ctx)
    (mod_ref, n1_ref, w_ref, qn_ref, kvn_ref, wuq_ref, wukv_ref,
     cosa_ref, sina_ref, cosc_ref, sinc_ref, cosd_ref, sind_ref) = refs[:13]
    outs = refs[13:]
    nb, tm, d = x.shape

    def put(ref, val):
        ref[...] = val.reshape(ref.shape)

    if first:
        put(outs[0], x)
        outs = outs[1:]
    (qa_ref, ka_ref, va_ref, bq_ref, bff_ref, bfb_ref, bv_ref, bg_ref,
     cq_ref, ck_ref, cv_ref, dq_ref, dk_ref, dv_ref) = outs
    m = mod_ref[:, 0]
    h = (_rms(x, n1_ref[...]) * (1.0 + m[:, 1:2, :]) + m[:, 0:1, :]).astype(BF16).reshape(nb * tm, d)
    tab = lambda ref: jnp.concatenate([ref[...]] * nb, axis=0)

    z = _dot(h, w_ref[:, OFF_A:OFF_A + 512])
    cosa, sina = tab(cosa_ref), tab(sina_ref)
    put(qa_ref, (_rope(z[:, 0:256], cosa, sina, 16) * (A_HEAD_DIM ** -0.5 * LOG2E)).astype(BF16))
    put(ka_ref, _dup_heads(_rope(z[:, 256:384], cosa[:, :LANES], sina[:, :LANES], 16)).astype(BF16))
    put(va_ref, _dup_heads(z[:, 384:512]).astype(BF16))

    z = _dot(h, w_ref[:, OFF_B:OFF_B + 1280])
    put(bq_ref, z[:, 0:256].astype(BF16))
    put(bff_ref, z[:, 256:512])
    put(bfb_ref, z[:, 512:768])
    put(bv_ref, z[:, 768:1024].astype(BF16))
    put(bg_ref, z[:, 1024:1280].astype(BF16))

    z = _dot(h, w_ref[:, OFF_C:OFF_C + 512])
    cosc, sinc = tab(cosc_ref), tab(sinc_ref)
    cq = _rms(z[:, 0:256], qn_ref[...]).astype(BF16)
    q = _rope(_dot(cq, wuq_ref[...]), cosc, sinc, 8)
    put(cq_ref, (q * ((C_NOPE + C_ROPE) ** -0.5 * LOG2E)).astype(BF16))
    ckv = _rms(z[:, 256:384], kvn_ref[...]).astype(BF16)
    kv = _dot(ckv, wukv_ref[...])
    k_rope = _rope(z[:, 384:512], cosc[:, :LANES], sinc[:, :LANES], 8)
    put(ck_ref, (kv[:, 0:512] + jnp.concatenate([k_rope] * C_HEADS, axis=1)).astype(BF16))
    put(cv_ref, kv[:, 512:768].astype(BF16))

    z = _dot(h, w_ref[:, OFF_D:OFF_D + 768])
    cosd, sind = tab(cosd_ref), tab(sind_ref)
    put(dq_ref, (_rope(z[:, 0:256], cosd, sind, 8) * (D_HEAD_DIM ** -0.5 * LOG2E)).astype(BF16))
    put(dk_ref, _rope(z[:, 256:512], cosd, sind, 8).astype(BF16))
    put(dv_ref, z[:, 512:768].astype(BF16))


def _inproj(x, ctx, mod, n1, w_perm, qn, kvn, wuq, wukv, tabs, n_lat_tiles):
    first = ctx is not None
    b, _, d = x.shape
    t = x.shape[1] + (ctx.shape[1] if first else 0)
    tm = TM_PROJ
    nt = pl.cdiv(t, tm)
    nb = MERGE_BATCH if b % MERGE_BATCH == 0 else 1
    tile = lambda w: pl.BlockSpec((nb, tm, w), lambda bi, i: (bi, i, 0))
    modspec = pl.BlockSpec((nb, 1, 8, d), lambda bi, i: (bi, i // n_lat_tiles, 0, 0))
    tab = lambda w: pl.BlockSpec((tm, w), lambda bi, i: (i, 0))
    if first:
        lat = pl.BlockSpec((nb, tm, d), lambda bi, i: (bi, jnp.minimum(i, n_lat_tiles - 1), 0))
        ctx_spec = pl.BlockSpec((nb, min(tm, ctx.shape[1]), d),
                                lambda bi, i: (bi, jnp.maximum(i - n_lat_tiles, 0), 0))
        in_specs, args = [lat, ctx_spec], [x, ctx]
    else:
        in_specs, args = [tile(d)], [x]
    in_specs += [modspec, _const_spec((1, d)), _const_spec((d, N_PERM)), _const_spec((1, 256)),
                 _const_spec((1, 128)), _const_spec((256, 512)), _const_spec((128, 768)),
                 tab(256), tab(256), tab(512), tab(512), tab(256), tab(256)]
    args += [mod, n1, w_perm, qn, kvn, wuq, wukv, *tabs]
    widths = [(256, BF16), (256, BF16), (256, BF16),
              (256, BF16), (256, F32), (256, F32), (256, BF16), (256, BF16),
              (512, BF16), (512, BF16), (256, BF16),
              (256, BF16), (256, BF16), (256, BF16)]
    if first:
        widths = [(d, F32)] + widths
    out_shape = [jax.ShapeDtypeStruct((b, t, w), dt) for w, dt in widths]
    out_specs = [tile(w) for w, _ in widths]
    res = pl.pallas_call(
        functools.partial(_inproj_kernel, first=first, n_lat_tiles=n_lat_tiles),
        out_shape=out_shape, grid=(b // nb, nt), in_specs=in_specs, out_specs=out_specs,
        compiler_params=_cparams(("parallel", "parallel")),
        name="inproj",
    )(*args)
    return (res[0], res[1:]) if first else (x, res)


def _win_kernel(sink_ref, q_ref, k_ref, v_ref, o_ref, *, n_lat, t_all):
    w = A_WINDOW
    qb = WIN_Q
    kb = qb + 2 * w
    n = pl.program_id(1)
    start = pl.multiple_of(jnp.clip(n * qb - w, 0, n_lat - kb), w)
    q = q_ref[0]
    lane = lax.broadcasted_iota(jnp.int32, (qb, LANES), 1)
    row = lax.broadcasted_iota(jnp.int32, (2 * qb, kb), 0)
    col = lax.broadcasted_iota(jnp.int32, (2 * qb, kb), 1)
    rel = (start + col) - (n * qb + row % qb)
    valid = (jnp.abs(rel) <= w) & (n < n_lat // qb)
    row1 = lax.broadcasted_iota(jnp.int32, (2 * qb, 1), 0)
    for j in range(A_KV_HEADS):
        sl = slice(LANES * j, LANES * (j + 1))
        qp = q[:, sl]
        zero = jnp.zeros_like(qp)
        lhs = jnp.concatenate([jnp.where(lane < 64, qp, zero), jnp.where(lane >= 64, qp, zero)], axis=0)
        sb = _dot_nt(lhs, k_ref[0, pl.ds(start, kb), sl])
        sc = _dot_nt(lhs, k_ref[0, n_lat:t_all, sl])
        sb = jnp.where(valid, sb, MASK_VALUE)
        sink = jnp.where(row1 < qb, sink_ref[2 * j], sink_ref[2 * j + 1]) * LOG2E
        m = jnp.maximum(jnp.maximum(jnp.max(sb, axis=1, keepdims=True), jnp.max(sc, axis=1, keepdims=True)), sink)
        pb = jnp.exp2(sb - m)
        pc = jnp.exp2(sc - m)
        den = jnp.sum(pb, axis=1, keepdims=True) + jnp.sum(pc, axis=1, keepdims=True) + jnp.exp2(sink - m)
        o = (_dot(pb.astype(BF16), v_ref[0, pl.ds(start, kb), sl])
             + _dot(pc.astype(BF16), v_ref[0, n_lat:t_all, sl])) / den
        o_ref[0, :, sl] = jnp.where(lane < 64, o[:qb], o[qb:]).astype(o_ref.dtype)


def _win_attention(sink, q, k, v, n_lat):
    b, t, _ = q.shape
    qb = WIN_Q
    return pl.pallas_call(
        functools.partial(_win_kernel, n_lat=n_lat, t_all=t),
        out_shape=jax.ShapeDtypeStruct((b, t, 256), BF16),
        grid=(b, t // qb),
        in_specs=[pl.BlockSpec(memory_space=pltpu.SMEM),
                  pl.BlockSpec((1, qb, 256), lambda bi, i: (bi, i, 0)),
                  pl.BlockSpec((1, t, 256), lambda bi, i: (bi, 0, 0)),
                  pl.BlockSpec((1, t, 256), lambda bi, i: (bi, 0, 0))],
        out_specs=pl.BlockSpec((1, qb, 256), lambda bi, i: (bi, i, 0)),
        compiler_params=_cparams(("parallel", "arbitrary")),
        name="win_attention",
    )(sink, q, k, v)


def _hgrn_level_matrix(rev):
    c = HG_CHUNK
    m = np.zeros((8, c, c), np.float32)
    for lvl in range(HG_LEVELS):
        size = c >> lvl
        for t in range(c):
            mid = (t // size) * size + size // 2
            upper = t >= mid
            if not rev:
                rng = range(mid, t + 1) if upper else range(t + 1, mid)
            else:
                rng = range(mid, t) if upper else range(t, mid)
            m[lvl, t, list(rng)] = 1.0
    for t in range(c):
        if not rev:
            m[6, t, :t + 1] = 1.0
            m[7, t, t + 1:] = 1.0
        else:
            m[6, t, t:] = 1.0
            m[7, t, :t] = 1.0
    m = m.reshape(8 * c, c)
    return np.concatenate([m, m], axis=1)


def _hgrn_chunk(q, k, logf, v, mlev, rev):
    c = HG_CHUNK
    hi = logf.astype(BF16)
    lo = (logf - hi.astype(F32)).astype(BF16)
    wgt = jnp.exp(_dot(mlev, jnp.concatenate([hi, lo], axis=0)))
    t = lax.broadcasted_iota(jnp.int32, (c, 256), 0)
    s_col = lax.broadcasted_iota(jnp.int32, (c, 256), 1) % c
    r_bd = lax.broadcasted_iota(jnp.int32, (256, 256), 0) // 64
    c_bd = lax.broadcasted_iota(jnp.int32, (256, 256), 1) // 64
    bd = r_bd == c_bd

    def block_diag(a):
        return jnp.where(bd, jnp.concatenate([a] * 4, axis=0), 0.0).astype(BF16)

    scores = jnp.where(t == s_col, _dot_nt(q.astype(BF16), block_diag(k)), 0.0)
    for lvl in range(HG_LEVELS):
        wl = wgt[c * lvl:c * (lvl + 1)]
        q_side = ((t >> (HG_LEVELS - 1 - lvl)) & 1) == (0 if rev else 1)
        ql = jnp.where(q_side, q * wl, 0.0).astype(BF16)
        kl = jnp.where(q_side, 0.0, k * wl)
        s = _dot_nt(ql, block_diag(kl))
        if lvl > 0:
            sh = HG_LEVELS - lvl
            s = jnp.where((t >> sh) == (s_col >> sh), s, 0.0)
        scores = scores + s
    q_in = (q * wgt[6 * c:7 * c]).astype(BF16)
    k_end = (k * wgt[7 * c:8 * c]).astype(BF16)
    o_intra = _dot(scores.astype(BF16), block_diag(v))
    edge = 6 * c if rev else 7 * c - 1
    total = wgt[edge:edge + 1]
    update = jnp.where(bd, _dot_tn(v.astype(BF16), k_end), 0.0)
    return o_intra, q_in, update, total


def _hgrn_kernel(mf_ref, mb_ref, lb_ref, qf_ref, zf_ref, vf_ref, qb_ref, zb_ref, vb_ref,
                 of_ref, ob_ref, stf_ref, stb_ref):
    c = HG_CHUNK
    nc = TM // c

    @pl.when(pl.program_id(1) == 0)
    def _():
        stf_ref[...] = jnp.zeros_like(stf_ref)
        stb_ref[...] = jnp.zeros_like(stb_ref)

    def scan(s, q_ref, z_ref, v_ref, lb, mlev, o_ref, st_ref, rev):
        sig = _sigmoid(z_ref[s])
        logf = jnp.log(jnp.maximum(lb + (1.0 - lb) * sig, GATE_FLOOR))
        k = (1.0 - lb) * (1.0 - sig)
        order = range(nc - 1, -1, -1) if rev else range(nc)
        rows = [slice(ci * c, (ci + 1) * c) for ci in order]
        parts = [_hgrn_chunk(q_ref[s, r, :].astype(F32), k[r], logf[r], v_ref[s, r, :].astype(F32), mlev, rev)
                 for r in rows]
        st = st_ref[s]
        for r, (o_intra, q_in, update, total) in zip(rows, parts):
            o_ref[s, r, :] = o_intra + _dot_nt(q_in, st.astype(BF16))
            st = st * total + update
        st_ref[s] = st

    for s in range(qf_ref.shape[0]):
        scan(s, qf_ref, zf_ref, vf_ref, lb_ref[0:1], mf_ref[...], of_ref, stf_ref, False)
        scan(s, qb_ref, zb_ref, vb_ref, lb_ref[1:2], mb_ref[...], ob_ref, stb_ref, True)


def _hgrn(lb, q, zff, zfb, v, n_lat):
    b, t, _ = q.shape
    nlb = n_lat // TM
    ncb = (t - n_lat) // TM
    fwd = lambda bi, i: (bi, jnp.where(i < ncb, nlb + i, i - ncb), 0)
    bwd = lambda bi, i: (bi, nlb + ncb - 1 - i, 0)
    nb = HG_BATCH if b % HG_BATCH == 0 else 1
    blk = lambda im: pl.BlockSpec((nb, TM, 256), im)
    mf = jnp.asarray(_hgrn_level_matrix(False), BF16)
    mb = jnp.asarray(_hgrn_level_matrix(True), BF16)
    return pl.pallas_call(
        _hgrn_kernel,
        out_shape=[jax.ShapeDtypeStruct((b, t, 256), F32)] * 2,
        grid=(b // nb, t // TM),
        in_specs=[_const_spec((8 * HG_CHUNK, 2 * HG_CHUNK)), _const_spec((8 * HG_CHUNK, 2 * HG_CHUNK)),
                  _const_spec((2, 256)),
                  blk(fwd), blk(fwd), blk(fwd), blk(bwd), blk(bwd), blk(bwd)],
        out_specs=[blk(fwd), blk(bwd)],
        scratch_shapes=[pltpu.VMEM((nb, 256, 256), F32), pltpu.VMEM((nb, 256, 256), F32)],
        compiler_params=_cparams(("parallel", "arbitrary")),
        name="hgrn",
    )(mf, mb, lb, q, zff, v, q, zfb, v)


def _key_chunks(n_lat, t_all):
    chunks = [(lo, min(lo + KV_CHUNK, t_all)) for lo in range(0, t_all, KV_CHUNK)]
    return chunks, [(n_lat, t_all)]


def _online_softmax(lhs, k_ref, v_ref, chunks):
    rows = lhs.shape[0]
    m = jnp.full((rows, 1), -jnp.inf, F32)
    l = jnp.zeros((rows, 1), F32)
    acc = jnp.zeros((rows, LANES), F32)
    for lo, hi in chunks:
        s = _dot_nt(lhs, k_ref[0, lo:hi, :])
        m_new = jnp.maximum(m, jnp.max(s, axis=1, keepdims=True))
        alpha = jnp.exp2(m - m_new)
        p = jnp.exp2(s - m_new)
        l = alpha * l + jnp.sum(p, axis=1, keepdims=True)
        acc = alpha * acc + _dot(p.astype(BF16), v_ref[0, lo:hi, :])
        m = m_new
    return acc / l


def _mla_kernel(q_ref, k_ref, v_ref, o_ref, *, chunks):
    tq = q_ref.shape[1]
    q = q_ref[0]
    zero = jnp.zeros_like(q)
    lane2 = lax.broadcasted_iota(jnp.int32, q.shape, 1)
    lhs = jnp.concatenate([jnp.where(lane2 < LANES, q, zero), jnp.where(lane2 >= LANES, q, zero)], axis=0)
    o = _online_softmax(lhs, k_ref, v_ref, chunks)
    lane = lax.broadcasted_iota(jnp.int32, (tq, LANES), 1)
    o_ref[0] = jnp.where(lane < 64, o[:tq], o[tq:]).astype(o_ref.dtype)


def _diff_kernel(lam_ref, g_ref, q_ref, k_ref, v_ref, o_ref, *, chunks, lam_init):
    tq = q_ref.shape[1]
    q = q_ref[0]
    zero = jnp.zeros_like(q)
    lane = lax.broadcasted_iota(jnp.int32, (tq, LANES), 1)
    lhs = jnp.concatenate([jnp.where(lane // D_HEAD_DIM == r, q, zero) for r in range(4)], axis=0)
    o = _online_softmax(lhs, k_ref, v_ref, chunks)
    lam = lam_ref[0]
    o0 = o[0:tq] - lam * o[tq:2 * tq]
    o1 = o[2 * tq:3 * tq] - lam * o[3 * tq:4 * tq]
    o = jnp.where(lane < 64, o0, o1)
    o_ref[0] = (_head_rms(o, g_ref[...], LANES) * (1.0 - lam_init)).astype(o_ref.dtype)


def _attention_call(body, name, extras, extra_specs, q, k, v, tq, row0, n_rows, out_init):
    b, t, qw = q.shape
    blk0 = row0 // tq
    n_extra = len(extras)
    in_specs = list(extra_specs) + [
        pl.BlockSpec((1, tq, qw // 2), lambda bi, p, i: (bi, blk0 + i, p)),
        pl.BlockSpec((1, t, k.shape[2] // 2), lambda bi, p, i: (bi, 0, p)),
        pl.BlockSpec((1, t, LANES), lambda bi, p, i: (bi, 0, p))]
    args = list(extras) + [q, k, v]
    aliases = {}
    if out_init is not None:
        in_specs.append(pl.BlockSpec(memory_space=pl.ANY))
        aliases = {len(args): 0}
        args.append(out_init)

    def kern(*refs):
        body(*refs[:n_extra + 3], refs[-1])

    return pl.pallas_call(
        kern,
        out_shape=jax.ShapeDtypeStruct((b, t, 256), BF16),
        grid=(b, 2, n_rows // tq),
        in_specs=in_specs,
        out_specs=pl.BlockSpec((1, tq, LANES), lambda bi, p, i: (bi, blk0 + i, p)),
        input_output_aliases=aliases,
        compiler_params=_cparams(("parallel", "parallel", "arbitrary")),
        name=name,
    )(*args)


def _two_pass_attention(body_of, name, extras, extra_specs, q, k, v, n_lat, tq_lat, with_ctx):
    t = q.shape[1]
    lat_chunks, ctx_chunks = _key_chunks(n_lat, t)
    out = jnp.zeros((q.shape[0], t, 256), BF16)
    out = _attention_call(body_of(lat_chunks), name, extras, extra_specs, q, k, v, tq_lat, 0, n_lat, out)
    if not with_ctx:
        return out
    return _attention_call(body_of(ctx_chunks), name + "_ctx", extras, extra_specs, q, k, v,
                           TQ, n_lat, t - n_lat, out)


def _mla_attention(q, k, v, n_lat, with_ctx):
    body_of = lambda chunks: functools.partial(_mla_kernel, chunks=chunks)
    return _two_pass_attention(body_of, "mla_attention", [], [], q, k, v, n_lat, TQ_MLA, with_ctx)


def _diff_attention(lam, q, k, v, gain, n_lat, lam_init, with_ctx):
    body_of = lambda chunks: functools.partial(_diff_kernel, chunks=chunks, lam_init=lam_init)
    specs = [pl.BlockSpec(memory_space=pltpu.SMEM), _const_spec((1, LANES))]
    return _two_pass_attention(body_of, "diff_attention", [lam, gain], specs, q, k, v, n_lat, TQ, with_ctx)


def _merge_kernel(x_ref, mod_ref, n1_ref, n2_ref, a_ref, of_ref, ob_ref, bg_ref, hg_ref, c_ref, d_ref,
                  wg_ref, wb_ref, wo_ref, wr_ref, br_ref, xo_ref, h2_ref, ri_ref, rw_ref):
    d = D_MODEL
    nb, tm, _ = x_ref.shape
    rows = nb * tm
    flat = lambda a: a.reshape(rows, a.shape[-1])
    x = x_ref[...]
    m = mod_ref[:, 0]
    mrow = lambda k: m[:, k:k + 1, :]
    h = flat((_rms(x, n1_ref[...]) * (1.0 + mrow(1)) + mrow(0)).astype(BF16))
    g = flat(bg_ref[...]).astype(F32)
    b_out = _head_rms(flat(of_ref[...] + ob_ref[...]), hg_ref[...], 256) * (g * _sigmoid(g))
    branches = (flat(a_ref[...]), b_out.astype(BF16), flat(c_ref[...]), flat(d_ref[...]))
    y = jnp.zeros((rows, d), F32)
    for i, br in enumerate(branches):
        gate = _sigmoid(_dot(h, wg_ref[:, d * i:d * (i + 1)]))
        y = y + gate * _dot(br, wb_ref[i])
    x = x + mrow(2) * _dot(y.astype(BF16), wo_ref[...]).reshape(nb, tm, d)
    xo_ref[...] = x
    h2 = flat(_rms(x, n2_ref[...]) * (1.0 + mrow(4)) + mrow(3))
    h2b = h2.astype(BF16)
    h2_ref[...] = _rows_first(h2).reshape(nb, tm, ROW_SLABS, LANES)

    h2lo = (h2 - h2b.astype(F32)).astype(BF16)
    r = _dot(jnp.concatenate([h2b, h2lo], axis=0), wr_ref[...])
    logits = r[:rows, :LANES] + r[:rows, LANES:] + r[rows:, :LANES] + br_ref[...]
    lane = lax.broadcasted_iota(jnp.int32, (rows, LANES), 1)
    neg = -jnp.inf
    is_g = (lane >= N_EXPERTS) & (lane < N_EXPERTS + N_GROUPS)
    lg = jnp.where(is_g, logits, neg)
    mg = jnp.max(lg, axis=1, keepdims=True)
    g_val = 1.0 / jnp.sum(jnp.exp(lg - mg), axis=1, keepdims=True)
    g_idx = jnp.min(jnp.where(lg == mg, lane, 4 * LANES), axis=1, keepdims=True) - N_EXPERTS
    in_group = (lane >= g_idx * EXPERTS_PER_GROUP) & (lane < (g_idx + 1) * EXPERTS_PER_GROUP)
    le = jnp.where(in_group, logits, neg)
    m1 = jnp.max(le, axis=1, keepdims=True)
    e1 = jnp.min(jnp.where(le == m1, lane, 4 * LANES), axis=1, keepdims=True)
    le2 = jnp.where(lane == e1, neg, le)
    m2 = jnp.max(le2, axis=1, keepdims=True)
    e2 = jnp.min(jnp.where(le2 == m2, lane, 4 * LANES), axis=1, keepdims=True)
    r2 = jnp.exp(m2 - m1)
    v1 = g_val / (1.0 + r2)
    ri_ref[...] = jnp.where(lane == 0, e1, jnp.where(lane == 1, e2, 0)).reshape(nb, tm, LANES)
    rw_ref[...] = jnp.where(lane == 0, v1, jnp.where(lane == 1, v1 * r2, 0.0)).reshape(nb, tm, LANES)


def _merge(x, mod, n1, n2, a_o, o_f, o_b, bg, hg, c_o, d_o, wg, wb, wo, wr, br, n_lat_tiles, t):
    b, _, d = x.shape
    tm = TM_PROJ
    nb = MERGE_BATCH if b % MERGE_BATCH == 0 else 1
    tile = lambda w: pl.BlockSpec((nb, tm, w), lambda bi, i: (bi, i, 0))
    return pl.pallas_call(
        _merge_kernel,
        out_shape=[jax.ShapeDtypeStruct((b, t, d), F32), jax.ShapeDtypeStruct((b, t, ROW_SLABS, LANES), F32),
                   jax.ShapeDtypeStruct((b, t, LANES), jnp.int32), jax.ShapeDtypeStruct((b, t, LANES), F32)],
        grid=(b // nb, pl.cdiv(t, tm)),
        in_specs=[tile(d), pl.BlockSpec((nb, 1, 8, d), lambda bi, i: (bi, i // n_lat_tiles, 0, 0)),
                  _const_spec((1, d)), _const_spec((1, d)),
                  tile(256), tile(256), tile(256), tile(256), _const_spec((1, 256)), tile(256), tile(256),
                  _const_spec((d, 4 * d)), _const_spec((4, 256, d)), _const_spec((d, d)),
                  _const_spec((d, 256)), _const_spec((1, LANES))],
        out_specs=[tile(d), pl.BlockSpec((nb, tm, ROW_SLABS, LANES), lambda bi, i: (bi, i, 0, 0)),
                   tile(LANES), tile(LANES)],
        compiler_params=_cparams(("parallel", "parallel")),
        name="merge",
    )(x, mod, n1, n2, a_o, o_f, o_b, bg, hg, c_o, d_o, wg, wb, wo, wr, br)


def _route_positions(ri, n_tok):
    ef = ri[:, :2].reshape(-1)
    rb = 2 * TM
    oh = (ef[:, None] == jnp.arange(N_EXPERTS, dtype=jnp.int32)[None, :]).astype(F32).reshape(-1, rb, N_EXPERTS)
    tri = (jnp.arange(rb)[:, None] >= jnp.arange(rb)[None, :]).astype(F32)
    within = jnp.einsum("ij,gje->gie", tri, oh)
    tot = within[:, -1, :]
    before = jnp.cumsum(tot, axis=0) - tot
    cnt = jnp.sum(tot, axis=0).astype(jnp.int32)
    pcnt = ((cnt + MOE_TILE - 1) // MOE_TILE) * MOE_TILE
    end = jnp.cumsum(pcnt)
    start = (end - pcnt).astype(F32)
    pos = jnp.sum((within + (before + start[None, :])[:, None, :]) * oh, axis=2) - 1.0
    pos = pos.reshape(-1)
    n_tiles = 2 * n_tok // MOE_TILE + N_EXPERTS
    n_act = (end[-1] // MOE_TILE).astype(jnp.int32)
    tile_row = jnp.minimum(jnp.arange(n_tiles, dtype=jnp.int32), n_act - 1) * MOE_TILE
    tile_expert = jnp.sum((end[None, :] <= tile_row[:, None]).astype(jnp.int32), axis=1)
    tile_expert = jnp.minimum(tile_expert, N_EXPERTS - 1)
    return pos.reshape(n_tok // TM, 1, 2 * TM).astype(jnp.int32), tile_expert, n_act.reshape(1)


def _dispatch_kernel(pos_ref, h_ref, xs_in_ref, xs_ref, sem):
    del xs_in_ref

    def body(r, carry):
        for s in range(2):
            p = pos_ref[0, 0, 2 * r + s]
            pltpu.make_async_copy(h_ref.at[pl.ds(r, 1)], xs_ref.at[pl.ds(p, 1)], sem.at[s]).start(priority=s)
        return carry

    lax.fori_loop(0, TM, body, 0)
    for s in range(2):
        pltpu.make_async_copy(h_ref, h_ref, sem.at[s]).wait()


def _dispatch(pos, h2, xs_init):
    n = h2.shape[0]
    return pl.pallas_call(
        _dispatch_kernel,
        out_shape=jax.ShapeDtypeStruct(xs_init.shape, F32),
        grid=(n // TM,),
        in_specs=[pl.BlockSpec((1, 1, 2 * TM), lambda t: (t, 0, 0), memory_space=pltpu.SMEM),
                  pl.BlockSpec((TM, ROW_SLABS, LANES), lambda t: (t, 0, 0)),
                  pl.BlockSpec(memory_space=pl.ANY)],
        out_specs=pl.BlockSpec(memory_space=pl.ANY),
        scratch_shapes=[pltpu.SemaphoreType.DMA((2,))],
        input_output_aliases={2: 0},
        compiler_params=_cparams(("arbitrary",)),
        name="moe_dispatch",
    )(pos, h2, xs_init)


def _experts_kernel(te_ref, na_ref, x_ref, wg_ref, wu_ref, wd_ref, o_ref):
    del te_ref
    t = pl.program_id(0)

    @pl.when(t < na_ref[0])
    def _():
        x = _slabs_first(x_ref[...])
        xb = jnp.concatenate([x[s] for s in range(ROW_SLABS)], axis=1).astype(BF16)
        gt = _dot(xb, wg_ref[0])
        up = _dot(xb, wu_ref[0])
        o_ref[...] = _rows_first(_dot((gt * _sigmoid(gt) * up).astype(BF16), wd_ref[0]))

    @pl.when(t >= na_ref[0])
    def _():
        o_ref[...] = jnp.zeros_like(o_ref)


def _experts(tile_expert, n_act, xs, wg, wu, wd):
    n_tiles = tile_expert.shape[0]
    d = D_MODEL
    rows = pl.BlockSpec((MOE_TILE, ROW_SLABS, LANES), lambda t, te, na: (t, 0, 0))
    w_in = pl.BlockSpec((1, d, EXPERT_HIDDEN), lambda t, te, na: (te[t], 0, 0))
    return pl.pallas_call(
        _experts_kernel,
        out_shape=jax.ShapeDtypeStruct((n_tiles * MOE_TILE, ROW_SLABS, LANES), F32),
        grid_spec=pltpu.PrefetchScalarGridSpec(
            num_scalar_prefetch=2, grid=(n_tiles,),
            in_specs=[rows, w_in, w_in,
                      pl.BlockSpec((1, EXPERT_HIDDEN, d), lambda t, te, na: (te[t], 0, 0))],
            out_specs=rows),
        compiler_params=_cparams(("arbitrary",)),
        name="moe_experts",
    )(tile_expert, n_act, xs, wg, wu, wd)


def _combine_kernel(pos_ref, posn_ref, ys_ref, w_ref, x_ref, mod_ref, g_ref, o_ref, buf, sem, *, final, n_steps):
    g = pl.program_id(0) * pl.num_programs(1) + pl.program_id(1)
    cur = g % 2

    def start_gather(p_ref, par):
        def body(r, carry):
            for s in range(2):
                p = p_ref[0, 0, 2 * r + s]
                pltpu.make_async_copy(ys_ref.at[pl.ds(p, 1)], buf.at[par, s, pl.ds(r, 1)],
                                      sem.at[par, s]).start(priority=s)
            return carry

        lax.fori_loop(0, TM, body, 0)

    @pl.when(g == 0)
    def _():
        start_gather(pos_ref, 0)

    @pl.when(g + 1 < n_steps)
    def _():
        start_gather(posn_ref, 1 - cur)

    for s in range(2):
        pltpu.make_async_copy(buf.at[cur, s], buf.at[cur, s], sem.at[cur, s]).wait()
    w = w_ref[0]
    y0, y1 = _slabs_first(buf[cur, 0]), _slabs_first(buf[cur, 1])
    f = jnp.concatenate([w[:, 0:1] * y0[k] + w[:, 1:2] * y1[k] for k in range(ROW_SLABS)], axis=1)
    x = x_ref[0] + mod_ref[0, 0, 5:6, :] * f
    o_ref[0] = _rms(x, g_ref[...]) if final else x


def _combine(pos, ys, rwt, x, mod, gain, n_lat):
    b, t, d = x.shape
    nt = t // TM
    n_steps = b * nt
    tile = lambda w: pl.BlockSpec((1, TM, w), lambda bi, i: (bi, i, 0))
    final = gain is not None
    pos_spec = lambda ahead: pl.BlockSpec(
        (1, 1, 2 * TM), lambda bi, i: (jnp.minimum(bi * nt + i + ahead, n_steps - 1), 0, 0),
        memory_space=pltpu.SMEM)
    return pl.pallas_call(
        functools.partial(_combine_kernel, final=final, n_steps=n_steps),
        out_shape=jax.ShapeDtypeStruct((b, t, d), F32),
        grid=(b, nt),
        in_specs=[pos_spec(0), pos_spec(1),
                  pl.BlockSpec(memory_space=pl.ANY),
                  tile(LANES), tile(d),
                  pl.BlockSpec((1, 1, 8, d), lambda bi, i: (bi, i // (n_lat // TM), 0, 0)),
                  _const_spec((1, d))],
        out_specs=tile(d),
        scratch_shapes=[pltpu.VMEM((2, 2, TM, ROW_SLABS, LANES), F32), pltpu.SemaphoreType.DMA((2, 2))],
        compiler_params=_cparams(("arbitrary", "arbitrary")),
        name="moe_combine",
    )(pos, pos, ys, rwt, x, mod, gain if final else jnp.ones((1, d), F32))


def _moe(x, h2, ri, rwt, wg, wu, wd, xs_init, mod, gain, n_lat):
    b, t, _ = x.shape
    n = b * t
    pos, tile_expert, n_act = _route_positions(ri.reshape(n, LANES), n)
    xs = _dispatch(pos, h2.reshape(n, ROW_SLABS, LANES), xs_init)
    ys = _experts(tile_expert, n_act, xs, wg, wu, wd)
    return _combine(pos, ys, rwt, x, mod, gain, n_lat), xs


def _rope_tables(n_lat, t_all, lane_rope, lane_off, dr):
    half, quarter = dr // 2, dr // 4
    inv_freq = 1.0 / (ROPE_BASE ** (jnp.arange(quarter, dtype=F32) / quarter))
    off = np.asarray(lane_off)
    use_col = off >= half
    j = (off % half) % quarter
    first = (off % half) < quarter
    tok = jnp.arange(n_lat, dtype=jnp.int32)
    row = (tok // GRID_W).astype(F32)
    col = (tok % GRID_W).astype(F32)
    pos = jnp.where(jnp.asarray(use_col)[None, :], col[:, None], row[:, None])
    ang = pos * inv_freq[jnp.asarray(j)][None, :]
    rope = jnp.asarray(lane_rope)[None, :]
    cos = jnp.where(rope, jnp.cos(ang), 1.0)
    sin = jnp.where(rope, jnp.sin(ang) * jnp.where(jnp.asarray(first), -1.0, 1.0)[None, :], 0.0)
    pad = t_all - n_lat
    w = off.shape[0]
    cos = jnp.concatenate([cos, jnp.ones((pad, w), F32)], axis=0)
    sin = jnp.concatenate([sin, jnp.zeros((pad, w), F32)], axis=0)
    return cos, sin


def _all_rope_tables(n_lat, t_all):
    rep = lambda tabs, n: tuple(jnp.tile(tb, (1, n)) for tb in tabs)
    ta = rep(_rope_tables(n_lat, t_all, np.ones(64, bool), np.arange(64), 64), 4)
    lc = np.arange(128)
    tc = rep(_rope_tables(n_lat, t_all, (lc >= 64) & (lc < 96), np.clip(lc - 64, 0, 31), 32), 4)
    td = rep(_rope_tables(n_lat, t_all, np.ones(32, bool), np.arange(32), 32), 8)
    return (*ta, *tc, *td)


def _permute_w_in(w):
    s = np.cumsum([0, 256, 128, 128, 256, 256, 256, 256, 256, 256, 128, 32, 256, 256, 256])
    w = w.astype(BF16)
    seg = lambda i: w[:, s[i]:s[i + 1]]
    z64 = jnp.zeros((w.shape[0], 64), w.dtype)
    z32 = jnp.zeros((w.shape[0], 32), w.dtype)
    kr = jnp.concatenate([z64, seg(10), z32], axis=1)
    cols = [seg(0), seg(1), seg(2),
            seg(3), seg(4), seg(5), seg(6), seg(7),
            seg(8), seg(9), kr,
            seg(11), seg(12), seg(13)]
    return jnp.concatenate(cols, axis=1)


def _permute_mla(w_uq, w_ukv):
    z32 = jnp.zeros((w_uq.shape[0], 32), w_uq.dtype)
    qd = C_NOPE + C_ROPE
    uq = jnp.concatenate([a for h in range(C_HEADS) for a in (w_uq[:, qd * h:qd * (h + 1)], z32)], axis=1)
    z64 = jnp.zeros((w_ukv.shape[0], 64), w_ukv.dtype)
    kd = C_NOPE + C_V
    uk = jnp.concatenate([a for h in range(C_HEADS) for a in (w_ukv[:, kd * h:kd * h + C_NOPE], z64)], axis=1)
    uv = jnp.concatenate([w_ukv[:, kd * h + C_NOPE:kd * (h + 1)] for h in range(C_HEADS)], axis=1)
    return uq.astype(BF16), jnp.concatenate([uk, uv], axis=1).astype(BF16)


def kernel(x, c, ctx, c_ctx, w_mod, b_mod, norm1, norm2, w_in, w_gate, w_branch, w_out, attn_sink, hgrn_lb_logits, hgrn_norm, mla_q_norm, mla_kv_norm, mla_w_uq, mla_w_ukv, diff_lambda, diff_subln, w_router_group, b_router_group, w_router_expert, b_router_expert, w_expert_gate, w_expert_up, w_expert_down, final_norm):
    b, n_lat, d = x.shape
    n_ctx = ctx.shape[1]
    depth = w_mod.shape[0]
    t_all = n_lat + n_ctx
    assert d == D_MODEL and n_lat % TM == 0 and n_ctx % TM == 0 and n_lat % GRID_W == 0 and n_ctx <= n_lat
    assert n_lat >= WIN_Q + 2 * A_WINDOW and n_lat % TQ_MLA == 0 and n_lat % TM_PROJ == 0 and b <= 15
    n_lat_tiles = n_lat // TM_PROJ

    c_all = jnp.concatenate([c, c_ctx[None], jnp.zeros((15 - b, d), F32)], axis=0)
    mod = _modulation(c_all, w_mod, b_mod).reshape(depth, 16, 6, d)
    mod = jnp.pad(mod, ((0, 0), (0, 0), (0, 2), (0, 0)))
    mod = jnp.stack([mod[:, :b], jnp.broadcast_to(mod[:, b:b + 1], (depth, b, 8, d))], axis=2)

    sm = jax.nn.softmax(hgrn_lb_logits.astype(F32), axis=0)
    lower_bounds = jnp.cumsum(sm, axis=0) - sm[0]
    tabs = _all_rope_tables(n_lat, t_all)

    n_tok = b * t_all
    moe_buf = jnp.zeros((2 * n_tok + N_EXPERTS * MOE_TILE, ROW_SLABS, LANES), F32)
    xs = x
    for li in range(depth):
        lam_init = 0.8 - 0.6 * math.exp(-0.3 * li)
        lp = diff_lambda[li].astype(F32)
        lam = (jnp.exp(jnp.sum(lp[0] * lp[1])) - jnp.exp(jnp.sum(lp[2] * lp[3])) + lam_init).reshape(1)
        w_perm = _permute_w_in(w_in[li])
        wuq, wukv = _permute_mla(mla_w_uq[li], mla_w_ukv[li])
        xs, z = _inproj(xs, ctx if li == 0 else None, mod[li], norm1[li][None], w_perm,
                        mla_q_norm[li][None], mla_kv_norm[li][None], wuq, wukv, tabs, n_lat_tiles)
        qa, ka, va, bq, bff, bfb, bv, bg, cq, ck, cv, dq, dk, dv = z
        a_o = _win_attention(attn_sink[li].astype(F32), qa, ka, va, n_lat)
        o_f, o_b = _hgrn(lower_bounds[li], bq, bff, bfb, bv, n_lat)
        with_ctx = li < depth - 1
        c_o = _mla_attention(cq, ck, cv, n_lat, with_ctx)
        d_o = _diff_attention(lam, dq, dk, dv, jnp.tile(diff_subln[li], 2)[None], n_lat, lam_init, with_ctx)
        wr = jnp.concatenate([w_router_expert[li], w_router_group[li],
                              jnp.zeros((d, LANES - N_EXPERTS - N_GROUPS), F32)], axis=1)
        wr_hi = wr.astype(BF16)
        wr_lo = (wr - wr_hi.astype(F32)).astype(BF16)
        br = jnp.concatenate([b_router_expert[li], b_router_group[li],
                              jnp.zeros((LANES - N_EXPERTS - N_GROUPS,), F32)])[None]
        t_keep = t_all if with_ctx else n_lat
        xs, h2, ri, rwt = _merge(xs, mod[li], norm1[li][None], norm2[li][None], a_o, o_f, o_b, bg,
                                 jnp.tile(hgrn_norm[li], B_HEADS)[None], c_o, d_o,
                                 w_gate[li].astype(BF16), w_branch[li].astype(BF16), w_out[li].astype(BF16),
                                 jnp.concatenate([wr_hi, wr_lo], axis=1), br, n_lat_tiles, t_keep)
        xs, moe_buf = _moe(xs, h2, ri, rwt, w_expert_gate[li].astype(BF16), w_expert_up[li].astype(BF16),
                           w_expert_down[li].astype(BF16), moe_buf, mod[li],
                           None if with_ctx else final_norm[None], n_lat)
    return xs
```

```python
import functools
import math

import numpy as np
import jax
import jax.numpy as jnp
from jax import lax
from jax.experimental import pallas as pl
from jax.experimental.pallas import tpu as pltpu

D_MODEL = 1024
GRID_W = 64
ROPE_BASE = 10000.0
NORM_EPS = 1e-6
MASK_VALUE = -1e30
GATE_FLOOR = 1e-30

A_HEADS, A_KV_HEADS, A_HEAD_DIM, A_WINDOW = 4, 2, 64, 128
B_HEADS, B_KEY_DIM, B_VAL_DIM = 4, 64, 64
C_HEADS, C_Q_LORA, C_KV_LORA, C_NOPE, C_ROPE, C_V = 4, 256, 128, 64, 32, 64
D_HEADS, D_HEAD_DIM = 4, 32
N_GROUPS, EXPERTS_PER_GROUP, EXPERT_HIDDEN = 4, 8, 256
N_EXPERTS = N_GROUPS * EXPERTS_PER_GROUP

TM = 256
TM_PROJ = 256
WIN_Q = 256
TQ = 256
TQ_MLA = 512
LOG2E = 1.4426950408889634
KV_CHUNK = 4096
MOE_TILE = 512
HG_BATCH = 2
MERGE_BATCH = 2
HG_CHUNK = 64
HG_LEVELS = 6
LANES = 128
ROW_SLABS = D_MODEL // LANES
VMEM_LIMIT = 52 * 1024 * 1024

OFF_A, OFF_B, OFF_C, OFF_D, N_PERM = 0, 512, 1792, 2304, 3072

F32 = jnp.float32
BF16 = jnp.bfloat16


def _dot(a, b):
    return jnp.dot(a, b, preferred_element_type=F32)


def _dot_nt(a, b):
    return lax.dot_general(a, b, (((1,), (1,)), ((), ())), preferred_element_type=F32)


def _dot_tn(a, b):
    return lax.dot_general(a, b, (((0,), (0,)), ((), ())), preferred_element_type=F32)


def _dot_hi(a, b):
    return jnp.dot(a, b, preferred_element_type=F32, precision=lax.Precision.HIGHEST)


def _sigmoid(x):
    return 1.0 / (1.0 + jnp.exp(-x))


def _rms(x, gain):
    return x * lax.rsqrt(jnp.mean(x * x, axis=-1, keepdims=True) + NORM_EPS) * gain


def _head_rms(o, gain, width):
    rows = o.shape[0]
    r = lax.broadcasted_iota(jnp.int32, (width, width), 0) // 64
    c = lax.broadcasted_iota(jnp.int32, (width, width), 1) // 64
    ones = jnp.where(r == c, 1.0 / 64.0, 0.0).astype(BF16)
    sq = o * o
    hi = sq.astype(BF16)
    lo = (sq - hi.astype(F32)).astype(BF16)
    ms = _dot(jnp.concatenate([hi, lo], axis=0), ones)
    return o * lax.rsqrt(ms[:rows] + ms[rows:] + NORM_EPS) * gain


def _rope(x, cos, sin, quarter):
    w = x.shape[-1]
    lane = lax.broadcasted_iota(jnp.int32, x.shape, 1)
    first = (lane % (2 * quarter)) < quarter
    sw = jnp.where(first, pltpu.roll(x, w - quarter, 1), pltpu.roll(x, quarter, 1))
    return x * cos + sw * sin


def _dup_heads(x):
    lane = lax.broadcasted_iota(jnp.int32, x.shape, 1)
    sw = pltpu.roll(x, 64, 1)
    return jnp.concatenate([jnp.where(lane < 64, x, sw), jnp.where(lane < 64, sw, x)], axis=1)


def _slabs_first(x):
    return pltpu.einshape("tsl->stl", x)


def _rows_first(y):
    slabs = jnp.stack([y[:, LANES * s:LANES * (s + 1)] for s in range(ROW_SLABS)], axis=0)
    return pltpu.einshape("stl->tsl", slabs)


def _cparams(sem):
    return pltpu.CompilerParams(dimension_semantics=sem, vmem_limit_bytes=VMEM_LIMIT)


def _const_spec(shape):
    n = len(shape)
    return pl.BlockSpec(shape, lambda *_: (0,) * n)


def _mod_kernel(c_ref, w_ref, b_ref, o_ref):
    c = c_ref[...]
    o_ref[0] = _dot_hi(c * _sigmoid(c), w_ref[0]) + b_ref[0]


def _modulation(c_all, w_mod, b_mod):
    depth, d, n = w_mod.shape
    nb = 1536
    return pl.pallas_call(
        _mod_kernel,
        out_shape=jax.ShapeDtypeStruct((depth, 16, n), F32),
        grid=(depth, n // nb),
        in_specs=[pl.BlockSpec((16, d), lambda l, j: (0, 0)),
                  pl.BlockSpec((1, d, nb), lambda l, j: (l, 0, j)),
                  pl.BlockSpec((1, 1, nb), lambda l, j: (l, 0, j))],
        out_specs=pl.BlockSpec((1, 16, nb), lambda l, j: (l, 0, j)),
        compiler_params=_cparams(("arbitrary", "arbitrary")),
        name="modulation",
    )(c_all, w_mod, b_mod.reshape(depth, 1, n))


def _inproj_kernel(*refs, first, n_lat_tiles):
    if not first:
        x = refs[0][...]
        refs = refs[1:]
    else:
        x_ref, ctx_ref = refs[:2]
        refs = refs[2:]
        ctx = ctx_ref[...]
        if ctx.shape[1] < x_ref.shape[1]:
            ctx = jnp.concatenate([ctx] * (x_ref.shape[1] // ctx.shape[1]), axis=1)
        x = jnp.where(pl.program_id(1) < n_lat_tiles, x_ref[...], ctx)
    (mod_ref, n1_ref, w_ref, qn_ref, kvn_ref, wuq_ref, wukv_ref,
     cosa_ref, sina_ref, cosc_ref, sinc_ref, cosd_ref, sind_ref) = refs[:13]
    outs = refs[13:]
    nb, tm, d = x.shape

    def put(ref, val):
        ref[...] = val.reshape(ref.shape)

    if first:
        put(outs[0], x)
        outs = outs[1:]
    (qa_ref, ka_ref, va_ref, bq_ref, bff_ref, bfb_ref, bv_ref, bg_ref,
     cq_ref, ck_ref, cv_ref, dq_ref, dk_ref, dv_ref) = outs
    m = mod_ref[:, 0]
    h = (_rms(x, n1_ref[...]) * (1.0 + m[:, 1:2, :]) + m[:, 0:1, :]).astype(BF16).reshape(nb * tm, d)
    tab = lambda ref: jnp.concatenate([ref[...]] * nb, axis=0)

    z = _dot(h, w_ref[:, OFF_A:OFF_A + 512])
    cosa, sina = tab(cosa_ref), tab(sina_ref)
    put(qa_ref, (_rope(z[:, 0:256], cosa, sina, 16) * (A_HEAD_DIM ** -0.5 * LOG2E)).astype(BF16))
    put(ka_ref, _dup_heads(_rope(z[:, 256:384], cosa[:, :LANES], sina[:, :LANES], 16)).astype(BF16))
    put(va_ref, _dup_heads(z[:, 384:512]).astype(BF16))

    z = _dot(h, w_ref[:, OFF_B:OFF_B + 1280])
    put(bq_ref, z[:, 0:256].astype(BF16))
    put(bff_ref, z[:, 256:512])
    put(bfb_ref, z[:, 512:768])
    put(bv_ref, z[:, 768:1024].astype(BF16))
    put(bg_ref, z[:, 1024:1280].astype(BF16))

    z = _dot(h, w_ref[:, OFF_C:OFF_C + 512])
    cosc, sinc = tab(cosc_ref), tab(sinc_ref)
    cq = _rms(z[:, 0:256], qn_ref[...]).astype(BF16)
    q = _rope(_dot(cq, wuq_ref[...]), cosc, sinc, 8)
    put(cq_ref, (q * ((C_NOPE + C_ROPE) ** -0.5 * LOG2E)).astype(BF16))
    ckv = _rms(z[:, 256:384], kvn_ref[...]).astype(BF16)
    kv = _dot(ckv, wukv_ref[...])
    k_rope = _rope(z[:, 384:512], cosc[:, :LANES], sinc[:, :LANES], 8)
    put(ck_ref, (kv[:, 0:512] + jnp.concatenate([k_rope] * C_HEADS, axis=1)).astype(BF16))
    put(cv_ref, kv[:, 512:768].astype(BF16))

    z = _dot(h, w_ref[:, OFF_D:OFF_D + 768])
    cosd, sind = tab(cosd_ref), tab(sind_ref)
    put(dq_ref, (_rope(z[:, 0:256], cosd, sind, 8) * (D_HEAD_DIM ** -0.5 * LOG2E)).astype(BF16))
    put(dk_ref, _rope(z[:, 256:512], cosd, sind, 8).astype(BF16))
    put(dv_ref, z[:, 512:768].astype(BF16))


def _inproj(x, ctx, mod, n1, w_perm, qn, kvn, wuq, wukv, tabs, n_lat_tiles):
    first = ctx is not None
    b, _, d = x.shape
    t = x.shape[1] + (ctx.shape[1] if first else 0)
    tm = TM_PROJ
    nt = pl.cdiv(t, tm)
    nb = MERGE_BATCH if b % MERGE_BATCH == 0 else 1
    tile = lambda w: pl.BlockSpec((nb, tm, w), lambda bi, i: (bi, i, 0))
    modspec = pl.BlockSpec((nb, 1, 8, d), lambda bi, i: (bi, i // n_lat_tiles, 0, 0))
    tab = lambda w: pl.BlockSpec((tm, w), lambda bi, i: (i, 0))
    if first:
        lat = pl.BlockSpec((nb, tm, d), lambda bi, i: (bi, jnp.minimum(i, n_lat_tiles - 1), 0))
        ctx_spec = pl.BlockSpec((nb, min(tm, ctx.shape[1]), d),
                                lambda bi, i: (bi, jnp.maximum(i - n_lat_tiles, 0), 0))
        in_specs, args = [lat, ctx_spec], [x, ctx]
    else:
        in_specs, args = [tile(d)], [x]
    in_specs += [modspec, _const_spec((1, d)), _const_spec((d, N_PERM)), _const_spec((1, 256)),
                 _const_spec((1, 128)), _const_spec((256, 512)), _const_spec((128, 768)),
                 tab(256), tab(256), tab(512), tab(512), tab(256), tab(256)]
    args += [mod, n1, w_perm, qn, kvn, wuq, wukv, *tabs]
    widths = [(256, BF16), (256, BF16), (256, BF16),
              (256, BF16), (256, F32), (256, F32), (256, BF16), (256, BF16),
              (512, BF16), (512, BF16), (256, BF16),
              (256, BF16), (256, BF16), (256, BF16)]
    if first:
        widths = [(d, F32)] + widths
    out_shape = [jax.ShapeDtypeStruct((b, t, w), dt) for w, dt in widths]
    out_specs = [tile(w) for w, _ in widths]
    res = pl.pallas_call(
        functools.partial(_inproj_kernel, first=first, n_lat_tiles=n_lat_tiles),
        out_shape=out_shape, grid=(b // nb, nt), in_specs=in_specs, out_specs=out_specs,
        compiler_params=_cparams(("parallel", "parallel")),
        name="inproj",
    )(*args)
    return (res[0], res[1:]) if first else (x, res)


def _win_kernel(sink_ref, q_ref, k_ref, v_ref, o_ref, *, n_lat, t_all):
    w = A_WINDOW
    qb = w
    kb = qb + 2 * w
    lane = lax.broadcasted_iota(jnp.int32, (qb, LANES), 1)
    row = lax.broadcasted_iota(jnp.int32, (2 * qb, kb), 0)
    col = lax.broadcasted_iota(jnp.int32, (2 * qb, kb), 1)
    row1 = lax.broadcasted_iota(jnp.int32, (2 * qb, 1), 0)
    for blk, j in [(a, c) for a in range(WIN_Q // w) for c in range(A_KV_HEADS)]:
        n = pl.program_id(1) * (WIN_Q // w) + blk
        start = pl.multiple_of(jnp.clip(n * qb - w, 0, n_lat - kb), w)
        rel = (start + col) - (n * qb + row % qb)
        valid = (jnp.abs(rel) <= w) & (n < n_lat // qb)
        sl = slice(LANES * j, LANES * (j + 1))
        rs = slice(qb * blk, qb * (blk + 1))
        qp = q_ref[0, rs, sl]
        zero = jnp.zeros_like(qp)
        lhs = jnp.concatenate([jnp.where(lane < 64, qp, zero), jnp.where(lane >= 64, qp, zero)], axis=0)
        sb = _dot_nt(lhs, k_ref[0, pl.ds(start, kb), sl])
        sc = _dot_nt(lhs, k_ref[0, n_lat:t_all, sl])
        sb = jnp.where(valid, sb, MASK_VALUE)
        sink = jnp.where(row1 < qb, sink_ref[2 * j], sink_ref[2 * j + 1]) * LOG2E
        m = jnp.maximum(jnp.maximum(jnp.max(sb, axis=1, keepdims=True), jnp.max(sc, axis=1, keepdims=True)), sink)
        pb = jnp.exp2(sb - m)
        pc = jnp.exp2(sc - m)
        den = jnp.sum(pb, axis=1, keepdims=True) + jnp.sum(pc, axis=1, keepdims=True) + jnp.exp2(sink - m)
        o = (_dot(pb.astype(BF16), v_ref[0, pl.ds(start, kb), sl])
             + _dot(pc.astype(BF16), v_ref[0, n_lat:t_all, sl])) / den
        o_ref[0, rs, sl] = jnp.where(lane < 64, o[:qb], o[qb:]).astype(o_ref.dtype)


def _win_attention(sink, q, k, v, n_lat):
    b, t, _ = q.shape
    qb = WIN_Q
    return pl.pallas_call(
        functools.partial(_win_kernel, n_lat=n_lat, t_all=t),
        out_shape=jax.ShapeDtypeStruct((b, t, 256), BF16),
        grid=(b, t // qb),
        in_specs=[pl.BlockSpec(memory_space=pltpu.SMEM),
                  pl.BlockSpec((1, qb, 256), lambda bi, i: (bi, i, 0)),
                  pl.BlockSpec((1, t, 256), lambda bi, i: (bi, 0, 0)),
                  pl.BlockSpec((1, t, 256), lambda bi, i: (bi, 0, 0))],
        out_specs=pl.BlockSpec((1, qb, 256), lambda bi, i: (bi, i, 0)),
        compiler_params=_cparams(("parallel", "arbitrary")),
        name="win_attention",
    )(sink, q, k, v)


def _hgrn_level_matrix(rev):
    c = HG_CHUNK
    m = np.zeros((8, c, c), np.float32)
    for lvl in range(HG_LEVELS):
        size = c >> lvl
        for t in range(c):
            mid = (t // size) * size + size // 2
            upper = t >= mid
            if not rev:
                rng = range(mid, t + 1) if upper else range(t + 1, mid)
            else:
                rng = range(mid, t) if upper else range(t, mid)
            m[lvl, t, list(rng)] = 1.0
    for t in range(c):
        if not rev:
            m[6, t, :t + 1] = 1.0
            m[7, t, t + 1:] = 1.0
        else:
            m[6, t, t:] = 1.0
            m[7, t, :t] = 1.0
    m = m.reshape(8 * c, c)
    return np.concatenate([m, m], axis=1)


def _hgrn_chunk(q, k, logf, v, mlev, rev):
    c = HG_CHUNK
    hi = logf.astype(BF16)
    lo = (logf - hi.astype(F32)).astype(BF16)
    wgt = jnp.exp(_dot(mlev, jnp.concatenate([hi, lo], axis=0)))
    t = lax.broadcasted_iota(jnp.int32, (c, 256), 0)
    s_col = lax.broadcasted_iota(jnp.int32, (c, 256), 1) % c
    r_bd = lax.broadcasted_iota(jnp.int32, (256, 256), 0) // 64
    c_bd = lax.broadcasted_iota(jnp.int32, (256, 256), 1) // 64
    bd = r_bd == c_bd

    def block_diag(a):
        return jnp.where(bd, jnp.concatenate([a] * 4, axis=0), 0.0).astype(BF16)

    scores = jnp.where(t == s_col, _dot_nt(q.astype(BF16), block_diag(k)), 0.0)
    for lvl in range(HG_LEVELS):
        wl = wgt[c * lvl:c * (lvl + 1)]
        q_side = ((t >> (HG_LEVELS - 1 - lvl)) & 1) == (0 if rev else 1)
        ql = jnp.where(q_side, q * wl, 0.0).astype(BF16)
        kl = jnp.where(q_side, 0.0, k * wl)
        s = _dot_nt(ql, block_diag(kl))
        if lvl > 0:
            sh = HG_LEVELS - lvl
            s = jnp.where((t >> sh) == (s_col >> sh), s, 0.0)
        scores = scores + s
    q_in = (q * wgt[6 * c:7 * c]).astype(BF16)
    k_end = (k * wgt[7 * c:8 * c]).astype(BF16)
    o_intra = _dot(scores.astype(BF16), block_diag(v))
    edge = 6 * c if rev else 7 * c - 1
    total = wgt[edge:edge + 1]
    update = jnp.where(bd, _dot_tn(v.astype(BF16), k_end), 0.0)
    return o_intra, q_in, update, total


def _hgrn_kernel(mf_ref, mb_ref, lb_ref, qf_ref, zf_ref, vf_ref, qb_ref, zb_ref, vb_ref,
                 of_ref, ob_ref, stf_ref, stb_ref):
    c = HG_CHUNK
    nc = TM // c

    @pl.when(pl.program_id(1) == 0)
    def _():
        stf_ref[...] = jnp.zeros_like(stf_ref)
        stb_ref[...] = jnp.zeros_like(stb_ref)

    def scan(s, q_ref, z_ref, v_ref, lb, mlev, o_ref, st_ref, rev):
        sig = _sigmoid(z_ref[s])
        logf = jnp.log(jnp.maximum(lb + (1.0 - lb) * sig, GATE_FLOOR))
        k = (1.0 - lb) * (1.0 - sig)
        order = range(nc - 1, -1, -1) if rev else range(nc)
        rows = [slice(ci * c, (ci + 1) * c) for ci in order]
        parts = [_hgrn_chunk(q_ref[s, r, :].astype(F32), k[r], logf[r], v_ref[s, r, :].astype(F32), mlev, rev)
                 for r in rows]
        st = st_ref[s]
        for r, (o_intra, q_in, update, total) in zip(rows, parts):
            o_ref[s, r, :] = o_intra + _dot_nt(q_in, st.astype(BF16))
            st = st * total + update
        st_ref[s] = st

    for s in range(qf_ref.shape[0]):
        scan(s, qf_ref, zf_ref, vf_ref, lb_ref[0:1], mf_ref[...], of_ref, stf_ref, False)
        scan(s, qb_ref, zb_ref, vb_ref, lb_ref[1:2], mb_ref[...], ob_ref, stb_ref, True)


def _hgrn(lb, q, zff, zfb, v, n_lat):
    b, t, _ = q.shape
    nlb = n_lat // TM
    ncb = (t - n_lat) // TM
    fwd = lambda bi, i: (bi, jnp.where(i < ncb, nlb + i, i - ncb), 0)
    bwd = lambda bi, i: (bi, nlb + ncb - 1 - i, 0)
    nb = HG_BATCH if b % HG_BATCH == 0 else 1
    blk = lambda im: pl.BlockSpec((nb, TM, 256), im)
    mf = jnp.asarray(_hgrn_level_matrix(False), BF16)
    mb = jnp.asarray(_hgrn_level_matrix(True), BF16)
    return pl.pallas_call(
        _hgrn_kernel,
        out_shape=[jax.ShapeDtypeStruct((b, t, 256), F32)] * 2,
        grid=(b // nb, t // TM),
        in_specs=[_const_spec((8 * HG_CHUNK, 2 * HG_CHUNK)), _const_spec((8 * HG_CHUNK, 2 * HG_CHUNK)),
                  _const_spec((2, 256)),
                  blk(fwd), blk(fwd), blk(fwd), blk(bwd), blk(bwd), blk(bwd)],
        out_specs=[blk(fwd), blk(bwd)],
        scratch_shapes=[pltpu.VMEM((nb, 256, 256), F32), pltpu.VMEM((nb, 256, 256), F32)],
        compiler_params=_cparams(("parallel", "arbitrary")),
        name="hgrn",
    )(mf, mb, lb, q, zff, v, q, zfb, v)


def _key_chunks(n_lat, t_all):
    chunks = [(lo, min(lo + KV_CHUNK, t_all)) for lo in range(0, t_all, KV_CHUNK)]
    return chunks, [(n_lat, t_all)]


def _online_softmax(lhs, k_ref, v_ref, chunks):
    rows = lhs.shape[0]
    m = jnp.full((rows, 1), -jnp.inf, F32)
    l = jnp.zeros((rows, 1), F32)
    acc = jnp.zeros((rows, LANES), F32)
    for lo, hi in chunks:
        s = _dot_nt(lhs, k_ref[0, lo:hi, :])
        m_new = jnp.maximum(m, jnp.max(s, axis=1, keepdims=True))
        alpha = jnp.exp2(m - m_new)
        p = jnp.exp2(s - m_new)
        l = alpha * l + jnp.sum(p, axis=1, keepdims=True)
        acc = alpha * acc + _dot(p.astype(BF16), v_ref[0, lo:hi, :])
        m = m_new
    return acc / l


def _mla_kernel(q_ref, k_ref, v_ref, o_ref, *, chunks):
    tq = q_ref.shape[1]
    q = q_ref[0]
    zero = jnp.zeros_like(q)
    lane2 = lax.broadcasted_iota(jnp.int32, q.shape, 1)
    lhs = jnp.concatenate([jnp.where(lane2 < LANES, q, zero), jnp.where(lane2 >= LANES, q, zero)], axis=0)
    o = _online_softmax(lhs, k_ref, v_ref, chunks)
    lane = lax.broadcasted_iota(jnp.int32, (tq, LANES), 1)
    o_ref[0] = jnp.where(lane < 64, o[:tq], o[tq:]).astype(o_ref.dtype)


def _diff_kernel(lam_ref, g_ref, q_ref, k_ref, v_ref, o_ref, *, chunks, lam_init):
    tq = q_ref.shape[1]
    q = q_ref[0]
    zero = jnp.zeros_like(q)
    lane = lax.broadcasted_iota(jnp.int32, (tq, LANES), 1)
    lhs = jnp.concatenate([jnp.where(lane // D_HEAD_DIM == r, q, zero) for r in range(4)], axis=0)
    o = _online_softmax(lhs, k_ref, v_ref, chunks)
    lam = lam_ref[0]
    o0 = o[0:tq] - lam * o[tq:2 * tq]
    o1 = o[2 * tq:3 * tq] - lam * o[3 * tq:4 * tq]
    o = jnp.where(lane < 64, o0, o1)
    o_ref[0] = (_head_rms(o, g_ref[...], LANES) * (1.0 - lam_init)).astype(o_ref.dtype)


def _attention_call(body, name, extras, extra_specs, q, k, v, tq, row0, n_rows, out_init):
    b, t, qw = q.shape
    blk0 = row0 // tq
    n_extra = len(extras)
    in_specs = list(extra_specs) + [
        pl.BlockSpec((1, tq, qw // 2), lambda bi, p, i: (bi, blk0 + i, p)),
        pl.BlockSpec((1, t, k.shape[2] // 2), lambda bi, p, i: (bi, 0, p)),
        pl.BlockSpec((1, t, LANES), lambda bi, p, i: (bi, 0, p))]
    args = list(extras) + [q, k, v]
    aliases = {}
    if out_init is not None:
        in_specs.append(pl.BlockSpec(memory_space=pl.ANY))
        aliases = {len(args): 0}
        args.append(out_init)

    def kern(*refs):
        body(*refs[:n_extra + 3], refs[-1])

    return pl.pallas_call(
        kern,
        out_shape=jax.ShapeDtypeStruct((b, t, 256), BF16),
        grid=(b, 2, n_rows // tq),
        in_specs=in_specs,
        out_specs=pl.BlockSpec((1, tq, LANES), lambda bi, p, i: (bi, blk0 + i, p)),
        input_output_aliases=aliases,
        compiler_params=_cparams(("parallel", "parallel", "arbitrary")),
        name=name,
    )(*args)


def _two_pass_attention(body_of, name, extras, extra_specs, q, k, v, n_lat, tq_lat, with_ctx):
    t = q.shape[1]
    lat_chunks, ctx_chunks = _key_chunks(n_lat, t)
    out = jnp.zeros((q.shape[0], t, 256), BF16)
    out = _attention_call(body_of(lat_chunks), name, extras, extra_specs, q, k, v, tq_lat, 0, n_lat, out)
    if not with_ctx:
        return out
    return _attention_call(body_of(ctx_chunks), name + "_ctx", extras, extra_specs, q, k, v,
                           TQ, n_lat, t - n_lat, out)


def _mla_attention(q, k, v, n_lat, with_ctx):
    body_of = lambda chunks: functools.partial(_mla_kernel, chunks=chunks)
    return _two_pass_attention(body_of, "mla_attention", [], [], q, k, v, n_lat, TQ_MLA, with_ctx)


def _diff_attention(lam, q, k, v, gain, n_lat, lam_init, with_ctx):
    body_of = lambda chunks: functools.partial(_diff_kernel, chunks=chunks, lam_init=lam_init)
    specs = [pl.BlockSpec(memory_space=pltpu.SMEM), _const_spec((1, LANES))]
    return _two_pass_attention(body_of, "diff_attention", [lam, gain], specs, q, k, v, n_lat, TQ, with_ctx)


def _merge_kernel(x_ref, mod_ref, n1_ref, n2_ref, a_ref, of_ref, ob_ref, bg_ref, hg_ref, c_ref, d_ref,
                  wg_ref, wb_ref, wo_ref, wr_ref, br_ref, xo_ref, h2_ref, ri_ref, rw_ref):
    d = D_MODEL
    nb, tm, _ = x_ref.shape
    rows = nb * tm
    flat = lambda a: a.reshape(rows, a.shape[-1])
    x = x_ref[...]
    m = mod_ref[:, 0]
    mrow = lambda k: m[:, k:k + 1, :]
    h = flat((_rms(x, n1_ref[...]) * (1.0 + mrow(1)) + mrow(0)).astype(BF16))
    g = flat(bg_ref[...]).astype(F32)
    b_out = _head_rms(flat(of_ref[...] + ob_ref[...]), hg_ref[...], 256) * (g * _sigmoid(g))
    branches = (flat(a_ref[...]), b_out.astype(BF16), flat(c_ref[...]), flat(d_ref[...]))
    y = jnp.zeros((rows, d), F32)
    for i, br in enumerate(branches):
        gate = _sigmoid(_dot(h, wg_ref[:, d * i:d * (i + 1)]))
        y = y + gate * _dot(br, wb_ref[i])
    x = x + mrow(2) * _dot(y.astype(BF16), wo_ref[...]).reshape(nb, tm, d)
    xo_ref[...] = x
    h2 = flat(_rms(x, n2_ref[...]) * (1.0 + mrow(4)) + mrow(3))
    h2b = h2.astype(BF16)
    h2_ref[...] = _rows_first(h2).reshape(nb, tm, ROW_SLABS, LANES)

    h2lo = (h2 - h2b.astype(F32)).astype(BF16)
    r = _dot(jnp.concatenate([h2b, h2lo], axis=0), wr_ref[...])
    logits = r[:rows, :LANES] + r[:rows, LANES:] + r[rows:, :LANES] + br_ref[...]
    lane = lax.broadcasted_iota(jnp.int32, (rows, LANES), 1)
    neg = -jnp.inf
    is_g = (lane >= N_EXPERTS) & (lane < N_EXPERTS + N_GROUPS)
    lg = jnp.where(is_g, logits, neg)
    mg = jnp.max(lg, axis=1, keepdims=True)
    g_val = 1.0 / jnp.sum(jnp.exp(lg - mg), axis=1, keepdims=True)
    g_idx = jnp.min(jnp.where(lg == mg, lane, 4 * LANES), axis=1, keepdims=True) - N_EXPERTS
    in_group = (lane >= g_idx * EXPERTS_PER_GROUP) & (lane < (g_idx + 1) * EXPERTS_PER_GROUP)
    le = jnp.where(in_group, logits, neg)
    m1 = jnp.max(le, axis=1, keepdims=True)
    e1 = jnp.min(jnp.where(le == m1, lane, 4 * LANES), axis=1, keepdims=True)
    le2 = jnp.where(lane == e1, neg, le)
    m2 = jnp.max(le2, axis=1, keepdims=True)
    e2 = jnp.min(jnp.where(le2 == m2, lane, 4 * LANES), axis=1, keepdims=True)
    r2 = jnp.exp(m2 - m1)
    v1 = g_val / (1.0 + r2)
    ri_ref[...] = jnp.where(lane == 0, e1, jnp.where(lane == 1, e2, 0)).reshape(nb, tm, LANES)
    rw_ref[...] = jnp.where(lane == 0, v1, jnp.where(lane == 1, v1 * r2, 0.0)).reshape(nb, tm, LANES)


def _merge(x, mod, n1, n2, a_o, o_f, o_b, bg, hg, c_o, d_o, wg, wb, wo, wr, br, n_lat_tiles, t):
    b, _, d = x.shape
    tm = TM_PROJ
    nb = MERGE_BATCH if b % MERGE_BATCH == 0 else 1
    tile = lambda w: pl.BlockSpec((nb, tm, w), lambda bi, i: (bi, i, 0))
    return pl.pallas_call(
        _merge_kernel,
        out_shape=[jax.ShapeDtypeStruct((b, t, d), F32), jax.ShapeDtypeStruct((b, t, ROW_SLABS, LANES), F32),
                   jax.ShapeDtypeStruct((b, t, LANES), jnp.int32), jax.ShapeDtypeStruct((b, t, LANES), F32)],
        grid=(b // nb, pl.cdiv(t, tm)),
        in_specs=[tile(d), pl.BlockSpec((nb, 1, 8, d), lambda bi, i: (bi, i // n_lat_tiles, 0, 0)),
                  _const_spec((1, d)), _const_spec((1, d)),
                  tile(256), tile(256), tile(256), tile(256), _const_spec((1, 256)), tile(256), tile(256),
                  _const_spec((d, 4 * d)), _const_spec((4, 256, d)), _const_spec((d, d)),
                  _const_spec((d, 256)), _const_spec((1, LANES))],
        out_specs=[tile(d), pl.BlockSpec((nb, tm, ROW_SLABS, LANES), lambda bi, i: (bi, i, 0, 0)),
                   tile(LANES), tile(LANES)],
        compiler_params=_cparams(("parallel", "parallel")),
        name="merge",
    )(x, mod, n1, n2, a_o, o_f, o_b, bg, hg, c_o, d_o, wg, wb, wo, wr, br)


def _route_positions(ri, n_tok):
    ef = ri[:, :2].reshape(-1)
    rb = 2 * TM
    oh = (ef[:, None] == jnp.arange(N_EXPERTS, dtype=jnp.int32)[None, :]).astype(F32).reshape(-1, rb, N_EXPERTS)
    tri = (jnp.arange(rb)[:, None] >= jnp.arange(rb)[None, :]).astype(F32)
    within = jnp.einsum("ij,gje->gie", tri, oh)
    tot = within[:, -1, :]
    before = jnp.cumsum(tot, axis=0) - tot
    cnt = jnp.sum(tot, axis=0).astype(jnp.int32)
    pcnt = ((cnt + MOE_TILE - 1) // MOE_TILE) * MOE_TILE
    end = jnp.cumsum(pcnt)
    start = (end - pcnt).astype(F32)
    pos = jnp.sum((within + (before + start[None, :])[:, None, :]) * oh, axis=2) - 1.0
    pos = pos.reshape(-1)
    n_tiles = 2 * n_tok // MOE_TILE + N_EXPERTS
    n_act = (end[-1] // MOE_TILE).astype(jnp.int32)
    tile_row = jnp.minimum(jnp.arange(n_tiles, dtype=jnp.int32), n_act - 1) * MOE_TILE
    tile_expert = jnp.sum((end[None, :] <= tile_row[:, None]).astype(jnp.int32), axis=1)
    tile_expert = jnp.minimum(tile_expert, N_EXPERTS - 1)
    return pos.reshape(n_tok // TM, 1, 2 * TM).astype(jnp.int32), tile_expert, n_act.reshape(1)


def _dispatch_kernel(pos_ref, h_ref, xs_in_ref, xs_ref, sem):
    del xs_in_ref

    def body(r, carry):
        for s in range(2):
            p = pos_ref[0, 0, 2 * r + s]
            pltpu.make_async_copy(h_ref.at[pl.ds(r, 1)], xs_ref.at[pl.ds(p, 1)], sem.at[s]).start(priority=s)
        return carry

    lax.fori_loop(0, TM, body, 0)
    for s in range(2):
        pltpu.make_async_copy(h_ref, h_ref, sem.at[s]).wait()


def _dispatch(pos, h2, xs_init):
    n = h2.shape[0]
    return pl.pallas_call(
        _dispatch_kernel,
        out_shape=jax.ShapeDtypeStruct(xs_init.shape, F32),
        grid=(n // TM,),
        in_specs=[pl.BlockSpec((1, 1, 2 * TM), lambda t: (t, 0, 0), memory_space=pltpu.SMEM),
                  pl.BlockSpec((TM, ROW_SLABS, LANES), lambda t: (t, 0, 0)),
                  pl.BlockSpec(memory_space=pl.ANY)],
        out_specs=pl.BlockSpec(memory_space=pl.ANY),
        scratch_shapes=[pltpu.SemaphoreType.DMA((2,))],
        input_output_aliases={2: 0},
        compiler_params=_cparams(("arbitrary",)),
        name="moe_dispatch",
    )(pos, h2, xs_init)


def _experts_kernel(te_ref, na_ref, x_ref, wg_ref, wu_ref, wd_ref, o_ref):
    del te_ref
    t = pl.program_id(0)

    @pl.when(t < na_ref[0])
    def _():
        x = _slabs_first(x_ref[...])
        xb = jnp.concatenate([x[s] for s in range(ROW_SLABS)], axis=1).astype(BF16)
        gt = _dot(xb, wg_ref[0])
        up = _dot(xb, wu_ref[0])
        o_ref[...] = _rows_first(_dot((gt * _sigmoid(gt) * up).astype(BF16), wd_ref[0]))

    @pl.when(t >= na_ref[0])
    def _():
        o_ref[...] = jnp.zeros_like(o_ref)


def _experts(tile_expert, n_act, xs, wg, wu, wd):
    n_tiles = tile_expert.shape[0]
    d = D_MODEL
    rows = pl.BlockSpec((MOE_TILE, ROW_SLABS, LANES), lambda t, te, na: (t, 0, 0))
    w_in = pl.BlockSpec((1, d, EXPERT_HIDDEN), lambda t, te, na: (te[t], 0, 0))
    return pl.pallas_call(
        _experts_kernel,
        out_shape=jax.ShapeDtypeStruct((n_tiles * MOE_TILE, ROW_SLABS, LANES), F32),
        grid_spec=pltpu.PrefetchScalarGridSpec(
            num_scalar_prefetch=2, grid=(n_tiles,),
            in_specs=[rows, w_in, w_in,
                      pl.BlockSpec((1, EXPERT_HIDDEN, d), lambda t, te, na: (te[t], 0, 0))],
            out_specs=rows),
        compiler_params=_cparams(("arbitrary",)),
        name="moe_experts",
    )(tile_expert, n_act, xs, wg, wu, wd)


def _combine_kernel(pos_ref, posn_ref, ys_ref, w_ref, x_ref, mod_ref, g_ref, o_ref, buf, sem, *, final, n_steps):
    g = pl.program_id(0) * pl.num_programs(1) + pl.program_id(1)
    cur = g % 2

    def start_gather(p_ref, par):
        def body(r, carry):
            for s in range(2):
                p = p_ref[0, 0, 2 * r + s]
                pltpu.make_async_copy(ys_ref.at[pl.ds(p, 1)], buf.at[par, s, pl.ds(r, 1)],
                                      sem.at[par, s]).start(priority=s)
            return carry

        lax.fori_loop(0, TM, body, 0)

    @pl.when(g == 0)
    def _():
        start_gather(pos_ref, 0)

    @pl.when(g + 1 < n_steps)
    def _():
        start_gather(posn_ref, 1 - cur)

    for s in range(2):
        pltpu.make_async_copy(buf.at[cur, s], buf.at[cur, s], sem.at[cur, s]).wait()
    w = w_ref[0]
    y0, y1 = _slabs_first(buf[cur, 0]), _slabs_first(buf[cur, 1])
    f = jnp.concatenate([w[:, 0:1] * y0[k] + w[:, 1:2] * y1[k] for k in range(ROW_SLABS)], axis=1)
    x = x_ref[0] + mod_ref[0, 0, 5:6, :] * f
    o_ref[0] = _rms(x, g_ref[...]) if final else x


def _combine(pos, ys, rwt, x, mod, gain, n_lat):
    b, t, d = x.shape
    nt = t // TM
    n_steps = b * nt
    tile = lambda w: pl.BlockSpec((1, TM, w), lambda bi, i: (bi, i, 0))
    final = gain is not None
    pos_spec = lambda ahead: pl.BlockSpec(
        (1, 1, 2 * TM), lambda bi, i: (jnp.minimum(bi * nt + i + ahead, n_steps - 1), 0, 0),
        memory_space=pltpu.SMEM)
    return pl.pallas_call(
        functools.partial(_combine_kernel, final=final, n_steps=n_steps),
        out_shape=jax.ShapeDtypeStruct((b, t, d), F32),
        grid=(b, nt),
        in_specs=[pos_spec(0), pos_spec(1),
                  pl.BlockSpec(memory_space=pl.ANY),
                  tile(LANES), tile(d),
                  pl.BlockSpec((1, 1, 8, d), lambda bi, i: (bi, i // (n_lat // TM), 0, 0)),
                  _const_spec((1, d))],
        out_specs=tile(d),
        scratch_shapes=[pltpu.VMEM((2, 2, TM, ROW_SLABS, LANES), F32), pltpu.SemaphoreType.DMA((2, 2))],
        compiler_params=_cparams(("arbitrary", "arbitrary")),
        name="moe_combine",
    )(pos, pos, ys, rwt, x, mod, gain if final else jnp.ones((1, d), F32))


def _moe(x, h2, ri, rwt, wg, wu, wd, xs_init, mod, gain, n_lat):
    b, t, _ = x.shape
    n = b * t
    pos, tile_expert, n_act = _route_positions(ri.reshape(n, LANES), n)
    xs = _dispatch(pos, h2.reshape(n, ROW_SLABS, LANES), xs_init)
    ys = _experts(tile_expert, n_act, xs, wg, wu, wd)
    return _combine(pos, ys, rwt, x, mod, gain, n_lat), xs


def _rope_tables(n_lat, t_all, lane_rope, lane_off, dr):
    half, quarter = dr // 2, dr // 4
    inv_freq = 1.0 / (ROPE_BASE ** (jnp.arange(quarter, dtype=F32) / quarter))
    off = np.asarray(lane_off)
    use_col = off >= half
    j = (off % half) % quarter
    first = (off % half) < quarter
    tok = jnp.arange(n_lat, dtype=jnp.int32)
    row = (tok // GRID_W).astype(F32)
    col = (tok % GRID_W).astype(F32)
    pos = jnp.where(jnp.asarray(use_col)[None, :], col[:, None], row[:, None])
    ang = pos * inv_freq[jnp.asarray(j)][None, :]
    rope = jnp.asarray(lane_rope)[None, :]
    cos = jnp.where(rope, jnp.cos(ang), 1.0)
    sin = jnp.where(rope, jnp.sin(ang) * jnp.where(jnp.asarray(first), -1.0, 1.0)[None, :], 0.0)
    pad = t_all - n_lat
    w = off.shape[0]
    cos = jnp.concatenate([cos, jnp.ones((pad, w), F32)], axis=0)
    sin = jnp.concatenate([sin, jnp.zeros((pad, w), F32)], axis=0)
    return cos, sin


def _all_rope_tables(n_lat, t_all):
    rep = lambda tabs, n: tuple(jnp.tile(tb, (1, n)) for tb in tabs)
    ta = rep(_rope_tables(n_lat, t_all, np.ones(64, bool), np.arange(64), 64), 4)
    lc = np.arange(128)
    tc = rep(_rope_tables(n_lat, t_all, (lc >= 64) & (lc < 96), np.clip(lc - 64, 0, 31), 32), 4)
    td = rep(_rope_tables(n_lat, t_all, np.ones(32, bool), np.arange(32), 32), 8)
    return (*ta, *tc, *td)


def _permute_w_in(w):
    s = np.cumsum([0, 256, 128, 128, 256, 256, 256, 256, 256, 256, 128, 32, 256, 256, 256])
    w = w.astype(BF16)
    seg = lambda i: w[:, s[i]:s[i + 1]]
    z64 = jnp.zeros((w.shape[0], 64), w.dtype)
    z32 = jnp.zeros((w.shape[0], 32), w.dtype)
    kr = jnp.concatenate([z64, seg(10), z32], axis=1)
    cols = [seg(0), seg(1), seg(2),
            seg(3), seg(4), seg(5), seg(6), seg(7),
            seg(8), seg(9), kr,
            seg(11), seg(12), seg(13)]
    return jnp.concatenate(cols, axis=1)


def _permute_mla(w_uq, w_ukv):
    z32 = jnp.zeros((w_uq.shape[0], 32), w_uq.dtype)
    qd = C_NOPE + C_ROPE
    uq = jnp.concatenate([a for h in range(C_HEADS) for a in (w_uq[:, qd * h:qd * (h + 1)], z32)], axis=1)
    z64 = jnp.zeros((w_ukv.shape[0], 64), w_ukv.dtype)
    kd = C_NOPE + C_V
    uk = jnp.concatenate([a for h in range(C_HEADS) for a in (w_ukv[:, kd * h:kd * h + C_NOPE], z64)], axis=1)
    uv = jnp.concatenate([w_ukv[:, kd * h + C_NOPE:kd * (h + 1)] for h in range(C_HEADS)], axis=1)
    return uq.astype(BF16), jnp.concatenate([uk, uv], axis=1).astype(BF16)


def kernel(x, c, ctx, c_ctx, w_mod, b_mod, norm1, norm2, w_in, w_gate, w_branch, w_out, attn_sink, hgrn_lb_logits, hgrn_norm, mla_q_norm, mla_kv_norm, mla_w_uq, mla_w_ukv, diff_lambda, diff_subln, w_router_group, b_router_group, w_router_expert, b_router_expert, w_expert_gate, w_expert_up, w_expert_down, final_norm):
    b, n_lat, d = x.shape
    n_ctx = ctx.shape[1]
    depth = w_mod.shape[0]
    t_all = n_lat + n_ctx
    assert d == D_MODEL and n_lat % TM == 0 and n_ctx % TM == 0 and n_lat % GRID_W == 0 and n_ctx <= n_lat
    assert n_lat >= WIN_Q + 2 * A_WINDOW and n_lat % TQ_MLA == 0 and n_lat % TM_PROJ == 0 and b <= 15
    n_lat_tiles = n_lat // TM_PROJ

    c_all = jnp.concatenate([c, c_ctx[None], jnp.zeros((15 - b, d), F32)], axis=0)
    mod = _modulation(c_all, w_mod, b_mod).reshape(depth, 16, 6, d)
    mod = jnp.pad(mod, ((0, 0), (0, 0), (0, 2), (0, 0)))
    mod = jnp.stack([mod[:, :b], jnp.broadcast_to(mod[:, b:b + 1], (depth, b, 8, d))], axis=2)

    sm = jax.nn.softmax(hgrn_lb_logits.astype(F32), axis=0)
    lower_bounds = jnp.cumsum(sm, axis=0) - sm[0]
    tabs = _all_rope_tables(n_lat, t_all)

    n_tok = b * t_all
    moe_buf = jnp.zeros((2 * n_tok + N_EXPERTS * MOE_TILE, ROW_SLABS, LANES), F32)
    xs = x
    for li in range(depth):
        lam_init = 0.8 - 0.6 * math.exp(-0.3 * li)
        lp = diff_lambda[li].astype(F32)
        lam = (jnp.exp(jnp.sum(lp[0] * lp[1])) - jnp.exp(jnp.sum(lp[2] * lp[3])) + lam_init).reshape(1)
        w_perm = _permute_w_in(w_in[li])
        wuq, wukv = _permute_mla(mla_w_uq[li], mla_w_ukv[li])
        xs, z = _inproj(xs, ctx if li == 0 else None, mod[li], norm1[li][None], w_perm,
                        mla_q_norm[li][None], mla_kv_norm[li][None], wuq, wukv, tabs, n_lat_tiles)
        qa, ka, va, bq, bff, bfb, bv, bg, cq, ck, cv, dq, dk, dv = z
        a_o = _win_attention(attn_sink[li].astype(F32), qa, ka, va, n_lat)
        o_f, o_b = _hgrn(lower_bounds[li], bq, bff, bfb, bv, n_lat)
        with_ctx = li < depth - 1
        c_o = _mla_attention(cq, ck, cv, n_lat, with_ctx)
        d_o = _diff_attention(lam, dq, dk, dv, jnp.tile(diff_subln[li], 2)[None], n_lat, lam_init, with_ctx)
        wr = jnp.concatenate([w_router_expert[li], w_router_group[li],
                              jnp.zeros((d, LANES - N_EXPERTS - N_GROUPS), F32)], axis=1)
        wr_hi = wr.astype(BF16)
        wr_lo = (wr - wr_hi.astype(F32)).astype(BF16)
        br = jnp.concatenate([b_router_expert[li], b_router_group[li],
                              jnp.zeros((LANES - N_EXPERTS - N_GROUPS,), F32)])[None]
        t_keep = t_all if with_ctx else n_lat
        xs, h2, ri, rwt = _merge(xs, mod[li], norm1[li][None], norm2[li][None], a_o, o_f, o_b, bg,
                                 jnp.tile(hgrn_norm[li], B_HEADS)[None], c_o, d_o,
                                 w_gate[li].astype(BF16), w_branch[li].astype(BF16), w_out[li].astype(BF16),
                                 jnp.concatenate([wr_hi, wr_lo], axis=1), br, n_lat_tiles, t_keep)
        xs, moe_buf = _moe(xs, h2, ri, rwt, w_expert_gate[li].astype(BF16), w_expert_up[li].astype(BF16),
                           w_expert_down[li].astype(BF16), moe_buf, mod[li],
                           None if with_ctx else final_norm[None], n_lat)
    return xs
```
